```python
import jax, jax.numpy as jnp
from jax import lax
import numpy as np

D_MODEL = 1024
BATCH = 4
SEQ = 8192
DEPTH = 2
DEC_BATCH = 8
DEC_SEQ = 64
PAST_LEN = 2048

CHUNK = 64
GLA_HEADS = 4
GLA_DK = 64
GLA_DV = 128
GLA_RANK = 16
GLA_NORMALIZER = 16.0
SWA_Q_HEADS = 8
SWA_KV_HEADS = 2
SWA_GROUP = SWA_Q_HEADS // SWA_KV_HEADS
SWA_HD = 64
WINDOW = 128
WIN_CHUNKS = WINDOW // CHUNK
D_FF = 2816
CONV_W = 3
RMS_EPS = 1e-6

GLA_QK = GLA_HEADS * GLA_DK
GLA_V = GLA_HEADS * GLA_DV
SWA_Q = SWA_Q_HEADS * SWA_HD
SWA_KV = SWA_KV_HEADS * SWA_HD
IN_SIZES = (GLA_QK, GLA_QK, GLA_V, GLA_V, GLA_RANK, SWA_Q, SWA_KV, SWA_KV)
IN_SPLITS = tuple(int(s) for s in np.cumsum(IN_SIZES)[:-1])
D_IN = sum(IN_SIZES)
D_MIX = GLA_V + SWA_Q

kernel_name = 'hymba_gla_swa_sink_convffn_adaln_stream_step'


def rmsnorm(x, g):
    xf = x.astype(jnp.float32)
    y = xf * lax.rsqrt(jnp.mean(xf * xf, axis=-1, keepdims=True) + RMS_EPS)
    return (y * g.astype(jnp.float32)).astype(x.dtype)


def gla_blocked(q, k, v, gk, s0, block):
    B, T, H, DK = q.shape
    n = T // block

    def to_blocks(a):
        return jnp.moveaxis(a.reshape(B, n, block, *a.shape[2:]), 1, 0)

    causal = jnp.tril(jnp.ones((block, block), bool))

    def step(S, inp):
        qb, kb, vb, gb = inp
        cum = jnp.cumsum(gb, axis=1)
        diff = cum[:, :, None] - cum[:, None, :]
        decay = jnp.exp(jnp.where(causal[None, :, :, None, None], diff, -jnp.inf))
        att = jnp.einsum('bthk,bshk,btshk->bhts', qb, kb, decay)
        o = (jnp.einsum('bhts,bshv->bthv', att, vb)
             + jnp.einsum('bthk,bhkv->bthv', qb * jnp.exp(cum), S))
        last = cum[:, -1]
        S = (jnp.exp(last)[..., None] * S
             + jnp.einsum('bshk,bshv->bhkv', kb * jnp.exp(last[:, None] - cum), vb))
        return S, o

    S, o = lax.scan(step, s0, (to_blocks(q), to_blocks(k), to_blocks(v), to_blocks(gk)))
    o = jnp.moveaxis(o, 0, 1).reshape(B, T, H, v.shape[-1])
    return o, S


def sink_attention(q, k, v, mask, sinks):
    s = jnp.einsum('bnqhgd,bnkhd->bnhgqk', q, k).astype(jnp.float32) * (SWA_HD ** -0.5)
    s = jnp.where(mask[None, :, None, None], s, -jnp.inf)
    sink = sinks.astype(jnp.float32)[None, None, :, :, None, None]
    m = jnp.maximum(jnp.max(s, axis=-1, keepdims=True), sink)
    p = jnp.exp(s - m)
    denom = jnp.sum(p, axis=-1, keepdims=True) + jnp.exp(sink - m)
    return jnp.einsum('bnhgqk,bnkhd->bnqhgd', p / denom, v.astype(jnp.float32))


def swa_prompt(q, k, v, sinks):
    B, T = q.shape[:2]
    n = T // CHUNK
    qb = q.reshape(B, n, CHUNK, SWA_KV_HEADS, SWA_GROUP, SWA_HD)
    pad = ((0, 0), (WIN_CHUNKS, 0), (0, 0), (0, 0), (0, 0))
    kp = jnp.pad(k.reshape(B, n, CHUNK, SWA_KV_HEADS, SWA_HD), pad)
    vp = jnp.pad(v.reshape(B, n, CHUNK, SWA_KV_HEADS, SWA_HD), pad)
    k_band = jnp.concatenate([kp[:, j:j + n] for j in range(WIN_CHUNKS + 1)], axis=2)
    v_band = jnp.concatenate([vp[:, j:j + n] for j in range(WIN_CHUNKS + 1)], axis=2)
    blk = jnp.arange(n)[:, None] - WIN_CHUNKS + jnp.arange(WIN_CHUNKS + 1)[None, :]
    mask = jnp.repeat(blk >= 0, CHUNK, axis=1)[:, None, :]
    o = sink_attention(qb, k_band, v_band, mask, sinks)
    return o.reshape(B, T, SWA_Q)


def swa_sample(q, k, v, k_cache, v_cache, sinks):
    B, T = q.shape[:2]
    qb = q.reshape(B, 1, T, SWA_KV_HEADS, SWA_GROUP, SWA_HD)
    kk = jnp.concatenate([k_cache, k], axis=1)[:, None]
    vv = jnp.concatenate([v_cache, v], axis=1)[:, None]
    mask = jnp.ones((1, 1, kk.shape[2]), bool)
    o = sink_attention(qb, kk, vv, mask, sinks)
    return o.reshape(B, T, SWA_Q)


def layer(x, c, gla_s0, k_cache, v_cache, conv_past, w_ada, b_ada, g_attn, g_ffn, w_in, w_gk2,
          b_gk, g_gla, sinks, w_out, w_up, conv_w, conv_b, w_down):
    B, T, _ = x.shape
    mod = jax.nn.silu(c) @ w_ada + b_ada
    sh1, sc1, gt1, sh2, sc2, gt2 = jnp.split(mod[:, None, :], 6, axis=-1)

    h = rmsnorm(x, g_attn) * (1 + sc1) + sh1
    proj = h @ w_in
    qg, kg, vg, og, lr, qs, ks, vs = jnp.split(proj, IN_SPLITS, axis=-1)

    gk = jax.nn.log_sigmoid((lr @ w_gk2 + b_gk).astype(jnp.float32)) / GLA_NORMALIZER
    gk = gk.reshape(B, T, GLA_HEADS, GLA_DK)
    qg = qg.reshape(B, T, GLA_HEADS, GLA_DK).astype(jnp.float32) * (GLA_DK ** -0.5)
    kg = kg.reshape(B, T, GLA_HEADS, GLA_DK).astype(jnp.float32)
    vg = vg.reshape(B, T, GLA_HEADS, GLA_DV).astype(jnp.float32)
    if gla_s0 is None:
        gla_s0 = jnp.zeros((B, GLA_HEADS, GLA_DK, GLA_DV), jnp.float32)
    o_g, s_new = gla_blocked(qg, kg, vg, gk, gla_s0.astype(jnp.float32), min(CHUNK, T))
    o_g = rmsnorm(o_g, g_gla) * jax.nn.silu(og.reshape(B, T, GLA_HEADS, GLA_DV).astype(jnp.float32))
    o_g = o_g.reshape(B, T, GLA_V).astype(x.dtype)

    qs = qs.reshape(B, T, SWA_KV_HEADS, SWA_GROUP, SWA_HD)
    ks = ks.reshape(B, T, SWA_KV_HEADS, SWA_HD)
    vs = vs.reshape(B, T, SWA_KV_HEADS, SWA_HD)
    sk = sinks.reshape(SWA_KV_HEADS, SWA_GROUP)
    if k_cache is None:
        o_s = swa_prompt(qs, ks, vs, sk)
        keep = min(WINDOW, T)
        k_new, v_new = ks[:, T - keep:], vs[:, T - keep:]
    else:
        o_s = swa_sample(qs, ks, vs, k_cache, v_cache, sk)
        k_new, v_new = ks, vs
    o_s = o_s.astype(x.dtype)

    mix = jnp.concatenate([o_g, o_s], axis=-1) @ w_out
    x = x + gt1 * mix

    h2 = rmsnorm(x, g_ffn) * (1 + sc2) + sh2
    u, val = jnp.split(h2 @ w_up, 2, axis=-1)
    if conv_past is None:
        conv_past = jnp.zeros((B, CONV_W - 1, D_FF), u.dtype)
    u_ext = jnp.concatenate([conv_past.astype(u.dtype), u], axis=1)
    uc = sum(conv_w[j] * u_ext[:, j:j + T] for j in range(CONV_W)) + conv_b
    ffn = (jax.nn.silu(uc) * val) @ w_down
    x = x + gt2 * ffn
    conv_new = u_ext[:, -(CONV_W - 1):]
    return x, s_new, k_new, v_new, conv_new


def setup_inputs(seed: int = 0) -> dict:
    key = jax.random.key(seed)
    ks = jax.random.split(key, 24)
    nrm = jax.random.normal
    f32 = jnp.float32
    swa_cache = min(WINDOW, PAST_LEN)
    return {
        'x_prompt': nrm(ks[0], (BATCH, SEQ, D_MODEL), f32),
        'x_sample': nrm(ks[1], (DEC_BATCH, DEC_SEQ, D_MODEL), f32),
        'state_gla': nrm(ks[2], (DEPTH, DEC_BATCH, GLA_HEADS, GLA_DK, GLA_DV), f32),
        'cache_swa_k': nrm(ks[3], (DEPTH, DEC_BATCH, swa_cache, SWA_KV_HEADS, SWA_HD), f32),
        'cache_swa_v': nrm(ks[4], (DEPTH, DEC_BATCH, swa_cache, SWA_KV_HEADS, SWA_HD), f32),
        'state_conv': nrm(ks[5], (DEPTH, DEC_BATCH, CONV_W - 1, D_FF), f32),
        'c_prompt': nrm(ks[6], (BATCH, D_MODEL), f32),
        'c_sample': nrm(ks[7], (DEC_BATCH, D_MODEL), f32),
        'w_ada': nrm(ks[8], (DEPTH, D_MODEL, 6 * D_MODEL), f32) * (0.5 * D_MODEL ** -0.5),
        'b_ada': nrm(ks[9], (DEPTH, 6 * D_MODEL), f32) * 0.01,
        'g_attn': 1.0 + 0.01 * nrm(ks[10], (DEPTH, D_MODEL), f32),
        'g_ffn': 1.0 + 0.01 * nrm(ks[11], (DEPTH, D_MODEL), f32),
        'w_in': nrm(ks[12], (DEPTH, D_MODEL, D_IN), f32) * D_MODEL ** -0.5,
        'w_gk2': nrm(ks[13], (DEPTH, GLA_RANK, GLA_QK), f32) * GLA_RANK ** -0.5,
        'b_gk': nrm(ks[14], (DEPTH, GLA_QK), f32) * 0.01,
        'g_gla': 1.0 + 0.01 * nrm(ks[15], (DEPTH, GLA_DV), f32),
        'sinks': nrm(ks[16], (DEPTH, SWA_Q_HEADS), f32),
        'w_out': nrm(ks[17], (DEPTH, D_MIX, D_MODEL), f32) * D_MIX ** -0.5,
        'w_up': nrm(ks[18], (DEPTH, D_MODEL, 2 * D_FF), f32) * D_MODEL ** -0.5,
        'conv_w': nrm(ks[19], (DEPTH, CONV_W, D_FF), f32) * CONV_W ** -0.5,
        'conv_b': nrm(ks[20], (DEPTH, D_FF), f32) * 0.01,
        'w_down': nrm(ks[21], (DEPTH, D_FF, D_MODEL), f32) * D_FF ** -0.5,
        'g_final': 1.0 + 0.01 * nrm(ks[22], (D_MODEL,), f32),
    }


def reference(x_prompt, x_sample, state_gla, cache_swa_k, cache_swa_v, state_conv, c_prompt,
              c_sample, w_ada, b_ada, g_attn, g_ffn, w_in, w_gk2, b_gk, g_gla, sinks, w_out, w_up,
              conv_w, conv_b, w_down, g_final):
    yp, ys = x_prompt, x_sample
    p_gla, p_k, p_v, p_conv = [], [], [], []
    s_gla, s_k, s_v, s_conv = [], [], [], []
    for i in range(DEPTH):
        lw = (w_ada[i], b_ada[i], g_attn[i], g_ffn[i], w_in[i], w_gk2[i], b_gk[i], g_gla[i],
              sinks[i], w_out[i], w_up[i], conv_w[i], conv_b[i], w_down[i])
        yp, a, b, c, d = layer(yp, c_prompt, None, None, None, None, *lw)
        p_gla.append(a); p_k.append(b); p_v.append(c); p_conv.append(d)
        ys, a, b, c, d = layer(ys, c_sample, state_gla[i], cache_swa_k[i], cache_swa_v[i],
                               state_conv[i], *lw)
        s_gla.append(a); s_k.append(b); s_v.append(c); s_conv.append(d)
    y_prompt = rmsnorm(yp, g_final)
    y_sample = rmsnorm(ys, g_final)
    return (y_prompt, y_sample,
            jnp.stack(p_gla), jnp.stack(p_k), jnp.stack(p_v), jnp.stack(p_conv),
            jnp.stack(s_gla), jnp.stack(s_k), jnp.stack(s_v), jnp.stack(s_conv))
```

```python
import functools

import jax
import jax.numpy as jnp
from jax import lax
from jax.experimental import pallas as pl
from jax.experimental.pallas import tpu as pltpu

F32 = jnp.float32
BF16 = jnp.bfloat16

D_MODEL = 1024
CHUNK = 64
GLA_HEADS = 4
GLA_DK = 64
GLA_DV = 128
GLA_RANK = 16
GLA_NORMALIZER = 16.0
SWA_Q_HEADS = 8
SWA_KV_HEADS = 2
SWA_GROUP = SWA_Q_HEADS // SWA_KV_HEADS
SWA_HD = 64
WINDOW = 128
D_FF = 2816
CONV_W = 3
RMS_EPS = 1e-6

GLA_QK = GLA_HEADS * GLA_DK
GLA_V = GLA_HEADS * GLA_DV
SWA_Q = SWA_Q_HEADS * SWA_HD
SWA_KV = SWA_KV_HEADS * SWA_HD
D_MIX = GLA_V + SWA_Q
BAND = WINDOW + CHUNK

LANE = 128
SUBLANE = 8

C_QG = 0
C_KG = C_QG + GLA_QK
C_VG = C_KG + GLA_QK
C_OG = C_VG + GLA_V
C_QS = C_OG + GLA_V
C_KS = C_QS + SWA_Q
C_VS = C_KS + SWA_KV
C_LR = C_VS + SWA_KV
LR_PAD = LANE
D_IN_P = C_LR + LR_PAD
D_IN = 2 * GLA_QK + 2 * GLA_V + GLA_RANK + SWA_Q + 2 * SWA_KV

ADA_TILE = 1536
ADA_ROWS = 16
PROMPT_TILE = 512
FF_SPLIT = 2
FF_TILE = D_FF // FF_SPLIT
VMEM_LIMIT = 56 * 1024 * 1024

NT_DIMS = (((1,), (1,)), ((), ()))
TN_DIMS = (((0,), (0,)), ((), ()))


def _dot(a, b):
    return jnp.dot(a, b, preferred_element_type=F32)


def _dot_nt(a, b):
    return lax.dot_general(a, b, NT_DIMS, preferred_element_type=F32)


def _dot_tn(a, b):
    return lax.dot_general(a, b, TN_DIMS, preferred_element_type=F32)


def _split_bf16(a):
    hi = a.astype(BF16)
    lo = (a - hi.astype(F32)).astype(BF16)
    return hi, lo


def _sigmoid(a):
    return 1.0 / (1.0 + jnp.exp(-a))


def _log_sigmoid(a):
    return jnp.minimum(a, 0.0) - jnp.log(1.0 + jnp.exp(-jnp.abs(a)))


def _rms_scale(a):
    return lax.rsqrt(jnp.mean(a * a, axis=-1, keepdims=True) + RMS_EPS)


def _ada_kernel(c_ref, w_ref, b_ref, o_ref):
    c = c_ref[...]
    a = (c * _sigmoid(c)).astype(BF16)
    o_ref[...] = _dot(a, w_ref[...].astype(BF16)) + b_ref[...]


def _ada_call(c_all, w_ada, b_ada):
    depth = w_ada.shape[0]
    n_tiles = (6 * D_MODEL) // ADA_TILE
    return pl.pallas_call(
        _ada_kernel,
        grid=(depth, n_tiles),
        in_specs=[
            pl.BlockSpec((ADA_ROWS, D_MODEL), lambda l, j: (0, 0)),
            pl.BlockSpec((None, D_MODEL, ADA_TILE), lambda l, j: (l, 0, j)),
            pl.BlockSpec((None, 1, ADA_TILE), lambda l, j: (l, 0, j)),
        ],
        out_specs=pl.BlockSpec((None, ADA_ROWS, ADA_TILE), lambda l, j: (l, 0, j)),
        out_shape=jax.ShapeDtypeStruct((depth, ADA_ROWS, 6 * D_MODEL), F32),
        compiler_params=pltpu.CompilerParams(
            dimension_semantics=("arbitrary", "arbitrary"), vmem_limit_bytes=VMEM_LIMIT),
        name="adaln_mod",
    )(c_all, w_ada, b_ada.reshape(depth, 1, 6 * D_MODEL))


def _gla_chunk(q, k, v, gk, s_ref, tri, causal, ones_col):
    gk_hi, gk_lo = _split_bf16(gk)
    cum = _dot(tri, gk_hi) + _dot(tri, gk_lo)
    last_b = _dot_tn(gk_hi, ones_col) + _dot_tn(gk_lo, ones_col)
    decay_b = jnp.exp(last_b)
    mid = cum[CHUNK // 2:CHUNK // 2 + 1, :]
    last = cum[CHUNK - 1:CHUNK, :]
    q_mid = (q * jnp.exp(cum - mid)).astype(BF16)
    k_mid = (k * jnp.exp(mid - cum)).astype(BF16)
    q_in = (q * jnp.exp(cum)).astype(BF16)
    k_out = (k * jnp.exp(last - cum)).astype(BF16)
    v16 = v.astype(BF16)
    outs = []
    for h in range(GLA_HEADS):
        ks = slice(h * GLA_DK, (h + 1) * GLA_DK)
        vs = slice(h * GLA_DV, (h + 1) * GLA_DV)
        att = _dot_nt(q_mid[:, ks], k_mid[:, ks])
        att = jnp.where(causal, att, 0.0).astype(BF16)
        s_old = s_ref[ks, :]
        o = _dot(att, v16[:, vs]) + _dot(q_in[:, ks], s_old.astype(BF16))
        s_ref[ks, :] = decay_b[ks, :] * s_old + _dot_tn(k_out[:, ks], v16[:, vs])
        outs.append(o)
    return outs


def _swa_chunk(q, kband, vband, sinks_ref, first_valid, col_ids):
    q16 = q.astype(BF16)
    k16 = kband.astype(BF16)
    v16 = vband.astype(BF16)
    outs = []
    for j in range(SWA_Q_HEADS):
        kv = j // SWA_GROUP
        qs = slice(j * SWA_HD, (j + 1) * SWA_HD)
        cs = slice(kv * SWA_HD, (kv + 1) * SWA_HD)
        s = _dot_nt(q16[:, qs], k16[:, cs]) * (SWA_HD ** -0.5)
        if first_valid is not None:
            s = jnp.where(col_ids >= first_valid, s, -jnp.inf)
        sink = sinks_ref[j]
        m = jnp.maximum(jnp.max(s, axis=-1, keepdims=True), sink)
        p = jnp.exp(s - m)
        den = jnp.sum(p, axis=-1, keepdims=True) + jnp.exp(sink - m)
        outs.append(_dot(p.astype(BF16), v16[:, cs]) / den)
    return outs


def _mixer_kernel(sample, n_chunks, *refs):
    if sample:
        (x_ref, mod_ref, g_attn_ref, w_in_ref, w_gk2_ref, b_gk_ref, g_gla_ref, sinks_ref, w_out_ref,
         s0_ref, kc_ref, vc_ref,
         xo_ref, so_ref, ko_ref, vo_ref,
         h_scr, proj_scr, omix_scr, s_scr) = refs
    else:
        (x_ref, mod_ref, g_attn_ref, w_in_ref, w_gk2_ref, b_gk_ref, g_gla_ref, sinks_ref, w_out_ref,
         xo_ref, so_ref, ko_ref, vo_ref,
         h_scr, proj_scr, omix_scr, s_scr, kv_scr) = refs
        t = pl.program_id(1)
        n_t = pl.num_programs(1)

        @pl.when(t == 0)
        def _():
            s_scr[...] = jnp.zeros_like(s_scr)
            kv_scr[0:WINDOW, :] = jnp.zeros((WINDOW, 2 * SWA_KV), F32)

    rows = n_chunks * CHUNK
    g_attn = g_attn_ref[...]

    def mod_row(c, idx):
        if sample:
            return mod_ref[c, idx:idx + 1, :]
        return mod_ref[idx:idx + 1, :]

    for c in range(n_chunks if sample else 1):
        rs = slice(c * CHUNK, (c + 1) * CHUNK) if sample else slice(0, rows)
        x = x_ref[rs, :]
        gain = g_attn * (1.0 + mod_row(c, 1))
        h_scr[rs, :] = (x * _rms_scale(x) * gain + mod_row(c, 0)).astype(BF16)

    proj_scr[...] = _dot(h_scr[...], w_in_ref[...])

    lr = proj_scr[:, C_LR:C_LR + LR_PAD].astype(BF16)
    gk_all = _log_sigmoid(_dot(lr, w_gk2_ref[...]) + b_gk_ref[...]) * (1.0 / GLA_NORMALIZER)

    if not sample:
        kv_scr[WINDOW:WINDOW + rows, :] = proj_scr[:, C_KS:C_KS + 2 * SWA_KV]

    r_id = lax.broadcasted_iota(jnp.int32, (CHUNK, CHUNK), 0)
    c_id = lax.broadcasted_iota(jnp.int32, (CHUNK, CHUNK), 1)
    causal = r_id >= c_id
    tri = jnp.where(causal, 1.0, 0.0).astype(BF16)
    ones_col = jnp.ones((CHUNK, LANE), BF16)
    band_ids = lax.broadcasted_iota(jnp.int32, (CHUNK, BAND), 1)
    g_gla = g_gla_ref[...]

    for c in range(n_chunks):
        rs = slice(c * CHUNK, (c + 1) * CHUNK)
        q = proj_scr[rs, C_QG:C_QG + GLA_QK] * (GLA_DK ** -0.5)
        k = proj_scr[rs, C_KG:C_KG + GLA_QK]
        v = proj_scr[rs, C_VG:C_VG + GLA_V]
        gk = gk_all[rs, :]
        if sample:
            s_scr[...] = s0_ref[c]
        o_heads = _gla_chunk(q, k, v, gk, s_scr, tri, causal, ones_col)
        if sample:
            so_ref[c] = s_scr[...]
        for h in range(GLA_HEADS):
            vs = slice(h * GLA_DV, (h + 1) * GLA_DV)
            o = o_heads[h]
            og = proj_scr[rs, C_OG + h * GLA_DV:C_OG + (h + 1) * GLA_DV]
            y = o * _rms_scale(o) * g_gla * (og * _sigmoid(og))
            omix_scr[rs, vs] = y.astype(BF16)
        qs = proj_scr[rs, C_QS:C_QS + SWA_Q]
        if sample:
            k_new = proj_scr[rs, C_KS:C_KS + SWA_KV]
            v_new = proj_scr[rs, C_VS:C_VS + SWA_KV]
            kband = jnp.concatenate([kc_ref[c], k_new], axis=0)
            vband = jnp.concatenate([vc_ref[c], v_new], axis=0)
            first_valid = None
        else:
            band = kv_scr[c * CHUNK:c * CHUNK + BAND, :]
            kband = band[:, 0:SWA_KV]
            vband = band[:, SWA_KV:2 * SWA_KV]
            first_valid = WINDOW - (t * rows + c * CHUNK)
        o_swa = _swa_chunk(qs, kband, vband, sinks_ref, first_valid, band_ids)
        for j in range(SWA_Q_HEADS):
            omix_scr[rs, GLA_V + j * SWA_HD:GLA_V + (j + 1) * SWA_HD] = o_swa[j].astype(BF16)

    mix = _dot(omix_scr[...], w_out_ref[...])
    for c in range(n_chunks if sample else 1):
        rs = slice(c * CHUNK, (c + 1) * CHUNK) if sample else slice(0, rows)
        xo_ref[rs, :] = x_ref[rs, :] + mod_row(c, 2) * mix[rs, :]

    if sample:
        ko_ref[...] = proj_scr[:, C_KS:C_KS + SWA_KV]
        vo_ref[...] = proj_scr[:, C_VS:C_VS + SWA_KV]
    else:
        kv_scr[0:WINDOW, :] = kv_scr[rows:rows + WINDOW, :]

        @pl.when(t == n_t - 1)
        def _():
            so_ref[...] = s_scr[...]
            ko_ref[...] = kv_scr[0:WINDOW, 0:SWA_KV]
            vo_ref[...] = kv_scr[0:WINDOW, SWA_KV:2 * SWA_KV]


def _const_spec(shape):
    zeros = (0,) * len(shape)
    return pl.BlockSpec(shape, lambda *_: zeros, pipeline_mode=pl.Buffered(1))


def _mixer_weight_specs():
    return [
        _const_spec((1, D_MODEL)),
        _const_spec((D_MODEL, D_IN_P)),
        _const_spec((LR_PAD, GLA_QK)),
        _const_spec((1, GLA_QK)),
        _const_spec((1, GLA_DV)),
        pl.BlockSpec(memory_space=pltpu.SMEM),
        _const_spec((D_MIX, D_MODEL)),
    ]


def _mixer_prompt_call(x, mod, weights):
    batch, seq, _ = x.shape
    tile = min(PROMPT_TILE, seq)
    n_chunks = tile // CHUNK
    n_t = seq // tile
    state_rows = GLA_HEADS * GLA_DK
    return pl.pallas_call(
        functools.partial(_mixer_kernel, False, n_chunks),
        grid=(batch, n_t),
        in_specs=[
            pl.BlockSpec((None, tile, D_MODEL), lambda b, t: (b, t, 0)),
            pl.BlockSpec((None, 6, D_MODEL), lambda b, t: (b, 0, 0)),
        ] + _mixer_weight_specs(),
        out_specs=[
            pl.BlockSpec((None, tile, D_MODEL), lambda b, t: (b, t, 0)),
            pl.BlockSpec((None, state_rows, GLA_DV), lambda b, t: (b, 0, 0)),
            pl.BlockSpec((None, WINDOW, SWA_KV), lambda b, t: (b, 0, 0)),
            pl.BlockSpec((None, WINDOW, SWA_KV), lambda b, t: (b, 0, 0)),
        ],
        out_shape=[
            jax.ShapeDtypeStruct((batch, seq, D_MODEL), F32),
            jax.ShapeDtypeStruct((batch, state_rows, GLA_DV), F32),
            jax.ShapeDtypeStruct((batch, WINDOW, SWA_KV), F32),
            jax.ShapeDtypeStruct((batch, WINDOW, SWA_KV), F32),
        ],
        scratch_shapes=[
            pltpu.VMEM((tile, D_MODEL), BF16),
            pltpu.VMEM((tile, D_IN_P), F32),
            pltpu.VMEM((tile, D_MIX), BF16),
            pltpu.VMEM((state_rows, GLA_DV), F32),
            pltpu.VMEM((WINDOW + tile, 2 * SWA_KV), F32),
        ],
        compiler_params=pltpu.CompilerParams(
            dimension_semantics=("arbitrary", "arbitrary"), vmem_limit_bytes=VMEM_LIMIT),
        name="mixer_prompt",
    )(x, mod, *weights)


def _mixer_sample_call(x, mod, weights, s0, k_cache, v_cache):
    batch, seq, _ = x.shape
    assert seq == CHUNK
    rows = batch * seq
    state_rows = GLA_HEADS * GLA_DK

    def full(shape):
        zeros = (0,) * len(shape)
        return pl.BlockSpec(shape, lambda i: zeros)

    return pl.pallas_call(
        functools.partial(_mixer_kernel, True, batch),
        grid=(1,),
        in_specs=[full((rows, D_MODEL)), full((batch, 6, D_MODEL))] + _mixer_weight_specs() + [
            full((batch, state_rows, GLA_DV)),
            full((batch, WINDOW, SWA_KV)),
            full((batch, WINDOW, SWA_KV)),
        ],
        out_specs=[
            full((rows, D_MODEL)),
            full((batch, state_rows, GLA_DV)),
            full((rows, SWA_KV)),
            full((rows, SWA_KV)),
        ],
        out_shape=[
            jax.ShapeDtypeStruct((rows, D_MODEL), F32),
            jax.ShapeDtypeStruct((batch, state_rows, GLA_DV), F32),
            jax.ShapeDtypeStruct((rows, SWA_KV), F32),
            jax.ShapeDtypeStruct((rows, SWA_KV), F32),
        ],
        scratch_shapes=[
            pltpu.VMEM((rows, D_MODEL), BF16),
            pltpu.VMEM((rows, D_IN_P), F32),
            pltpu.VMEM((rows, D_MIX), BF16),
            pltpu.VMEM((state_rows, GLA_DV), F32),
        ],
        compiler_params=pltpu.CompilerParams(
            dimension_semantics=("arbitrary",), vmem_limit_bytes=VMEM_LIMIT),
        name="mixer_sample",
    )(x.reshape(rows, D_MODEL), mod, *weights, s0, k_cache, v_cache)


def _ffn_kernel(sample, final, n_seg, seg_len, *refs):
    if sample:
        (x_ref, mod_ref, g_ffn_ref, w_up_ref, conv_w_ref, conv_b_ref, w_down_ref, g_final_ref,
         past_ref, xo_ref, co_ref, h_scr, ub_scr, act_scr) = refs
    else:
        (x_ref, mod_ref, g_ffn_ref, w_up_ref, conv_w_ref, conv_b_ref, w_down_ref, g_final_ref,
         xo_ref, co_ref, h_scr, ub_scr, act_scr, past_scr) = refs
        t = pl.program_id(1)
        n_t = pl.num_programs(1)

        @pl.when(t == 0)
        def _():
            past_scr[...] = jnp.zeros_like(past_scr)

    rows = n_seg * seg_len
    stride = seg_len + SUBLANE
    g_ffn = g_ffn_ref[...]

    def mod_row(c, idx):
        if sample:
            return mod_ref[c, idx:idx + 1, :]
        return mod_ref[idx:idx + 1, :]

    for c in range(n_seg):
        rs = slice(c * seg_len, (c + 1) * seg_len)
        x = x_ref[rs, :]
        gain = g_ffn * (1.0 + mod_row(c, 4))
        h_scr[rs, :] = (x * _rms_scale(x) * gain + mod_row(c, 3)).astype(BF16)

    acc = None
    for part in range(FF_SPLIT):
        cols = slice(part * FF_TILE, (part + 1) * FF_TILE)
        h = h_scr[...]
        u = _dot(h, w_up_ref[:, part * FF_TILE:(part + 1) * FF_TILE])
        val = _dot(h, w_up_ref[:, D_FF + part * FF_TILE:D_FF + (part + 1) * FF_TILE])
        w0 = conv_w_ref[0:1, cols]
        w1 = conv_w_ref[1:2, cols]
        w2 = conv_w_ref[2:3, cols]
        cb = conv_b_ref[:, cols]
        for c in range(n_seg):
            base = c * stride
            rs = slice(c * seg_len, (c + 1) * seg_len)
            if sample:
                ub_scr[base + SUBLANE - 2:base + SUBLANE, :] = past_ref[c, :, cols]
            else:
                ub_scr[base + SUBLANE - 2:base + SUBLANE, :] = past_scr[:, cols]
            u_seg = u[rs, :]
            ub_scr[base + SUBLANE:base + SUBLANE + seg_len, :] = u_seg
            u1 = ub_scr[base + SUBLANE - 1:base + SUBLANE - 1 + seg_len, :]
            u2 = ub_scr[base + SUBLANE - 2:base + SUBLANE - 2 + seg_len, :]
            uc = w0 * u2 + w1 * u1 + w2 * u_seg + cb
            act_scr[rs, :] = (uc * _sigmoid(uc) * val[rs, :]).astype(BF16)
            tail = ub_scr[base + seg_len + SUBLANE - 2:base + seg_len + SUBLANE, :]
            if sample:
                co_ref[c, :, cols] = tail
            else:
                past_scr[:, cols] = tail
        d = _dot(act_scr[...], w_down_ref[part * FF_TILE:(part + 1) * FF_TILE, :])
        acc = d if acc is None else acc + d

    for c in range(n_seg):
        rs = slice(c * seg_len, (c + 1) * seg_len)
        y = x_ref[rs, :] + mod_row(c, 5) * acc[rs, :]
        if final:
            y = y * _rms_scale(y) * g_final_ref[...]
        xo_ref[rs, :] = y

    if not sample:
        @pl.when(t == n_t - 1)
        def _():
            co_ref[...] = past_scr[...]


def _ffn_weight_specs():
    return [
        _const_spec((1, D_MODEL)),
        _const_spec((D_MODEL, 2 * D_FF)),
        _const_spec((CONV_W, D_FF)),
        _const_spec((1, D_FF)),
        _const_spec((D_FF, D_MODEL)),
        _const_spec((1, D_MODEL)),
    ]


def _ffn_prompt_call(x, mod, weights, final):
    batch, seq, _ = x.shape
    tile = min(PROMPT_TILE, seq)
    n_t = seq // tile
    return pl.pallas_call(
        functools.partial(_ffn_kernel, False, final, 1, tile),
        grid=(batch, n_t),
        in_specs=[
            pl.BlockSpec((None, tile, D_MODEL), lambda b, t: (b, t, 0)),
            pl.BlockSpec((None, 6, D_MODEL), lambda b, t: (b, 0, 0)),
        ] + _ffn_weight_specs(),
        out_specs=[
            pl.BlockSpec((None, tile, D_MODEL), lambda b, t: (b, t, 0)),
            pl.BlockSpec((None, CONV_W - 1, D_FF), lambda b, t: (b, 0, 0)),
        ],
        out_shape=[
            jax.ShapeDtypeStruct((batch, seq, D_MODEL), F32),
            jax.ShapeDtypeStruct((batch, CONV_W - 1, D_FF), F32),
        ],
        scratch_shapes=[
            pltpu.VMEM((tile, D_MODEL), BF16),
            pltpu.VMEM((tile + SUBLANE, FF_TILE), F32),
            pltpu.VMEM((tile, FF_TILE), BF16),
            pltpu.VMEM((CONV_W - 1, D_FF), F32),
        ],
        compiler_params=pltpu.CompilerParams(
            dimension_semantics=("arbitrary", "arbitrary"), vmem_limit_bytes=VMEM_LIMIT),
        name="ffn_prompt",
    )(x, mod, *weights)


def _ffn_sample_call(x2d, mod, weights, past, final, batch, seq):
    rows = batch * seq

    def full(shape):
        zeros = (0,) * len(shape)
        return pl.BlockSpec(shape, lambda i: zeros)

    return pl.pallas_call(
        functools.partial(_ffn_kernel, True, final, batch, seq),
        grid=(1,),
        in_specs=[full((rows, D_MODEL)), full((batch, 6, D_MODEL))] + _ffn_weight_specs() + [
            full((batch, CONV_W - 1, D_FF)),
        ],
        out_specs=[full((rows, D_MODEL)), full((batch, CONV_W - 1, D_FF))],
        out_shape=[
            jax.ShapeDtypeStruct((rows, D_MODEL), F32),
            jax.ShapeDtypeStruct((batch, CONV_W - 1, D_FF), F32),
        ],
        scratch_shapes=[
            pltpu.VMEM((rows, D_MODEL), BF16),
            pltpu.VMEM((batch * (seq + SUBLANE), FF_TILE), F32),
            pltpu.VMEM((rows, FF_TILE), BF16),
        ],
        compiler_params=pltpu.CompilerParams(
            dimension_semantics=("arbitrary",), vmem_limit_bytes=VMEM_LIMIT),
        name="ffn_sample",
    )(x2d, mod, *weights, past)


def _reorder_w_in(w):
    lr0 = 2 * GLA_QK + 2 * GLA_V
    pad = jnp.zeros((D_MODEL, LR_PAD - GLA_RANK), w.dtype)
    return jnp.concatenate([w[:, :lr0], w[:, lr0 + GLA_RANK:], w[:, lr0:lr0 + GLA_RANK], pad], axis=1)


def kernel(x_prompt, x_sample, state_gla, cache_swa_k, cache_swa_v, state_conv, c_prompt, c_sample,
           w_ada, b_ada, g_attn, g_ffn, w_in, w_gk2, b_gk, g_gla, sinks, w_out, w_up, conv_w, conv_b,
           w_down, g_final):
    depth = w_ada.shape[0]
    batch, seq, _ = x_prompt.shape
    dec_batch, dec_seq, _ = x_sample.shape
    state_rows = GLA_HEADS * GLA_DK

    c_all = jnp.concatenate(
        [c_prompt, c_sample, jnp.zeros((ADA_ROWS - batch - dec_batch, D_MODEL), F32)], axis=0)
    mod_all = _ada_call(c_all, w_ada, b_ada).reshape(depth, ADA_ROWS, 6, D_MODEL)

    yp = x_prompt
    ys = x_sample.reshape(dec_batch * dec_seq, D_MODEL)
    outs = [[] for _ in range(8)]
    for i in range(depth):
        mod_p = mod_all[i, :batch]
        mod_s = mod_all[i, batch:batch + dec_batch]
        w_gk2_p = jnp.concatenate(
            [w_gk2[i], jnp.zeros((LR_PAD - GLA_RANK, GLA_QK), F32)], axis=0).astype(BF16)
        mixer_w = (g_attn[i][None], _reorder_w_in(w_in[i]).astype(BF16), w_gk2_p, b_gk[i][None],
                   g_gla[i][None], sinks[i], w_out[i].astype(BF16))
        ffn_w = (g_ffn[i][None], w_up[i].astype(BF16), conv_w[i], conv_b[i][None],
                 w_down[i].astype(BF16), g_final[None])
        final = i == depth - 1

        yp, s_p, k_p, v_p = _mixer_prompt_call(yp, mod_p, mixer_w)
        yp, conv_p = _ffn_prompt_call(yp, mod_p, ffn_w, final)

        ys, s_s, k_s, v_s = _mixer_sample_call(
            ys.reshape(dec_batch, dec_seq, D_MODEL), mod_s, mixer_w,
            state_gla[i].reshape(dec_batch, state_rows, GLA_DV),
            cache_swa_k[i].reshape(dec_batch, WINDOW, SWA_KV),
            cache_swa_v[i].reshape(dec_batch, WINDOW, SWA_KV))
        ys, conv_s = _ffn_sample_call(ys, mod_s, ffn_w, state_conv[i], final, dec_batch, dec_seq)

        keep = min(WINDOW, seq)
        outs[0].append(s_p.reshape(batch, GLA_HEADS, GLA_DK, GLA_DV))
        outs[1].append(k_p.reshape(batch, keep, SWA_KV_HEADS, SWA_HD))
        outs[2].append(v_p.reshape(batch, keep, SWA_KV_HEADS, SWA_HD))
        outs[3].append(conv_p)
        outs[4].append(s_s.reshape(dec_batch, GLA_HEADS, GLA_DK, GLA_DV))
        outs[5].append(k_s.reshape(dec_batch, dec_seq, SWA_KV_HEADS, SWA_HD))
        outs[6].append(v_s.reshape(dec_batch, dec_seq, SWA_KV_HEADS, SWA_HD))
        outs[7].append(conv_s)

    return (yp, ys.reshape(dec_batch, dec_seq, D_MODEL)) + tuple(jnp.stack(o) for o in outs)
```

```python
import functools

import jax
import jax.numpy as jnp
from jax import lax
from jax.experimental import pallas as pl
from jax.experimental.pallas import tpu as pltpu

F32 = jnp.float32
BF16 = jnp.bfloat16

D_MODEL = 1024
CHUNK = 64
GLA_HEADS = 4
GLA_DK = 64
GLA_DV = 128
GLA_RANK = 16
GLA_NORMALIZER = 16.0
SWA_Q_HEADS = 8
SWA_KV_HEADS = 2
SWA_GROUP = SWA_Q_HEADS // SWA_KV_HEADS
SWA_HD = 64
WINDOW = 128
D_FF = 2816
CONV_W = 3
RMS_EPS = 1e-6

GLA_QK = GLA_HEADS * GLA_DK
GLA_V = GLA_HEADS * GLA_DV
SWA_Q = SWA_Q_HEADS * SWA_HD
SWA_KV = SWA_KV_HEADS * SWA_HD
D_MIX = GLA_V + SWA_Q
BAND = WINDOW + CHUNK

LANE = 128
SUBLANE = 8
BF16_ROWS = 16

C_QG = 0
C_KG = C_QG + GLA_QK
C_VG = C_KG + GLA_QK
C_OG = C_VG + GLA_V
C_QS = C_OG + GLA_V
C_KS = C_QS + SWA_Q
C_VS = C_KS + SWA_KV
C_LR = C_VS + SWA_KV
LR_PAD = LANE
D_IN_P = C_LR + LR_PAD
D_IN = 2 * GLA_QK + 2 * GLA_V + GLA_RANK + SWA_Q + 2 * SWA_KV

assert GLA_DK * 2 == LANE and SWA_HD * 2 == LANE and GLA_DV == LANE and SWA_KV == LANE
GLA_PAIRS = GLA_HEADS // 2
PAIR_V = 2 * GLA_DV
SWA_GRP_Q = SWA_GROUP * SWA_HD

ADA_TILE = 1536
ADA_ROWS = 16
PROMPT_TILE = 512
FF_SPLIT = 2
FF_TILE = D_FF // FF_SPLIT
VMEM_LIMIT = 56 * 1024 * 1024

NT_DIMS = (((1,), (1,)), ((), ()))
TN_DIMS = (((0,), (0,)), ((), ()))


def _dot(a, b):
    return jnp.dot(a, b, preferred_element_type=F32)


def _dot_nt(a, b):
    return lax.dot_general(a, b, NT_DIMS, preferred_element_type=F32)


def _dot_tn(a, b):
    return lax.dot_general(a, b, TN_DIMS, preferred_element_type=F32)


def _split_bf16(a):
    hi = a.astype(BF16)
    lo = (a - hi.astype(F32)).astype(BF16)
    return hi, lo


def _sigmoid(a):
    return 1.0 / (1.0 + jnp.exp(-a))


def _log_sigmoid(a):
    return jnp.minimum(a, 0.0) - jnp.log(1.0 + jnp.exp(-jnp.abs(a)))


def _rms_scale(a):
    return lax.rsqrt(jnp.mean(a * a, axis=-1, keepdims=True) + RMS_EPS)


def _ada_kernel(c_ref, w_ref, b_ref, o_ref):
    c = c_ref[...]
    a = (c * _sigmoid(c)).astype(BF16)
    o_ref[...] = _dot(a, w_ref[...].astype(BF16)) + b_ref[...]


def _ada_call(c_all, w_ada, b_ada):
    depth = w_ada.shape[0]
    n_tiles = (6 * D_MODEL) // ADA_TILE
    return pl.pallas_call(
        _ada_kernel,
        grid=(depth, n_tiles),
        in_specs=[
            pl.BlockSpec((ADA_ROWS, D_MODEL), lambda l, j: (0, 0)),
            pl.BlockSpec((None, D_MODEL, ADA_TILE), lambda l, j: (l, 0, j)),
            pl.BlockSpec((None, 1, ADA_TILE), lambda l, j: (l, 0, j)),
        ],
        out_specs=pl.BlockSpec((None, ADA_ROWS, ADA_TILE), lambda l, j: (l, 0, j)),
        out_shape=jax.ShapeDtypeStruct((depth, ADA_ROWS, 6 * D_MODEL), F32),
        compiler_params=pltpu.CompilerParams(
            dimension_semantics=("arbitrary", "arbitrary"), vmem_limit_bytes=VMEM_LIMIT),
        name="adaln_mod",
    )(c_all, w_ada, b_ada.reshape(depth, 1, 6 * D_MODEL))


def _dup_halves(a, low_half):
    swapped = pltpu.roll(a, LANE // 2, axis=1)
    return jnp.where(low_half, a, swapped), jnp.where(low_half, swapped, a)


def _mixer_kernel(sample, n_chunks, *refs):
    if sample:
        (x_ref, mod_ref, g_attn_ref, w_in_ref, w_gk2_ref, b_gk_ref, g_gla_ref, sinks_ref, w_out_ref,
         s0_ref, kc_ref, vc_ref,
         xo_ref, so_ref, ko_ref, vo_ref,
         h_scr, proj_scr, omix_scr, qm_scr, km_scr, qi_scr, ko_scr, vb_scr, dec_scr, u_scr, sb_scr,
         kd_scr, vd_scr) = refs
        t = None
    else:
        (x_ref, mod_ref, g_attn_ref, w_in_ref, w_gk2_ref, b_gk_ref, g_gla_ref, sinks_ref, w_out_ref,
         xo_ref, so_ref, ko_ref, vo_ref,
         h_scr, proj_scr, omix_scr, qm_scr, km_scr, qi_scr, ko_scr, vb_scr, dec_scr, u_scr, sb_scr,
         kd_scr, vd_scr, s_scr) = refs
        t = pl.program_id(1)
        n_t = pl.num_programs(1)

        @pl.when(t == 0)
        def _():
            s_scr[...] = jnp.zeros_like(s_scr)
            kd_scr[0:WINDOW, :] = jnp.zeros((WINDOW, 2 * LANE), BF16)
            vd_scr[0:WINDOW, :] = jnp.zeros((WINDOW, 2 * LANE), BF16)

    rows = n_chunks * CHUNK
    g_attn = g_attn_ref[...]

    def mod_row(c, idx):
        if sample:
            return mod_ref[c, idx:idx + 1, :]
        return mod_ref[idx:idx + 1, :]

    def chunk_rows(c):
        return slice(c * CHUNK, (c + 1) * CHUNK)

    for c in range(n_chunks if sample else 1):
        rs = chunk_rows(c) if sample else slice(0, rows)
        x = x_ref[rs, :]
        gain = g_attn * (1.0 + mod_row(c, 1))
        h_scr[rs, :] = (x * _rms_scale(x) * gain + mod_row(c, 0)).astype(BF16)

    proj_scr[...] = _dot(h_scr[...], w_in_ref[...])

    lr = proj_scr[:, C_LR:C_LR + LR_PAD].astype(BF16)
    gk_all = _log_sigmoid(_dot(lr, w_gk2_ref[...]) + b_gk_ref[...]) * (1.0 / GLA_NORMALIZER)
    vb_scr[...] = proj_scr[:, C_VG:C_VG + GLA_V].astype(BF16)

    r64 = lax.broadcasted_iota(jnp.int32, (CHUNK, CHUNK), 0)
    c64 = lax.broadcasted_iota(jnp.int32, (CHUNK, CHUNK), 1)
    tri = jnp.where(r64 >= c64, 1.0, 0.0).astype(BF16)
    r_pair = lax.broadcasted_iota(jnp.int32, (CHUNK, LANE), 0)
    l_pair = lax.broadcasted_iota(jnp.int32, (CHUNK, LANE), 1)
    causal_pair = r_pair >= (l_pair & (CHUNK - 1))
    low_half = l_pair < LANE // 2
    sel_last = jnp.where(
        lax.broadcasted_iota(jnp.int32, (BF16_ROWS, LANE), 0) == BF16_ROWS - 1, 1.0, 0.0).astype(BF16)
    diag128 = ((lax.broadcasted_iota(jnp.int32, (LANE, LANE), 0) >= LANE // 2)
               == (lax.broadcasted_iota(jnp.int32, (LANE, LANE), 1) >= LANE // 2))
    diag_v = ((lax.broadcasted_iota(jnp.int32, (LANE, PAIR_V), 0) >= LANE // 2)
              == (lax.broadcasted_iota(jnp.int32, (LANE, PAIR_V), 1) >= GLA_DV))
    g_gla = g_gla_ref[...]

    for c in range(n_chunks):
        rs = chunk_rows(c)
        gk_hi, gk_lo = _split_bf16(gk_all[rs, :])
        cum = _dot(tri, gk_hi) + _dot(tri, gk_lo)
        mid = cum[CHUNK // 2:CHUNK // 2 + 1, :]
        last = cum[CHUNK - 1:CHUNK, :]
        q = proj_scr[rs, C_QG:C_QG + GLA_QK] * (GLA_DK ** -0.5)
        k = proj_scr[rs, C_KG:C_KG + GLA_QK]
        qm_scr[rs, :] = (q * jnp.exp(cum - mid)).astype(BF16)
        km_scr[rs, :] = (k * jnp.exp(mid - cum)).astype(BF16)
        qi_scr[rs, :] = (q * jnp.exp(cum)).astype(BF16)
        ko_scr[rs, :] = (k * jnp.exp(last - cum)).astype(BF16)
        tail_hi, tail_lo = _split_bf16(cum[CHUNK - BF16_ROWS:CHUNK, :])
        dec_scr[c] = jnp.exp(_dot_tn(tail_hi, sel_last) + _dot_tn(tail_lo, sel_last))

    for c in range(n_chunks):
        rs = chunk_rows(c)
        for p in range(GLA_PAIRS):
            upd = _dot_tn(ko_scr[rs, p * LANE:(p + 1) * LANE], vb_scr[rs, p * PAIR_V:(p + 1) * PAIR_V])
            u_scr[c, p * LANE:p * LANE + GLA_DK, :] = upd[0:GLA_DK, 0:GLA_DV]
            u_scr[c, p * LANE + GLA_DK:(p + 1) * LANE, :] = upd[GLA_DK:LANE, GLA_DV:PAIR_V]

    state = None if sample else s_scr[...]
    for c in range(n_chunks):
        if sample:
            state = s0_ref[c]
        sb_scr[c] = state.astype(BF16)
        state = dec_scr[c] * state + u_scr[c]
        if sample:
            so_ref[c] = state
    if not sample:
        s_scr[...] = state

    for c in range(n_chunks):
        rs = chunk_rows(c)
        for p in range(GLA_PAIRS):
            ls = slice(p * LANE, (p + 1) * LANE)
            km = km_scr[rs, ls]
            k_bd = jnp.where(diag128, jnp.concatenate([km, km], axis=0), 0.0)
            att = jnp.where(causal_pair, _dot_nt(qm_scr[rs, ls], k_bd), 0.0).astype(BF16)
            v_pair = vb_scr[rs, p * PAIR_V:(p + 1) * PAIR_V]
            s_pair = sb_scr[c, ls, :]
            w_top = jnp.where(diag_v, jnp.concatenate([v_pair, v_pair], axis=0), 0.0)
            w_bot = jnp.where(diag_v, jnp.concatenate([s_pair, s_pair], axis=1), 0.0)
            o_pair = _dot(jnp.concatenate([att, qi_scr[rs, ls]], axis=1),
                          jnp.concatenate([w_top, w_bot], axis=0))
            for hh in range(2):
                h = 2 * p + hh
                o = o_pair[:, hh * GLA_DV:(hh + 1) * GLA_DV]
                og = proj_scr[rs, C_OG + h * GLA_DV:C_OG + (h + 1) * GLA_DV]
                y = o * _rms_scale(o) * g_gla * (og * _sigmoid(og))
                omix_scr[rs, h * GLA_DV:(h + 1) * GLA_DV] = y.astype(BF16)

    band_base = 0 if sample else WINDOW
    k_new = proj_scr[:, C_KS:C_KS + SWA_KV]
    v_new = proj_scr[:, C_VS:C_VS + SWA_KV]
    low_rows = lax.broadcasted_iota(jnp.int32, (rows, LANE), 1) < LANE // 2
    k_dup = [a.astype(BF16) for a in _dup_halves(k_new, low_rows)]
    v_dup = [a.astype(BF16) for a in _dup_halves(v_new, low_rows)]
    if sample:
        low_win = lax.broadcasted_iota(jnp.int32, (WINDOW, LANE), 1) < LANE // 2
        for c in range(n_chunks):
            kc_dup = _dup_halves(kc_ref[c], low_win)
            vc_dup = _dup_halves(vc_ref[c], low_win)
            for kv in range(SWA_KV_HEADS):
                ls = slice(kv * LANE, (kv + 1) * LANE)
                kd_scr[c * BAND:c * BAND + WINDOW, ls] = kc_dup[kv].astype(BF16)
                vd_scr[c * BAND:c * BAND + WINDOW, ls] = vc_dup[kv].astype(BF16)
                kd_scr[c * BAND + WINDOW:(c + 1) * BAND, ls] = k_dup[kv][chunk_rows(c), :]
                vd_scr[c * BAND + WINDOW:(c + 1) * BAND, ls] = v_dup[kv][chunk_rows(c), :]
    else:
        for kv in range(SWA_KV_HEADS):
            ls = slice(kv * LANE, (kv + 1) * LANE)
            kd_scr[WINDOW:WINDOW + rows, ls] = k_dup[kv]
            vd_scr[WINDOW:WINDOW + rows, ls] = v_dup[kv]

    lane_q = lax.broadcasted_iota(jnp.int32, (1, SWA_GRP_Q), 1)
    key_ids = lax.broadcasted_iota(jnp.int32, (BAND, SWA_GRP_Q), 0)
    sink_vecs = []
    for kv in range(SWA_KV_HEADS):
        vec = jnp.full((1, SWA_GRP_Q), sinks_ref[kv * SWA_GROUP + SWA_GROUP - 1], F32)
        for g in range(SWA_GROUP - 2, -1, -1):
            vec = jnp.where(lane_q < (g + 1) * SWA_HD, sinks_ref[kv * SWA_GROUP + g], vec)
        sink_vecs.append(vec)

    for c in range(n_chunks):
        rs = chunk_rows(c)
        band = slice(c * BAND, (c + 1) * BAND) if sample else slice(c * CHUNK, c * CHUNK + BAND)
        for kv in range(SWA_KV_HEADS):
            ls = slice(kv * LANE, (kv + 1) * LANE)
            qg = (proj_scr[rs, C_QS + kv * SWA_GRP_Q:C_QS + (kv + 1) * SWA_GRP_Q]
                  * (SWA_HD ** -0.5)).astype(BF16)
            q_stack = jnp.concatenate(
                [jnp.where(low_half if hh == 0 else ~low_half, qg[:, pp * LANE:(pp + 1) * LANE], 0.0)
                 for pp in range(SWA_GROUP // 2) for hh in range(2)], axis=0)
            s_t = _dot_nt(kd_scr[band, ls], q_stack)
            if not sample and c * CHUNK < WINDOW:
                first_valid = WINDOW - (t * rows + c * CHUNK)
                s_t = jnp.where(key_ids >= first_valid, s_t, -jnp.inf)
            sink = sink_vecs[kv]
            m = jnp.maximum(jnp.max(s_t, axis=0, keepdims=True), sink)
            p_t = jnp.exp(s_t - m)
            den = jnp.sum(p_t, axis=0, keepdims=True) + jnp.exp(sink - m)
            pn_t = (p_t * (1.0 / den)).astype(BF16)
            o_t = _dot_tn(pn_t, vd_scr[band, ls])
            for pp in range(SWA_GROUP // 2):
                o_pair = jnp.where(low_half, o_t[(2 * pp) * CHUNK:(2 * pp + 1) * CHUNK, :],
                                   o_t[(2 * pp + 1) * CHUNK:(2 * pp + 2) * CHUNK, :])
                col = GLA_V + kv * SWA_GRP_Q + pp * LANE
                omix_scr[rs, col:col + LANE] = o_pair.astype(BF16)

    mix = _dot(omix_scr[...], w_out_ref[...])
    for c in range(n_chunks if sample else 1):
        rs = chunk_rows(c) if sample else slice(0, rows)
        xo_ref[rs, :] = x_ref[rs, :] + mod_row(c, 2) * mix[rs, :]

    if sample:
        ko_ref[...] = k_new
        vo_ref[...] = v_new
    else:
        kd_scr[0:WINDOW, :] = kd_scr[rows:rows + WINDOW, :]
        vd_scr[0:WINDOW, :] = vd_scr[rows:rows + WINDOW, :]

        @pl.when(t == n_t - 1)
        def _():
            so_ref[...] = state
            ko_ref[...] = k_new[rows - WINDOW:rows, :]
            vo_ref[...] = v_new[rows - WINDOW:rows, :]


def _const_spec(shape):
    zeros = (0,) * len(shape)
    return pl.BlockSpec(shape, lambda *_: zeros, pipeline_mode=pl.Buffered(1))


def _mixer_weight_specs():
    return [
        _const_spec((1, D_MODEL)),
        _const_spec((D_MODEL, D_IN_P)),
        _const_spec((LR_PAD, GLA_QK)),
        _const_spec((1, GLA_QK)),
        _const_spec((1, GLA_DV)),
        pl.BlockSpec(memory_space=pltpu.SMEM),
        _const_spec((D_MIX, D_MODEL)),
    ]


def _mixer_scratch(rows, n_chunks, band_rows):
    state_rows = GLA_HEADS * GLA_DK
    return [
        pltpu.VMEM((rows, D_MODEL), BF16),
        pltpu.VMEM((rows, D_IN_P), F32),
        pltpu.VMEM((rows, D_MIX), BF16),
        pltpu.VMEM((rows, GLA_QK), BF16),
        pltpu.VMEM((rows, GLA_QK), BF16),
        pltpu.VMEM((rows, GLA_QK), BF16),
        pltpu.VMEM((rows, GLA_QK), BF16),
        pltpu.VMEM((rows, GLA_V), BF16),
        pltpu.VMEM((n_chunks, state_rows, GLA_DV), F32),
        pltpu.VMEM((n_chunks, state_rows, GLA_DV), F32),
        pltpu.VMEM((n_chunks, state_rows, GLA_DV), BF16),
        pltpu.VMEM((band_rows, 2 * LANE), BF16),
        pltpu.VMEM((band_rows, 2 * LANE), BF16),
    ]


def _mixer_prompt_call(x, mod, weights):
    batch, seq, _ = x.shape
    tile = min(PROMPT_TILE, seq)
    assert seq % tile == 0 and tile % CHUNK == 0 and tile >= WINDOW
    n_chunks = tile // CHUNK
    n_t = seq // tile
    state_rows = GLA_HEADS * GLA_DK
    return pl.pallas_call(
        functools.partial(_mixer_kernel, False, n_chunks),
        grid=(batch, n_t),
        in_specs=[
            pl.BlockSpec((None, tile, D_MODEL), lambda b, t: (b, t, 0)),
            pl.BlockSpec((None, 6, D_MODEL), lambda b, t: (b, 0, 0)),
        ] + _mixer_weight_specs(),
        out_specs=[
            pl.BlockSpec((None, tile, D_MODEL), lambda b, t: (b, t, 0)),
            pl.BlockSpec((None, state_rows, GLA_DV), lambda b, t: (b, 0, 0)),
            pl.BlockSpec((None, WINDOW, SWA_KV), lambda b, t: (b, 0, 0)),
            pl.BlockSpec((None, WINDOW, SWA_KV), lambda b, t: (b, 0, 0)),
        ],
        out_shape=[
            jax.ShapeDtypeStruct((batch, seq, D_MODEL), F32),
            jax.ShapeDtypeStruct((batch, state_rows, GLA_DV), F32),
            jax.ShapeDtypeStruct((batch, WINDOW, SWA_KV), F32),
            jax.ShapeDtypeStruct((batch, WINDOW, SWA_KV), F32),
        ],
        scratch_shapes=_mixer_scratch(tile, n_chunks, WINDOW + tile) + [
            pltpu.VMEM((state_rows, GLA_DV), F32),
        ],
        compiler_params=pltpu.CompilerParams(
            dimension_semantics=("arbitrary", "arbitrary"), vmem_limit_bytes=VMEM_LIMIT),
        name="mixer_prompt",
    )(x, mod, *weights)


def _mixer_sample_call(x, mod, weights, s0, k_cache, v_cache):
    batch, seq, _ = x.shape
    assert seq == CHUNK
    rows = batch * seq
    state_rows = GLA_HEADS * GLA_DK

    def full(shape):
        zeros = (0,) * len(shape)
        return pl.BlockSpec(shape, lambda i: zeros)

    return pl.pallas_call(
        functools.partial(_mixer_kernel, True, batch),
        grid=(1,),
        in_specs=[full((rows, D_MODEL)), full((batch, 6, D_MODEL))] + _mixer_weight_specs() + [
            full((batch, state_rows, GLA_DV)),
            full((batch, WINDOW, SWA_KV)),
            full((batch, WINDOW, SWA_KV)),
        ],
        out_specs=[
            full((rows, D_MODEL)),
            full((batch, state_rows, GLA_DV)),
            full((rows, SWA_KV)),
            full((rows, SWA_KV)),
        ],
        out_shape=[
            jax.ShapeDtypeStruct((rows, D_MODEL), F32),
            jax.ShapeDtypeStruct((batch, state_rows, GLA_DV), F32),
            jax.ShapeDtypeStruct((rows, SWA_KV), F32),
            jax.ShapeDtypeStruct((rows, SWA_KV), F32),
        ],
        scratch_shapes=_mixer_scratch(rows, batch, batch * BAND),
        compiler_params=pltpu.CompilerParams(
            dimension_semantics=("arbitrary",), vmem_limit_bytes=VMEM_LIMIT),
        name="mixer_sample",
    )(x.reshape(rows, D_MODEL), mod, *weights, s0, k_cache, v_cache)


def _ffn_kernel(sample, final, n_seg, seg_len, *refs):
    if sample:
        (x_ref, mod_ref, g_ffn_ref, w_up_ref, conv_w_ref, conv_b_ref, w_down_ref, g_final_ref,
         past_ref, xo_ref, co_ref, h_scr, ub_scr, act_scr) = refs
    else:
        (x_ref, mod_ref, g_ffn_ref, w_up_ref, conv_w_ref, conv_b_ref, w_down_ref, g_final_ref,
         xo_ref, co_ref, h_scr, ub_scr, act_scr, past_scr) = refs
        t = pl.program_id(1)
        n_t = pl.num_programs(1)

        @pl.when(t == 0)
        def _():
            past_scr[...] = jnp.zeros_like(past_scr)

    rows = n_seg * seg_len
    stride = seg_len + SUBLANE
    g_ffn = g_ffn_ref[...]

    def mod_row(c, idx):
        if sample:
            return mod_ref[c, idx:idx + 1, :]
        return mod_ref[idx:idx + 1, :]

    for c in range(n_seg):
        rs = slice(c * seg_len, (c + 1) * seg_len)
        x = x_ref[rs, :]
        gain = g_ffn * (1.0 + mod_row(c, 4))
        h_scr[rs, :] = (x * _rms_scale(x) * gain + mod_row(c, 3)).astype(BF16)

    acc = None
    for part in range(FF_SPLIT):
        cols = slice(part * FF_TILE, (part + 1) * FF_TILE)
        h = h_scr[...]
        u = _dot(h, w_up_ref[:, part * FF_TILE:(part + 1) * FF_TILE])
        val = _dot(h, w_up_ref[:, D_FF + part * FF_TILE:D_FF + (part + 1) * FF_TILE])
        w0 = conv_w_ref[0:1, cols]
        w1 = conv_w_ref[1:2, cols]
        w2 = conv_w_ref[2:3, cols]
        cb = conv_b_ref[:, cols]
        for c in range(n_seg):
            base = c * stride
            rs = slice(c * seg_len, (c + 1) * seg_len)
            if sample:
                ub_scr[base + SUBLANE - 2:base + SUBLANE, :] = past_ref[c, :, cols]
            else:
                ub_scr[base + SUBLANE - 2:base + SUBLANE, :] = past_scr[:, cols]
            u_seg = u[rs, :]
            ub_scr[base + SUBLANE:base + SUBLANE + seg_len, :] = u_seg
            u1 = ub_scr[base + SUBLANE - 1:base + SUBLANE - 1 + seg_len, :]
            u2 = ub_scr[base + SUBLANE - 2:base + SUBLANE - 2 + seg_len, :]
            uc = w0 * u2 + w1 * u1 + w2 * u_seg + cb
            act_scr[rs, :] = (uc * _sigmoid(uc) * val[rs, :]).astype(BF16)
            tail = ub_scr[base + seg_len + SUBLANE - 2:base + seg_len + SUBLANE, :]
            if sample:
                co_ref[c, :, cols] = tail
            else:
                past_scr[:, cols] = tail
        d = _dot(act_scr[...], w_down_ref[part * FF_TILE:(part + 1) * FF_TILE, :])
        acc = d if acc is None else acc + d

    for c in range(n_seg):
        rs = slice(c * seg_len, (c + 1) * seg_len)
        y = x_ref[rs, :] + mod_row(c, 5) * acc[rs, :]
        if final:
            y = y * _rms_scale(y) * g_final_ref[...]
        xo_ref[rs, :] = y

    if not sample:
        @pl.when(t == n_t - 1)
        def _():
            co_ref[...] = past_scr[...]


def _ffn_weight_specs():
    return [
        _const_spec((1, D_MODEL)),
        _const_spec((D_MODEL, 2 * D_FF)),
        _const_spec((CONV_W, D_FF)),
        _const_spec((1, D_FF)),
        _const_spec((D_FF, D_MODEL)),
        _const_spec((1, D_MODEL)),
    ]


def _ffn_prompt_call(x, mod, weights, final):
    batch, seq, _ = x.shape
    tile = min(PROMPT_TILE, seq)
    n_t = seq // tile
    return pl.pallas_call(
        functools.partial(_ffn_kernel, False, final, 1, tile),
        grid=(batch, n_t),
        in_specs=[
            pl.BlockSpec((None, tile, D_MODEL), lambda b, t: (b, t, 0)),
            pl.BlockSpec((None, 6, D_MODEL), lambda b, t: (b, 0, 0)),
        ] + _ffn_weight_specs(),
        out_specs=[
            pl.BlockSpec((None, tile, D_MODEL), lambda b, t: (b, t, 0)),
            pl.BlockSpec((None, CONV_W - 1, D_FF), lambda b, t: (b, 0, 0)),
        ],
        out_shape=[
            jax.ShapeDtypeStruct((batch, seq, D_MODEL), F32),
            jax.ShapeDtypeStruct((batch, CONV_W - 1, D_FF), F32),
        ],
        scratch_shapes=[
            pltpu.VMEM((tile, D_MODEL), BF16),
            pltpu.VMEM((tile + SUBLANE, FF_TILE), F32),
            pltpu.VMEM((tile, FF_TILE), BF16),
            pltpu.VMEM((CONV_W - 1, D_FF), F32),
        ],
        compiler_params=pltpu.CompilerParams(
            dimension_semantics=("arbitrary", "arbitrary"), vmem_limit_bytes=VMEM_LIMIT),
        name="ffn_prompt",
    )(x, mod, *weights)


def _ffn_sample_call(x2d, mod, weights, past, final, batch, seq):
    rows = batch * seq

    def full(shape):
        zeros = (0,) * len(shape)
        return pl.BlockSpec(shape, lambda i: zeros)

    return pl.pallas_call(
        functools.partial(_ffn_kernel, True, final, batch, seq),
        grid=(1,),
        in_specs=[full((rows, D_MODEL)), full((batch, 6, D_MODEL))] + _ffn_weight_specs() + [
            full((batch, CONV_W - 1, D_FF)),
        ],
        out_specs=[full((rows, D_MODEL)), full((batch, CONV_W - 1, D_FF))],
        out_shape=[
            jax.ShapeDtypeStruct((rows, D_MODEL), F32),
            jax.ShapeDtypeStruct((batch, CONV_W - 1, D_FF), F32),
        ],
        scratch_shapes=[
            pltpu.VMEM((rows, D_MODEL), BF16),
            pltpu.VMEM((batch * (seq + SUBLANE), FF_TILE), F32),
            pltpu.VMEM((rows, FF_TILE), BF16),
        ],
        compiler_params=pltpu.CompilerParams(
            dimension_semantics=("arbitrary",), vmem_limit_bytes=VMEM_LIMIT),
        name="ffn_sample",
    )(x2d, mod, *weights, past)


def _reorder_w_in(w):
    lr0 = 2 * GLA_QK + 2 * GLA_V
    pad = jnp.zeros((D_MODEL, LR_PAD - GLA_RANK), w.dtype)
    return jnp.concatenate([w[:, :lr0], w[:, lr0 + GLA_RANK:], w[:, lr0:lr0 + GLA_RANK], pad], axis=1)


def kernel(x_prompt, x_sample, state_gla, cache_swa_k, cache_swa_v, state_conv, c_prompt, c_sample,
           w_ada, b_ada, g_attn, g_ffn, w_in, w_gk2, b_gk, g_gla, sinks, w_out, w_up, conv_w, conv_b,
           w_down, g_final):
    depth = w_ada.shape[0]
    batch, seq, _ = x_prompt.shape
    dec_batch, dec_seq, _ = x_sample.shape
    state_rows = GLA_HEADS * GLA_DK

    c_all = jnp.concatenate(
        [c_prompt, c_sample, jnp.zeros((ADA_ROWS - batch - dec_batch, D_MODEL), F32)], axis=0)
    mod_all = _ada_call(c_all, w_ada, b_ada).reshape(depth, ADA_ROWS, 6, D_MODEL)

    yp = x_prompt
    ys = x_sample.reshape(dec_batch * dec_seq, D_MODEL)
    outs = [[] for _ in range(8)]
    for i in range(depth):
        mod_p = mod_all[i, :batch]
        mod_s = mod_all[i, batch:batch + dec_batch]
        w_gk2_p = jnp.concatenate(
            [w_gk2[i], jnp.zeros((LR_PAD - GLA_RANK, GLA_QK), F32)], axis=0).astype(BF16)
        mixer_w = (g_attn[i][None], _reorder_w_in(w_in[i]).astype(BF16), w_gk2_p, b_gk[i][None],
                   g_gla[i][None], sinks[i], w_out[i].astype(BF16))
        ffn_w = (g_ffn[i][None], w_up[i].astype(BF16), conv_w[i], conv_b[i][None],
                 w_down[i].astype(BF16), g_final[None])
        final = i == depth - 1

        yp, s_p, k_p, v_p = _mixer_prompt_call(yp, mod_p, mixer_w)
        yp, conv_p = _ffn_prompt_call(yp, mod_p, ffn_w, final)

        ys, s_s, k_s, v_s = _mixer_sample_call(
            ys.reshape(dec_batch, dec_seq, D_MODEL), mod_s, mixer_w,
            state_gla[i].reshape(dec_batch, state_rows, GLA_DV),
            cache_swa_k[i].reshape(dec_batch, WINDOW, SWA_KV),
            cache_swa_v[i].reshape(dec_batch, WINDOW, SWA_KV))
        ys, conv_s = _ffn_sample_call(ys, mod_s, ffn_w, state_conv[i], final, dec_batch, dec_seq)

        keep = min(WINDOW, seq)
        outs[0].append(s_p.reshape(batch, GLA_HEADS, GLA_DK, GLA_DV))
        outs[1].append(k_p.reshape(batch, keep, SWA_KV_HEADS, SWA_HD))
        outs[2].append(v_p.reshape(batch, keep, SWA_KV_HEADS, SWA_HD))
        outs[3].append(conv_p)
        outs[4].append(s_s.reshape(dec_batch, GLA_HEADS, GLA_DK, GLA_DV))
        outs[5].append(k_s.reshape(dec_batch, dec_seq, SWA_KV_HEADS, SWA_HD))
        outs[6].append(v_s.reshape(dec_batch, dec_seq, SWA_KV_HEADS, SWA_HD))
        outs[7].append(conv_s)

    return (yp, ys.reshape(dec_batch, dec_seq, D_MODEL)) + tuple(jnp.stack(o) for o in outs)
```

```python
import functools

import jax
import jax.numpy as jnp
from jax import lax
from jax.experimental import pallas as pl
from jax.experimental.pallas import tpu as pltpu

F32 = jnp.float32
BF16 = jnp.bfloat16

D_MODEL = 1024
CHUNK = 64
GLA_HEADS = 4
GLA_DK = 64
GLA_DV = 128
GLA_RANK = 16
GLA_NORMALIZER = 16.0
SWA_Q_HEADS = 8
SWA_KV_HEADS = 2
SWA_GROUP = SWA_Q_HEADS // SWA_KV_HEADS
SWA_HD = 64
WINDOW = 128
D_FF = 2816
CONV_W = 3
RMS_EPS = 1e-6

GLA_QK = GLA_HEADS * GLA_DK
GLA_V = GLA_HEADS * GLA_DV
SWA_Q = SWA_Q_HEADS * SWA_HD
SWA_KV = SWA_KV_HEADS * SWA_HD
D_MIX = GLA_V + SWA_Q
BAND = WINDOW + CHUNK

LANE = 128
SUBLANE = 8
BF16_ROWS = 16

C_QG = 0
C_KG = C_QG + GLA_QK
C_VG = C_KG + GLA_QK
C_OG = C_VG + GLA_V
C_QS = C_OG + GLA_V
C_KS = C_QS + SWA_Q
C_VS = C_KS + SWA_KV
C_LR = C_VS + SWA_KV
LR_PAD = LANE
D_IN_P = C_LR + LR_PAD
D_IN = 2 * GLA_QK + 2 * GLA_V + GLA_RANK + SWA_Q + 2 * SWA_KV

assert GLA_DK * 2 == LANE and SWA_HD * 2 == LANE and GLA_DV == LANE and SWA_KV == LANE
GLA_PAIRS = GLA_HEADS // 2
PAIR_V = 2 * GLA_DV
SWA_GRP_Q = SWA_GROUP * SWA_HD

ADA_TILE = 1536
ADA_ROWS = 16
PROMPT_TILE = 512
SUB_CHUNKS = 4
DENSE_PIECE = 256
FF_SPLIT = 2
FF_TILE = D_FF // FF_SPLIT
VMEM_LIMIT = 56 * 1024 * 1024

NT_DIMS = (((1,), (1,)), ((), ()))
TN_DIMS = (((0,), (0,)), ((), ()))


def _dot(a, b):
    return jnp.dot(a, b, preferred_element_type=F32)


def _dot_nt(a, b):
    return lax.dot_general(a, b, NT_DIMS, preferred_element_type=F32)


def _dot_tn(a, b):
    return lax.dot_general(a, b, TN_DIMS, preferred_element_type=F32)


def _split_bf16(a):
    hi = a.astype(BF16)
    lo = (a - hi.astype(F32)).astype(BF16)
    return hi, lo


def _sigmoid(a):
    return 1.0 / (1.0 + jnp.exp(-a))


def _log_sigmoid(a):
    return jnp.minimum(a, 0.0) - jnp.log(1.0 + jnp.exp(-jnp.abs(a)))


def _rms_scale(a):
    return lax.rsqrt(jnp.mean(a * a, axis=-1, keepdims=True) + RMS_EPS)


def _ada_kernel(c_ref, w_ref, b_ref, o_ref):
    c = c_ref[...]
    a = (c * _sigmoid(c)).astype(BF16)
    o_ref[...] = _dot(a, w_ref[...].astype(BF16)) + b_ref[...]


def _ada_call(c_all, w_ada, b_ada):
    depth = w_ada.shape[0]
    n_tiles = (6 * D_MODEL) // ADA_TILE
    return pl.pallas_call(
        _ada_kernel,
        grid=(depth, n_tiles),
        in_specs=[
            pl.BlockSpec((ADA_ROWS, D_MODEL), lambda l, j: (0, 0)),
            pl.BlockSpec((None, D_MODEL, ADA_TILE), lambda l, j: (l, 0, j)),
            pl.BlockSpec((None, 1, ADA_TILE), lambda l, j: (l, 0, j)),
        ],
        out_specs=pl.BlockSpec((None, ADA_ROWS, ADA_TILE), lambda l, j: (l, 0, j)),
        out_shape=jax.ShapeDtypeStruct((depth, ADA_ROWS, 6 * D_MODEL), F32),
        compiler_params=pltpu.CompilerParams(
            dimension_semantics=("arbitrary", "arbitrary"), vmem_limit_bytes=VMEM_LIMIT),
        name="adaln_mod",
    )(c_all, w_ada, b_ada.reshape(depth, 1, 6 * D_MODEL))


def _dup_halves(a, low_half):
    swapped = pltpu.roll(a, LANE // 2, axis=1)
    return jnp.where(low_half, a, swapped), jnp.where(low_half, swapped, a)


def _mixer_kernel(sample, n_chunks, *refs):
    if sample:
        (x_ref, mod_ref, g_attn_ref, w_in_ref, w_gk2_ref, b_gk_ref, g_gla_ref, sinks_ref, w_out_ref,
         s0_ref, kc_ref, vc_ref,
         xo_ref, so_ref, ko_ref, vo_ref,
         h_scr, proj_scr, omix_scr, qm_scr, km_scr, qi_scr, ko_scr, vb_scr, gk_scr, dec_scr, u_scr, sb_scr,
         tail_scr, att_scr, pn_scr,
         kd_scr, vd_scr) = refs
        t = None
    else:
        (x_ref, mod_ref, g_attn_ref, w_in_ref, w_gk2_ref, b_gk_ref, g_gla_ref, sinks_ref, w_out_ref,
         xo_ref, so_ref, ko_ref, vo_ref,
         h_scr, proj_scr, omix_scr, qm_scr, km_scr, qi_scr, ko_scr, vb_scr, gk_scr, dec_scr, u_scr, sb_scr,
         tail_scr, att_scr, pn_scr,
         kd_scr, vd_scr, s_scr) = refs
        t = pl.program_id(1)
        n_t = pl.num_programs(1)

        @pl.when(t == 0)
        def _():
            s_scr[...] = jnp.zeros_like(s_scr)
            kd_scr[0:WINDOW, :] = jnp.zeros((WINDOW, 2 * LANE), BF16)
            vd_scr[0:WINDOW, :] = jnp.zeros((WINDOW, 2 * LANE), BF16)

    rows = n_chunks * CHUNK
    g_attn = g_attn_ref[...]

    def mod_row(c, idx):
        if sample:
            return mod_ref[c, idx:idx + 1, :]
        return mod_ref[idx:idx + 1, :]

    def chunk_rows(c):
        return slice(c * CHUNK, (c + 1) * CHUNK)

    sub_chunks = min(SUB_CHUNKS, n_chunks)
    n_sub = n_chunks // sub_chunks
    sub_rows = sub_chunks * CHUNK

    def sub_slice(s):
        return slice(s * sub_rows, (s + 1) * sub_rows)

    def norm_groups(s):
        if sample:
            return [(c, chunk_rows(c)) for c in range(s * sub_chunks, (s + 1) * sub_chunks)]
        return [(0, sub_slice(s))]

    r64 = lax.broadcasted_iota(jnp.int32, (CHUNK, CHUNK), 0)
    c64 = lax.broadcasted_iota(jnp.int32, (CHUNK, CHUNK), 1)
    tri = jnp.where(r64 >= c64, 1.0, 0.0).astype(BF16)
    r_pair = lax.broadcasted_iota(jnp.int32, (CHUNK, LANE), 0)
    l_pair = lax.broadcasted_iota(jnp.int32, (CHUNK, LANE), 1)
    causal_pair = r_pair >= (l_pair & (CHUNK - 1))
    low_half = l_pair < LANE // 2
    sel_last = jnp.where(
        lax.broadcasted_iota(jnp.int32, (BF16_ROWS, LANE), 0) == BF16_ROWS - 1, 1.0, 0.0).astype(BF16)
    diag128 = ((lax.broadcasted_iota(jnp.int32, (LANE, LANE), 0) >= LANE // 2)
               == (lax.broadcasted_iota(jnp.int32, (LANE, LANE), 1) >= LANE // 2))
    diag_v = ((lax.broadcasted_iota(jnp.int32, (LANE, PAIR_V), 0) >= LANE // 2)
              == (lax.broadcasted_iota(jnp.int32, (LANE, PAIR_V), 1) >= GLA_DV))
    g_gla = g_gla_ref[...]
    low_sub = lax.broadcasted_iota(jnp.int32, (sub_rows, LANE), 1) < LANE // 2

    def project_steps(s):
        ss = sub_slice(s)

        def norm():
            for mi, rs in norm_groups(s):
                x = x_ref[rs, :]
                gain = g_attn * (1.0 + mod_row(mi, 1))
                h_scr[rs, :] = (x * _rms_scale(x) * gain + mod_row(mi, 0)).astype(BF16)

        def piece(lo, hi):
            proj_scr[ss, lo:hi] = _dot(h_scr[ss, :], w_in_ref[:, lo:hi])

        def gates():
            lr = proj_scr[ss, C_LR:C_LR + LR_PAD].astype(BF16)
            gk_scr[ss, :] = (_log_sigmoid(_dot(lr, w_gk2_ref[...]) + b_gk_ref[...])
                             * (1.0 / GLA_NORMALIZER))
            vb_scr[ss, :] = proj_scr[ss, C_VG:C_VG + GLA_V].astype(BF16)

        def bands():
            k_dup = _dup_halves(proj_scr[ss, C_KS:C_KS + SWA_KV], low_sub)
            v_dup = _dup_halves(proj_scr[ss, C_VS:C_VS + SWA_KV], low_sub)
            for kv in range(SWA_KV_HEADS):
                ls = slice(kv * LANE, (kv + 1) * LANE)
                if sample:
                    for c in range(s * sub_chunks, (s + 1) * sub_chunks):
                        local = slice((c - s * sub_chunks) * CHUNK, (c - s * sub_chunks + 1) * CHUNK)
                        kd_scr[c * BAND + WINDOW:(c + 1) * BAND, ls] = k_dup[kv][local, :].astype(BF16)
                        vd_scr[c * BAND + WINDOW:(c + 1) * BAND, ls] = v_dup[kv][local, :].astype(BF16)
                else:
                    band_rows = slice(WINDOW + s * sub_rows, WINDOW + (s + 1) * sub_rows)
                    kd_scr[band_rows, ls] = k_dup[kv].astype(BF16)
                    vd_scr[band_rows, ls] = v_dup[kv].astype(BF16)

        steps = [norm]
        for lo in range(0, D_IN_P, DENSE_PIECE):
            steps.append(functools.partial(piece, lo, min(lo + DENSE_PIECE, D_IN_P)))
        return steps + [gates, bands]

    def gla_prepare(c):
        rs = chunk_rows(c)
        gk_hi, gk_lo = _split_bf16(gk_scr[rs, :])
        cum = _dot(tri, gk_hi) + _dot(tri, gk_lo)
        mid = cum[CHUNK // 2:CHUNK // 2 + 1, :]
        last = cum[CHUNK - 1:CHUNK, :]
        q = proj_scr[rs, C_QG:C_QG + GLA_QK] * (GLA_DK ** -0.5)
        k = proj_scr[rs, C_KG:C_KG + GLA_QK]
        qm_scr[rs, :] = (q * jnp.exp(cum - mid)).astype(BF16)
        km_scr[rs, :] = (k * jnp.exp(mid - cum)).astype(BF16)
        qi_scr[rs, :] = (q * jnp.exp(cum)).astype(BF16)
        ko_scr[rs, :] = (k * jnp.exp(last - cum)).astype(BF16)
        tail_scr[c] = cum[CHUNK - BF16_ROWS:CHUNK, :]

    def gla_decay(c):
        tail_hi, tail_lo = _split_bf16(tail_scr[c])
        dec_scr[c] = jnp.exp(_dot_tn(tail_hi, sel_last) + _dot_tn(tail_lo, sel_last))

    def gla_increment(c):
        rs = chunk_rows(c)
        for p in range(GLA_PAIRS):
            upd = _dot_tn(ko_scr[rs, p * LANE:(p + 1) * LANE], vb_scr[rs, p * PAIR_V:(p + 1) * PAIR_V])
            u_scr[c, p * LANE:p * LANE + GLA_DK, :] = upd[0:GLA_DK, 0:GLA_DV]
            u_scr[c, p * LANE + GLA_DK:(p + 1) * LANE, :] = upd[GLA_DK:LANE, GLA_DV:PAIR_V]

    def gla_recurrence(s):
        state = None if sample else s_scr[...]
        for c in range(s * sub_chunks, (s + 1) * sub_chunks):
            if sample:
                state = s0_ref[c]
            sb_scr[c] = state.astype(BF16)
            state = dec_scr[c] * state + u_scr[c]
            if sample:
                so_ref[c] = state
        if not sample:
            s_scr[...] = state

    def gla_scores(c, p):
        rs = chunk_rows(c)
        ls = slice(p * LANE, (p + 1) * LANE)
        km = km_scr[rs, ls]
        k_bd = jnp.where(diag128, jnp.concatenate([km, km], axis=0), 0.0)
        att_scr[rs, ls] = jnp.where(causal_pair, _dot_nt(qm_scr[rs, ls], k_bd), 0.0).astype(BF16)

    def gla_output(c, p):
        rs = chunk_rows(c)
        ls = slice(p * LANE, (p + 1) * LANE)
        v_pair = vb_scr[rs, p * PAIR_V:(p + 1) * PAIR_V]
        s_pair = sb_scr[c, ls, :]
        w_top = jnp.where(diag_v, jnp.concatenate([v_pair, v_pair], axis=0), 0.0)
        w_bot = jnp.where(diag_v, jnp.concatenate([s_pair, s_pair], axis=1), 0.0)
        o_pair = _dot(jnp.concatenate([att_scr[rs, ls], qi_scr[rs, ls]], axis=1),
                      jnp.concatenate([w_top, w_bot], axis=0))
        for hh in range(2):
            h = 2 * p + hh
            o = o_pair[:, hh * GLA_DV:(hh + 1) * GLA_DV]
            og = proj_scr[rs, C_OG + h * GLA_DV:C_OG + (h + 1) * GLA_DV]
            y = o * _rms_scale(o) * g_gla * (og * _sigmoid(og))
            omix_scr[rs, h * GLA_DV:(h + 1) * GLA_DV] = y.astype(BF16)

    if sample:
        low_win = lax.broadcasted_iota(jnp.int32, (WINDOW, LANE), 1) < LANE // 2
        for c in range(n_chunks):
            kc_dup = _dup_halves(kc_ref[c], low_win)
            vc_dup = _dup_halves(vc_ref[c], low_win)
            for kv in range(SWA_KV_HEADS):
                ls = slice(kv * LANE, (kv + 1) * LANE)
                kd_scr[c * BAND:c * BAND + WINDOW, ls] = kc_dup[kv].astype(BF16)
                vd_scr[c * BAND:c * BAND + WINDOW, ls] = vc_dup[kv].astype(BF16)

    lane_q = lax.broadcasted_iota(jnp.int32, (1, SWA_GRP_Q), 1)
    key_ids = lax.broadcasted_iota(jnp.int32, (BAND, SWA_GRP_Q), 0)
    sink_vecs = []
    for kv in range(SWA_KV_HEADS):
        vec = jnp.full((1, SWA_GRP_Q), sinks_ref[kv * SWA_GROUP + SWA_GROUP - 1], F32)
        for g in range(SWA_GROUP - 2, -1, -1):
            vec = jnp.where(lane_q < (g + 1) * SWA_HD, sinks_ref[kv * SWA_GROUP + g], vec)
        sink_vecs.append(vec)

    def band_rows(c):
        return slice(c * BAND, (c + 1) * BAND) if sample else slice(c * CHUNK, c * CHUNK + BAND)

    def swa_scores(c, kv):
        rs = chunk_rows(c)
        band = band_rows(c)
        ls = slice(kv * LANE, (kv + 1) * LANE)
        qg = (proj_scr[rs, C_QS + kv * SWA_GRP_Q:C_QS + (kv + 1) * SWA_GRP_Q]
              * (SWA_HD ** -0.5)).astype(BF16)
        q_stack = jnp.concatenate(
            [jnp.where(low_half if hh == 0 else ~low_half, qg[:, pp * LANE:(pp + 1) * LANE], 0.0)
             for pp in range(SWA_GROUP // 2) for hh in range(2)], axis=0)
        s_t = _dot_nt(kd_scr[band, ls], q_stack)
        if not sample and c * CHUNK < WINDOW:
            first_valid = WINDOW - (t * rows + c * CHUNK)
            s_t = jnp.where(key_ids >= first_valid, s_t, -jnp.inf)
        sink = sink_vecs[kv]
        m = jnp.maximum(jnp.max(s_t, axis=0, keepdims=True), sink)
        p_t = jnp.exp(s_t - m)
        den = jnp.sum(p_t, axis=0, keepdims=True) + jnp.exp(sink - m)
        pn_scr[c * SWA_KV_HEADS + kv] = (p_t * (1.0 / den)).astype(BF16)

    def swa_output(c, kv):
        rs = chunk_rows(c)
        ls = slice(kv * LANE, (kv + 1) * LANE)
        o_t = _dot_tn(pn_scr[c * SWA_KV_HEADS + kv], vd_scr[band_rows(c), ls])
        for pp in range(SWA_GROUP // 2):
            o_pair = jnp.where(low_half, o_t[(2 * pp) * CHUNK:(2 * pp + 1) * CHUNK, :],
                               o_t[(2 * pp + 1) * CHUNK:(2 * pp + 2) * CHUNK, :])
            col = GLA_V + kv * SWA_GRP_Q + pp * LANE
            omix_scr[rs, col:col + LANE] = o_pair.astype(BF16)

    def out_steps(s):
        ss = sub_slice(s)
        base = s * sub_rows

        def piece(lo, hi):
            mix = _dot(omix_scr[ss, :], w_out_ref[:, lo:hi])
            for mi, rs in norm_groups(s):
                local = slice(rs.start - base, rs.stop - base)
                xo_ref[rs, lo:hi] = x_ref[rs, lo:hi] + mod_row(mi, 2)[:, lo:hi] * mix[local, :]

        return [functools.partial(piece, lo, lo + DENSE_PIECE) for lo in range(0, D_MODEL, DENSE_PIECE)]

    def block_steps(s):
        blocks = range(s * sub_chunks, (s + 1) * sub_chunks)
        pairs = [(c, p) for c in blocks for p in range(GLA_PAIRS)]
        groups = [(c, kv) for c in blocks for kv in range(SWA_KV_HEADS)]
        steps = [functools.partial(gla_prepare, c) for c in blocks]
        steps += [functools.partial(swa_scores, c, kv) for c, kv in groups]
        steps += [functools.partial(gla_decay, c) for c in blocks]
        steps += [functools.partial(gla_increment, c) for c in blocks]
        steps += [functools.partial(gla_scores, c, p) for c, p in pairs]
        steps.append(functools.partial(gla_recurrence, s))
        steps += [functools.partial(swa_output, c, kv) for c, kv in groups]
        steps += [functools.partial(gla_output, c, p) for c, p in pairs]
        return steps

    for step in project_steps(0):
        step()
    for s in range(n_sub):
        dense = (project_steps(s + 1) if s + 1 < n_sub else []) + (out_steps(s - 1) if s > 0 else [])
        work = block_steps(s)
        issued = 0
        for i, step in enumerate(work):
            while issued < len(dense) and issued * len(work) < (i + 1) * len(dense):
                dense[issued]()
                issued += 1
            step()
        for step in dense[issued:]:
            step()
    for step in out_steps(n_sub - 1):
        step()

    if sample:
        ko_ref[...] = proj_scr[:, C_KS:C_KS + SWA_KV]
        vo_ref[...] = proj_scr[:, C_VS:C_VS + SWA_KV]
    else:
        kd_scr[0:WINDOW, :] = kd_scr[rows:rows + WINDOW, :]
        vd_scr[0:WINDOW, :] = vd_scr[rows:rows + WINDOW, :]

        @pl.when(t == n_t - 1)
        def _():
            so_ref[...] = s_scr[...]
            ko_ref[...] = proj_scr[rows - WINDOW:rows, C_KS:C_KS + SWA_KV]
            vo_ref[...] = proj_scr[rows - WINDOW:rows, C_VS:C_VS + SWA_KV]


def _const_spec(shape):
    zeros = (0,) * len(shape)
    return pl.BlockSpec(shape, lambda *_: zeros, pipeline_mode=pl.Buffered(1))


def _mixer_weight_specs():
    return [
        _const_spec((1, D_MODEL)),
        _const_spec((D_MODEL, D_IN_P)),
        _const_spec((LR_PAD, GLA_QK)),
        _const_spec((1, GLA_QK)),
        _const_spec((1, GLA_DV)),
        pl.BlockSpec(memory_space=pltpu.SMEM),
        _const_spec((D_MIX, D_MODEL)),
    ]


def _mixer_scratch(rows, n_chunks, band_rows):
    state_rows = GLA_HEADS * GLA_DK
    return [
        pltpu.VMEM((rows, D_MODEL), BF16),
        pltpu.VMEM((rows, D_IN_P), F32),
        pltpu.VMEM((rows, D_MIX), BF16),
        pltpu.VMEM((rows, GLA_QK), BF16),
        pltpu.VMEM((rows, GLA_QK), BF16),
        pltpu.VMEM((rows, GLA_QK), BF16),
        pltpu.VMEM((rows, GLA_QK), BF16),
        pltpu.VMEM((rows, GLA_V), BF16),
        pltpu.VMEM((rows, GLA_QK), F32),
        pltpu.VMEM((n_chunks, state_rows, GLA_DV), F32),
        pltpu.VMEM((n_chunks, state_rows, GLA_DV), F32),
        pltpu.VMEM((n_chunks, state_rows, GLA_DV), BF16),
        pltpu.VMEM((n_chunks, BF16_ROWS, GLA_QK), F32),
        pltpu.VMEM((rows, GLA_QK), BF16),
        pltpu.VMEM((n_chunks * SWA_KV_HEADS, BAND, SWA_GRP_Q), BF16),
        pltpu.VMEM((band_rows, 2 * LANE), BF16),
        pltpu.VMEM((band_rows, 2 * LANE), BF16),
    ]


def _mixer_prompt_call(x, mod, weights):
    batch, seq, _ = x.shape
    tile = min(PROMPT_TILE, seq)
    assert seq % tile == 0 and tile % CHUNK == 0 and tile >= WINDOW
    n_chunks = tile // CHUNK
    n_t = seq // tile
    state_rows = GLA_HEADS * GLA_DK
    return pl.pallas_call(
        functools.partial(_mixer_kernel, False, n_chunks),
        grid=(batch, n_t),
        in_specs=[
            pl.BlockSpec((None, tile, D_MODEL), lambda b, t: (b, t, 0)),
            pl.BlockSpec((None, 6, D_MODEL), lambda b, t: (b, 0, 0)),
        ] + _mixer_weight_specs(),
        out_specs=[
            pl.BlockSpec((None, tile, D_MODEL), lambda b, t: (b, t, 0)),
            pl.BlockSpec((None, state_rows, GLA_DV), lambda b, t: (b, 0, 0)),
            pl.BlockSpec((None, WINDOW, SWA_KV), lambda b, t: (b, 0, 0)),
            pl.BlockSpec((None, WINDOW, SWA_KV), lambda b, t: (b, 0, 0)),
        ],
        out_shape=[
            jax.ShapeDtypeStruct((batch, seq, D_MODEL), F32),
            jax.ShapeDtypeStruct((batch, state_rows, GLA_DV), F32),
            jax.ShapeDtypeStruct((batch, WINDOW, SWA_KV), F32),
            jax.ShapeDtypeStruct((batch, WINDOW, SWA_KV), F32),
        ],
        scratch_shapes=_mixer_scratch(tile, n_chunks, WINDOW + tile) + [
            pltpu.VMEM((state_rows, GLA_DV), F32),
        ],
        compiler_params=pltpu.CompilerParams(
            dimension_semantics=("arbitrary", "arbitrary"), vmem_limit_bytes=VMEM_LIMIT),
        name="mixer_prompt",
    )(x, mod, *weights)


def _mixer_sample_call(x, mod, weights, s0, k_cache, v_cache):
    batch, seq, _ = x.shape
    assert seq == CHUNK
    rows = batch * seq
    state_rows = GLA_HEADS * GLA_DK

    def full(shape):
        zeros = (0,) * len(shape)
        return pl.BlockSpec(shape, lambda i: zeros)

    return pl.pallas_call(
        functools.partial(_mixer_kernel, True, batch),
        grid=(1,),
        in_specs=[full((rows, D_MODEL)), full((batch, 6, D_MODEL))] + _mixer_weight_specs() + [
            full((batch, state_rows, GLA_DV)),
            full((batch, WINDOW, SWA_KV)),
            full((batch, WINDOW, SWA_KV)),
        ],
        out_specs=[
            full((rows, D_MODEL)),
            full((batch, state_rows, GLA_DV)),
            full((rows, SWA_KV)),
            full((rows, SWA_KV)),
        ],
        out_shape=[
            jax.ShapeDtypeStruct((rows, D_MODEL), F32),
            jax.ShapeDtypeStruct((batch, state_rows, GLA_DV), F32),
            jax.ShapeDtypeStruct((rows, SWA_KV), F32),
            jax.ShapeDtypeStruct((rows, SWA_KV), F32),
        ],
        scratch_shapes=_mixer_scratch(rows, batch, batch * BAND),
        compiler_params=pltpu.CompilerParams(
            dimension_semantics=("arbitrary",), vmem_limit_bytes=VMEM_LIMIT),
        name="mixer_sample",
    )(x.reshape(rows, D_MODEL), mod, *weights, s0, k_cache, v_cache)


def _ffn_kernel(sample, final, n_seg, seg_len, *refs):
    if sample:
        (x_ref, mod_ref, g_ffn_ref, w_up_ref, conv_w_ref, conv_b_ref, w_down_ref, g_final_ref,
         past_ref, xo_ref, co_ref, h_scr, ub_scr, act_scr) = refs
    else:
        (x_ref, mod_ref, g_ffn_ref, w_up_ref, conv_w_ref, conv_b_ref, w_down_ref, g_final_ref,
         xo_ref, co_ref, h_scr, ub_scr, act_scr, past_scr) = refs
        t = pl.program_id(1)
        n_t = pl.num_programs(1)

        @pl.when(t == 0)
        def _():
            past_scr[...] = jnp.zeros_like(past_scr)

    rows = n_seg * seg_len
    stride = seg_len + SUBLANE
    g_ffn = g_ffn_ref[...]

    def mod_row(c, idx):
        if sample:
            return mod_ref[c, idx:idx + 1, :]
        return mod_ref[idx:idx + 1, :]

    for c in range(n_seg):
        rs = slice(c * seg_len, (c + 1) * seg_len)
        x = x_ref[rs, :]
        gain = g_ffn * (1.0 + mod_row(c, 4))
        h_scr[rs, :] = (x * _rms_scale(x) * gain + mod_row(c, 3)).astype(BF16)

    acc = None
    for part in range(FF_SPLIT):
        cols = slice(part * FF_TILE, (part + 1) * FF_TILE)
        h = h_scr[...]
        u = _dot(h, w_up_ref[:, part * FF_TILE:(part + 1) * FF_TILE])
        val = _dot(h, w_up_ref[:, D_FF + part * FF_TILE:D_FF + (part + 1) * FF_TILE])
        w0 = conv_w_ref[0:1, cols]
        w1 = conv_w_ref[1:2, cols]
        w2 = conv_w_ref[2:3, cols]
        cb = conv_b_ref[:, cols]
        for c in range(n_seg):
            base = c * stride
            rs = slice(c * seg_len, (c + 1) * seg_len)
            if sample:
                ub_scr[base + SUBLANE - 2:base + SUBLANE, :] = past_ref[c, :, cols]
            else:
                ub_scr[base + SUBLANE - 2:base + SUBLANE, :] = past_scr[:, cols]
            u_seg = u[rs, :]
            ub_scr[base + SUBLANE:base + SUBLANE + seg_len, :] = u_seg
            u1 = ub_scr[base + SUBLANE - 1:base + SUBLANE - 1 + seg_len, :]
            u2 = ub_scr[base + SUBLANE - 2:base + SUBLANE - 2 + seg_len, :]
            uc = w0 * u2 + w1 * u1 + w2 * u_seg + cb
            act_scr[rs, :] = (uc * _sigmoid(uc) * val[rs, :]).astype(BF16)
            tail = ub_scr[base + seg_len + SUBLANE - 2:base + seg_len + SUBLANE, :]
            if sample:
                co_ref[c, :, cols] = tail
            else:
                past_scr[:, cols] = tail
        d = _dot(act_scr[...], w_down_ref[part * FF_TILE:(part + 1) * FF_TILE, :])
        acc = d if acc is None else acc + d

    for c in range(n_seg):
        rs = slice(c * seg_len, (c + 1) * seg_len)
        y = x_ref[rs, :] + mod_row(c, 5) * acc[rs, :]
        if final:
            y = y * _rms_scale(y) * g_final_ref[...]
        xo_ref[rs, :] = y

    if not sample:
        @pl.when(t == n_t - 1)
        def _():
            co_ref[...] = past_scr[...]


def _ffn_weight_specs():
    return [
        _const_spec((1, D_MODEL)),
        _const_spec((D_MODEL, 2 * D_FF)),
        _const_spec((CONV_W, D_FF)),
        _const_spec((1, D_FF)),
        _const_spec((D_FF, D_MODEL)),
        _const_spec((1, D_MODEL)),
    ]


def _ffn_prompt_call(x, mod, weights, final):
    batch, seq, _ = x.shape
    tile = min(PROMPT_TILE, seq)
    n_t = seq // tile
    return pl.pallas_call(
        functools.partial(_ffn_kernel, False, final, 1, tile),
        grid=(batch, n_t),
        in_specs=[
            pl.BlockSpec((None, tile, D_MODEL), lambda b, t: (b, t, 0)),
            pl.BlockSpec((None, 6, D_MODEL), lambda b, t: (b, 0, 0)),
        ] + _ffn_weight_specs(),
        out_specs=[
            pl.BlockSpec((None, tile, D_MODEL), lambda b, t: (b, t, 0)),
            pl.BlockSpec((None, CONV_W - 1, D_FF), lambda b, t: (b, 0, 0)),
        ],
        out_shape=[
            jax.ShapeDtypeStruct((batch, seq, D_MODEL), F32),
            jax.ShapeDtypeStruct((batch, CONV_W - 1, D_FF), F32),
        ],
        scratch_shapes=[
            pltpu.VMEM((tile, D_MODEL), BF16),
            pltpu.VMEM((tile + SUBLANE, FF_TILE), F32),
            pltpu.VMEM((tile, FF_TILE), BF16),
            pltpu.VMEM((CONV_W - 1, D_FF), F32),
        ],
        compiler_params=pltpu.CompilerParams(
            dimension_semantics=("arbitrary", "arbitrary"), vmem_limit_bytes=VMEM_LIMIT),
        name="ffn_prompt",
    )(x, mod, *weights)


def _ffn_sample_call(x2d, mod, weights, past, final, batch, seq):
    rows = batch * seq

    def full(shape):
        zeros = (0,) * len(shape)
        return pl.BlockSpec(shape, lambda i: zeros)

    return pl.pallas_call(
        functools.partial(_ffn_kernel, True, final, batch, seq),
        grid=(1,),
        in_specs=[full((rows, D_MODEL)), full((batch, 6, D_MODEL))] + _ffn_weight_specs() + [
            full((batch, CONV_W - 1, D_FF)),
        ],
        out_specs=[full((rows, D_MODEL)), full((batch, CONV_W - 1, D_FF))],
        out_shape=[
            jax.ShapeDtypeStruct((rows, D_MODEL), F32),
            jax.ShapeDtypeStruct((batch, CONV_W - 1, D_FF), F32),
        ],
        scratch_shapes=[
            pltpu.VMEM((rows, D_MODEL), BF16),
            pltpu.VMEM((batch * (seq + SUBLANE), FF_TILE), F32),
            pltpu.VMEM((rows, FF_TILE), BF16),
        ],
        compiler_params=pltpu.CompilerParams(
            dimension_semantics=("arbitrary",), vmem_limit_bytes=VMEM_LIMIT),
        name="ffn_sample",
    )(x2d, mod, *weights, past)


def _reorder_w_in(w):
    lr0 = 2 * GLA_QK + 2 * GLA_V
    pad = jnp.zeros((D_MODEL, LR_PAD - GLA_RANK), w.dtype)
    return jnp.concatenate([w[:, :lr0], w[:, lr0 + GLA_RANK:], w[:, lr0:lr0 + GLA_RANK], pad], axis=1)


def kernel(x_prompt, x_sample, state_gla, cache_swa_k, cache_swa_v, state_conv, c_prompt, c_sample,
           w_ada, b_ada, g_attn, g_ffn, w_in, w_gk2, b_gk, g_gla, sinks, w_out, w_up, conv_w, conv_b,
           w_down, g_final):
    depth = w_ada.shape[0]
    batch, seq, _ = x_prompt.shape
    dec_batch, dec_seq, _ = x_sample.shape
    state_rows = GLA_HEADS * GLA_DK

    c_all = jnp.concatenate(
        [c_prompt, c_sample, jnp.zeros((ADA_ROWS - batch - dec_batch, D_MODEL), F32)], axis=0)
    mod_all = _ada_call(c_all, w_ada, b_ada).reshape(depth, ADA_ROWS, 6, D_MODEL)

    yp = x_prompt
    ys = x_sample.reshape(dec_batch * dec_seq, D_MODEL)
    outs = [[] for _ in range(8)]
    for i in range(depth):
        mod_p = mod_all[i, :batch]
        mod_s = mod_all[i, batch:batch + dec_batch]
        w_gk2_p = jnp.concatenate(
            [w_gk2[i], jnp.zeros((LR_PAD - GLA_RANK, GLA_QK), F32)], axis=0).astype(BF16)
        mixer_w = (g_attn[i][None], _reorder_w_in(w_in[i]).astype(BF16), w_gk2_p, b_gk[i][None],
                   g_gla[i][None], sinks[i], w_out[i].astype(BF16))
        ffn_w = (g_ffn[i][None], w_up[i].astype(BF16), conv_w[i], conv_b[i][None],
                 w_down[i].astype(BF16), g_final[None])
        final = i == depth - 1

        yp, s_p, k_p, v_p = _mixer_prompt_call(yp, mod_p, mixer_w)
        yp, conv_p = _ffn_prompt_call(yp, mod_p, ffn_w, final)

        ys, s_s, k_s, v_s = _mixer_sample_call(
            ys.reshape(dec_batch, dec_seq, D_MODEL), mod_s, mixer_w,
            state_gla[i].reshape(dec_batch, state_rows, GLA_DV),
            cache_swa_k[i].reshape(dec_batch, WINDOW, SWA_KV),
            cache_swa_v[i].reshape(dec_batch, WINDOW, SWA_KV))
        ys, conv_s = _ffn_sample_call(ys, mod_s, ffn_w, state_conv[i], final, dec_batch, dec_seq)

        keep = min(WINDOW, seq)
        outs[0].append(s_p.reshape(batch, GLA_HEADS, GLA_DK, GLA_DV))
        outs[1].append(k_p.reshape(batch, keep, SWA_KV_HEADS, SWA_HD))
        outs[2].append(v_p.reshape(batch, keep, SWA_KV_HEADS, SWA_HD))
        outs[3].append(conv_p)
        outs[4].append(s_s.reshape(dec_batch, GLA_HEADS, GLA_DK, GLA_DV))
        outs[5].append(k_s.reshape(dec_batch, dec_seq, SWA_KV_HEADS, SWA_HD))
        outs[6].append(v_s.reshape(dec_batch, dec_seq, SWA_KV_HEADS, SWA_HD))
        outs[7].append(conv_s)

    return (yp, ys.reshape(dec_batch, dec_seq, D_MODEL)) + tuple(jnp.stack(o) for o in outs)
```

```python
import functools

import jax
import jax.numpy as jnp
from jax import lax
from jax.experimental import pallas as pl
from jax.experimental.pallas import tpu as pltpu

F32 = jnp.float32
BF16 = jnp.bfloat16

D_MODEL = 1024
CHUNK = 64
GLA_HEADS = 4
GLA_DK = 64
GLA_DV = 128
GLA_RANK = 16
GLA_NORMALIZER = 16.0
SWA_Q_HEADS = 8
SWA_KV_HEADS = 2
SWA_GROUP = SWA_Q_HEADS // SWA_KV_HEADS
SWA_HD = 64
WINDOW = 128
D_FF = 2816
CONV_W = 3
RMS_EPS = 1e-6

GLA_QK = GLA_HEADS * GLA_DK
GLA_V = GLA_HEADS * GLA_DV
SWA_Q = SWA_Q_HEADS * SWA_HD
SWA_KV = SWA_KV_HEADS * SWA_HD
D_MIX = GLA_V + SWA_Q
BAND = WINDOW + CHUNK

LANE = 128
SUBLANE = 8

C_QG = 0
C_KG = C_QG + GLA_QK
C_VG = C_KG + GLA_QK
C_OG = C_VG + GLA_V
C_QS = C_OG + GLA_V
C_KS = C_QS + SWA_Q
C_VS = C_KS + SWA_KV
C_LR = C_VS + SWA_KV
LR_PAD = LANE
D_IN_P = C_LR + LR_PAD
D_IN = 2 * GLA_QK + 2 * GLA_V + GLA_RANK + SWA_Q + 2 * SWA_KV

assert GLA_DK * 2 == LANE and SWA_HD * 2 == LANE and GLA_DV == LANE and SWA_KV == LANE
GLA_PAIRS = GLA_HEADS // 2
PAIR_V = 2 * GLA_DV
SWA_GRP_Q = SWA_GROUP * SWA_HD

ADA_TILE = 1536
ADA_ROWS = 16
PROMPT_TILE = 512
MIXER_TILE = 1024
SUB_CHUNKS = 4
DENSE_PIECE = 256
FF_PART = 768
FF_PARTS = tuple((lo, min(lo + FF_PART, D_FF)) for lo in range(0, D_FF, FF_PART))
VMEM_LIMIT = 56 * 1024 * 1024

NT_DIMS = (((1,), (1,)), ((), ()))
TN_DIMS = (((0,), (0,)), ((), ()))


def _dot(a, b):
    return jnp.dot(a, b, preferred_element_type=F32)


def _dot_nt(a, b):
    return lax.dot_general(a, b, NT_DIMS, preferred_element_type=F32)


def _dot_tn(a, b):
    return lax.dot_general(a, b, TN_DIMS, preferred_element_type=F32)


def _split_bf16(a):
    hi = a.astype(BF16)
    lo = (a - hi.astype(F32)).astype(BF16)
    return hi, lo


def _sigmoid(a):
    return 1.0 / (1.0 + jnp.exp(-a))


def _log_sigmoid(a):
    return jnp.minimum(a, 0.0) - jnp.log(1.0 + jnp.exp(-jnp.abs(a)))


def _rms_scale(a):
    return lax.rsqrt(jnp.mean(a * a, axis=-1, keepdims=True) + RMS_EPS)


def _ada_kernel(c_ref, w_ref, b_ref, o_ref):
    c = c_ref[...]
    a = (c * _sigmoid(c)).astype(BF16)
    o_ref[...] = _dot(a, w_ref[...].astype(BF16)) + b_ref[...]


def _ada_call(c_all, w_ada, b_ada):
    depth = w_ada.shape[0]
    n_tiles = (6 * D_MODEL) // ADA_TILE
    return pl.pallas_call(
        _ada_kernel,
        grid=(depth, n_tiles),
        in_specs=[
            pl.BlockSpec((ADA_ROWS, D_MODEL), lambda l, j: (0, 0)),
            pl.BlockSpec((None, D_MODEL, ADA_TILE), lambda l, j: (l, 0, j)),
            pl.BlockSpec((None, 1, ADA_TILE), lambda l, j: (l, 0, j)),
        ],
        out_specs=pl.BlockSpec((None, ADA_ROWS, ADA_TILE), lambda l, j: (l, 0, j)),
        out_shape=jax.ShapeDtypeStruct((depth, ADA_ROWS, 6 * D_MODEL), F32),
        compiler_params=pltpu.CompilerParams(
            dimension_semantics=("arbitrary", "arbitrary"), vmem_limit_bytes=VMEM_LIMIT),
        name="adaln_mod",
    )(c_all, w_ada, b_ada.reshape(depth, 1, 6 * D_MODEL))


def _dup_halves(a, low_half):
    swapped = pltpu.roll(a, LANE // 2, axis=1)
    return jnp.where(low_half, a, swapped), jnp.where(low_half, swapped, a)


def _mixer_kernel(sample, n_chunks, *refs):
    if sample:
        (x_ref, mod_ref, g_attn_ref, w_in_ref, w_gk2_ref, b_gk_ref, g_gla_ref, sinks_ref, w_out_ref,
         s0_ref, kc_ref, vc_ref,
         xo_ref, so_ref, ko_ref, vo_ref,
         h_scr, proj_scr, omix_scr, qm_scr, km_scr, qi_scr, ko_scr, vb_scr, gk_scr, dec_scr, u_scr, sb_scr,
         att_scr, pn_scr,
         kd_scr, vd_scr) = refs
        t = None
    else:
        (x_ref, mod_ref, g_attn_ref, w_in_ref, w_gk2_ref, b_gk_ref, g_gla_ref, sinks_ref, w_out_ref,
         xo_ref, so_ref, ko_ref, vo_ref,
         h_scr, proj_scr, omix_scr, qm_scr, km_scr, qi_scr, ko_scr, vb_scr, gk_scr, dec_scr, u_scr, sb_scr,
         att_scr, pn_scr,
         kd_scr, vd_scr, s_scr) = refs
        t = pl.program_id(1)
        n_t = pl.num_programs(1)

        @pl.when(t == 0)
        def _():
            s_scr[...] = jnp.zeros_like(s_scr)
            kd_scr[0:WINDOW, :] = jnp.zeros((WINDOW, 2 * LANE), BF16)
            vd_scr[0:WINDOW, :] = jnp.zeros((WINDOW, 2 * LANE), BF16)

    rows = n_chunks * CHUNK
    g_attn = g_attn_ref[...]

    def mod_row(c, idx):
        if sample:
            return mod_ref[c, idx:idx + 1, :]
        return mod_ref[idx:idx + 1, :]

    def chunk_rows(c):
        return slice(c * CHUNK, (c + 1) * CHUNK)

    sub_chunks = min(SUB_CHUNKS, n_chunks)
    n_sub = n_chunks // sub_chunks
    sub_rows = sub_chunks * CHUNK

    def sub_slice(s):
        return slice(s * sub_rows, (s + 1) * sub_rows)

    def norm_groups(s):
        if sample:
            return [(c, chunk_rows(c)) for c in range(s * sub_chunks, (s + 1) * sub_chunks)]
        return [(0, sub_slice(s))]

    r64 = lax.broadcasted_iota(jnp.int32, (CHUNK, CHUNK), 0)
    c64 = lax.broadcasted_iota(jnp.int32, (CHUNK, CHUNK), 1)
    tri = jnp.where(r64 >= c64, 1.0, 0.0).astype(BF16)
    r_pair = lax.broadcasted_iota(jnp.int32, (CHUNK, LANE), 0)
    l_pair = lax.broadcasted_iota(jnp.int32, (CHUNK, LANE), 1)
    causal_pair = r_pair >= (l_pair & (CHUNK - 1))
    low_half = l_pair < LANE // 2
    diag128 = ((lax.broadcasted_iota(jnp.int32, (LANE, LANE), 0) >= LANE // 2)
               == (lax.broadcasted_iota(jnp.int32, (LANE, LANE), 1) >= LANE // 2))
    diag_v = ((lax.broadcasted_iota(jnp.int32, (LANE, PAIR_V), 0) >= LANE // 2)
              == (lax.broadcasted_iota(jnp.int32, (LANE, PAIR_V), 1) >= GLA_DV))
    g_gla = g_gla_ref[...]
    low_sub = lax.broadcasted_iota(jnp.int32, (sub_rows, LANE), 1) < LANE // 2

    def project_steps(s):
        ss = sub_slice(s)

        def norm():
            for mi, rs in norm_groups(s):
                x = x_ref[rs, :]
                gain = g_attn * (1.0 + mod_row(mi, 1))
                h_scr[rs, :] = (x * _rms_scale(x) * gain + mod_row(mi, 0)).astype(BF16)

        def piece(lo, hi):
            proj_scr[ss, lo:hi] = _dot(h_scr[ss, :], w_in_ref[:, lo:hi])

        def gates():
            lr = proj_scr[ss, C_LR:C_LR + LR_PAD].astype(BF16)
            gk_scr[ss, :] = (_log_sigmoid(_dot(lr, w_gk2_ref[...]) + b_gk_ref[...])
                             * (1.0 / GLA_NORMALIZER))
            vb_scr[ss, :] = proj_scr[ss, C_VG:C_VG + GLA_V].astype(BF16)

        def bands():
            k_dup = _dup_halves(proj_scr[ss, C_KS:C_KS + SWA_KV], low_sub)
            v_dup = _dup_halves(proj_scr[ss, C_VS:C_VS + SWA_KV], low_sub)
            for kv in range(SWA_KV_HEADS):
                ls = slice(kv * LANE, (kv + 1) * LANE)
                if sample:
                    for c in range(s * sub_chunks, (s + 1) * sub_chunks):
                        local = slice((c - s * sub_chunks) * CHUNK, (c - s * sub_chunks + 1) * CHUNK)
                        kd_scr[c * BAND + WINDOW:(c + 1) * BAND, ls] = k_dup[kv][local, :].astype(BF16)
                        vd_scr[c * BAND + WINDOW:(c + 1) * BAND, ls] = v_dup[kv][local, :].astype(BF16)
                else:
                    band_rows = slice(WINDOW + s * sub_rows, WINDOW + (s + 1) * sub_rows)
                    kd_scr[band_rows, ls] = k_dup[kv].astype(BF16)
                    vd_scr[band_rows, ls] = v_dup[kv].astype(BF16)

        steps = [norm]
        for lo in range(0, D_IN_P, DENSE_PIECE):
            steps.append(functools.partial(piece, lo, min(lo + DENSE_PIECE, D_IN_P)))
        return steps + [gates, bands]

    def gla_prepare(c):
        rs = chunk_rows(c)
        gk_hi, gk_lo = _split_bf16(gk_scr[rs, :])
        cum = _dot(tri, gk_hi) + _dot(tri, gk_lo)
        mid = cum[CHUNK // 2:CHUNK // 2 + 1, :]
        last = cum[CHUNK - 1:CHUNK, :]
        q = proj_scr[rs, C_QG:C_QG + GLA_QK] * (GLA_DK ** -0.5)
        k = proj_scr[rs, C_KG:C_KG + GLA_QK]
        qm_scr[rs, :] = (q * jnp.exp(cum - mid)).astype(BF16)
        km_scr[rs, :] = (k * jnp.exp(mid - cum)).astype(BF16)
        qi_scr[rs, :] = (q * jnp.exp(cum)).astype(BF16)
        ko_scr[rs, :] = (k * jnp.exp(last - cum)).astype(BF16)
        dec_scr[c] = jnp.broadcast_to(jnp.exp(last), (LANE, GLA_QK)).T

    def gla_increment(c):
        rs = chunk_rows(c)
        for p in range(GLA_PAIRS):
            upd = _dot_tn(ko_scr[rs, p * LANE:(p + 1) * LANE], vb_scr[rs, p * PAIR_V:(p + 1) * PAIR_V])
            u_scr[c, p * LANE:p * LANE + GLA_DK, :] = upd[0:GLA_DK, 0:GLA_DV]
            u_scr[c, p * LANE + GLA_DK:(p + 1) * LANE, :] = upd[GLA_DK:LANE, GLA_DV:PAIR_V]

    def gla_recurrence(s):
        state = None if sample else s_scr[...]
        for c in range(s * sub_chunks, (s + 1) * sub_chunks):
            if sample:
                state = s0_ref[c]
            sb_scr[c] = state.astype(BF16)
            state = dec_scr[c] * state + u_scr[c]
            if sample:
                so_ref[c] = state
        if not sample:
            s_scr[...] = state

    def gla_scores(c, p):
        rs = chunk_rows(c)
        ls = slice(p * LANE, (p + 1) * LANE)
        km = km_scr[rs, ls]
        k_bd = jnp.where(diag128, jnp.concatenate([km, km], axis=0), 0.0)
        att_scr[rs, ls] = jnp.where(causal_pair, _dot_nt(qm_scr[rs, ls], k_bd), 0.0).astype(BF16)

    def gla_output(c, p):
        rs = chunk_rows(c)
        ls = slice(p * LANE, (p + 1) * LANE)
        v_pair = vb_scr[rs, p * PAIR_V:(p + 1) * PAIR_V]
        s_pair = sb_scr[c, ls, :]
        w_top = jnp.where(diag_v, jnp.concatenate([v_pair, v_pair], axis=0), 0.0)
        w_bot = jnp.where(diag_v, jnp.concatenate([s_pair, s_pair], axis=1), 0.0)
        o_pair = _dot(jnp.concatenate([att_scr[rs, ls], qi_scr[rs, ls]], axis=1),
                      jnp.concatenate([w_top, w_bot], axis=0))
        for hh in range(2):
            h = 2 * p + hh
            o = o_pair[:, hh * GLA_DV:(hh + 1) * GLA_DV]
            og = proj_scr[rs, C_OG + h * GLA_DV:C_OG + (h + 1) * GLA_DV]
            y = o * _rms_scale(o) * g_gla * (og * _sigmoid(og))
            omix_scr[rs, h * GLA_DV:(h + 1) * GLA_DV] = y.astype(BF16)

    if sample:
        low_win = lax.broadcasted_iota(jnp.int32, (WINDOW, LANE), 1) < LANE // 2
        for c in range(n_chunks):
            kc_dup = _dup_halves(kc_ref[c], low_win)
            vc_dup = _dup_halves(vc_ref[c], low_win)
            for kv in range(SWA_KV_HEADS):
                ls = slice(kv * LANE, (kv + 1) * LANE)
                kd_scr[c * BAND:c * BAND + WINDOW, ls] = kc_dup[kv].astype(BF16)
                vd_scr[c * BAND:c * BAND + WINDOW, ls] = vc_dup[kv].astype(BF16)

    lane_q = lax.broadcasted_iota(jnp.int32, (1, SWA_GRP_Q), 1)
    key_ids = lax.broadcasted_iota(jnp.int32, (BAND, SWA_GRP_Q), 0)
    sink_vecs = []
    for kv in range(SWA_KV_HEADS):
        vec = jnp.full((1, SWA_GRP_Q), sinks_ref[kv * SWA_GROUP + SWA_GROUP - 1], F32)
        for g in range(SWA_GROUP - 2, -1, -1):
            vec = jnp.where(lane_q < (g + 1) * SWA_HD, sinks_ref[kv * SWA_GROUP + g], vec)
        sink_vecs.append(vec)

    def band_rows(c):
        return slice(c * BAND, (c + 1) * BAND) if sample else slice(c * CHUNK, c * CHUNK + BAND)

    def swa_scores(c, kv):
        rs = chunk_rows(c)
        band = band_rows(c)
        ls = slice(kv * LANE, (kv + 1) * LANE)
        qg = (proj_scr[rs, C_QS + kv * SWA_GRP_Q:C_QS + (kv + 1) * SWA_GRP_Q]
              * (SWA_HD ** -0.5)).astype(BF16)
        q_stack = jnp.concatenate(
            [jnp.where(low_half if hh == 0 else ~low_half, qg[:, pp * LANE:(pp + 1) * LANE], 0.0)
             for pp in range(SWA_GROUP // 2) for hh in range(2)], axis=0)
        s_t = _dot_nt(kd_scr[band, ls], q_stack)
        if not sample and c * CHUNK < WINDOW:
            first_valid = WINDOW - (t * rows + c * CHUNK)
            s_t = jnp.where(key_ids >= first_valid, s_t, -jnp.inf)
        sink = sink_vecs[kv]
        m = jnp.maximum(jnp.max(s_t, axis=0, keepdims=True), sink)
        p_t = jnp.exp(s_t - m)
        den = jnp.sum(p_t, axis=0, keepdims=True) + jnp.exp(sink - m)
        pn_scr[c * SWA_KV_HEADS + kv] = (p_t * (1.0 / den)).astype(BF16)

    def swa_output(c, kv):
        rs = chunk_rows(c)
        ls = slice(kv * LANE, (kv + 1) * LANE)
        o_t = _dot_tn(pn_scr[c * SWA_KV_HEADS + kv], vd_scr[band_rows(c), ls])
        for pp in range(SWA_GROUP // 2):
            o_pair = jnp.where(low_half, o_t[(2 * pp) * CHUNK:(2 * pp + 1) * CHUNK, :],
                               o_t[(2 * pp + 1) * CHUNK:(2 * pp + 2) * CHUNK, :])
            col = GLA_V + kv * SWA_GRP_Q + pp * LANE
            omix_scr[rs, col:col + LANE] = o_pair.astype(BF16)

    def out_steps(s):
        ss = sub_slice(s)
        base = s * sub_rows

        def piece(lo, hi):
            mix = _dot(omix_scr[ss, :], w_out_ref[:, lo:hi])
            for mi, rs in norm_groups(s):
                local = slice(rs.start - base, rs.stop - base)
                xo_ref[rs, lo:hi] = x_ref[rs, lo:hi] + mod_row(mi, 2)[:, lo:hi] * mix[local, :]

        return [functools.partial(piece, lo, lo + DENSE_PIECE) for lo in range(0, D_MODEL, DENSE_PIECE)]

    def block_steps(s):
        blocks = range(s * sub_chunks, (s + 1) * sub_chunks)
        pairs = [(c, p) for c in blocks for p in range(GLA_PAIRS)]
        groups = [(c, kv) for c in blocks for kv in range(SWA_KV_HEADS)]
        steps = [functools.partial(gla_prepare, c) for c in blocks]
        steps += [functools.partial(swa_scores, c, kv) for c, kv in groups]
        steps += [functools.partial(gla_increment, c) for c in blocks]
        steps += [functools.partial(gla_scores, c, p) for c, p in pairs]
        steps.append(functools.partial(gla_recurrence, s))
        steps += [functools.partial(swa_output, c, kv) for c, kv in groups]
        steps += [functools.partial(gla_output, c, p) for c, p in pairs]
        return steps

    for step in project_steps(0):
        step()
    for s in range(n_sub):
        dense = (project_steps(s + 1) if s + 1 < n_sub else []) + (out_steps(s - 1) if s > 0 else [])
        work = block_steps(s)
        issued = 0
        for i, step in enumerate(work):
            while issued < len(dense) and issued * len(work) < (i + 1) * len(dense):
                dense[issued]()
                issued += 1
            step()
        for step in dense[issued:]:
            step()
    for step in out_steps(n_sub - 1):
        step()

    if sample:
        ko_ref[...] = proj_scr[:, C_KS:C_KS + SWA_KV]
        vo_ref[...] = proj_scr[:, C_VS:C_VS + SWA_KV]
    else:
        kd_scr[0:WINDOW, :] = kd_scr[rows:rows + WINDOW, :]
        vd_scr[0:WINDOW, :] = vd_scr[rows:rows + WINDOW, :]

        @pl.when(t == n_t - 1)
        def _():
            so_ref[...] = s_scr[...]
            ko_ref[...] = proj_scr[rows - WINDOW:rows, C_KS:C_KS + SWA_KV]
            vo_ref[...] = proj_scr[rows - WINDOW:rows, C_VS:C_VS + SWA_KV]


def _const_spec(shape):
    zeros = (0,) * len(shape)
    return pl.BlockSpec(shape, lambda *_: zeros, pipeline_mode=pl.Buffered(1))


def _mixer_weight_specs():
    return [
        _const_spec((1, D_MODEL)),
        _const_spec((D_MODEL, D_IN_P)),
        _const_spec((LR_PAD, GLA_QK)),
        _const_spec((1, GLA_QK)),
        _const_spec((1, GLA_DV)),
        pl.BlockSpec(memory_space=pltpu.SMEM),
        _const_spec((D_MIX, D_MODEL)),
    ]


def _mixer_scratch(rows, n_chunks, band_rows):
    state_rows = GLA_HEADS * GLA_DK
    return [
        pltpu.VMEM((rows, D_MODEL), BF16),
        pltpu.VMEM((rows, D_IN_P), F32),
        pltpu.VMEM((rows, D_MIX), BF16),
        pltpu.VMEM((rows, GLA_QK), BF16),
        pltpu.VMEM((rows, GLA_QK), BF16),
        pltpu.VMEM((rows, GLA_QK), BF16),
        pltpu.VMEM((rows, GLA_QK), BF16),
        pltpu.VMEM((rows, GLA_V), BF16),
        pltpu.VMEM((rows, GLA_QK), F32),
        pltpu.VMEM((n_chunks, state_rows, GLA_DV), F32),
        pltpu.VMEM((n_chunks, state_rows, GLA_DV), F32),
        pltpu.VMEM((n_chunks, state_rows, GLA_DV), BF16),
        pltpu.VMEM((rows, GLA_QK), BF16),
        pltpu.VMEM((n_chunks * SWA_KV_HEADS, BAND, SWA_GRP_Q), BF16),
        pltpu.VMEM((band_rows, 2 * LANE), BF16),
        pltpu.VMEM((band_rows, 2 * LANE), BF16),
    ]


def _mixer_prompt_call(x, mod, weights):
    batch, seq, _ = x.shape
    tile = min(MIXER_TILE, seq)
    assert seq % tile == 0 and tile % CHUNK == 0 and tile >= WINDOW
    n_chunks = tile // CHUNK
    n_t = seq // tile
    state_rows = GLA_HEADS * GLA_DK
    return pl.pallas_call(
        functools.partial(_mixer_kernel, False, n_chunks),
        grid=(batch, n_t),
        in_specs=[
            pl.BlockSpec((None, tile, D_MODEL), lambda b, t: (b, t, 0)),
            pl.BlockSpec((None, 6, D_MODEL), lambda b, t: (b, 0, 0)),
        ] + _mixer_weight_specs(),
        out_specs=[
            pl.BlockSpec((None, tile, D_MODEL), lambda b, t: (b, t, 0)),
            pl.BlockSpec((None, state_rows, GLA_DV), lambda b, t: (b, 0, 0)),
            pl.BlockSpec((None, WINDOW, SWA_KV), lambda b, t: (b, 0, 0)),
            pl.BlockSpec((None, WINDOW, SWA_KV), lambda b, t: (b, 0, 0)),
        ],
        out_shape=[
            jax.ShapeDtypeStruct((batch, seq, D_MODEL), F32),
            jax.ShapeDtypeStruct((batch, state_rows, GLA_DV), F32),
            jax.ShapeDtypeStruct((batch, WINDOW, SWA_KV), F32),
            jax.ShapeDtypeStruct((batch, WINDOW, SWA_KV), F32),
        ],
        scratch_shapes=_mixer_scratch(tile, n_chunks, WINDOW + tile) + [
            pltpu.VMEM((state_rows, GLA_DV), F32),
        ],
        compiler_params=pltpu.CompilerParams(
            dimension_semantics=("arbitrary", "arbitrary"), vmem_limit_bytes=VMEM_LIMIT),
        name="mixer_prompt",
    )(x, mod, *weights)


def _mixer_sample_call(x, mod, weights, s0, k_cache, v_cache):
    batch, seq, _ = x.shape
    assert seq == CHUNK
    rows = batch * seq
    state_rows = GLA_HEADS * GLA_DK

    def full(shape):
        zeros = (0,) * len(shape)
        return pl.BlockSpec(shape, lambda i: zeros)

    return pl.pallas_call(
        functools.partial(_mixer_kernel, True, batch),
        grid=(1,),
        in_specs=[full((rows, D_MODEL)), full((batch, 6, D_MODEL))] + _mixer_weight_specs() + [
            full((batch, state_rows, GLA_DV)),
            full((batch, WINDOW, SWA_KV)),
            full((batch, WINDOW, SWA_KV)),
        ],
        out_specs=[
            full((rows, D_MODEL)),
            full((batch, state_rows, GLA_DV)),
            full((rows, SWA_KV)),
            full((rows, SWA_KV)),
        ],
        out_shape=[
            jax.ShapeDtypeStruct((rows, D_MODEL), F32),
            jax.ShapeDtypeStruct((batch, state_rows, GLA_DV), F32),
            jax.ShapeDtypeStruct((rows, SWA_KV), F32),
            jax.ShapeDtypeStruct((rows, SWA_KV), F32),
        ],
        scratch_shapes=_mixer_scratch(rows, batch, batch * BAND),
        compiler_params=pltpu.CompilerParams(
            dimension_semantics=("arbitrary",), vmem_limit_bytes=VMEM_LIMIT),
        name="mixer_sample",
    )(x.reshape(rows, D_MODEL), mod, *weights, s0, k_cache, v_cache)


def _ffn_kernel(sample, final, n_seg, seg_len, *refs):
    if sample:
        (x_ref, mod_ref, g_ffn_ref, w_up_ref, conv_w_ref, conv_b_ref, w_down_ref, g_final_ref,
         past_ref, xo_ref, co_ref, h_scr, ub_scr, act_scr) = refs
    else:
        (x_ref, mod_ref, g_ffn_ref, w_up_ref, conv_w_ref, conv_b_ref, w_down_ref, g_final_ref,
         xo_ref, co_ref, h_scr, ub_scr, act_scr, past_scr) = refs
        t = pl.program_id(1)
        n_t = pl.num_programs(1)

        @pl.when(t == 0)
        def _():
            past_scr[...] = jnp.zeros_like(past_scr)

    rows = n_seg * seg_len
    stride = seg_len + SUBLANE
    g_ffn = g_ffn_ref[...]

    def mod_row(c, idx):
        if sample:
            return mod_ref[c, idx:idx + 1, :]
        return mod_ref[idx:idx + 1, :]

    for c in range(n_seg):
        rs = slice(c * seg_len, (c + 1) * seg_len)
        x = x_ref[rs, :]
        gain = g_ffn * (1.0 + mod_row(c, 4))
        h_scr[rs, :] = (x * _rms_scale(x) * gain + mod_row(c, 3)).astype(BF16)

    def up(j):
        lo, hi = FF_PARTS[j]
        h = h_scr[...]
        return _dot(h, w_up_ref[:, lo:hi]), _dot(h, w_up_ref[:, D_FF + lo:D_FF + hi])

    def activate(j, u, val):
        lo, hi = FF_PARTS[j]
        width = hi - lo
        w0 = conv_w_ref[0:1, lo:hi]
        w1 = conv_w_ref[1:2, lo:hi]
        w2 = conv_w_ref[2:3, lo:hi]
        cb = conv_b_ref[:, lo:hi]
        for c in range(n_seg):
            base = c * stride
            rs = slice(c * seg_len, (c + 1) * seg_len)
            if sample:
                ub_scr[base + SUBLANE - 2:base + SUBLANE, 0:width] = past_ref[c, :, lo:hi]
            else:
                ub_scr[base + SUBLANE - 2:base + SUBLANE, 0:width] = past_scr[:, lo:hi]
            u_seg = u[rs, :]
            ub_scr[base + SUBLANE:base + SUBLANE + seg_len, 0:width] = u_seg
            u1 = ub_scr[base + SUBLANE - 1:base + SUBLANE - 1 + seg_len, 0:width]
            u2 = ub_scr[base + SUBLANE - 2:base + SUBLANE - 2 + seg_len, 0:width]
            uc = w0 * u2 + w1 * u1 + w2 * u_seg + cb
            act_scr[rs, lo:hi] = (uc * _sigmoid(uc) * val[rs, :]).astype(BF16)
            tail = ub_scr[base + seg_len + SUBLANE - 2:base + seg_len + SUBLANE, 0:width]
            if sample:
                co_ref[c, :, lo:hi] = tail
            else:
                past_scr[:, lo:hi] = tail

    def down(j):
        lo, hi = FF_PARTS[j]
        return _dot(act_scr[:, lo:hi], w_down_ref[lo:hi, :])

    n_parts = len(FF_PARTS)
    pending = {0: up(0)}
    acc = None
    for j in range(n_parts):
        if j + 1 < n_parts:
            pending[j + 1] = up(j + 1)
        activate(j, *pending.pop(j))
        if j > 0:
            d = down(j - 1)
            acc = d if acc is None else acc + d
    acc = acc + down(n_parts - 1)

    for c in range(n_seg):
        rs = slice(c * seg_len, (c + 1) * seg_len)
        y = x_ref[rs, :] + mod_row(c, 5) * acc[rs, :]
        if final:
            y = y * _rms_scale(y) * g_final_ref[...]
        xo_ref[rs, :] = y

    if not sample:
        @pl.when(t == n_t - 1)
        def _():
            co_ref[...] = past_scr[...]


def _ffn_weight_specs():
    return [
        _const_spec((1, D_MODEL)),
        _const_spec((D_MODEL, 2 * D_FF)),
        _const_spec((CONV_W, D_FF)),
        _const_spec((1, D_FF)),
        _const_spec((D_FF, D_MODEL)),
        _const_spec((1, D_MODEL)),
    ]


def _ffn_prompt_call(x, mod, weights, final):
    batch, seq, _ = x.shape
    tile = min(PROMPT_TILE, seq)
    n_t = seq // tile
    return pl.pallas_call(
        functools.partial(_ffn_kernel, False, final, 1, tile),
        grid=(batch, n_t),
        in_specs=[
            pl.BlockSpec((None, tile, D_MODEL), lambda b, t: (b, t, 0)),
            pl.BlockSpec((None, 6, D_MODEL), lambda b, t: (b, 0, 0)),
        ] + _ffn_weight_specs(),
        out_specs=[
            pl.BlockSpec((None, tile, D_MODEL), lambda b, t: (b, t, 0)),
            pl.BlockSpec((None, CONV_W - 1, D_FF), lambda b, t: (b, 0, 0)),
        ],
        out_shape=[
            jax.ShapeDtypeStruct((batch, seq, D_MODEL), F32),
            jax.ShapeDtypeStruct((batch, CONV_W - 1, D_FF), F32),
        ],
        scratch_shapes=[
            pltpu.VMEM((tile, D_MODEL), BF16),
            pltpu.VMEM((tile + SUBLANE, FF_PART), F32),
            pltpu.VMEM((tile, D_FF), BF16),
            pltpu.VMEM((CONV_W - 1, D_FF), F32),
        ],
        compiler_params=pltpu.CompilerParams(
            dimension_semantics=("arbitrary", "arbitrary"), vmem_limit_bytes=VMEM_LIMIT),
        name="ffn_prompt",
    )(x, mod, *weights)


def _ffn_sample_call(x2d, mod, weights, past, final, batch, seq):
    rows = batch * seq

    def full(shape):
        zeros = (0,) * len(shape)
        return pl.BlockSpec(shape, lambda i: zeros)

    return pl.pallas_call(
        functools.partial(_ffn_kernel, True, final, batch, seq),
        grid=(1,),
        in_specs=[full((rows, D_MODEL)), full((batch, 6, D_MODEL))] + _ffn_weight_specs() + [
            full((batch, CONV_W - 1, D_FF)),
        ],
        out_specs=[full((rows, D_MODEL)), full((batch, CONV_W - 1, D_FF))],
        out_shape=[
            jax.ShapeDtypeStruct((rows, D_MODEL), F32),
            jax.ShapeDtypeStruct((batch, CONV_W - 1, D_FF), F32),
        ],
        scratch_shapes=[
            pltpu.VMEM((rows, D_MODEL), BF16),
            pltpu.VMEM((batch * (seq + SUBLANE), FF_PART), F32),
            pltpu.VMEM((rows, D_FF), BF16),
        ],
        compiler_params=pltpu.CompilerParams(
            dimension_semantics=("arbitrary",), vmem_limit_bytes=VMEM_LIMIT),
        name="ffn_sample",
    )(x2d, mod, *weights, past)


def _reorder_w_in(w):
    lr0 = 2 * GLA_QK + 2 * GLA_V
    w = w.astype(BF16)
    pad = jnp.zeros((D_MODEL, LR_PAD - GLA_RANK), BF16)
    return jnp.concatenate([w[:, :lr0], w[:, lr0 + GLA_RANK:], w[:, lr0:lr0 + GLA_RANK], pad], axis=1)


def kernel(x_prompt, x_sample, state_gla, cache_swa_k, cache_swa_v, state_conv, c_prompt, c_sample,
           w_ada, b_ada, g_attn, g_ffn, w_in, w_gk2, b_gk, g_gla, sinks, w_out, w_up, conv_w, conv_b,
           w_down, g_final):
    depth = w_ada.shape[0]
    batch, seq, _ = x_prompt.shape
    dec_batch, dec_seq, _ = x_sample.shape
    state_rows = GLA_HEADS * GLA_DK

    c_all = jnp.concatenate(
        [c_prompt, c_sample, jnp.zeros((ADA_ROWS - batch - dec_batch, D_MODEL), F32)], axis=0)
    mod_all = _ada_call(c_all, w_ada, b_ada).reshape(depth, ADA_ROWS, 6, D_MODEL)

    yp = x_prompt
    ys = x_sample.reshape(dec_batch * dec_seq, D_MODEL)
    outs = [[] for _ in range(8)]
    for i in range(depth):
        mod_p = mod_all[i, :batch]
        mod_s = mod_all[i, batch:batch + dec_batch]
        w_gk2_p = jnp.concatenate(
            [w_gk2[i], jnp.zeros((LR_PAD - GLA_RANK, GLA_QK), F32)], axis=0).astype(BF16)
        mixer_w = (g_attn[i][None], _reorder_w_in(w_in[i]), w_gk2_p, b_gk[i][None],
                   g_gla[i][None], sinks[i], w_out[i].astype(BF16))
        ffn_w = (g_ffn[i][None], w_up[i].astype(BF16), conv_w[i], conv_b[i][None],
                 w_down[i].astype(BF16), g_final[None])
        final = i == depth - 1

        yp, s_p, k_p, v_p = _mixer_prompt_call(yp, mod_p, mixer_w)
        yp, conv_p = _ffn_prompt_call(yp, mod_p, ffn_w, final)

        ys, s_s, k_s, v_s = _mixer_sample_call(
            ys.reshape(dec_batch, dec_seq, D_MODEL), mod_s, mixer_w,
            state_gla[i].reshape(dec_batch, state_rows, GLA_DV),
            cache_swa_k[i].reshape(dec_batch, WINDOW, SWA_KV),
            cache_swa_v[i].reshape(dec_batch, WINDOW, SWA_KV))
        ys, conv_s = _ffn_sample_call(ys, mod_s, ffn_w, state_conv[i], final, dec_batch, dec_seq)

        keep = min(WINDOW, seq)
        outs[0].append(s_p.reshape(batch, GLA_HEADS, GLA_DK, GLA_DV))
        outs[1].append(k_p.reshape(batch, keep, SWA_KV_HEADS, SWA_HD))
        outs[2].append(v_p.reshape(batch, keep, SWA_KV_HEADS, SWA_HD))
        outs[3].append(conv_p)
        outs[4].append(s_s.reshape(dec_batch, GLA_HEADS, GLA_DK, GLA_DV))
        outs[5].append(k_s.reshape(dec_batch, dec_seq, SWA_KV_HEADS, SWA_HD))
        outs[6].append(v_s.reshape(dec_batch, dec_seq, SWA_KV_HEADS, SWA_HD))
        outs[7].append(conv_s)

    return (yp, ys.reshape(dec_batch, dec_seq, D_MODEL)) + tuple(jnp.stack(o) for o in outs)
```

```python
import functools

import jax
import jax.numpy as jnp
from jax import lax
from jax.experimental import pallas as pl
from jax.experimental.pallas import tpu as pltpu

F32 = jnp.float32
BF16 = jnp.bfloat16

D_MODEL = 1024
CHUNK = 64
GLA_HEADS = 4
GLA_DK = 64
GLA_DV = 128
GLA_RANK = 16
GLA_NORMALIZER = 16.0
SWA_Q_HEADS = 8
SWA_KV_HEADS = 2
SWA_GROUP = SWA_Q_HEADS // SWA_KV_HEADS
SWA_HD = 64
WINDOW = 128
D_FF = 2816
CONV_W = 3
RMS_EPS = 1e-6

GLA_QK = GLA_HEADS * GLA_DK
GLA_V = GLA_HEADS * GLA_DV
SWA_Q = SWA_Q_HEADS * SWA_HD
SWA_KV = SWA_KV_HEADS * SWA_HD
D_MIX = GLA_V + SWA_Q
BAND = WINDOW + CHUNK

LANE = 128
SUBLANE = 8

C_QG = 0
C_KG = C_QG + GLA_QK
C_VG = C_KG + GLA_QK
C_OG = C_VG + GLA_V
C_QS = C_OG + GLA_V
C_KS = C_QS + SWA_Q
C_VS = C_KS + SWA_KV
C_LR = C_VS + SWA_KV
LR_PAD = LANE
D_IN_P = C_LR + LR_PAD
D_IN = 2 * GLA_QK + 2 * GLA_V + GLA_RANK + SWA_Q + 2 * SWA_KV

assert GLA_DK * 2 == LANE and SWA_HD * 2 == LANE and GLA_DV == LANE and SWA_KV == LANE
GLA_PAIRS = GLA_HEADS // 2
PAIR_V = 2 * GLA_DV
SWA_GRP_Q = SWA_GROUP * SWA_HD

ADA_TILE = 1536
ADA_ROWS = 16
PROMPT_TILE = 512
MIXER_TILE = 1024
SUB_CHUNKS = 4
DENSE_PIECE = 256
FF_PART = 768
FF_PARTS = tuple((lo, min(lo + FF_PART, D_FF)) for lo in range(0, D_FF, FF_PART))
NORM_AHEAD_AFTER_PART = 1
VMEM_LIMIT = 56 * 1024 * 1024

NT_DIMS = (((1,), (1,)), ((), ()))
TN_DIMS = (((0,), (0,)), ((), ()))


def _dot(a, b):
    return jnp.dot(a, b, preferred_element_type=F32)


def _dot_nt(a, b):
    return lax.dot_general(a, b, NT_DIMS, preferred_element_type=F32)


def _dot_tn(a, b):
    return lax.dot_general(a, b, TN_DIMS, preferred_element_type=F32)


def _split_bf16(a):
    hi = a.astype(BF16)
    lo = (a - hi.astype(F32)).astype(BF16)
    return hi, lo


def _sigmoid(a):
    return 1.0 / (1.0 + jnp.exp(-a))


def _log_sigmoid(a):
    return jnp.minimum(a, 0.0) - jnp.log(1.0 + jnp.exp(-jnp.abs(a)))


def _rms_scale(a):
    return lax.rsqrt(jnp.mean(a * a, axis=-1, keepdims=True) + RMS_EPS)


def _ada_kernel(c_ref, w_ref, b_ref, o_ref):
    c = c_ref[...]
    a = (c * _sigmoid(c)).astype(BF16)
    o_ref[...] = _dot(a, w_ref[...].astype(BF16)) + b_ref[...]


def _ada_call(c_all, w_ada, b_ada):
    depth = w_ada.shape[0]
    n_tiles = (6 * D_MODEL) // ADA_TILE
    return pl.pallas_call(
        _ada_kernel,
        grid=(depth, n_tiles),
        in_specs=[
            pl.BlockSpec((ADA_ROWS, D_MODEL), lambda l, j: (0, 0)),
            pl.BlockSpec((None, D_MODEL, ADA_TILE), lambda l, j: (l, 0, j)),
            pl.BlockSpec((None, 1, ADA_TILE), lambda l, j: (l, 0, j)),
        ],
        out_specs=pl.BlockSpec((None, ADA_ROWS, ADA_TILE), lambda l, j: (l, 0, j)),
        out_shape=jax.ShapeDtypeStruct((depth, ADA_ROWS, 6 * D_MODEL), F32),
        compiler_params=pltpu.CompilerParams(
            dimension_semantics=("arbitrary", "arbitrary"), vmem_limit_bytes=VMEM_LIMIT),
        name="adaln_mod",
    )(c_all, w_ada, b_ada.reshape(depth, 1, 6 * D_MODEL))


def _dup_halves(a, low_half):
    swapped = pltpu.roll(a, LANE // 2, axis=1)
    return jnp.where(low_half, a, swapped), jnp.where(low_half, swapped, a)


def _mixer_kernel(sample, n_chunks, *refs):
    if sample:
        (x_ref, mod_ref, g_attn_ref, w_in_ref, w_gk2_ref, b_gk_ref, g_gla_ref, sinks_ref, w_out_ref,
         s0_ref, kc_ref, vc_ref,
         xo_ref, so_ref, ko_ref, vo_ref,
         h_scr, proj_scr, omix_scr, qm_scr, km_scr, qi_scr, ko_scr, vb_scr, gk_scr, dec_scr, u_scr, sb_scr,
         att_scr, pn_scr,
         kd_scr, vd_scr) = refs
        t = None
    else:
        (x_ref, mod_ref, xn_ref, modn_ref,
         g_attn_ref, w_in_ref, w_gk2_ref, b_gk_ref, g_gla_ref, sinks_ref, w_out_ref,
         xo_ref, so_ref, ko_ref, vo_ref,
         h_scr, proj_scr, omix_scr, qm_scr, km_scr, qi_scr, ko_scr, vb_scr, gk_scr, dec_scr, u_scr, sb_scr,
         att_scr, pn_scr,
         kd_scr, vd_scr, s_scr) = refs
        t = pl.program_id(1)
        n_t = pl.num_programs(1)

        @pl.when(t == 0)
        def _():
            s_scr[...] = jnp.zeros_like(s_scr)
            kd_scr[0:WINDOW, :] = jnp.zeros((WINDOW, 2 * LANE), BF16)
            vd_scr[0:WINDOW, :] = jnp.zeros((WINDOW, 2 * LANE), BF16)

    rows = n_chunks * CHUNK
    g_attn = g_attn_ref[...]

    def mod_row(c, idx):
        if sample:
            return mod_ref[c, idx:idx + 1, :]
        return mod_ref[idx:idx + 1, :]

    def chunk_rows(c):
        return slice(c * CHUNK, (c + 1) * CHUNK)

    sub_chunks = min(SUB_CHUNKS, n_chunks)
    n_sub = n_chunks // sub_chunks
    sub_rows = sub_chunks * CHUNK

    def sub_slice(s):
        return slice(s * sub_rows, (s + 1) * sub_rows)

    def norm_groups(s):
        if sample:
            return [(c, chunk_rows(c)) for c in range(s * sub_chunks, (s + 1) * sub_chunks)]
        return [(0, sub_slice(s))]

    r64 = lax.broadcasted_iota(jnp.int32, (CHUNK, CHUNK), 0)
    c64 = lax.broadcasted_iota(jnp.int32, (CHUNK, CHUNK), 1)
    tri = jnp.where(r64 >= c64, 1.0, 0.0).astype(BF16)
    r_pair = lax.broadcasted_iota(jnp.int32, (CHUNK, LANE), 0)
    l_pair = lax.broadcasted_iota(jnp.int32, (CHUNK, LANE), 1)
    causal_pair = r_pair >= (l_pair & (CHUNK - 1))
    low_half = l_pair < LANE // 2
    diag128 = ((lax.broadcasted_iota(jnp.int32, (LANE, LANE), 0) >= LANE // 2)
               == (lax.broadcasted_iota(jnp.int32, (LANE, LANE), 1) >= LANE // 2))
    diag_v = ((lax.broadcasted_iota(jnp.int32, (LANE, PAIR_V), 0) >= LANE // 2)
              == (lax.broadcasted_iota(jnp.int32, (LANE, PAIR_V), 1) >= GLA_DV))
    g_gla = g_gla_ref[...]
    low_sub = lax.broadcasted_iota(jnp.int32, (sub_rows, LANE), 1) < LANE // 2

    def project_steps(s, ahead=False):
        ss = sub_slice(s)

        def norm():
            if ahead:
                x = xn_ref[...]
                gain = g_attn * (1.0 + modn_ref[1:2, :])
                h_scr[ss, :] = (x * _rms_scale(x) * gain + modn_ref[0:1, :]).astype(BF16)
                return
            for mi, rs in norm_groups(s):
                x = x_ref[rs, :]
                gain = g_attn * (1.0 + mod_row(mi, 1))
                h_scr[rs, :] = (x * _rms_scale(x) * gain + mod_row(mi, 0)).astype(BF16)

        def piece(lo, hi):
            proj_scr[ss, lo:hi] = _dot(h_scr[ss, :], w_in_ref[:, lo:hi])

        def gates():
            lr = proj_scr[ss, C_LR:C_LR + LR_PAD].astype(BF16)
            gk_scr[ss, :] = (_log_sigmoid(_dot(lr, w_gk2_ref[...]) + b_gk_ref[...])
                             * (1.0 / GLA_NORMALIZER))
            vb_scr[ss, :] = proj_scr[ss, C_VG:C_VG + GLA_V].astype(BF16)

        def bands():
            k_dup = _dup_halves(proj_scr[ss, C_KS:C_KS + SWA_KV], low_sub)
            v_dup = _dup_halves(proj_scr[ss, C_VS:C_VS + SWA_KV], low_sub)
            for kv in range(SWA_KV_HEADS):
                ls = slice(kv * LANE, (kv + 1) * LANE)
                if sample:
                    for c in range(s * sub_chunks, (s + 1) * sub_chunks):
                        local = slice((c - s * sub_chunks) * CHUNK, (c - s * sub_chunks + 1) * CHUNK)
                        kd_scr[c * BAND + WINDOW:(c + 1) * BAND, ls] = k_dup[kv][local, :].astype(BF16)
                        vd_scr[c * BAND + WINDOW:(c + 1) * BAND, ls] = v_dup[kv][local, :].astype(BF16)
                else:
                    band_rows = slice(WINDOW + s * sub_rows, WINDOW + (s + 1) * sub_rows)
                    kd_scr[band_rows, ls] = k_dup[kv].astype(BF16)
                    vd_scr[band_rows, ls] = v_dup[kv].astype(BF16)

        steps = [norm]
        for lo in range(0, D_IN_P, DENSE_PIECE):
            steps.append(functools.partial(piece, lo, min(lo + DENSE_PIECE, D_IN_P)))
        return steps + [gates, bands]

    def gla_prepare(c):
        rs = chunk_rows(c)
        gk_hi, gk_lo = _split_bf16(gk_scr[rs, :])
        cum = _dot(tri, gk_hi) + _dot(tri, gk_lo)
        mid = cum[CHUNK // 2:CHUNK // 2 + 1, :]
        last = cum[CHUNK - 1:CHUNK, :]
        q = proj_scr[rs, C_QG:C_QG + GLA_QK] * (GLA_DK ** -0.5)
        k = proj_scr[rs, C_KG:C_KG + GLA_QK]
        qm_scr[rs, :] = (q * jnp.exp(cum - mid)).astype(BF16)
        km_scr[rs, :] = (k * jnp.exp(mid - cum)).astype(BF16)
        qi_scr[rs, :] = (q * jnp.exp(cum)).astype(BF16)
        ko_scr[rs, :] = (k * jnp.exp(last - cum)).astype(BF16)
        dec_scr[c] = jnp.broadcast_to(jnp.exp(last), (LANE, GLA_QK)).T

    def gla_increment(c):
        rs = chunk_rows(c)
        for p in range(GLA_PAIRS):
            upd = _dot_tn(ko_scr[rs, p * LANE:(p + 1) * LANE], vb_scr[rs, p * PAIR_V:(p + 1) * PAIR_V])
            u_scr[c, p * LANE:p * LANE + GLA_DK, :] = upd[0:GLA_DK, 0:GLA_DV]
            u_scr[c, p * LANE + GLA_DK:(p + 1) * LANE, :] = upd[GLA_DK:LANE, GLA_DV:PAIR_V]

    def gla_recurrence(s):
        state = None if sample else s_scr[...]
        for c in range(s * sub_chunks, (s + 1) * sub_chunks):
            if sample:
                state = s0_ref[c]
            sb_scr[c] = state.astype(BF16)
            state = dec_scr[c] * state + u_scr[c]
            if sample:
                so_ref[c] = state
        if not sample:
            s_scr[...] = state

    def gla_scores(c, p):
        rs = chunk_rows(c)
        ls = slice(p * LANE, (p + 1) * LANE)
        km = km_scr[rs, ls]
        k_bd = jnp.where(diag128, jnp.concatenate([km, km], axis=0), 0.0)
        att_scr[rs, ls] = jnp.where(causal_pair, _dot_nt(qm_scr[rs, ls], k_bd), 0.0).astype(BF16)

    def gla_output(c, p):
        rs = chunk_rows(c)
        ls = slice(p * LANE, (p + 1) * LANE)
        v_pair = vb_scr[rs, p * PAIR_V:(p + 1) * PAIR_V]
        s_pair = sb_scr[c, ls, :]
        w_top = jnp.where(diag_v, jnp.concatenate([v_pair, v_pair], axis=0), 0.0)
        w_bot = jnp.where(diag_v, jnp.concatenate([s_pair, s_pair], axis=1), 0.0)
        o_pair = _dot(jnp.concatenate([att_scr[rs, ls], qi_scr[rs, ls]], axis=1),
                      jnp.concatenate([w_top, w_bot], axis=0))
        for hh in range(2):
            h = 2 * p + hh
            o = o_pair[:, hh * GLA_DV:(hh + 1) * GLA_DV]
            og = proj_scr[rs, C_OG + h * GLA_DV:C_OG + (h + 1) * GLA_DV]
            y = o * _rms_scale(o) * g_gla * (og * _sigmoid(og))
            omix_scr[rs, h * GLA_DV:(h + 1) * GLA_DV] = y.astype(BF16)

    if sample:
        low_win = lax.broadcasted_iota(jnp.int32, (WINDOW, LANE), 1) < LANE // 2
        for c in range(n_chunks):
            kc_dup = _dup_halves(kc_ref[c], low_win)
            vc_dup = _dup_halves(vc_ref[c], low_win)
            for kv in range(SWA_KV_HEADS):
                ls = slice(kv * LANE, (kv + 1) * LANE)
                kd_scr[c * BAND:c * BAND + WINDOW, ls] = kc_dup[kv].astype(BF16)
                vd_scr[c * BAND:c * BAND + WINDOW, ls] = vc_dup[kv].astype(BF16)

    lane_q = lax.broadcasted_iota(jnp.int32, (1, SWA_GRP_Q), 1)
    key_ids = lax.broadcasted_iota(jnp.int32, (BAND, SWA_GRP_Q), 0)
    sink_vecs = []
    for kv in range(SWA_KV_HEADS):
        vec = jnp.full((1, SWA_GRP_Q), sinks_ref[kv * SWA_GROUP + SWA_GROUP - 1], F32)
        for g in range(SWA_GROUP - 2, -1, -1):
            vec = jnp.where(lane_q < (g + 1) * SWA_HD, sinks_ref[kv * SWA_GROUP + g], vec)
        sink_vecs.append(vec)

    def band_rows(c):
        return slice(c * BAND, (c + 1) * BAND) if sample else slice(c * CHUNK, c * CHUNK + BAND)

    def swa_scores(c, kv):
        rs = chunk_rows(c)
        band = band_rows(c)
        ls = slice(kv * LANE, (kv + 1) * LANE)
        qg = (proj_scr[rs, C_QS + kv * SWA_GRP_Q:C_QS + (kv + 1) * SWA_GRP_Q]
              * (SWA_HD ** -0.5)).astype(BF16)
        q_stack = jnp.concatenate(
            [jnp.where(low_half if hh == 0 else ~low_half, qg[:, pp * LANE:(pp + 1) * LANE], 0.0)
             for pp in range(SWA_GROUP // 2) for hh in range(2)], axis=0)
        s_t = _dot_nt(kd_scr[band, ls], q_stack)
        if not sample and c * CHUNK < WINDOW:
            first_valid = WINDOW - (t * rows + c * CHUNK)
            s_t = jnp.where(key_ids >= first_valid, s_t, -jnp.inf)
        sink = sink_vecs[kv]
        m = jnp.maximum(jnp.max(s_t, axis=0, keepdims=True), sink)
        p_t = jnp.exp(s_t - m)
        den = jnp.sum(p_t, axis=0, keepdims=True) + jnp.exp(sink - m)
        pn_scr[c * SWA_KV_HEADS + kv] = (p_t * (1.0 / den)).astype(BF16)

    def swa_output(c, kv):
        rs = chunk_rows(c)
        ls = slice(kv * LANE, (kv + 1) * LANE)
        o_t = _dot_tn(pn_scr[c * SWA_KV_HEADS + kv], vd_scr[band_rows(c), ls])
        for pp in range(SWA_GROUP // 2):
            o_pair = jnp.where(low_half, o_t[(2 * pp) * CHUNK:(2 * pp + 1) * CHUNK, :],
                               o_t[(2 * pp + 1) * CHUNK:(2 * pp + 2) * CHUNK, :])
            col = GLA_V + kv * SWA_GRP_Q + pp * LANE
            omix_scr[rs, col:col + LANE] = o_pair.astype(BF16)

    def out_steps(s):
        ss = sub_slice(s)
        base = s * sub_rows

        def piece(lo, hi):
            mix = _dot(omix_scr[ss, :], w_out_ref[:, lo:hi])
            for mi, rs in norm_groups(s):
                local = slice(rs.start - base, rs.stop - base)
                xo_ref[rs, lo:hi] = x_ref[rs, lo:hi] + mod_row(mi, 2)[:, lo:hi] * mix[local, :]

        return [functools.partial(piece, lo, lo + DENSE_PIECE) for lo in range(0, D_MODEL, DENSE_PIECE)]

    def block_steps(s):
        blocks = range(s * sub_chunks, (s + 1) * sub_chunks)
        pairs = [(c, p) for c in blocks for p in range(GLA_PAIRS)]
        groups = [(c, kv) for c in blocks for kv in range(SWA_KV_HEADS)]
        steps = [functools.partial(gla_prepare, c) for c in blocks]
        steps += [functools.partial(swa_scores, c, kv) for c, kv in groups]
        steps += [functools.partial(gla_increment, c) for c in blocks]
        steps += [functools.partial(gla_scores, c, p) for c, p in pairs]
        steps.append(functools.partial(gla_recurrence, s))
        steps += [functools.partial(swa_output, c, kv) for c, kv in groups]
        steps += [functools.partial(gla_output, c, p) for c, p in pairs]
        return steps

    look_ahead = (not sample) and n_sub > 1
    if look_ahead:
        @pl.when((pl.program_id(0) == 0) & (t == 0))
        def _():
            for step in project_steps(0):
                step()
        ahead_steps = project_steps(0, ahead=True)
        ahead_bands = ahead_steps.pop()
    else:
        for step in project_steps(0):
            step()
        ahead_steps = []
    for s in range(n_sub):
        dense = (project_steps(s + 1) if s + 1 < n_sub else ahead_steps) + (out_steps(s - 1) if s > 0 else [])
        work = block_steps(s)
        issued = 0
        for i, step in enumerate(work):
            while issued < len(dense) and issued * len(work) < (i + 1) * len(dense):
                dense[issued]()
                issued += 1
            step()
        for step in dense[issued:]:
            step()
    for step in out_steps(n_sub - 1):
        step()

    if sample:
        ko_ref[...] = proj_scr[:, C_KS:C_KS + SWA_KV]
        vo_ref[...] = proj_scr[:, C_VS:C_VS + SWA_KV]
    else:
        kd_scr[0:WINDOW, :] = kd_scr[rows:rows + WINDOW, :]
        vd_scr[0:WINDOW, :] = vd_scr[rows:rows + WINDOW, :]

        @pl.when(t == n_t - 1)
        def _():
            so_ref[...] = s_scr[...]
            ko_ref[...] = proj_scr[rows - WINDOW:rows, C_KS:C_KS + SWA_KV]
            vo_ref[...] = proj_scr[rows - WINDOW:rows, C_VS:C_VS + SWA_KV]

        if look_ahead:
            ahead_bands()


def _const_spec(shape):
    zeros = (0,) * len(shape)
    return pl.BlockSpec(shape, lambda *_: zeros, pipeline_mode=pl.Buffered(1))


def _mixer_weight_specs():
    return [
        _const_spec((1, D_MODEL)),
        _const_spec((D_MODEL, D_IN_P)),
        _const_spec((LR_PAD, GLA_QK)),
        _const_spec((1, GLA_QK)),
        _const_spec((1, GLA_DV)),
        pl.BlockSpec(memory_space=pltpu.SMEM),
        _const_spec((D_MIX, D_MODEL)),
    ]


def _mixer_scratch(rows, n_chunks, band_rows):
    state_rows = GLA_HEADS * GLA_DK
    return [
        pltpu.VMEM((rows, D_MODEL), BF16),
        pltpu.VMEM((rows, D_IN_P), F32),
        pltpu.VMEM((rows, D_MIX), BF16),
        pltpu.VMEM((rows, GLA_QK), BF16),
        pltpu.VMEM((rows, GLA_QK), BF16),
        pltpu.VMEM((rows, GLA_QK), BF16),
        pltpu.VMEM((rows, GLA_QK), BF16),
        pltpu.VMEM((rows, GLA_V), BF16),
        pltpu.VMEM((rows, GLA_QK), F32),
        pltpu.VMEM((n_chunks, state_rows, GLA_DV), F32),
        pltpu.VMEM((n_chunks, state_rows, GLA_DV), F32),
        pltpu.VMEM((n_chunks, state_rows, GLA_DV), BF16),
        pltpu.VMEM((rows, GLA_QK), BF16),
        pltpu.VMEM((n_chunks * SWA_KV_HEADS, BAND, SWA_GRP_Q), BF16),
        pltpu.VMEM((band_rows, 2 * LANE), BF16),
        pltpu.VMEM((band_rows, 2 * LANE), BF16),
    ]


def _mixer_prompt_call(x, mod, weights):
    batch, seq, _ = x.shape
    tile = min(MIXER_TILE, seq)
    assert seq % tile == 0 and tile % CHUNK == 0 and tile >= WINDOW
    n_chunks = tile // CHUNK
    n_t = seq // tile
    state_rows = GLA_HEADS * GLA_DK
    sub_rows = min(SUB_CHUNKS, n_chunks) * CHUNK
    n_sub = tile // sub_rows
    assert n_sub == 1 or tile >= sub_rows + WINDOW

    def next_tile(b, t):
        wrap = (t + 1) // n_t
        return jnp.minimum(b + wrap, batch - 1), (t + 1) - wrap * n_t

    return pl.pallas_call(
        functools.partial(_mixer_kernel, False, n_chunks),
        grid=(batch, n_t),
        in_specs=[
            pl.BlockSpec((None, tile, D_MODEL), lambda b, t: (b, t, 0)),
            pl.BlockSpec((None, 6, D_MODEL), lambda b, t: (b, 0, 0)),
            pl.BlockSpec((None, sub_rows, D_MODEL),
                         lambda b, t: (next_tile(b, t)[0], next_tile(b, t)[1] * n_sub, 0)),
            pl.BlockSpec((None, 6, D_MODEL), lambda b, t: (next_tile(b, t)[0], 0, 0)),
        ] + _mixer_weight_specs(),
        out_specs=[
            pl.BlockSpec((None, tile, D_MODEL), lambda b, t: (b, t, 0)),
            pl.BlockSpec((None, state_rows, GLA_DV), lambda b, t: (b, 0, 0)),
            pl.BlockSpec((None, WINDOW, SWA_KV), lambda b, t: (b, 0, 0)),
            pl.BlockSpec((None, WINDOW, SWA_KV), lambda b, t: (b, 0, 0)),
        ],
        out_shape=[
            jax.ShapeDtypeStruct((batch, seq, D_MODEL), F32),
            jax.ShapeDtypeStruct((batch, state_rows, GLA_DV), F32),
            jax.ShapeDtypeStruct((batch, WINDOW, SWA_KV), F32),
            jax.ShapeDtypeStruct((batch, WINDOW, SWA_KV), F32),
        ],
        scratch_shapes=_mixer_scratch(tile, n_chunks, WINDOW + tile) + [
            pltpu.VMEM((state_rows, GLA_DV), F32),
        ],
        compiler_params=pltpu.CompilerParams(
            dimension_semantics=("arbitrary", "arbitrary"), vmem_limit_bytes=VMEM_LIMIT),
        name="mixer_prompt",
    )(x, mod, x, mod, *weights)


def _mixer_sample_call(x, mod, weights, s0, k_cache, v_cache):
    batch, seq, _ = x.shape
    assert seq == CHUNK
    rows = batch * seq
    state_rows = GLA_HEADS * GLA_DK

    def full(shape):
        zeros = (0,) * len(shape)
        return pl.BlockSpec(shape, lambda i: zeros)

    return pl.pallas_call(
        functools.partial(_mixer_kernel, True, batch),
        grid=(1,),
        in_specs=[full((rows, D_MODEL)), full((batch, 6, D_MODEL))] + _mixer_weight_specs() + [
            full((batch, state_rows, GLA_DV)),
            full((batch, WINDOW, SWA_KV)),
            full((batch, WINDOW, SWA_KV)),
        ],
        out_specs=[
            full((rows, D_MODEL)),
            full((batch, state_rows, GLA_DV)),
            full((rows, SWA_KV)),
            full((rows, SWA_KV)),
        ],
        out_shape=[
            jax.ShapeDtypeStruct((rows, D_MODEL), F32),
            jax.ShapeDtypeStruct((batch, state_rows, GLA_DV), F32),
            jax.ShapeDtypeStruct((rows, SWA_KV), F32),
            jax.ShapeDtypeStruct((rows, SWA_KV), F32),
        ],
        scratch_shapes=_mixer_scratch(rows, batch, batch * BAND),
        compiler_params=pltpu.CompilerParams(
            dimension_semantics=("arbitrary",), vmem_limit_bytes=VMEM_LIMIT),
        name="mixer_sample",
    )(x.reshape(rows, D_MODEL), mod, *weights, s0, k_cache, v_cache)


def _ffn_kernel(sample, final, n_seg, seg_len, *refs):
    if sample:
        (x_ref, mod_ref, g_ffn_ref, w_up_ref, conv_w_ref, conv_b_ref, w_down_ref, g_final_ref,
         past_ref, xo_ref, co_ref, h_scr, ub_scr, act_scr) = refs
    else:
        (x_ref, mod_ref, xn_ref, modn_ref,
         g_ffn_ref, w_up_ref, conv_w_ref, conv_b_ref, w_down_ref, g_final_ref,
         xo_ref, co_ref, h_scr, ub_scr, act_scr, past_scr) = refs
        t = pl.program_id(1)
        n_t = pl.num_programs(1)

        @pl.when(t == 0)
        def _():
            past_scr[...] = jnp.zeros_like(past_scr)

    rows = n_seg * seg_len
    stride = seg_len + SUBLANE
    g_ffn = g_ffn_ref[...]

    def mod_row(c, idx):
        if sample:
            return mod_ref[c, idx:idx + 1, :]
        return mod_ref[idx:idx + 1, :]

    def normalise(x, scale, shift):
        return (x * _rms_scale(x) * (g_ffn * (1.0 + scale)) + shift).astype(BF16)

    if sample:
        slot = 0
        for c in range(n_seg):
            rs = slice(c * seg_len, (c + 1) * seg_len)
            h_scr[0, rs, :] = normalise(x_ref[rs, :], mod_row(c, 4), mod_row(c, 3))
    else:
        step = pl.program_id(0) * n_t + t
        slot = lax.rem(step, 2)

        @pl.when(step == 0)
        def _():
            h_scr[0] = normalise(x_ref[...], mod_ref[4:5, :], mod_ref[3:4, :])

    def up(j):
        lo, hi = FF_PARTS[j]
        h = h_scr[slot]
        return _dot(h, w_up_ref[:, lo:hi]), _dot(h, w_up_ref[:, D_FF + lo:D_FF + hi])

    def activate(j, u, val):
        lo, hi = FF_PARTS[j]
        width = hi - lo
        w0 = conv_w_ref[0:1, lo:hi]
        w1 = conv_w_ref[1:2, lo:hi]
        w2 = conv_w_ref[2:3, lo:hi]
        cb = conv_b_ref[:, lo:hi]
        for c in range(n_seg):
            base = c * stride
            rs = slice(c * seg_len, (c + 1) * seg_len)
            if sample:
                ub_scr[base + SUBLANE - 2:base + SUBLANE, 0:width] = past_ref[c, :, lo:hi]
            else:
                ub_scr[base + SUBLANE - 2:base + SUBLANE, 0:width] = past_scr[:, lo:hi]
            u_seg = u[rs, :]
            ub_scr[base + SUBLANE:base + SUBLANE + seg_len, 0:width] = u_seg
            u1 = ub_scr[base + SUBLANE - 1:base + SUBLANE - 1 + seg_len, 0:width]
            u2 = ub_scr[base + SUBLANE - 2:base + SUBLANE - 2 + seg_len, 0:width]
            uc = w0 * u2 + w1 * u1 + w2 * u_seg + cb
            act_scr[rs, lo:hi] = (uc * _sigmoid(uc) * val[rs, :]).astype(BF16)
            tail = ub_scr[base + seg_len + SUBLANE - 2:base + seg_len + SUBLANE, 0:width]
            if sample:
                co_ref[c, :, lo:hi] = tail
            else:
                past_scr[:, lo:hi] = tail

    def down(j):
        lo, hi = FF_PARTS[j]
        return _dot(act_scr[:, lo:hi], w_down_ref[lo:hi, :])

    n_parts = len(FF_PARTS)
    pending = {0: up(0)}
    acc = None
    for j in range(n_parts):
        if j + 1 < n_parts:
            pending[j + 1] = up(j + 1)
        activate(j, *pending.pop(j))
        if j > 0:
            d = down(j - 1)
            acc = d if acc is None else acc + d
        if j == NORM_AHEAD_AFTER_PART and not sample:
            h_scr[1 - slot] = normalise(xn_ref[...], modn_ref[4:5, :], modn_ref[3:4, :])
    acc = acc + down(n_parts - 1)

    for c in range(n_seg):
        rs = slice(c * seg_len, (c + 1) * seg_len)
        y = x_ref[rs, :] + mod_row(c, 5) * acc[rs, :]
        if final:
            y = y * _rms_scale(y) * g_final_ref[...]
        xo_ref[rs, :] = y

    if not sample:
        @pl.when(t == n_t - 1)
        def _():
            co_ref[...] = past_scr[...]


def _ffn_weight_specs():
    return [
        _const_spec((1, D_MODEL)),
        _const_spec((D_MODEL, 2 * D_FF)),
        _const_spec((CONV_W, D_FF)),
        _const_spec((1, D_FF)),
        _const_spec((D_FF, D_MODEL)),
        _const_spec((1, D_MODEL)),
    ]


def _ffn_prompt_call(x, mod, weights, final):
    batch, seq, _ = x.shape
    tile = min(PROMPT_TILE, seq)
    n_t = seq // tile

    def next_tile(b, t):
        wrap = (t + 1) // n_t
        return jnp.minimum(b + wrap, batch - 1), (t + 1) - wrap * n_t

    return pl.pallas_call(
        functools.partial(_ffn_kernel, False, final, 1, tile),
        grid=(batch, n_t),
        in_specs=[
            pl.BlockSpec((None, tile, D_MODEL), lambda b, t: (b, t, 0)),
            pl.BlockSpec((None, 6, D_MODEL), lambda b, t: (b, 0, 0)),
            pl.BlockSpec((None, tile, D_MODEL), lambda b, t: next_tile(b, t) + (0,)),
            pl.BlockSpec((None, 6, D_MODEL), lambda b, t: (next_tile(b, t)[0], 0, 0)),
        ] + _ffn_weight_specs(),
        out_specs=[
            pl.BlockSpec((None, tile, D_MODEL), lambda b, t: (b, t, 0)),
            pl.BlockSpec((None, CONV_W - 1, D_FF), lambda b, t: (b, 0, 0)),
        ],
        out_shape=[
            jax.ShapeDtypeStruct((batch, seq, D_MODEL), F32),
            jax.ShapeDtypeStruct((batch, CONV_W - 1, D_FF), F32),
        ],
        scratch_shapes=[
            pltpu.VMEM((2, tile, D_MODEL), BF16),
            pltpu.VMEM((tile + SUBLANE, FF_PART), F32),
            pltpu.VMEM((tile, D_FF), BF16),
            pltpu.VMEM((CONV_W - 1, D_FF), F32),
        ],
        compiler_params=pltpu.CompilerParams(
            dimension_semantics=("arbitrary", "arbitrary"), vmem_limit_bytes=VMEM_LIMIT),
        name="ffn_prompt",
    )(x, mod, x, mod, *weights)


def _ffn_sample_call(x2d, mod, weights, past, final, batch, seq):
    rows = batch * seq

    def full(shape):
        zeros = (0,) * len(shape)
        return pl.BlockSpec(shape, lambda i: zeros)

    return pl.pallas_call(
        functools.partial(_ffn_kernel, True, final, batch, seq),
        grid=(1,),
        in_specs=[full((rows, D_MODEL)), full((batch, 6, D_MODEL))] + _ffn_weight_specs() + [
            full((batch, CONV_W - 1, D_FF)),
        ],
        out_specs=[full((rows, D_MODEL)), full((batch, CONV_W - 1, D_FF))],
        out_shape=[
            jax.ShapeDtypeStruct((rows, D_MODEL), F32),
            jax.ShapeDtypeStruct((batch, CONV_W - 1, D_FF), F32),
        ],
        scratch_shapes=[
            pltpu.VMEM((1, rows, D_MODEL), BF16),
            pltpu.VMEM((batch * (seq + SUBLANE), FF_PART), F32),
            pltpu.VMEM((rows, D_FF), BF16),
        ],
        compiler_params=pltpu.CompilerParams(
            dimension_semantics=("arbitrary",), vmem_limit_bytes=VMEM_LIMIT),
        name="ffn_sample",
    )(x2d, mod, *weights, past)


def _reorder_w_in(w):
    lr0 = 2 * GLA_QK + 2 * GLA_V
    w = w.astype(BF16)
    pad = jnp.zeros((D_MODEL, LR_PAD - GLA_RANK), BF16)
    return jnp.concatenate([w[:, :lr0], w[:, lr0 + GLA_RANK:], w[:, lr0:lr0 + GLA_RANK], pad], axis=1)


def kernel(x_prompt, x_sample, state_gla, cache_swa_k, cache_swa_v, state_conv, c_prompt, c_sample,
           w_ada, b_ada, g_attn, g_ffn, w_in, w_gk2, b_gk, g_gla, sinks, w_out, w_up, conv_w, conv_b,
           w_down, g_final):
    depth = w_ada.shape[0]
    batch, seq, _ = x_prompt.shape
    dec_batch, dec_seq, _ = x_sample.shape
    state_rows = GLA_HEADS * GLA_DK

    c_all = jnp.concatenate(
        [c_prompt, c_sample, jnp.zeros((ADA_ROWS - batch - dec_batch, D_MODEL), F32)], axis=0)
    mod_all = _ada_call(c_all, w_ada, b_ada).reshape(depth, ADA_ROWS, 6, D_MODEL)

    yp = x_prompt
    ys = x_sample.reshape(dec_batch * dec_seq, D_MODEL)
    outs = [[] for _ in range(8)]
    for i in range(depth):
        mod_p = mod_all[i, :batch]
        mod_s = mod_all[i, batch:batch + dec_batch]
        w_gk2_p = jnp.concatenate(
            [w_gk2[i], jnp.zeros((LR_PAD - GLA_RANK, GLA_QK), F32)], axis=0).astype(BF16)
        mixer_w = (g_attn[i][None], _reorder_w_in(w_in[i]), w_gk2_p, b_gk[i][None],
                   g_gla[i][None], sinks[i], w_out[i].astype(BF16))
        ffn_w = (g_ffn[i][None], w_up[i].astype(BF16), conv_w[i], conv_b[i][None],
                 w_down[i].astype(BF16), g_final[None])
        final = i == depth - 1

        yp, s_p, k_p, v_p = _mixer_prompt_call(yp, mod_p, mixer_w)
        yp, conv_p = _ffn_prompt_call(yp, mod_p, ffn_w, final)

        ys, s_s, k_s, v_s = _mixer_sample_call(
            ys.reshape(dec_batch, dec_seq, D_MODEL), mod_s, mixer_w,
            state_gla[i].reshape(dec_batch, state_rows, GLA_DV),
            cache_swa_k[i].reshape(dec_batch, WINDOW, SWA_KV),
            cache_swa_v[i].reshape(dec_batch, WINDOW, SWA_KV))
        ys, conv_s = _ffn_sample_call(ys, mod_s, ffn_w, state_conv[i], final, dec_batch, dec_seq)

        keep = min(WINDOW, seq)
        outs[0].append(s_p.reshape(batch, GLA_HEADS, GLA_DK, GLA_DV))
        outs[1].append(k_p.reshape(batch, keep, SWA_KV_HEADS, SWA_HD))
        outs[2].append(v_p.reshape(batch, keep, SWA_KV_HEADS, SWA_HD))
        outs[3].append(conv_p)
        outs[4].append(s_s.reshape(dec_batch, GLA_HEADS, GLA_DK, GLA_DV))
        outs[5].append(k_s.reshape(dec_batch, dec_seq, SWA_KV_HEADS, SWA_HD))
        outs[6].append(v_s.reshape(dec_batch, dec_seq, SWA_KV_HEADS, SWA_HD))
        outs[7].append(conv_s)

    return (yp, ys.reshape(dec_batch, dec_seq, D_MODEL)) + tuple(jnp.stack(o) for o in outs)
```

```python
import functools

import jax
import jax.numpy as jnp
from jax import lax
from jax.experimental import pallas as pl
from jax.experimental.pallas import tpu as pltpu

F32 = jnp.float32
BF16 = jnp.bfloat16

D_MODEL = 1024
CHUNK = 64
GLA_HEADS = 4
GLA_DK = 64
GLA_DV = 128
GLA_RANK = 16
GLA_NORMALIZER = 16.0
SWA_Q_HEADS = 8
SWA_KV_HEADS = 2
SWA_GROUP = SWA_Q_HEADS // SWA_KV_HEADS
SWA_HD = 64
WINDOW = 128
D_FF = 2816
CONV_W = 3
RMS_EPS = 1e-6

GLA_QK = GLA_HEADS * GLA_DK
GLA_V = GLA_HEADS * GLA_DV
SWA_Q = SWA_Q_HEADS * SWA_HD
SWA_KV = SWA_KV_HEADS * SWA_HD
D_MIX = GLA_V + SWA_Q
BAND = WINDOW + CHUNK

LANE = 128
SUBLANE = 8

C_QG = 0
C_KG = C_QG + GLA_QK
C_VG = C_KG + GLA_QK
C_OG = C_VG + GLA_V
C_QS = C_OG + GLA_V
C_KS = C_QS + SWA_Q
C_VS = C_KS + SWA_KV
C_LR = C_VS + SWA_KV
LR_PAD = LANE
D_IN_P = C_LR + LR_PAD
D_IN = 2 * GLA_QK + 2 * GLA_V + GLA_RANK + SWA_Q + 2 * SWA_KV

assert GLA_DK * 2 == LANE and SWA_HD * 2 == LANE and GLA_DV == LANE and SWA_KV == LANE
GLA_PAIRS = GLA_HEADS // 2
PAIR_V = 2 * GLA_DV
SWA_GRP_Q = SWA_GROUP * SWA_HD

ADA_TILE = 1536
ADA_ROWS = 16
PROMPT_TILE = 512
MIXER_TILE = 1024
SUB_CHUNKS = 4
DENSE_PIECE = 256
FF_PART = 768
FF_PARTS = tuple((lo, min(lo + FF_PART, D_FF)) for lo in range(0, D_FF, FF_PART))
VMEM_LIMIT = 56 * 1024 * 1024

NT_DIMS = (((1,), (1,)), ((), ()))
TN_DIMS = (((0,), (0,)), ((), ()))


def _dot(a, b):
    return jnp.dot(a, b, preferred_element_type=F32)


def _dot_nt(a, b):
    return lax.dot_general(a, b, NT_DIMS, preferred_element_type=F32)


def _dot_tn(a, b):
    return lax.dot_general(a, b, TN_DIMS, preferred_element_type=F32)


def _split_bf16(a):
    hi = a.astype(BF16)
    lo = (a - hi.astype(F32)).astype(BF16)
    return hi, lo


def _sigmoid(a):
    return 1.0 / (1.0 + jnp.exp(-a))


def _log_sigmoid(a):
    return jnp.minimum(a, 0.0) - jnp.log(1.0 + jnp.exp(-jnp.abs(a)))


def _rms_scale(a):
    return lax.rsqrt(jnp.mean(a * a, axis=-1, keepdims=True) + RMS_EPS)


def _ada_kernel(c_ref, w_ref, b_ref, o_ref):
    c = c_ref[...]
    a = (c * _sigmoid(c)).astype(BF16)
    o_ref[...] = _dot(a, w_ref[...].astype(BF16)) + b_ref[...]


def _ada_call(c_all, w_ada, b_ada):
    depth = w_ada.shape[0]
    n_tiles = (6 * D_MODEL) // ADA_TILE
    return pl.pallas_call(
        _ada_kernel,
        grid=(depth, n_tiles),
        in_specs=[
            pl.BlockSpec((ADA_ROWS, D_MODEL), lambda l, j: (0, 0)),
            pl.BlockSpec((None, D_MODEL, ADA_TILE), lambda l, j: (l, 0, j)),
            pl.BlockSpec((None, 1, ADA_TILE), lambda l, j: (l, 0, j)),
        ],
        out_specs=pl.BlockSpec((None, ADA_ROWS, ADA_TILE), lambda l, j: (l, 0, j)),
        out_shape=jax.ShapeDtypeStruct((depth, ADA_ROWS, 6 * D_MODEL), F32),
        compiler_params=pltpu.CompilerParams(
            dimension_semantics=("arbitrary", "arbitrary"), vmem_limit_bytes=VMEM_LIMIT),
        name="adaln_mod",
    )(c_all, w_ada, b_ada.reshape(depth, 1, 6 * D_MODEL))


def _dup_halves(a, low_half):
    swapped = pltpu.roll(a, LANE // 2, axis=1)
    return jnp.where(low_half, a, swapped), jnp.where(low_half, swapped, a)


def _mixer_kernel(sample, n_chunks, layer, *refs):
    if sample:
        (x_ref, mod_ref, g_attn_ref, w_in_ref, w_gk2_ref, b_gk_ref, g_gla_ref, sinks_ref, w_out_ref,
         s0_ref, kc_ref, vc_ref,
         xo_ref, so_ref, ko_ref, vo_ref,
         h_scr, proj_scr, omix_scr, qm_scr, km_scr, qi_scr, ko_scr, vb_scr, gk_scr, dec_scr, u_scr, sb_scr,
         att_scr, pn_scr,
         kd_scr, vd_scr) = refs
        t = None
    else:
        (x_ref, mod_ref, g_attn_ref, w_in_ref, w_gk2_ref, b_gk_ref, g_gla_ref, sinks_ref, w_out_ref,
         xo_ref, so_ref, ko_ref, vo_ref,
         h_scr, proj_scr, omix_scr, qm_scr, km_scr, qi_scr, ko_scr, vb_scr, gk_scr, dec_scr, u_scr, sb_scr,
         att_scr, pn_scr,
         kd_scr, vd_scr, s_scr) = refs
        t = pl.program_id(1)
        n_t = pl.num_programs(1)

        @pl.when(t == 0)
        def _():
            s_scr[...] = jnp.zeros_like(s_scr)
            kd_scr[0:WINDOW, :] = jnp.zeros((WINDOW, 2 * LANE), BF16)
            vd_scr[0:WINDOW, :] = jnp.zeros((WINDOW, 2 * LANE), BF16)

    rows = n_chunks * CHUNK
    g_attn = g_attn_ref[...]

    def mod_row(c, idx):
        if sample:
            return mod_ref[c, idx:idx + 1, :]
        return mod_ref[idx:idx + 1, :]

    def chunk_rows(c):
        return slice(c * CHUNK, (c + 1) * CHUNK)

    sub_chunks = min(SUB_CHUNKS, n_chunks)
    n_sub = n_chunks // sub_chunks
    sub_rows = sub_chunks * CHUNK

    def sub_slice(s):
        return slice(s * sub_rows, (s + 1) * sub_rows)

    def norm_groups(s):
        if sample:
            return [(c, chunk_rows(c)) for c in range(s * sub_chunks, (s + 1) * sub_chunks)]
        return [(0, sub_slice(s))]

    r64 = lax.broadcasted_iota(jnp.int32, (CHUNK, CHUNK), 0)
    c64 = lax.broadcasted_iota(jnp.int32, (CHUNK, CHUNK), 1)
    tri = jnp.where(r64 >= c64, 1.0, 0.0).astype(BF16)
    tri2 = jnp.concatenate([tri, tri], axis=1)
    l_pair = lax.broadcasted_iota(jnp.int32, (CHUNK, LANE), 1)
    low_half = l_pair < LANE // 2
    causal_heads = (lax.broadcasted_iota(jnp.int32, (CHUNK, GLA_QK), 0)
                    >= (lax.broadcasted_iota(jnp.int32, (CHUNK, GLA_QK), 1) & (CHUNK - 1)))
    diag_heads = ((lax.broadcasted_iota(jnp.int32, (GLA_QK, GLA_QK), 0) // GLA_DK)
                  == (lax.broadcasted_iota(jnp.int32, (GLA_QK, GLA_QK), 1) // GLA_DK))
    diag_v = ((lax.broadcasted_iota(jnp.int32, (LANE, PAIR_V), 0) >= LANE // 2)
              == (lax.broadcasted_iota(jnp.int32, (LANE, PAIR_V), 1) >= GLA_DV))
    g_gla = g_gla_ref[...]
    low_sub = lax.broadcasted_iota(jnp.int32, (sub_rows, LANE), 1) < LANE // 2

    def project_steps(s):
        ss = sub_slice(s)

        def norm():
            for mi, rs in norm_groups(s):
                x = x_ref[rs, :]
                gain = g_attn * (1.0 + mod_row(mi, 1))
                h_scr[rs, :] = (x * _rms_scale(x) * gain + mod_row(mi, 0)).astype(BF16)

        def piece(lo, hi):
            proj_scr[ss, lo:hi] = _dot(h_scr[ss, :], w_in_ref[:, lo:hi])

        def gates():
            lr = proj_scr[ss, C_LR:C_LR + LR_PAD].astype(BF16)
            gk_scr[ss, :] = (_log_sigmoid(_dot(lr, w_gk2_ref[...]) + b_gk_ref[...])
                             * (1.0 / GLA_NORMALIZER))
            vb_scr[ss, :] = proj_scr[ss, C_VG:C_VG + GLA_V].astype(BF16)

        def bands():
            k_dup = _dup_halves(proj_scr[ss, C_KS:C_KS + SWA_KV], low_sub)
            v_dup = _dup_halves(proj_scr[ss, C_VS:C_VS + SWA_KV], low_sub)
            for kv in range(SWA_KV_HEADS):
                ls = slice(kv * LANE, (kv + 1) * LANE)
                if sample:
                    for c in range(s * sub_chunks, (s + 1) * sub_chunks):
                        local = slice((c - s * sub_chunks) * CHUNK, (c - s * sub_chunks + 1) * CHUNK)
                        kd_scr[c * BAND + WINDOW:(c + 1) * BAND, ls] = k_dup[kv][local, :].astype(BF16)
                        vd_scr[c * BAND + WINDOW:(c + 1) * BAND, ls] = v_dup[kv][local, :].astype(BF16)
                else:
                    band_rows = slice(WINDOW + s * sub_rows, WINDOW + (s + 1) * sub_rows)
                    kd_scr[band_rows, ls] = k_dup[kv].astype(BF16)
                    vd_scr[band_rows, ls] = v_dup[kv].astype(BF16)

        steps = [norm]
        for lo in range(0, D_IN_P, DENSE_PIECE):
            steps.append(functools.partial(piece, lo, min(lo + DENSE_PIECE, D_IN_P)))
        return steps + [gates, bands]

    def gla_prepare(c):
        rs = chunk_rows(c)
        gk_hi, gk_lo = _split_bf16(gk_scr[rs, :])
        cum = _dot(tri2, jnp.concatenate([gk_hi, gk_lo], axis=0))
        mid = cum[CHUNK // 2:CHUNK // 2 + 1, :]
        last = cum[CHUNK - 1:CHUNK, :]
        q = proj_scr[rs, C_QG:C_QG + GLA_QK] * (GLA_DK ** -0.5)
        k = proj_scr[rs, C_KG:C_KG + GLA_QK]
        qm_scr[rs, :] = (q * jnp.exp(cum - mid)).astype(BF16)
        km_scr[rs, :] = (k * jnp.exp(mid - cum)).astype(BF16)
        qi_scr[rs, :] = (q * jnp.exp(cum)).astype(BF16)
        ko_scr[rs, :] = (k * jnp.exp(last - cum)).astype(BF16)
        dec_scr[c] = jnp.broadcast_to(jnp.exp(last), (LANE, GLA_QK)).T

    def gla_increment(c):
        rs = chunk_rows(c)
        for p in range(GLA_PAIRS):
            upd = _dot_tn(ko_scr[rs, p * LANE:(p + 1) * LANE], vb_scr[rs, p * PAIR_V:(p + 1) * PAIR_V])
            u_scr[c, p * LANE:p * LANE + GLA_DK, :] = upd[0:GLA_DK, 0:GLA_DV]
            u_scr[c, p * LANE + GLA_DK:(p + 1) * LANE, :] = upd[GLA_DK:LANE, GLA_DV:PAIR_V]

    def gla_recurrence(s):
        state = None if sample else s_scr[...]
        for c in range(s * sub_chunks, (s + 1) * sub_chunks):
            if sample:
                state = s0_ref[c]
            sb_scr[c] = state.astype(BF16)
            state = dec_scr[c] * state + u_scr[c]
            if sample:
                so_ref[c] = state
        if not sample:
            s_scr[...] = state

    def gla_scores(c):
        rs = chunk_rows(c)
        km = km_scr[rs, :]
        k_bd = jnp.where(diag_heads, jnp.concatenate([km] * GLA_HEADS, axis=0), 0.0)
        att_scr[rs, :] = jnp.where(causal_heads, _dot_nt(qm_scr[rs, :], k_bd), 0.0).astype(BF16)

    def gla_output(c, p):
        rs = chunk_rows(c)
        ls = slice(p * LANE, (p + 1) * LANE)
        v_pair = vb_scr[rs, p * PAIR_V:(p + 1) * PAIR_V]
        s_pair = sb_scr[c, ls, :]
        w_top = jnp.where(diag_v, jnp.concatenate([v_pair, v_pair], axis=0), 0.0)
        w_bot = jnp.where(diag_v, jnp.concatenate([s_pair, s_pair], axis=1), 0.0)
        o_pair = _dot(jnp.concatenate([att_scr[rs, ls], qi_scr[rs, ls]], axis=1),
                      jnp.concatenate([w_top, w_bot], axis=0))
        for hh in range(2):
            h = 2 * p + hh
            o = o_pair[:, hh * GLA_DV:(hh + 1) * GLA_DV]
            og = proj_scr[rs, C_OG + h * GLA_DV:C_OG + (h + 1) * GLA_DV]
            y = o * _rms_scale(o) * g_gla * (og * _sigmoid(og))
            omix_scr[rs, h * GLA_DV:(h + 1) * GLA_DV] = y.astype(BF16)

    if sample:
        low_win = lax.broadcasted_iota(jnp.int32, (WINDOW, LANE), 1) < LANE // 2
        for c in range(n_chunks):
            kc_dup = _dup_halves(kc_ref[c], low_win)
            vc_dup = _dup_halves(vc_ref[c], low_win)
            for kv in range(SWA_KV_HEADS):
                ls = slice(kv * LANE, (kv + 1) * LANE)
                kd_scr[c * BAND:c * BAND + WINDOW, ls] = kc_dup[kv].astype(BF16)
                vd_scr[c * BAND:c * BAND + WINDOW, ls] = vc_dup[kv].astype(BF16)

    lane_q = lax.broadcasted_iota(jnp.int32, (1, SWA_GRP_Q), 1)
    key_ids = lax.broadcasted_iota(jnp.int32, (BAND, SWA_GRP_Q), 0)
    sink_vecs = []
    for kv in range(SWA_KV_HEADS):
        vec = jnp.full((1, SWA_GRP_Q), sinks_ref[layer, kv * SWA_GROUP + SWA_GROUP - 1], F32)
        for g in range(SWA_GROUP - 2, -1, -1):
            vec = jnp.where(lane_q < (g + 1) * SWA_HD, sinks_ref[layer, kv * SWA_GROUP + g], vec)
        sink_vecs.append(vec)

    def band_rows(c):
        return slice(c * BAND, (c + 1) * BAND) if sample else slice(c * CHUNK, c * CHUNK + BAND)

    def swa_scores(c, kv):
        rs = chunk_rows(c)
        band = band_rows(c)
        ls = slice(kv * LANE, (kv + 1) * LANE)
        qg = (proj_scr[rs, C_QS + kv * SWA_GRP_Q:C_QS + (kv + 1) * SWA_GRP_Q]
              * (SWA_HD ** -0.5)).astype(BF16)
        q_stack = jnp.concatenate(
            [jnp.where(low_half if hh == 0 else ~low_half, qg[:, pp * LANE:(pp + 1) * LANE], 0.0)
             for pp in range(SWA_GROUP // 2) for hh in range(2)], axis=0)
        s_t = _dot_nt(kd_scr[band, ls], q_stack)
        if not sample and c * CHUNK < WINDOW:
            first_valid = WINDOW - (t * rows + c * CHUNK)
            s_t = jnp.where(key_ids >= first_valid, s_t, -jnp.inf)
        sink = sink_vecs[kv]
        m = jnp.maximum(jnp.max(s_t, axis=0, keepdims=True), sink)
        p_t = jnp.exp(s_t - m)
        den = jnp.sum(p_t, axis=0, keepdims=True) + jnp.exp(sink - m)
        pn_scr[c * SWA_KV_HEADS + kv] = (p_t * (1.0 / den)).astype(BF16)

    def swa_output(c, kv):
        rs = chunk_rows(c)
        ls = slice(kv * LANE, (kv + 1) * LANE)
        o_t = _dot_tn(vd_scr[band_rows(c), ls], pn_scr[c * SWA_KV_HEADS + kv]).T
        for pp in range(SWA_GROUP // 2):
            o_pair = jnp.where(low_half, o_t[(2 * pp) * CHUNK:(2 * pp + 1) * CHUNK, :],
                               o_t[(2 * pp + 1) * CHUNK:(2 * pp + 2) * CHUNK, :])
            col = GLA_V + kv * SWA_GRP_Q + pp * LANE
            omix_scr[rs, col:col + LANE] = o_pair.astype(BF16)

    def out_steps(s):
        ss = sub_slice(s)
        base = s * sub_rows

        def piece(lo, hi):
            mix = _dot(omix_scr[ss, :], w_out_ref[:, lo:hi])
            for mi, rs in norm_groups(s):
                local = slice(rs.start - base, rs.stop - base)
                xo_ref[rs, lo:hi] = x_ref[rs, lo:hi] + mod_row(mi, 2)[:, lo:hi] * mix[local, :]

        return [functools.partial(piece, lo, lo + DENSE_PIECE) for lo in range(0, D_MODEL, DENSE_PIECE)]

    def block_steps(s):
        blocks = range(s * sub_chunks, (s + 1) * sub_chunks)
        pairs = [(c, p) for c in blocks for p in range(GLA_PAIRS)]
        groups = [(c, kv) for c in blocks for kv in range(SWA_KV_HEADS)]
        steps = [functools.partial(gla_prepare, c) for c in blocks]
        steps += [functools.partial(swa_scores, c, kv) for c, kv in groups]
        steps += [functools.partial(gla_increment, c) for c in blocks]
        steps += [functools.partial(gla_scores, c) for c in blocks]
        steps.append(functools.partial(gla_recurrence, s))
        steps += [functools.partial(swa_output, c, kv) for c, kv in groups]
        steps += [functools.partial(gla_output, c, p) for c, p in pairs]
        return steps

    for step in project_steps(0):
        step()
    for s in range(n_sub):
        dense = (project_steps(s + 1) if s + 1 < n_sub else []) + (out_steps(s - 1) if s > 0 else [])
        work = block_steps(s)
        issued = 0
        for i, step in enumerate(work):
            while issued < len(dense) and issued * len(work) < (i + 1) * len(dense):
                dense[issued]()
                issued += 1
            step()
        for step in dense[issued:]:
            step()
    for step in out_steps(n_sub - 1):
        step()

    if sample:
        ko_ref[...] = proj_scr[:, C_KS:C_KS + SWA_KV]
        vo_ref[...] = proj_scr[:, C_VS:C_VS + SWA_KV]
    else:
        kd_scr[0:WINDOW, :] = kd_scr[rows:rows + WINDOW, :]
        vd_scr[0:WINDOW, :] = vd_scr[rows:rows + WINDOW, :]

        @pl.when(t == n_t - 1)
        def _():
            so_ref[...] = s_scr[...]
            ko_ref[...] = proj_scr[rows - WINDOW:rows, C_KS:C_KS + SWA_KV]
            vo_ref[...] = proj_scr[rows - WINDOW:rows, C_VS:C_VS + SWA_KV]


def _const_spec(shape):
    zeros = (0,) * len(shape)
    return pl.BlockSpec(shape, lambda *_: zeros, pipeline_mode=pl.Buffered(1))


def _layer_spec(layer, shape):
    index = (layer,) + (0,) * len(shape)
    return pl.BlockSpec((None,) + shape, lambda *_: index, pipeline_mode=pl.Buffered(1))


def _mixer_weight_specs(layer):
    return [
        _layer_spec(layer, (1, D_MODEL)),
        _layer_spec(layer, (D_MODEL, D_IN_P)),
        _layer_spec(layer, (LR_PAD, GLA_QK)),
        _layer_spec(layer, (1, GLA_QK)),
        _layer_spec(layer, (1, GLA_DV)),
        pl.BlockSpec(memory_space=pltpu.SMEM),
        _layer_spec(layer, (D_MIX, D_MODEL)),
    ]


def _mixer_scratch(rows, n_chunks, band_rows):
    state_rows = GLA_HEADS * GLA_DK
    return [
        pltpu.VMEM((rows, D_MODEL), BF16),
        pltpu.VMEM((rows, D_IN_P), F32),
        pltpu.VMEM((rows, D_MIX), BF16),
        pltpu.VMEM((rows, GLA_QK), BF16),
        pltpu.VMEM((rows, GLA_QK), BF16),
        pltpu.VMEM((rows, GLA_QK), BF16),
        pltpu.VMEM((rows, GLA_QK), BF16),
        pltpu.VMEM((rows, GLA_V), BF16),
        pltpu.VMEM((rows, GLA_QK), F32),
        pltpu.VMEM((n_chunks, state_rows, GLA_DV), F32),
        pltpu.VMEM((n_chunks, state_rows, GLA_DV), F32),
        pltpu.VMEM((n_chunks, state_rows, GLA_DV), BF16),
        pltpu.VMEM((rows, GLA_QK), BF16),
        pltpu.VMEM((n_chunks * SWA_KV_HEADS, BAND, SWA_GRP_Q), BF16),
        pltpu.VMEM((band_rows, 2 * LANE), BF16),
        pltpu.VMEM((band_rows, 2 * LANE), BF16),
    ]


def _mixer_prompt_call(layer, x, mod, mod_row0, weights):
    batch, seq, _ = x.shape
    tile = min(MIXER_TILE, seq)
    assert seq % tile == 0 and tile % CHUNK == 0 and tile >= WINDOW
    n_chunks = tile // CHUNK
    n_t = seq // tile
    state_rows = GLA_HEADS * GLA_DK
    return pl.pallas_call(
        functools.partial(_mixer_kernel, False, n_chunks, layer),
        grid=(batch, n_t),
        in_specs=[
            pl.BlockSpec((None, tile, D_MODEL), lambda b, t: (b, t, 0)),
            pl.BlockSpec((None, None, 6, D_MODEL), lambda b, t: (layer, mod_row0 + b, 0, 0)),
        ] + _mixer_weight_specs(layer),
        out_specs=[
            pl.BlockSpec((None, tile, D_MODEL), lambda b, t: (b, t, 0)),
            pl.BlockSpec((None, state_rows, GLA_DV), lambda b, t: (b, 0, 0)),
            pl.BlockSpec((None, WINDOW, SWA_KV), lambda b, t: (b, 0, 0)),
            pl.BlockSpec((None, WINDOW, SWA_KV), lambda b, t: (b, 0, 0)),
        ],
        out_shape=[
            jax.ShapeDtypeStruct((batch, seq, D_MODEL), F32),
            jax.ShapeDtypeStruct((batch, state_rows, GLA_DV), F32),
            jax.ShapeDtypeStruct((batch, WINDOW, SWA_KV), F32),
            jax.ShapeDtypeStruct((batch, WINDOW, SWA_KV), F32),
        ],
        scratch_shapes=_mixer_scratch(tile, n_chunks, WINDOW + tile) + [
            pltpu.VMEM((state_rows, GLA_DV), F32),
        ],
        compiler_params=pltpu.CompilerParams(
            dimension_semantics=("arbitrary", "arbitrary"), vmem_limit_bytes=VMEM_LIMIT),
        name="mixer_prompt",
    )(x, mod, *weights)


def _mixer_sample_call(layer, x, mod, weights, s0, k_cache, v_cache):
    batch, seq, _ = x.shape
    assert seq == CHUNK
    rows = batch * seq
    state_rows = GLA_HEADS * GLA_DK

    def full(shape):
        zeros = (0,) * len(shape)
        return pl.BlockSpec(shape, lambda i: zeros)

    def of_layer(shape):
        index = (layer,) + (0,) * len(shape)
        return pl.BlockSpec((None,) + shape, lambda i: index)

    return pl.pallas_call(
        functools.partial(_mixer_kernel, True, batch, layer),
        grid=(1,),
        in_specs=[full((rows, D_MODEL)), of_layer((batch, 6, D_MODEL))] + _mixer_weight_specs(layer) + [
            of_layer((batch, state_rows, GLA_DV)),
            of_layer((batch, WINDOW, SWA_KV)),
            of_layer((batch, WINDOW, SWA_KV)),
        ],
        out_specs=[
            full((rows, D_MODEL)),
            full((batch, state_rows, GLA_DV)),
            full((rows, SWA_KV)),
            full((rows, SWA_KV)),
        ],
        out_shape=[
            jax.ShapeDtypeStruct((rows, D_MODEL), F32),
            jax.ShapeDtypeStruct((batch, state_rows, GLA_DV), F32),
            jax.ShapeDtypeStruct((rows, SWA_KV), F32),
            jax.ShapeDtypeStruct((rows, SWA_KV), F32),
        ],
        scratch_shapes=_mixer_scratch(rows, batch, batch * BAND),
        compiler_params=pltpu.CompilerParams(
            dimension_semantics=("arbitrary",), vmem_limit_bytes=VMEM_LIMIT),
        name="mixer_sample",
    )(x.reshape(rows, D_MODEL), mod, *weights, s0, k_cache, v_cache)


def _ffn_kernel(sample, final, n_seg, seg_len, *refs):
    if sample:
        (x_ref, mod_ref, g_ffn_ref, w_up_ref, conv_w_ref, conv_b_ref, w_down_ref, g_final_ref,
         past_ref, xo_ref, co_ref, h_scr, ub_scr, act_scr) = refs
    else:
        (x_ref, mod_ref, g_ffn_ref, w_up_ref, conv_w_ref, conv_b_ref, w_down_ref, g_final_ref,
         xo_ref, co_ref, h_scr, ub_scr, act_scr, past_scr) = refs
        t = pl.program_id(1)
        n_t = pl.num_programs(1)

        @pl.when(t == 0)
        def _():
            past_scr[...] = jnp.zeros_like(past_scr)

    rows = n_seg * seg_len
    stride = seg_len + SUBLANE
    g_ffn = g_ffn_ref[...]

    def mod_row(c, idx):
        if sample:
            return mod_ref[c, idx:idx + 1, :]
        return mod_ref[idx:idx + 1, :]

    for c in range(n_seg):
        rs = slice(c * seg_len, (c + 1) * seg_len)
        x = x_ref[rs, :]
        gain = g_ffn * (1.0 + mod_row(c, 4))
        h_scr[rs, :] = (x * _rms_scale(x) * gain + mod_row(c, 3)).astype(BF16)

    def up(j):
        lo, hi = FF_PARTS[j]
        h = h_scr[...]
        return _dot(h, w_up_ref[:, lo:hi]), _dot(h, w_up_ref[:, D_FF + lo:D_FF + hi])

    def activate(j, u, val):
        lo, hi = FF_PARTS[j]
        width = hi - lo
        w0 = conv_w_ref[0:1, lo:hi]
        w1 = conv_w_ref[1:2, lo:hi]
        w2 = conv_w_ref[2:3, lo:hi]
        cb = conv_b_ref[:, lo:hi]
        for c in range(n_seg):
            base = c * stride
            rs = slice(c * seg_len, (c + 1) * seg_len)
            if sample:
                ub_scr[base + SUBLANE - 2:base + SUBLANE, 0:width] = past_ref[c, :, lo:hi]
            else:
                ub_scr[base + SUBLANE - 2:base + SUBLANE, 0:width] = past_scr[:, lo:hi]
            u_seg = u[rs, :]
            ub_scr[base + SUBLANE:base + SUBLANE + seg_len, 0:width] = u_seg
            u1 = ub_scr[base + SUBLANE - 1:base + SUBLANE - 1 + seg_len, 0:width]
            u2 = ub_scr[base + SUBLANE - 2:base + SUBLANE - 2 + seg_len, 0:width]
            uc = w0 * u2 + w1 * u1 + w2 * u_seg + cb
            act_scr[rs, lo:hi] = (uc * _sigmoid(uc) * val[rs, :]).astype(BF16)
            tail = ub_scr[base + seg_len + SUBLANE - 2:base + seg_len + SUBLANE, 0:width]
            if sample:
                co_ref[c, :, lo:hi] = tail
            else:
                past_scr[:, lo:hi] = tail

    def down(j):
        lo, hi = FF_PARTS[j]
        return _dot(act_scr[:, lo:hi], w_down_ref[lo:hi, :])

    n_parts = len(FF_PARTS)
    pending = {0: up(0)}
    acc = None
    for j in range(n_parts):
        if j + 1 < n_parts:
            pending[j + 1] = up(j + 1)
        activate(j, *pending.pop(j))
        if j > 0:
            d = down(j - 1)
            acc = d if acc is None else acc + d
    acc = acc + down(n_parts - 1)

    for c in range(n_seg):
        rs = slice(c * seg_len, (c + 1) * seg_len)
        y = x_ref[rs, :] + mod_row(c, 5) * acc[rs, :]
        if final:
            y = y * _rms_scale(y) * g_final_ref[...]
        xo_ref[rs, :] = y

    if not sample:
        @pl.when(t == n_t - 1)
        def _():
            co_ref[...] = past_scr[...]


def _ffn_weight_specs(layer):
    return [
        _layer_spec(layer, (1, D_MODEL)),
        _layer_spec(layer, (D_MODEL, 2 * D_FF)),
        _layer_spec(layer, (CONV_W, D_FF)),
        _layer_spec(layer, (1, D_FF)),
        _layer_spec(layer, (D_FF, D_MODEL)),
        _const_spec((1, D_MODEL)),
    ]


def _ffn_prompt_call(layer, x, mod, mod_row0, weights, final):
    batch, seq, _ = x.shape
    tile = min(PROMPT_TILE, seq)
    n_t = seq // tile
    return pl.pallas_call(
        functools.partial(_ffn_kernel, False, final, 1, tile),
        grid=(batch, n_t),
        in_specs=[
            pl.BlockSpec((None, tile, D_MODEL), lambda b, t: (b, t, 0)),
            pl.BlockSpec((None, None, 6, D_MODEL), lambda b, t: (layer, mod_row0 + b, 0, 0)),
        ] + _ffn_weight_specs(layer),
        out_specs=[
            pl.BlockSpec((None, tile, D_MODEL), lambda b, t: (b, t, 0)),
            pl.BlockSpec((None, CONV_W - 1, D_FF), lambda b, t: (b, 0, 0)),
        ],
        out_shape=[
            jax.ShapeDtypeStruct((batch, seq, D_MODEL), F32),
            jax.ShapeDtypeStruct((batch, CONV_W - 1, D_FF), F32),
        ],
        scratch_shapes=[
            pltpu.VMEM((tile, D_MODEL), BF16),
            pltpu.VMEM((tile + SUBLANE, FF_PART), F32),
            pltpu.VMEM((tile, D_FF), BF16),
            pltpu.VMEM((CONV_W - 1, D_FF), F32),
        ],
        compiler_params=pltpu.CompilerParams(
            dimension_semantics=("arbitrary", "arbitrary"), vmem_limit_bytes=VMEM_LIMIT),
        name="ffn_prompt",
    )(x, mod, *weights)


def _ffn_sample_call(layer, x2d, mod, weights, past, final, batch, seq):
    rows = batch * seq

    def full(shape):
        zeros = (0,) * len(shape)
        return pl.BlockSpec(shape, lambda i: zeros)

    def of_layer(shape):
        index = (layer,) + (0,) * len(shape)
        return pl.BlockSpec((None,) + shape, lambda i: index)

    return pl.pallas_call(
        functools.partial(_ffn_kernel, True, final, batch, seq),
        grid=(1,),
        in_specs=[full((rows, D_MODEL)), of_layer((batch, 6, D_MODEL))] + _ffn_weight_specs(layer) + [
            of_layer((batch, CONV_W - 1, D_FF)),
        ],
        out_specs=[full((rows, D_MODEL)), full((batch, CONV_W - 1, D_FF))],
        out_shape=[
            jax.ShapeDtypeStruct((rows, D_MODEL), F32),
            jax.ShapeDtypeStruct((batch, CONV_W - 1, D_FF), F32),
        ],
        scratch_shapes=[
            pltpu.VMEM((rows, D_MODEL), BF16),
            pltpu.VMEM((batch * (seq + SUBLANE), FF_PART), F32),
            pltpu.VMEM((rows, D_FF), BF16),
        ],
        compiler_params=pltpu.CompilerParams(
            dimension_semantics=("arbitrary",), vmem_limit_bytes=VMEM_LIMIT),
        name="ffn_sample",
    )(x2d, mod, *weights, past)


def _reorder_w_in(w):
    lr0 = 2 * GLA_QK + 2 * GLA_V
    w = w.astype(BF16)
    pad = jnp.zeros(w.shape[:-1] + (LR_PAD - GLA_RANK,), BF16)
    return jnp.concatenate(
        [w[..., :lr0], w[..., lr0 + GLA_RANK:], w[..., lr0:lr0 + GLA_RANK], pad], axis=-1)


def kernel(x_prompt, x_sample, state_gla, cache_swa_k, cache_swa_v, state_conv, c_prompt, c_sample,
           w_ada, b_ada, g_attn, g_ffn, w_in, w_gk2, b_gk, g_gla, sinks, w_out, w_up, conv_w, conv_b,
           w_down, g_final):
    depth = w_ada.shape[0]
    batch, seq, _ = x_prompt.shape
    dec_batch, dec_seq, _ = x_sample.shape
    state_rows = GLA_HEADS * GLA_DK

    c_all = jnp.concatenate(
        [c_sample, c_prompt, jnp.zeros((ADA_ROWS - batch - dec_batch, D_MODEL), F32)], axis=0)
    mod_all = _ada_call(c_all, w_ada, b_ada).reshape(depth, ADA_ROWS, 6, D_MODEL)

    w_gk2_p = jnp.concatenate(
        [w_gk2, jnp.zeros((depth, LR_PAD - GLA_RANK, GLA_QK), F32)], axis=1).astype(BF16)
    mixer_w = (g_attn[:, None], _reorder_w_in(w_in), w_gk2_p, b_gk[:, None], g_gla[:, None], sinks,
               w_out.astype(BF16))
    ffn_w = (g_ffn[:, None], w_up.astype(BF16), conv_w, conv_b[:, None], w_down.astype(BF16),
             g_final[None])
    s0_all = state_gla.reshape(depth, dec_batch, state_rows, GLA_DV)
    kc_all = cache_swa_k.reshape(depth, dec_batch, WINDOW, SWA_KV)
    vc_all = cache_swa_v.reshape(depth, dec_batch, WINDOW, SWA_KV)

    yp = x_prompt
    ys = x_sample.reshape(dec_batch * dec_seq, D_MODEL)
    outs = [[] for _ in range(8)]
    for i in range(depth):
        final = i == depth - 1

        yp, s_p, k_p, v_p = _mixer_prompt_call(i, yp, mod_all, dec_batch, mixer_w)
        yp, conv_p = _ffn_prompt_call(i, yp, mod_all, dec_batch, ffn_w, final)

        ys, s_s, k_s, v_s = _mixer_sample_call(
            i, ys.reshape(dec_batch, dec_seq, D_MODEL), mod_all, mixer_w, s0_all, kc_all, vc_all)
        ys, conv_s = _ffn_sample_call(i, ys, mod_all, ffn_w, state_conv, final, dec_batch, dec_seq)

        keep = min(WINDOW, seq)
        outs[0].append(s_p.reshape(batch, GLA_HEADS, GLA_DK, GLA_DV))
        outs[1].append(k_p.reshape(batch, keep, SWA_KV_HEADS, SWA_HD))
        outs[2].append(v_p.reshape(batch, keep, SWA_KV_HEADS, SWA_HD))
        outs[3].append(conv_p)
        outs[4].append(s_s.reshape(dec_batch, GLA_HEADS, GLA_DK, GLA_DV))
        outs[5].append(k_s.reshape(dec_batch, dec_seq, SWA_KV_HEADS, SWA_HD))
        outs[6].append(v_s.reshape(dec_batch, dec_seq, SWA_KV_HEADS, SWA_HD))
        outs[7].append(conv_s)

    return (yp, ys.reshape(dec_batch, dec_seq, D_MODEL)) + tuple(jnp.stack(o) for o in outs)
```

```python
import functools

import jax
import jax.numpy as jnp
from jax import lax
from jax.experimental import pallas as pl
from jax.experimental.pallas import tpu as pltpu

F32 = jnp.float32
BF16 = jnp.bfloat16

D_MODEL = 1024
CHUNK = 64
GLA_HEADS = 4
GLA_DK = 64
GLA_DV = 128
GLA_RANK = 16
GLA_NORMALIZER = 16.0
SWA_Q_HEADS = 8
SWA_KV_HEADS = 2
SWA_GROUP = SWA_Q_HEADS // SWA_KV_HEADS
SWA_HD = 64
WINDOW = 128
D_FF = 2816
CONV_W = 3
RMS_EPS = 1e-6

GLA_QK = GLA_HEADS * GLA_DK
GLA_V = GLA_HEADS * GLA_DV
SWA_Q = SWA_Q_HEADS * SWA_HD
SWA_KV = SWA_KV_HEADS * SWA_HD
D_MIX = GLA_V + SWA_Q
BAND = WINDOW + CHUNK

LANE = 128
SUBLANE = 8

C_QG = 0
C_KG = C_QG + GLA_QK
C_VG = C_KG + GLA_QK
C_OG = C_VG + GLA_V
C_LR = C_OG + GLA_V
LR_PAD = LANE
C_QS = C_LR + LR_PAD
C_KS = C_QS + SWA_Q
C_VS = C_KS + SWA_KV
D_IN_P = C_VS + SWA_KV
D_IN = 2 * GLA_QK + 2 * GLA_V + GLA_RANK + SWA_Q + 2 * SWA_KV

assert GLA_DK * 2 == LANE and SWA_HD * 2 == LANE and GLA_DV == LANE and SWA_KV == LANE
GLA_PAIRS = GLA_HEADS // 2
PAIR_V = 2 * GLA_DV
SWA_GRP_Q = SWA_GROUP * SWA_HD

ADA_TILE = 1536
ADA_ROWS = 16
PROMPT_TILE = 512
MIXER_TILE = 1024
SUB_CHUNKS = 4
DENSE_PIECE = 256
W_PREP_ROWS = 256
FF_PART = 768
FF_PARTS = tuple((lo, min(lo + FF_PART, D_FF)) for lo in range(0, D_FF, FF_PART))
VMEM_LIMIT = 56 * 1024 * 1024

NT_DIMS = (((1,), (1,)), ((), ()))
TN_DIMS = (((0,), (0,)), ((), ()))


def _dot(a, b):
    return jnp.dot(a, b, preferred_element_type=F32)


def _dot_nt(a, b):
    return lax.dot_general(a, b, NT_DIMS, preferred_element_type=F32)


def _dot_tn(a, b):
    return lax.dot_general(a, b, TN_DIMS, preferred_element_type=F32)


def _split_bf16(a):
    hi = a.astype(BF16)
    lo = (a - hi.astype(F32)).astype(BF16)
    return hi, lo


def _sigmoid(a):
    return 1.0 / (1.0 + jnp.exp(-a))


def _log_sigmoid(a):
    return jnp.minimum(a, 0.0) - jnp.log(1.0 + jnp.exp(-jnp.abs(a)))


def _rms_scale(a):
    return lax.rsqrt(jnp.mean(a * a, axis=-1, keepdims=True) + RMS_EPS)


def _ada_kernel(c_ref, w_ref, b_ref, o_ref):
    c = c_ref[...]
    a = (c * _sigmoid(c)).astype(BF16)
    o_ref[...] = _dot(a, w_ref[...].astype(BF16)) + b_ref[...]


def _ada_call(c_all, w_ada, b_ada):
    depth = w_ada.shape[0]
    n_tiles = (6 * D_MODEL) // ADA_TILE
    return pl.pallas_call(
        _ada_kernel,
        grid=(depth, n_tiles),
        in_specs=[
            pl.BlockSpec((ADA_ROWS, D_MODEL), lambda l, j: (0, 0)),
            pl.BlockSpec((None, D_MODEL, ADA_TILE), lambda l, j: (l, 0, j)),
            pl.BlockSpec((None, 1, ADA_TILE), lambda l, j: (l, 0, j)),
        ],
        out_specs=pl.BlockSpec((None, ADA_ROWS, ADA_TILE), lambda l, j: (l, 0, j)),
        out_shape=jax.ShapeDtypeStruct((depth, ADA_ROWS, 6 * D_MODEL), F32),
        compiler_params=pltpu.CompilerParams(
            dimension_semantics=("arbitrary", "arbitrary"), vmem_limit_bytes=VMEM_LIMIT),
        name="adaln_mod",
    )(c_all, w_ada, b_ada.reshape(depth, 1, 6 * D_MODEL))


def _dup_halves(a, low_half):
    swapped = pltpu.roll(a, LANE // 2, axis=1)
    return jnp.where(low_half, a, swapped), jnp.where(low_half, swapped, a)


def _mixer_kernel(sample, n_chunks, layer, *refs):
    if sample:
        (x_ref, mod_ref, g_attn_ref, w_in_ref, w_gk2_ref, b_gk_ref, g_gla_ref, sinks_ref, w_out_ref,
         s0_ref, kc_ref, vc_ref,
         xo_ref, so_ref, ko_ref, vo_ref,
         h_scr, proj_scr, omix_scr, qm_scr, km_scr, qi_scr, ko_scr, vb_scr, gk_scr, dec_scr, u_scr, sb_scr,
         att_scr, pn_scr,
         kd_scr, vd_scr) = refs
        t = None
    else:
        (x_ref, mod_ref, g_attn_ref, w_in_ref, w_gk2_ref, b_gk_ref, g_gla_ref, sinks_ref, w_out_ref,
         xo_ref, so_ref, ko_ref, vo_ref,
         h_scr, proj_scr, omix_scr, qm_scr, km_scr, qi_scr, ko_scr, vb_scr, gk_scr, dec_scr, u_scr, sb_scr,
         att_scr, pn_scr,
         kd_scr, vd_scr, s_scr) = refs
        t = pl.program_id(1)
        n_t = pl.num_programs(1)

        @pl.when(t == 0)
        def _():
            s_scr[...] = jnp.zeros_like(s_scr)
            kd_scr[0:WINDOW, :] = jnp.zeros((WINDOW, 2 * LANE), BF16)
            vd_scr[0:WINDOW, :] = jnp.zeros((WINDOW, 2 * LANE), BF16)

    rows = n_chunks * CHUNK
    g_attn = g_attn_ref[...]

    def mod_row(c, idx):
        if sample:
            return mod_ref[c, idx:idx + 1, :]
        return mod_ref[idx:idx + 1, :]

    def chunk_rows(c):
        return slice(c * CHUNK, (c + 1) * CHUNK)

    sub_chunks = min(SUB_CHUNKS, n_chunks)
    n_sub = n_chunks // sub_chunks
    sub_rows = sub_chunks * CHUNK

    def sub_slice(s):
        return slice(s * sub_rows, (s + 1) * sub_rows)

    def norm_groups(s):
        if sample:
            return [(c, chunk_rows(c)) for c in range(s * sub_chunks, (s + 1) * sub_chunks)]
        return [(0, sub_slice(s))]

    r64 = lax.broadcasted_iota(jnp.int32, (CHUNK, CHUNK), 0)
    c64 = lax.broadcasted_iota(jnp.int32, (CHUNK, CHUNK), 1)
    tri = jnp.where(r64 >= c64, 1.0, 0.0).astype(BF16)
    tri2 = jnp.concatenate([tri, tri], axis=1)
    l_pair = lax.broadcasted_iota(jnp.int32, (CHUNK, LANE), 1)
    low_half = l_pair < LANE // 2
    causal_heads = (lax.broadcasted_iota(jnp.int32, (CHUNK, GLA_QK), 0)
                    >= (lax.broadcasted_iota(jnp.int32, (CHUNK, GLA_QK), 1) & (CHUNK - 1)))
    diag_heads = ((lax.broadcasted_iota(jnp.int32, (GLA_QK, GLA_QK), 0) // GLA_DK)
                  == (lax.broadcasted_iota(jnp.int32, (GLA_QK, GLA_QK), 1) // GLA_DK))
    diag_v = ((lax.broadcasted_iota(jnp.int32, (LANE, PAIR_V), 0) >= LANE // 2)
              == (lax.broadcasted_iota(jnp.int32, (LANE, PAIR_V), 1) >= GLA_DV))
    g_gla = g_gla_ref[...]
    low_sub = lax.broadcasted_iota(jnp.int32, (sub_rows, LANE), 1) < LANE // 2

    def project_steps(s):
        ss = sub_slice(s)

        def norm():
            for mi, rs in norm_groups(s):
                x = x_ref[rs, :]
                gain = g_attn * (1.0 + mod_row(mi, 1))
                h_scr[rs, :] = (x * _rms_scale(x) * gain + mod_row(mi, 0)).astype(BF16)

        def piece(lo, hi):
            proj_scr[ss, lo:hi] = _dot(h_scr[ss, :], w_in_ref[:, lo:hi])

        def gates():
            lr = proj_scr[ss, C_LR:C_LR + LR_PAD].astype(BF16)
            gk_scr[ss, :] = (_log_sigmoid(_dot(lr, w_gk2_ref[...]) + b_gk_ref[...])
                             * (1.0 / GLA_NORMALIZER))
            vb_scr[ss, :] = proj_scr[ss, C_VG:C_VG + GLA_V].astype(BF16)

        def bands():
            k_dup = _dup_halves(proj_scr[ss, C_KS:C_KS + SWA_KV], low_sub)
            v_dup = _dup_halves(proj_scr[ss, C_VS:C_VS + SWA_KV], low_sub)
            for kv in range(SWA_KV_HEADS):
                ls = slice(kv * LANE, (kv + 1) * LANE)
                if sample:
                    for c in range(s * sub_chunks, (s + 1) * sub_chunks):
                        local = slice((c - s * sub_chunks) * CHUNK, (c - s * sub_chunks + 1) * CHUNK)
                        kd_scr[c * BAND + WINDOW:(c + 1) * BAND, ls] = k_dup[kv][local, :].astype(BF16)
                        vd_scr[c * BAND + WINDOW:(c + 1) * BAND, ls] = v_dup[kv][local, :].astype(BF16)
                else:
                    band_rows = slice(WINDOW + s * sub_rows, WINDOW + (s + 1) * sub_rows)
                    kd_scr[band_rows, ls] = k_dup[kv].astype(BF16)
                    vd_scr[band_rows, ls] = v_dup[kv].astype(BF16)

        steps = [norm]
        for lo in range(0, D_IN_P, DENSE_PIECE):
            steps.append(functools.partial(piece, lo, min(lo + DENSE_PIECE, D_IN_P)))
        return steps + [gates, bands]

    def gla_prepare(c):
        rs = chunk_rows(c)
        gk_hi, gk_lo = _split_bf16(gk_scr[rs, :])
        cum = _dot(tri2, jnp.concatenate([gk_hi, gk_lo], axis=0))
        mid = cum[CHUNK // 2:CHUNK // 2 + 1, :]
        last = cum[CHUNK - 1:CHUNK, :]
        q = proj_scr[rs, C_QG:C_QG + GLA_QK] * (GLA_DK ** -0.5)
        k = proj_scr[rs, C_KG:C_KG + GLA_QK]
        qm_scr[rs, :] = (q * jnp.exp(cum - mid)).astype(BF16)
        km_scr[rs, :] = (k * jnp.exp(mid - cum)).astype(BF16)
        qi_scr[rs, :] = (q * jnp.exp(cum)).astype(BF16)
        ko_scr[rs, :] = (k * jnp.exp(last - cum)).astype(BF16)
        dec_scr[c] = jnp.broadcast_to(jnp.exp(last), (LANE, GLA_QK)).T

    def gla_increment(c):
        rs = chunk_rows(c)
        for p in range(GLA_PAIRS):
            upd = _dot_tn(ko_scr[rs, p * LANE:(p + 1) * LANE], vb_scr[rs, p * PAIR_V:(p + 1) * PAIR_V])
            u_scr[c, p * LANE:p * LANE + GLA_DK, :] = upd[0:GLA_DK, 0:GLA_DV]
            u_scr[c, p * LANE + GLA_DK:(p + 1) * LANE, :] = upd[GLA_DK:LANE, GLA_DV:PAIR_V]

    def gla_recurrence(s):
        state = None if sample else s_scr[...]
        for c in range(s * sub_chunks, (s + 1) * sub_chunks):
            if sample:
                state = s0_ref[c]
            sb_scr[c] = state.astype(BF16)
            state = dec_scr[c] * state + u_scr[c]
            if sample:
                so_ref[c] = state
        if not sample:
            s_scr[...] = state

    def gla_scores(c):
        rs = chunk_rows(c)
        km = km_scr[rs, :]
        k_bd = jnp.where(diag_heads, jnp.concatenate([km] * GLA_HEADS, axis=0), 0.0)
        att_scr[rs, :] = jnp.where(causal_heads, _dot_nt(qm_scr[rs, :], k_bd), 0.0).astype(BF16)

    def gla_output(c, p):
        rs = chunk_rows(c)
        ls = slice(p * LANE, (p + 1) * LANE)
        v_pair = vb_scr[rs, p * PAIR_V:(p + 1) * PAIR_V]
        s_pair = sb_scr[c, ls, :]
        w_top = jnp.where(diag_v, jnp.concatenate([v_pair, v_pair], axis=0), 0.0)
        w_bot = jnp.where(diag_v, jnp.concatenate([s_pair, s_pair], axis=1), 0.0)
        o_pair = _dot(jnp.concatenate([att_scr[rs, ls], qi_scr[rs, ls]], axis=1),
                      jnp.concatenate([w_top, w_bot], axis=0))
        for hh in range(2):
            h = 2 * p + hh
            o = o_pair[:, hh * GLA_DV:(hh + 1) * GLA_DV]
            og = proj_scr[rs, C_OG + h * GLA_DV:C_OG + (h + 1) * GLA_DV]
            y = o * _rms_scale(o) * g_gla * (og * _sigmoid(og))
            omix_scr[rs, h * GLA_DV:(h + 1) * GLA_DV] = y.astype(BF16)

    if sample:
        low_win = lax.broadcasted_iota(jnp.int32, (WINDOW, LANE), 1) < LANE // 2
        for c in range(n_chunks):
            kc_dup = _dup_halves(kc_ref[c], low_win)
            vc_dup = _dup_halves(vc_ref[c], low_win)
            for kv in range(SWA_KV_HEADS):
                ls = slice(kv * LANE, (kv + 1) * LANE)
                kd_scr[c * BAND:c * BAND + WINDOW, ls] = kc_dup[kv].astype(BF16)
                vd_scr[c * BAND:c * BAND + WINDOW, ls] = vc_dup[kv].astype(BF16)

    lane_q = lax.broadcasted_iota(jnp.int32, (1, SWA_GRP_Q), 1)
    key_ids = lax.broadcasted_iota(jnp.int32, (BAND, SWA_GRP_Q), 0)
    sink_vecs = []
    for kv in range(SWA_KV_HEADS):
        vec = jnp.full((1, SWA_GRP_Q), sinks_ref[layer, kv * SWA_GROUP + SWA_GROUP - 1], F32)
        for g in range(SWA_GROUP - 2, -1, -1):
            vec = jnp.where(lane_q < (g + 1) * SWA_HD, sinks_ref[layer, kv * SWA_GROUP + g], vec)
        sink_vecs.append(vec)

    def band_rows(c):
        return slice(c * BAND, (c + 1) * BAND) if sample else slice(c * CHUNK, c * CHUNK + BAND)

    def swa_scores(c, kv):
        rs = chunk_rows(c)
        band = band_rows(c)
        ls = slice(kv * LANE, (kv + 1) * LANE)
        qg = (proj_scr[rs, C_QS + kv * SWA_GRP_Q:C_QS + (kv + 1) * SWA_GRP_Q]
              * (SWA_HD ** -0.5)).astype(BF16)
        q_stack = jnp.concatenate(
            [jnp.where(low_half if hh == 0 else ~low_half, qg[:, pp * LANE:(pp + 1) * LANE], 0.0)
             for pp in range(SWA_GROUP // 2) for hh in range(2)], axis=0)
        s_t = _dot_nt(kd_scr[band, ls], q_stack)
        if not sample and c * CHUNK < WINDOW:
            first_valid = WINDOW - (t * rows + c * CHUNK)
            s_t = jnp.where(key_ids >= first_valid, s_t, -jnp.inf)
        sink = sink_vecs[kv]
        m = jnp.maximum(jnp.max(s_t, axis=0, keepdims=True), sink)
        p_t = jnp.exp(s_t - m)
        den = jnp.sum(p_t, axis=0, keepdims=True) + jnp.exp(sink - m)
        pn_scr[c * SWA_KV_HEADS + kv] = (p_t * (1.0 / den)).astype(BF16)

    def swa_output(c, kv):
        rs = chunk_rows(c)
        ls = slice(kv * LANE, (kv + 1) * LANE)
        o_t = _dot_tn(vd_scr[band_rows(c), ls], pn_scr[c * SWA_KV_HEADS + kv]).T
        for pp in range(SWA_GROUP // 2):
            o_pair = jnp.where(low_half, o_t[(2 * pp) * CHUNK:(2 * pp + 1) * CHUNK, :],
                               o_t[(2 * pp + 1) * CHUNK:(2 * pp + 2) * CHUNK, :])
            col = GLA_V + kv * SWA_GRP_Q + pp * LANE
            omix_scr[rs, col:col + LANE] = o_pair.astype(BF16)

    def out_steps(s):
        ss = sub_slice(s)
        base = s * sub_rows

        def piece(lo, hi):
            mix = _dot(omix_scr[ss, :], w_out_ref[:, lo:hi])
            for mi, rs in norm_groups(s):
                local = slice(rs.start - base, rs.stop - base)
                xo_ref[rs, lo:hi] = x_ref[rs, lo:hi] + mod_row(mi, 2)[:, lo:hi] * mix[local, :]

        return [functools.partial(piece, lo, lo + DENSE_PIECE) for lo in range(0, D_MODEL, DENSE_PIECE)]

    def block_steps(s):
        blocks = range(s * sub_chunks, (s + 1) * sub_chunks)
        pairs = [(c, p) for c in blocks for p in range(GLA_PAIRS)]
        groups = [(c, kv) for c in blocks for kv in range(SWA_KV_HEADS)]
        steps = [functools.partial(gla_prepare, c) for c in blocks]
        steps += [functools.partial(swa_scores, c, kv) for c, kv in groups]
        steps += [functools.partial(gla_increment, c) for c in blocks]
        steps += [functools.partial(gla_scores, c) for c in blocks]
        steps.append(functools.partial(gla_recurrence, s))
        steps += [functools.partial(swa_output, c, kv) for c, kv in groups]
        steps += [functools.partial(gla_output, c, p) for c, p in pairs]
        return steps

    for step in project_steps(0):
        step()
    for s in range(n_sub):
        dense = (project_steps(s + 1) if s + 1 < n_sub else []) + (out_steps(s - 1) if s > 0 else [])
        work = block_steps(s)
        issued = 0
        for i, step in enumerate(work):
            while issued < len(dense) and issued * len(work) < (i + 1) * len(dense):
                dense[issued]()
                issued += 1
            step()
        for step in dense[issued:]:
            step()
    for step in out_steps(n_sub - 1):
        step()

    if sample:
        ko_ref[...] = proj_scr[:, C_KS:C_KS + SWA_KV]
        vo_ref[...] = proj_scr[:, C_VS:C_VS + SWA_KV]
    else:
        kd_scr[0:WINDOW, :] = kd_scr[rows:rows + WINDOW, :]
        vd_scr[0:WINDOW, :] = vd_scr[rows:rows + WINDOW, :]

        @pl.when(t == n_t - 1)
        def _():
            so_ref[...] = s_scr[...]
            ko_ref[...] = proj_scr[rows - WINDOW:rows, C_KS:C_KS + SWA_KV]
            vo_ref[...] = proj_scr[rows - WINDOW:rows, C_VS:C_VS + SWA_KV]


def _const_spec(shape):
    zeros = (0,) * len(shape)
    return pl.BlockSpec(shape, lambda *_: zeros, pipeline_mode=pl.Buffered(1))


def _layer_spec(layer, shape):
    index = (layer,) + (0,) * len(shape)
    return pl.BlockSpec((None,) + shape, lambda *_: index, pipeline_mode=pl.Buffered(1))


def _mixer_weight_specs(layer):
    return [
        _layer_spec(layer, (1, D_MODEL)),
        _layer_spec(layer, (D_MODEL, D_IN_P)),
        _layer_spec(layer, (LR_PAD, GLA_QK)),
        _layer_spec(layer, (1, GLA_QK)),
        _layer_spec(layer, (1, GLA_DV)),
        pl.BlockSpec(memory_space=pltpu.SMEM),
        _layer_spec(layer, (D_MIX, D_MODEL)),
    ]


def _mixer_scratch(rows, n_chunks, band_rows):
    state_rows = GLA_HEADS * GLA_DK
    return [
        pltpu.VMEM((rows, D_MODEL), BF16),
        pltpu.VMEM((rows, D_IN_P), F32),
        pltpu.VMEM((rows, D_MIX), BF16),
        pltpu.VMEM((rows, GLA_QK), BF16),
        pltpu.VMEM((rows, GLA_QK), BF16),
        pltpu.VMEM((rows, GLA_QK), BF16),
        pltpu.VMEM((rows, GLA_QK), BF16),
        pltpu.VMEM((rows, GLA_V), BF16),
        pltpu.VMEM((rows, GLA_QK), F32),
        pltpu.VMEM((n_chunks, state_rows, GLA_DV), F32),
        pltpu.VMEM((n_chunks, state_rows, GLA_DV), F32),
        pltpu.VMEM((n_chunks, state_rows, GLA_DV), BF16),
        pltpu.VMEM((rows, GLA_QK), BF16),
        pltpu.VMEM((n_chunks * SWA_KV_HEADS, BAND, SWA_GRP_Q), BF16),
        pltpu.VMEM((band_rows, 2 * LANE), BF16),
        pltpu.VMEM((band_rows, 2 * LANE), BF16),
    ]


def _mixer_prompt_call(layer, x, mod, mod_row0, weights):
    batch, seq, _ = x.shape
    tile = min(MIXER_TILE, seq)
    assert seq % tile == 0 and tile % CHUNK == 0 and tile >= WINDOW
    n_chunks = tile // CHUNK
    n_t = seq // tile
    state_rows = GLA_HEADS * GLA_DK
    return pl.pallas_call(
        functools.partial(_mixer_kernel, False, n_chunks, layer),
        grid=(batch, n_t),
        in_specs=[
            pl.BlockSpec((None, tile, D_MODEL), lambda b, t: (b, t, 0)),
            pl.BlockSpec((None, None, 6, D_MODEL), lambda b, t: (layer, mod_row0 + b, 0, 0)),
        ] + _mixer_weight_specs(layer),
        out_specs=[
            pl.BlockSpec((None, tile, D_MODEL), lambda b, t: (b, t, 0)),
            pl.BlockSpec((None, state_rows, GLA_DV), lambda b, t: (b, 0, 0)),
            pl.BlockSpec((None, WINDOW, SWA_KV), lambda b, t: (b, 0, 0)),
            pl.BlockSpec((None, WINDOW, SWA_KV), lambda b, t: (b, 0, 0)),
        ],
        out_shape=[
            jax.ShapeDtypeStruct((batch, seq, D_MODEL), F32),
            jax.ShapeDtypeStruct((batch, state_rows, GLA_DV), F32),
            jax.ShapeDtypeStruct((batch, WINDOW, SWA_KV), F32),
            jax.ShapeDtypeStruct((batch, WINDOW, SWA_KV), F32),
        ],
        scratch_shapes=_mixer_scratch(tile, n_chunks, WINDOW + tile) + [
            pltpu.VMEM((state_rows, GLA_DV), F32),
        ],
        compiler_params=pltpu.CompilerParams(
            dimension_semantics=("arbitrary", "arbitrary"), vmem_limit_bytes=VMEM_LIMIT),
        name="mixer_prompt",
    )(x, mod, *weights)


def _mixer_sample_call(layer, x, mod, weights, s0, k_cache, v_cache):
    batch, seq, _ = x.shape
    assert seq == CHUNK
    rows = batch * seq
    state_rows = GLA_HEADS * GLA_DK

    def full(shape):
        zeros = (0,) * len(shape)
        return pl.BlockSpec(shape, lambda i: zeros)

    def of_layer(shape):
        index = (layer,) + (0,) * len(shape)
        return pl.BlockSpec((None,) + shape, lambda i: index)

    return pl.pallas_call(
        functools.partial(_mixer_kernel, True, batch, layer),
        grid=(1,),
        in_specs=[full((rows, D_MODEL)), of_layer((batch, 6, D_MODEL))] + _mixer_weight_specs(layer) + [
            of_layer((batch, state_rows, GLA_DV)),
            of_layer((batch, WINDOW, SWA_KV)),
            of_layer((batch, WINDOW, SWA_KV)),
        ],
        out_specs=[
            full((rows, D_MODEL)),
            full((batch, state_rows, GLA_DV)),
            full((rows, SWA_KV)),
            full((rows, SWA_KV)),
        ],
        out_shape=[
            jax.ShapeDtypeStruct((rows, D_MODEL), F32),
            jax.ShapeDtypeStruct((batch, state_rows, GLA_DV), F32),
            jax.ShapeDtypeStruct((rows, SWA_KV), F32),
            jax.ShapeDtypeStruct((rows, SWA_KV), F32),
        ],
        scratch_shapes=_mixer_scratch(rows, batch, batch * BAND),
        compiler_params=pltpu.CompilerParams(
            dimension_semantics=("arbitrary",), vmem_limit_bytes=VMEM_LIMIT),
        name="mixer_sample",
    )(x.reshape(rows, D_MODEL), mod, *weights, s0, k_cache, v_cache)


def _ffn_kernel(sample, final, n_seg, seg_len, *refs):
    if sample:
        (x_ref, mod_ref, g_ffn_ref, w_up_ref, conv_w_ref, conv_b_ref, w_down_ref, g_final_ref,
         past_ref, xo_ref, co_ref, h_scr, ub_scr, act_scr) = refs
    else:
        (x_ref, mod_ref, g_ffn_ref, w_up_ref, conv_w_ref, conv_b_ref, w_down_ref, g_final_ref,
         xo_ref, co_ref, h_scr, ub_scr, act_scr, past_scr) = refs
        t = pl.program_id(1)
        n_t = pl.num_programs(1)

        @pl.when(t == 0)
        def _():
            past_scr[...] = jnp.zeros_like(past_scr)

    stride = seg_len + SUBLANE
    g_ffn = g_ffn_ref[...]

    def mod_row(c, idx):
        if sample:
            return mod_ref[c, idx:idx + 1, :]
        return mod_ref[idx:idx + 1, :]

    for c in range(n_seg):
        rs = slice(c * seg_len, (c + 1) * seg_len)
        x = x_ref[rs, :]
        gain = g_ffn * (1.0 + mod_row(c, 4))
        h_scr[rs, :] = (x * _rms_scale(x) * gain + mod_row(c, 3)).astype(BF16)

    def up(j):
        lo, hi = FF_PARTS[j]
        h = h_scr[...]
        return _dot(h, w_up_ref[:, lo:hi]), _dot(h, w_up_ref[:, D_FF + lo:D_FF + hi])

    def activate(j, u, val):
        lo, hi = FF_PARTS[j]
        width = hi - lo
        w0 = conv_w_ref[0:1, lo:hi]
        w1 = conv_w_ref[1:2, lo:hi]
        w2 = conv_w_ref[2:3, lo:hi]
        cb = conv_b_ref[:, lo:hi]
        for c in range(n_seg):
            base = c * stride
            rs = slice(c * seg_len, (c + 1) * seg_len)
            if sample:
                ub_scr[base + SUBLANE - 2:base + SUBLANE, 0:width] = past_ref[c, :, lo:hi]
            else:
                ub_scr[base + SUBLANE - 2:base + SUBLANE, 0:width] = past_scr[:, lo:hi]
            u_seg = u[rs, :]
            ub_scr[base + SUBLANE:base + SUBLANE + seg_len, 0:width] = u_seg
            u1 = ub_scr[base + SUBLANE - 1:base + SUBLANE - 1 + seg_len, 0:width]
            u2 = ub_scr[base + SUBLANE - 2:base + SUBLANE - 2 + seg_len, 0:width]
            uc = w0 * u2 + w1 * u1 + w2 * u_seg + cb
            act_scr[rs, lo:hi] = (uc * _sigmoid(uc) * val[rs, :]).astype(BF16)
            tail = ub_scr[base + seg_len + SUBLANE - 2:base + seg_len + SUBLANE, 0:width]
            if sample:
                co_ref[c, :, lo:hi] = tail
            else:
                past_scr[:, lo:hi] = tail

    def down(j):
        lo, hi = FF_PARTS[j]
        return _dot(act_scr[:, lo:hi], w_down_ref[lo:hi, :])

    n_parts = len(FF_PARTS)
    pending = {0: up(0)}
    acc = None
    for j in range(n_parts):
        if j + 1 < n_parts:
            pending[j + 1] = up(j + 1)
        activate(j, *pending.pop(j))
        if j > 0:
            d = down(j - 1)
            acc = d if acc is None else acc + d
    acc = acc + down(n_parts - 1)

    for c in range(n_seg):
        rs = slice(c * seg_len, (c + 1) * seg_len)
        y = x_ref[rs, :] + mod_row(c, 5) * acc[rs, :]
        if final:
            y = y * _rms_scale(y) * g_final_ref[...]
        xo_ref[rs, :] = y

    if not sample:
        @pl.when(t == n_t - 1)
        def _():
            co_ref[...] = past_scr[...]


def _ffn_weight_specs(layer):
    return [
        _layer_spec(layer, (1, D_MODEL)),
        _layer_spec(layer, (D_MODEL, 2 * D_FF)),
        _layer_spec(layer, (CONV_W, D_FF)),
        _layer_spec(layer, (1, D_FF)),
        _layer_spec(layer, (D_FF, D_MODEL)),
        _const_spec((1, D_MODEL)),
    ]


def _ffn_prompt_call(layer, x, mod, mod_row0, weights, final):
    batch, seq, _ = x.shape
    tile = min(PROMPT_TILE, seq)
    n_t = seq // tile
    return pl.pallas_call(
        functools.partial(_ffn_kernel, False, final, 1, tile),
        grid=(batch, n_t),
        in_specs=[
            pl.BlockSpec((None, tile, D_MODEL), lambda b, t: (b, t, 0)),
            pl.BlockSpec((None, None, 6, D_MODEL), lambda b, t: (layer, mod_row0 + b, 0, 0)),
        ] + _ffn_weight_specs(layer),
        out_specs=[
            pl.BlockSpec((None, tile, D_MODEL), lambda b, t: (b, t, 0)),
            pl.BlockSpec((None, CONV_W - 1, D_FF), lambda b, t: (b, 0, 0)),
        ],
        out_shape=[
            jax.ShapeDtypeStruct((batch, seq, D_MODEL), F32),
            jax.ShapeDtypeStruct((batch, CONV_W - 1, D_FF), F32),
        ],
        scratch_shapes=[
            pltpu.VMEM((tile, D_MODEL), BF16),
            pltpu.VMEM((tile + SUBLANE, FF_PART), F32),
            pltpu.VMEM((tile, D_FF), BF16),
            pltpu.VMEM((CONV_W - 1, D_FF), F32),
        ],
        compiler_params=pltpu.CompilerParams(
            dimension_semantics=("arbitrary", "arbitrary"), vmem_limit_bytes=VMEM_LIMIT),
        name="ffn_prompt",
    )(x, mod, *weights)


def _ffn_sample_call(layer, x2d, mod, weights, past, final, batch, seq):
    rows = batch * seq

    def full(shape):
        zeros = (0,) * len(shape)
        return pl.BlockSpec(shape, lambda i: zeros)

    def of_layer(shape):
        index = (layer,) + (0,) * len(shape)
        return pl.BlockSpec((None,) + shape, lambda i: index)

    return pl.pallas_call(
        functools.partial(_ffn_kernel, True, final, batch, seq),
        grid=(1,),
        in_specs=[full((rows, D_MODEL)), of_layer((batch, 6, D_MODEL))] + _ffn_weight_specs(layer) + [
            of_layer((batch, CONV_W - 1, D_FF)),
        ],
        out_specs=[full((rows, D_MODEL)), full((batch, CONV_W - 1, D_FF))],
        out_shape=[
            jax.ShapeDtypeStruct((rows, D_MODEL), F32),
            jax.ShapeDtypeStruct((batch, CONV_W - 1, D_FF), F32),
        ],
        scratch_shapes=[
            pltpu.VMEM((rows, D_MODEL), BF16),
            pltpu.VMEM((batch * (seq + SUBLANE), FF_PART), F32),
            pltpu.VMEM((rows, D_FF), BF16),
        ],
        compiler_params=pltpu.CompilerParams(
            dimension_semantics=("arbitrary",), vmem_limit_bytes=VMEM_LIMIT),
        name="ffn_sample",
    )(x2d, mod, *weights, past)


def _pad_w_in_kernel(w_ref, o_ref):
    head = C_LR + GLA_RANK
    w = w_ref[...]
    o_ref[:, 0:C_LR] = w[:, 0:C_LR].astype(BF16)
    gate = jnp.concatenate(
        [w[:, C_LR:head], jnp.zeros((w.shape[0], LR_PAD - GLA_RANK), F32)], axis=1)
    o_ref[:, C_LR:C_QS] = gate.astype(BF16)
    o_ref[:, C_QS:D_IN_P] = w[:, head:D_IN].astype(BF16)


def _pad_w_in(w):
    depth = w.shape[0]
    n_blocks = D_MODEL // W_PREP_ROWS
    return pl.pallas_call(
        _pad_w_in_kernel,
        grid=(depth, n_blocks),
        in_specs=[pl.BlockSpec((None, W_PREP_ROWS, D_IN), lambda l, r: (l, r, 0))],
        out_specs=pl.BlockSpec((None, W_PREP_ROWS, D_IN_P), lambda l, r: (l, r, 0)),
        out_shape=jax.ShapeDtypeStruct((depth, D_MODEL, D_IN_P), BF16),
        compiler_params=pltpu.CompilerParams(
            dimension_semantics=("arbitrary", "arbitrary"), vmem_limit_bytes=VMEM_LIMIT),
        name="pad_w_in",
    )(w)


def kernel(x_prompt, x_sample, state_gla, cache_swa_k, cache_swa_v, state_conv, c_prompt, c_sample,
           w_ada, b_ada, g_attn, g_ffn, w_in, w_gk2, b_gk, g_gla, sinks, w_out, w_up, conv_w, conv_b,
           w_down, g_final):
    depth = w_ada.shape[0]
    batch, seq, _ = x_prompt.shape
    dec_batch, dec_seq, _ = x_sample.shape
    state_rows = GLA_HEADS * GLA_DK

    c_all = jnp.concatenate(
        [c_sample, c_prompt, jnp.zeros((ADA_ROWS - batch - dec_batch, D_MODEL), F32)], axis=0)
    mod_all = _ada_call(c_all, w_ada, b_ada).reshape(depth, ADA_ROWS, 6, D_MODEL)

    w_gk2_p = jnp.concatenate(
        [w_gk2, jnp.zeros((depth, LR_PAD - GLA_RANK, GLA_QK), F32)], axis=1).astype(BF16)
    mixer_w = (g_attn[:, None], _pad_w_in(w_in), w_gk2_p, b_gk[:, None], g_gla[:, None], sinks,
               w_out.astype(BF16))
    ffn_w = (g_ffn[:, None], w_up.astype(BF16), conv_w, conv_b[:, None], w_down.astype(BF16),
             g_final[None])
    s0_all = state_gla.reshape(depth, dec_batch, state_rows, GLA_DV)
    kc_all = cache_swa_k.reshape(depth, dec_batch, WINDOW, SWA_KV)
    vc_all = cache_swa_v.reshape(depth, dec_batch, WINDOW, SWA_KV)

    yp = x_prompt
    ys = x_sample.reshape(dec_batch * dec_seq, D_MODEL)
    outs = [[] for _ in range(8)]
    for i in range(depth):
        final = i == depth - 1

        yp, s_p, k_p, v_p = _mixer_prompt_call(i, yp, mod_all, dec_batch, mixer_w)
        yp, conv_p = _ffn_prompt_call(i, yp, mod_all, dec_batch, ffn_w, final)

        ys, s_s, k_s, v_s = _mixer_sample_call(
            i, ys.reshape(dec_batch, dec_seq, D_MODEL), mod_all, mixer_w, s0_all, kc_all, vc_all)
        ys, conv_s = _ffn_sample_call(i, ys, mod_all, ffn_w, state_conv, final, dec_batch, dec_seq)

        keep = min(WINDOW, seq)
        outs[0].append(s_p.reshape(batch, GLA_HEADS, GLA_DK, GLA_DV))
        outs[1].append(k_p.reshape(batch, keep, SWA_KV_HEADS, SWA_HD))
        outs[2].append(v_p.reshape(batch, keep, SWA_KV_HEADS, SWA_HD))
        outs[3].append(conv_p)
        outs[4].append(s_s.reshape(dec_batch, GLA_HEADS, GLA_DK, GLA_DV))
        outs[5].append(k_s.reshape(dec_batch, dec_seq, SWA_KV_HEADS, SWA_HD))
        outs[6].append(v_s.reshape(dec_batch, dec_seq, SWA_KV_HEADS, SWA_HD))
        outs[7].append(conv_s)

    return (yp, ys.reshape(dec_batch, dec_seq, D_MODEL)) + tuple(jnp.stack(o) for o in outs)
```

```python
import functools

import jax
import jax.numpy as jnp
from jax import lax
from jax.experimental import pallas as pl
from jax.experimental.pallas import tpu as pltpu

F32 = jnp.float32
BF16 = jnp.bfloat16

D_MODEL = 1024
CHUNK = 64
GLA_HEADS = 4
GLA_DK = 64
GLA_DV = 128
GLA_RANK = 16
GLA_NORMALIZER = 16.0
SWA_Q_HEADS = 8
SWA_KV_HEADS = 2
SWA_GROUP = SWA_Q_HEADS // SWA_KV_HEADS
SWA_HD = 64
WINDOW = 128
D_FF = 2816
CONV_W = 3
RMS_EPS = 1e-6

GLA_QK = GLA_HEADS * GLA_DK
GLA_V = GLA_HEADS * GLA_DV
SWA_Q = SWA_Q_HEADS * SWA_HD
SWA_KV = SWA_KV_HEADS * SWA_HD
D_MIX = GLA_V + SWA_Q
BAND = WINDOW + CHUNK

LANE = 128
SUBLANE = 8

C_QG = 0
C_KG = C_QG + GLA_QK
C_VG = C_KG + GLA_QK
C_OG = C_VG + GLA_V
C_LR = C_OG + GLA_V
LR_PAD = LANE
C_QS = C_LR + LR_PAD
C_KS = C_QS + SWA_Q
C_VS = C_KS + SWA_KV
D_IN_P = C_VS + SWA_KV
D_IN = 2 * GLA_QK + 2 * GLA_V + GLA_RANK + SWA_Q + 2 * SWA_KV

assert GLA_DK * 2 == LANE and SWA_HD * 2 == LANE and GLA_DV == LANE and SWA_KV == LANE
GLA_PAIRS = GLA_HEADS // 2
PAIR_V = 2 * GLA_DV
SWA_GRP_Q = SWA_GROUP * SWA_HD

ADA_TILE = 1536
ADA_ROWS = 16
PROMPT_TILE = 512
MIXER_TILE = 1024
SUB_CHUNKS = 4
DENSE_PIECE = 256
DENSE_LEAD = 2
W_PREP_ROWS = 256
FF_PART = 768
FF_PARTS = tuple((lo, min(lo + FF_PART, D_FF)) for lo in range(0, D_FF, FF_PART))
VMEM_LIMIT = 56 * 1024 * 1024

NT_DIMS = (((1,), (1,)), ((), ()))
TN_DIMS = (((0,), (0,)), ((), ()))


def _dot(a, b):
    return jnp.dot(a, b, preferred_element_type=F32)


def _dot_nt(a, b):
    return lax.dot_general(a, b, NT_DIMS, preferred_element_type=F32)


def _dot_tn(a, b):
    return lax.dot_general(a, b, TN_DIMS, preferred_element_type=F32)


def _split_bf16(a):
    hi = a.astype(BF16)
    lo = (a - hi.astype(F32)).astype(BF16)
    return hi, lo


def _sigmoid(a):
    return 1.0 / (1.0 + jnp.exp(-a))


def _log_sigmoid(a):
    return jnp.minimum(a, 0.0) - jnp.log(1.0 + jnp.exp(-jnp.abs(a)))


def _rms_scale(a):
    return lax.rsqrt(jnp.mean(a * a, axis=-1, keepdims=True) + RMS_EPS)


def _ada_kernel(c_ref, w_ref, b_ref, o_ref):
    c = c_ref[...]
    a = (c * _sigmoid(c)).astype(BF16)
    o_ref[...] = _dot(a, w_ref[...].astype(BF16)) + b_ref[...]


def _ada_call(c_all, w_ada, b_ada):
    depth = w_ada.shape[0]
    n_tiles = (6 * D_MODEL) // ADA_TILE
    return pl.pallas_call(
        _ada_kernel,
        grid=(depth, n_tiles),
        in_specs=[
            pl.BlockSpec((ADA_ROWS, D_MODEL), lambda l, j: (0, 0)),
            pl.BlockSpec((None, D_MODEL, ADA_TILE), lambda l, j: (l, 0, j)),
            pl.BlockSpec((None, 1, ADA_TILE), lambda l, j: (l, 0, j)),
        ],
        out_specs=pl.BlockSpec((None, ADA_ROWS, ADA_TILE), lambda l, j: (l, 0, j)),
        out_shape=jax.ShapeDtypeStruct((depth, ADA_ROWS, 6 * D_MODEL), F32),
        compiler_params=pltpu.CompilerParams(
            dimension_semantics=("arbitrary", "arbitrary"), vmem_limit_bytes=VMEM_LIMIT),
        name="adaln_mod",
    )(c_all, w_ada, b_ada.reshape(depth, 1, 6 * D_MODEL))


def _dup_halves(a, low_half):
    swapped = pltpu.roll(a, LANE // 2, axis=1)
    return jnp.where(low_half, a, swapped), jnp.where(low_half, swapped, a)


def _mixer_kernel(sample, n_chunks, layer, *refs):
    if sample:
        (x_ref, mod_ref, g_attn_ref, w_in_ref, w_gk2_ref, b_gk_ref, g_gla_ref, sinks_ref, w_out_ref,
         s0_ref, kc_ref, vc_ref,
         xo_ref, so_ref, ko_ref, vo_ref,
         h_scr, proj_scr, omix_scr, qm_scr, km_scr, qi_scr, ko_scr, vb_scr, gk_scr, dec_scr, u_scr, sb_scr,
         att_scr, pn_scr,
         kd_scr, vd_scr) = refs
        t = None
    else:
        (x_ref, mod_ref, g_attn_ref, w_in_ref, w_gk2_ref, b_gk_ref, g_gla_ref, sinks_ref, w_out_ref,
         xo_ref, so_ref, ko_ref, vo_ref,
         h_scr, proj_scr, omix_scr, qm_scr, km_scr, qi_scr, ko_scr, vb_scr, gk_scr, dec_scr, u_scr, sb_scr,
         att_scr, pn_scr,
         kd_scr, vd_scr, s_scr) = refs
        t = pl.program_id(1)
        n_t = pl.num_programs(1)

        @pl.when(t == 0)
        def _():
            s_scr[...] = jnp.zeros_like(s_scr)
            kd_scr[0:WINDOW, :] = jnp.zeros((WINDOW, 2 * LANE), BF16)
            vd_scr[0:WINDOW, :] = jnp.zeros((WINDOW, 2 * LANE), BF16)

    rows = n_chunks * CHUNK
    g_attn = g_attn_ref[...]

    def mod_row(c, idx):
        if sample:
            return mod_ref[c, idx:idx + 1, :]
        return mod_ref[idx:idx + 1, :]

    def chunk_rows(c):
        return slice(c * CHUNK, (c + 1) * CHUNK)

    sub_chunks = min(SUB_CHUNKS, n_chunks)
    n_sub = n_chunks // sub_chunks
    sub_rows = sub_chunks * CHUNK

    def sub_slice(s):
        return slice(s * sub_rows, (s + 1) * sub_rows)

    def norm_groups(s):
        if sample:
            return [(c, chunk_rows(c)) for c in range(s * sub_chunks, (s + 1) * sub_chunks)]
        return [(0, sub_slice(s))]

    r64 = lax.broadcasted_iota(jnp.int32, (CHUNK, CHUNK), 0)
    c64 = lax.broadcasted_iota(jnp.int32, (CHUNK, CHUNK), 1)
    tri = jnp.where(r64 >= c64, 1.0, 0.0).astype(BF16)
    tri2 = jnp.concatenate([tri, tri], axis=1)
    l_pair = lax.broadcasted_iota(jnp.int32, (CHUNK, LANE), 1)
    low_half = l_pair < LANE // 2
    causal_heads = (lax.broadcasted_iota(jnp.int32, (CHUNK, GLA_QK), 0)
                    >= (lax.broadcasted_iota(jnp.int32, (CHUNK, GLA_QK), 1) & (CHUNK - 1)))
    diag_heads = ((lax.broadcasted_iota(jnp.int32, (GLA_QK, GLA_QK), 0) // GLA_DK)
                  == (lax.broadcasted_iota(jnp.int32, (GLA_QK, GLA_QK), 1) // GLA_DK))
    diag_v = ((lax.broadcasted_iota(jnp.int32, (LANE, PAIR_V), 0) >= LANE // 2)
              == (lax.broadcasted_iota(jnp.int32, (LANE, PAIR_V), 1) >= GLA_DV))
    g_gla = g_gla_ref[...]
    low_sub = lax.broadcasted_iota(jnp.int32, (sub_rows, LANE), 1) < LANE // 2

    def project_steps(s):
        ss = sub_slice(s)

        def norm():
            for mi, rs in norm_groups(s):
                x = x_ref[rs, :]
                gain = g_attn * (1.0 + mod_row(mi, 1))
                h_scr[rs, :] = (x * _rms_scale(x) * gain + mod_row(mi, 0)).astype(BF16)

        def piece(lo, hi):
            proj_scr[ss, lo:hi] = _dot(h_scr[ss, :], w_in_ref[:, lo:hi])

        def gates():
            lr = proj_scr[ss, C_LR:C_LR + LR_PAD].astype(BF16)
            gk_scr[ss, :] = (_log_sigmoid(_dot(lr, w_gk2_ref[...]) + b_gk_ref[...])
                             * (1.0 / GLA_NORMALIZER))
            vb_scr[ss, :] = proj_scr[ss, C_VG:C_VG + GLA_V].astype(BF16)

        def bands():
            k_dup = _dup_halves(proj_scr[ss, C_KS:C_KS + SWA_KV], low_sub)
            v_dup = _dup_halves(proj_scr[ss, C_VS:C_VS + SWA_KV], low_sub)
            for kv in range(SWA_KV_HEADS):
                ls = slice(kv * LANE, (kv + 1) * LANE)
                if sample:
                    for c in range(s * sub_chunks, (s + 1) * sub_chunks):
                        local = slice((c - s * sub_chunks) * CHUNK, (c - s * sub_chunks + 1) * CHUNK)
                        kd_scr[c * BAND + WINDOW:(c + 1) * BAND, ls] = k_dup[kv][local, :].astype(BF16)
                        vd_scr[c * BAND + WINDOW:(c + 1) * BAND, ls] = v_dup[kv][local, :].astype(BF16)
                else:
                    band_rows = slice(WINDOW + s * sub_rows, WINDOW + (s + 1) * sub_rows)
                    kd_scr[band_rows, ls] = k_dup[kv].astype(BF16)
                    vd_scr[band_rows, ls] = v_dup[kv].astype(BF16)

        steps = [norm]
        for lo in range(0, D_IN_P, DENSE_PIECE):
            steps.append(functools.partial(piece, lo, min(lo + DENSE_PIECE, D_IN_P)))
        return steps + [gates, bands]

    def gla_prepare(c):
        rs = chunk_rows(c)
        gk_hi, gk_lo = _split_bf16(gk_scr[rs, :])
        cum = _dot(tri2, jnp.concatenate([gk_hi, gk_lo], axis=0))
        mid = cum[CHUNK // 2:CHUNK // 2 + 1, :]
        last = cum[CHUNK - 1:CHUNK, :]
        q = proj_scr[rs, C_QG:C_QG + GLA_QK] * (GLA_DK ** -0.5)
        k = proj_scr[rs, C_KG:C_KG + GLA_QK]
        qm_scr[rs, :] = (q * jnp.exp(cum - mid)).astype(BF16)
        km_scr[rs, :] = (k * jnp.exp(mid - cum)).astype(BF16)
        qi_scr[rs, :] = (q * jnp.exp(cum)).astype(BF16)
        ko_scr[rs, :] = (k * jnp.exp(last - cum)).astype(BF16)
        dec_scr[c] = jnp.broadcast_to(jnp.exp(last), (LANE, GLA_QK)).T

    def gla_increment(c):
        rs = chunk_rows(c)
        for p in range(GLA_PAIRS):
            upd = _dot_tn(ko_scr[rs, p * LANE:(p + 1) * LANE], vb_scr[rs, p * PAIR_V:(p + 1) * PAIR_V])
            u_scr[c, p * LANE:p * LANE + GLA_DK, :] = upd[0:GLA_DK, 0:GLA_DV]
            u_scr[c, p * LANE + GLA_DK:(p + 1) * LANE, :] = upd[GLA_DK:LANE, GLA_DV:PAIR_V]

    def gla_recurrence(s):
        state = None if sample else s_scr[...]
        for c in range(s * sub_chunks, (s + 1) * sub_chunks):
            if sample:
                state = s0_ref[c]
            sb_scr[c] = state.astype(BF16)
            state = dec_scr[c] * state + u_scr[c]
            if sample:
                so_ref[c] = state
        if not sample:
            s_scr[...] = state

    def gla_scores(c):
        rs = chunk_rows(c)
        km = km_scr[rs, :]
        k_bd = jnp.where(diag_heads, jnp.concatenate([km] * GLA_HEADS, axis=0), 0.0)
        att_scr[rs, :] = jnp.where(causal_heads, _dot_nt(qm_scr[rs, :], k_bd), 0.0).astype(BF16)

    def gla_output(c, p):
        rs = chunk_rows(c)
        ls = slice(p * LANE, (p + 1) * LANE)
        v_pair = vb_scr[rs, p * PAIR_V:(p + 1) * PAIR_V]
        s_pair = sb_scr[c, ls, :]
        w_top = jnp.where(diag_v, jnp.concatenate([v_pair, v_pair], axis=0), 0.0)
        w_bot = jnp.where(diag_v, jnp.concatenate([s_pair, s_pair], axis=1), 0.0)
        o_pair = _dot(jnp.concatenate([att_scr[rs, ls], qi_scr[rs, ls]], axis=1),
                      jnp.concatenate([w_top, w_bot], axis=0))
        for hh in range(2):
            h = 2 * p + hh
            o = o_pair[:, hh * GLA_DV:(hh + 1) * GLA_DV]
            og = proj_scr[rs, C_OG + h * GLA_DV:C_OG + (h + 1) * GLA_DV]
            y = o * _rms_scale(o) * g_gla * (og * _sigmoid(og))
            omix_scr[rs, h * GLA_DV:(h + 1) * GLA_DV] = y.astype(BF16)

    if sample:
        low_win = lax.broadcasted_iota(jnp.int32, (WINDOW, LANE), 1) < LANE // 2
        for c in range(n_chunks):
            kc_dup = _dup_halves(kc_ref[c], low_win)
            vc_dup = _dup_halves(vc_ref[c], low_win)
            for kv in range(SWA_KV_HEADS):
                ls = slice(kv * LANE, (kv + 1) * LANE)
                kd_scr[c * BAND:c * BAND + WINDOW, ls] = kc_dup[kv].astype(BF16)
                vd_scr[c * BAND:c * BAND + WINDOW, ls] = vc_dup[kv].astype(BF16)

    lane_q = lax.broadcasted_iota(jnp.int32, (1, SWA_GRP_Q), 1)
    key_ids = lax.broadcasted_iota(jnp.int32, (BAND, SWA_GRP_Q), 0)
    sink_vecs = []
    for kv in range(SWA_KV_HEADS):
        vec = jnp.full((1, SWA_GRP_Q), sinks_ref[layer, kv * SWA_GROUP + SWA_GROUP - 1], F32)
        for g in range(SWA_GROUP - 2, -1, -1):
            vec = jnp.where(lane_q < (g + 1) * SWA_HD, sinks_ref[layer, kv * SWA_GROUP + g], vec)
        sink_vecs.append(vec)

    def band_rows(c):
        return slice(c * BAND, (c + 1) * BAND) if sample else slice(c * CHUNK, c * CHUNK + BAND)

    def swa_scores(c, kv):
        rs = chunk_rows(c)
        band = band_rows(c)
        ls = slice(kv * LANE, (kv + 1) * LANE)
        qg = (proj_scr[rs, C_QS + kv * SWA_GRP_Q:C_QS + (kv + 1) * SWA_GRP_Q]
              * (SWA_HD ** -0.5)).astype(BF16)
        q_stack = jnp.concatenate(
            [jnp.where(low_half if hh == 0 else ~low_half, qg[:, pp * LANE:(pp + 1) * LANE], 0.0)
             for pp in range(SWA_GROUP // 2) for hh in range(2)], axis=0)
        s_t = _dot_nt(kd_scr[band, ls], q_stack)
        if not sample and c * CHUNK < WINDOW:
            first_valid = WINDOW - (t * rows + c * CHUNK)
            s_t = jnp.where(key_ids >= first_valid, s_t, -jnp.inf)
        sink = sink_vecs[kv]
        m = jnp.maximum(jnp.max(s_t, axis=0, keepdims=True), sink)
        p_t = jnp.exp(s_t - m)
        den = jnp.sum(p_t, axis=0, keepdims=True) + jnp.exp(sink - m)
        pn_scr[c * SWA_KV_HEADS + kv] = (p_t * (1.0 / den)).astype(BF16)

    def swa_output(c, kv):
        rs = chunk_rows(c)
        ls = slice(kv * LANE, (kv + 1) * LANE)
        o_t = _dot_tn(vd_scr[band_rows(c), ls], pn_scr[c * SWA_KV_HEADS + kv]).T
        for pp in range(SWA_GROUP // 2):
            o_pair = jnp.where(low_half, o_t[(2 * pp) * CHUNK:(2 * pp + 1) * CHUNK, :],
                               o_t[(2 * pp + 1) * CHUNK:(2 * pp + 2) * CHUNK, :])
            col = GLA_V + kv * SWA_GRP_Q + pp * LANE
            omix_scr[rs, col:col + LANE] = o_pair.astype(BF16)

    def out_steps(s):
        ss = sub_slice(s)
        base = s * sub_rows

        def piece(lo, hi):
            mix = _dot(omix_scr[ss, :], w_out_ref[:, lo:hi])
            for mi, rs in norm_groups(s):
                local = slice(rs.start - base, rs.stop - base)
                xo_ref[rs, lo:hi] = x_ref[rs, lo:hi] + mod_row(mi, 2)[:, lo:hi] * mix[local, :]

        return [functools.partial(piece, lo, lo + DENSE_PIECE) for lo in range(0, D_MODEL, DENSE_PIECE)]

    def block_steps(s):
        blocks = range(s * sub_chunks, (s + 1) * sub_chunks)
        pairs = [(c, p) for c in blocks for p in range(GLA_PAIRS)]
        groups = [(c, kv) for c in blocks for kv in range(SWA_KV_HEADS)]
        steps = [functools.partial(gla_prepare, c) for c in blocks]
        steps += [functools.partial(swa_scores, c, kv) for c, kv in groups]
        steps += [functools.partial(gla_increment, c) for c in blocks]
        steps += [functools.partial(gla_scores, c) for c in blocks]
        steps.append(functools.partial(gla_recurrence, s))
        steps += [functools.partial(swa_output, c, kv) for c, kv in groups]
        steps += [functools.partial(gla_output, c, p) for c, p in pairs]
        return steps

    for step in project_steps(0):
        step()
    for s in range(n_sub):
        dense = (project_steps(s + 1) if s + 1 < n_sub else []) + (out_steps(s - 1) if s > 0 else [])
        work = block_steps(s)
        issued = 0
        for i, step in enumerate(work):
            while issued < len(dense) and (issued - DENSE_LEAD) * len(work) < (i + 1) * len(dense):
                dense[issued]()
                issued += 1
            step()
        for step in dense[issued:]:
            step()
    for step in out_steps(n_sub - 1):
        step()

    if sample:
        ko_ref[...] = proj_scr[:, C_KS:C_KS + SWA_KV]
        vo_ref[...] = proj_scr[:, C_VS:C_VS + SWA_KV]
    else:
        kd_scr[0:WINDOW, :] = kd_scr[rows:rows + WINDOW, :]
        vd_scr[0:WINDOW, :] = vd_scr[rows:rows + WINDOW, :]

        @pl.when(t == n_t - 1)
        def _():
            so_ref[...] = s_scr[...]
            ko_ref[...] = proj_scr[rows - WINDOW:rows, C_KS:C_KS + SWA_KV]
            vo_ref[...] = proj_scr[rows - WINDOW:rows, C_VS:C_VS + SWA_KV]


def _const_spec(shape):
    zeros = (0,) * len(shape)
    return pl.BlockSpec(shape, lambda *_: zeros, pipeline_mode=pl.Buffered(1))


def _layer_spec(layer, shape):
    index = (layer,) + (0,) * len(shape)
    return pl.BlockSpec((None,) + shape, lambda *_: index, pipeline_mode=pl.Buffered(1))


def _mixer_weight_specs(layer):
    return [
        _layer_spec(layer, (1, D_MODEL)),
        _layer_spec(layer, (D_MODEL, D_IN_P)),
        _layer_spec(layer, (LR_PAD, GLA_QK)),
        _layer_spec(layer, (1, GLA_QK)),
        _layer_spec(layer, (1, GLA_DV)),
        pl.BlockSpec(memory_space=pltpu.SMEM),
        _layer_spec(layer, (D_MIX, D_MODEL)),
    ]


def _mixer_scratch(rows, n_chunks, band_rows):
    state_rows = GLA_HEADS * GLA_DK
    return [
        pltpu.VMEM((rows, D_MODEL), BF16),
        pltpu.VMEM((rows, D_IN_P), F32),
        pltpu.VMEM((rows, D_MIX), BF16),
        pltpu.VMEM((rows, GLA_QK), BF16),
        pltpu.VMEM((rows, GLA_QK), BF16),
        pltpu.VMEM((rows, GLA_QK), BF16),
        pltpu.VMEM((rows, GLA_QK), BF16),
        pltpu.VMEM((rows, GLA_V), BF16),
        pltpu.VMEM((rows, GLA_QK), F32),
        pltpu.VMEM((n_chunks, state_rows, GLA_DV), F32),
        pltpu.VMEM((n_chunks, state_rows, GLA_DV), F32),
        pltpu.VMEM((n_chunks, state_rows, GLA_DV), BF16),
        pltpu.VMEM((rows, GLA_QK), BF16),
        pltpu.VMEM((n_chunks * SWA_KV_HEADS, BAND, SWA_GRP_Q), BF16),
        pltpu.VMEM((band_rows, 2 * LANE), BF16),
        pltpu.VMEM((band_rows, 2 * LANE), BF16),
    ]


def _mixer_prompt_call(layer, x, mod, mod_row0, weights):
    batch, seq, _ = x.shape
    tile = min(MIXER_TILE, seq)
    assert seq % tile == 0 and tile % CHUNK == 0 and tile >= WINDOW
    n_chunks = tile // CHUNK
    n_t = seq // tile
    state_rows = GLA_HEADS * GLA_DK
    return pl.pallas_call(
        functools.partial(_mixer_kernel, False, n_chunks, layer),
        grid=(batch, n_t),
        in_specs=[
            pl.BlockSpec((None, tile, D_MODEL), lambda b, t: (b, t, 0)),
            pl.BlockSpec((None, None, 6, D_MODEL), lambda b, t: (layer, mod_row0 + b, 0, 0)),
        ] + _mixer_weight_specs(layer),
        out_specs=[
            pl.BlockSpec((None, tile, D_MODEL), lambda b, t: (b, t, 0)),
            pl.BlockSpec((None, state_rows, GLA_DV), lambda b, t: (b, 0, 0)),
            pl.BlockSpec((None, WINDOW, SWA_KV), lambda b, t: (b, 0, 0)),
            pl.BlockSpec((None, WINDOW, SWA_KV), lambda b, t: (b, 0, 0)),
        ],
        out_shape=[
            jax.ShapeDtypeStruct((batch, seq, D_MODEL), F32),
            jax.ShapeDtypeStruct((batch, state_rows, GLA_DV), F32),
            jax.ShapeDtypeStruct((batch, WINDOW, SWA_KV), F32),
            jax.ShapeDtypeStruct((batch, WINDOW, SWA_KV), F32),
        ],
        scratch_shapes=_mixer_scratch(tile, n_chunks, WINDOW + tile) + [
            pltpu.VMEM((state_rows, GLA_DV), F32),
        ],
        compiler_params=pltpu.CompilerParams(
            dimension_semantics=("arbitrary", "arbitrary"), vmem_limit_bytes=VMEM_LIMIT),
        name="mixer_prompt",
    )(x, mod, *weights)


def _mixer_sample_call(layer, x, mod, weights, s0, k_cache, v_cache):
    batch, seq, _ = x.shape
    assert seq == CHUNK
    rows = batch * seq
    state_rows = GLA_HEADS * GLA_DK

    def full(shape):
        zeros = (0,) * len(shape)
        return pl.BlockSpec(shape, lambda i: zeros)

    def of_layer(shape):
        index = (layer,) + (0,) * len(shape)
        return pl.BlockSpec((None,) + shape, lambda i: index)

    return pl.pallas_call(
        functools.partial(_mixer_kernel, True, batch, layer),
        grid=(1,),
        in_specs=[full((rows, D_MODEL)), of_layer((batch, 6, D_MODEL))] + _mixer_weight_specs(layer) + [
            of_layer((batch, state_rows, GLA_DV)),
            of_layer((batch, WINDOW, SWA_KV)),
            of_layer((batch, WINDOW, SWA_KV)),
        ],
        out_specs=[
            full((rows, D_MODEL)),
            full((batch, state_rows, GLA_DV)),
            full((rows, SWA_KV)),
            full((rows, SWA_KV)),
        ],
        out_shape=[
            jax.ShapeDtypeStruct((rows, D_MODEL), F32),
            jax.ShapeDtypeStruct((batch, state_rows, GLA_DV), F32),
            jax.ShapeDtypeStruct((rows, SWA_KV), F32),
            jax.ShapeDtypeStruct((rows, SWA_KV), F32),
        ],
        scratch_shapes=_mixer_scratch(rows, batch, batch * BAND),
        compiler_params=pltpu.CompilerParams(
            dimension_semantics=("arbitrary",), vmem_limit_bytes=VMEM_LIMIT),
        name="mixer_sample",
    )(x.reshape(rows, D_MODEL), mod, *weights, s0, k_cache, v_cache)


def _ffn_kernel(sample, final, n_seg, seg_len, *refs):
    if sample:
        (x_ref, mod_ref, g_ffn_ref, w_up_ref, conv_w_ref, conv_b_ref, w_down_ref, g_final_ref,
         past_ref, xo_ref, co_ref, h_scr, ub_scr, act_scr) = refs
    else:
        (x_ref, mod_ref, g_ffn_ref, w_up_ref, conv_w_ref, conv_b_ref, w_down_ref, g_final_ref,
         xo_ref, co_ref, h_scr, ub_scr, act_scr, past_scr) = refs
        t = pl.program_id(1)
        n_t = pl.num_programs(1)

        @pl.when(t == 0)
        def _():
            past_scr[...] = jnp.zeros_like(past_scr)

    stride = seg_len + SUBLANE
    g_ffn = g_ffn_ref[...]

    def mod_row(c, idx):
        if sample:
            return mod_ref[c, idx:idx + 1, :]
        return mod_ref[idx:idx + 1, :]

    for c in range(n_seg):
        rs = slice(c * seg_len, (c + 1) * seg_len)
        x = x_ref[rs, :]
        gain = g_ffn * (1.0 + mod_row(c, 4))
        h_scr[rs, :] = (x * _rms_scale(x) * gain + mod_row(c, 3)).astype(BF16)

    def up(j):
        lo, hi = FF_PARTS[j]
        h = h_scr[...]
        return _dot(h, w_up_ref[:, lo:hi]), _dot(h, w_up_ref[:, D_FF + lo:D_FF + hi])

    def activate(j, u, val):
        lo, hi = FF_PARTS[j]
        width = hi - lo
        w0 = conv_w_ref[0:1, lo:hi]
        w1 = conv_w_ref[1:2, lo:hi]
        w2 = conv_w_ref[2:3, lo:hi]
        cb = conv_b_ref[:, lo:hi]
        for c in range(n_seg):
            base = c * stride
            rs = slice(c * seg_len, (c + 1) * seg_len)
            if sample:
                ub_scr[base + SUBLANE - 2:base + SUBLANE, 0:width] = past_ref[c, :, lo:hi]
            else:
                ub_scr[base + SUBLANE - 2:base + SUBLANE, 0:width] = past_scr[:, lo:hi]
            u_seg = u[rs, :]
            ub_scr[base + SUBLANE:base + SUBLANE + seg_len, 0:width] = u_seg
            u1 = ub_scr[base + SUBLANE - 1:base + SUBLANE - 1 + seg_len, 0:width]
            u2 = ub_scr[base + SUBLANE - 2:base + SUBLANE - 2 + seg_len, 0:width]
            uc = w0 * u2 + w1 * u1 + w2 * u_seg + cb
            act_scr[rs, lo:hi] = (uc * _sigmoid(uc) * val[rs, :]).astype(BF16)
            tail = ub_scr[base + seg_len + SUBLANE - 2:base + seg_len + SUBLANE, 0:width]
            if sample:
                co_ref[c, :, lo:hi] = tail
            else:
                past_scr[:, lo:hi] = tail

    def down(j):
        lo, hi = FF_PARTS[j]
        return _dot(act_scr[:, lo:hi], w_down_ref[lo:hi, :])

    n_parts = len(FF_PARTS)
    pending = {0: up(0)}
    acc = None
    for j in range(n_parts):
        if j + 1 < n_parts:
            pending[j + 1] = up(j + 1)
        activate(j, *pending.pop(j))
        if j > 0:
            d = down(j - 1)
            acc = d if acc is None else acc + d
    acc = acc + down(n_parts - 1)

    for c in range(n_seg):
        rs = slice(c * seg_len, (c + 1) * seg_len)
        y = x_ref[rs, :] + mod_row(c, 5) * acc[rs, :]
        if final:
            y = y * _rms_scale(y) * g_final_ref[...]
        xo_ref[rs, :] = y

    if not sample:
        @pl.when(t == n_t - 1)
        def _():
            co_ref[...] = past_scr[...]


def _ffn_weight_specs(layer):
    return [
        _layer_spec(layer, (1, D_MODEL)),
        _layer_spec(layer, (D_MODEL, 2 * D_FF)),
        _layer_spec(layer, (CONV_W, D_FF)),
        _layer_spec(layer, (1, D_FF)),
        _layer_spec(layer, (D_FF, D_MODEL)),
        _const_spec((1, D_MODEL)),
    ]


def _ffn_prompt_call(layer, x, mod, mod_row0, weights, final):
    batch, seq, _ = x.shape
    tile = min(PROMPT_TILE, seq)
    n_t = seq // tile
    return pl.pallas_call(
        functools.partial(_ffn_kernel, False, final, 1, tile),
        grid=(batch, n_t),
        in_specs=[
            pl.BlockSpec((None, tile, D_MODEL), lambda b, t: (b, t, 0)),
            pl.BlockSpec((None, None, 6, D_MODEL), lambda b, t: (layer, mod_row0 + b, 0, 0)),
        ] + _ffn_weight_specs(layer),
        out_specs=[
            pl.BlockSpec((None, tile, D_MODEL), lambda b, t: (b, t, 0)),
            pl.BlockSpec((None, CONV_W - 1, D_FF), lambda b, t: (b, 0, 0)),
        ],
        out_shape=[
            jax.ShapeDtypeStruct((batch, seq, D_MODEL), F32),
            jax.ShapeDtypeStruct((batch, CONV_W - 1, D_FF), F32),
        ],
        scratch_shapes=[
            pltpu.VMEM((tile, D_MODEL), BF16),
            pltpu.VMEM((tile + SUBLANE, FF_PART), F32),
            pltpu.VMEM((tile, D_FF), BF16),
            pltpu.VMEM((CONV_W - 1, D_FF), F32),
        ],
        compiler_params=pltpu.CompilerParams(
            dimension_semantics=("arbitrary", "arbitrary"), vmem_limit_bytes=VMEM_LIMIT),
        name="ffn_prompt",
    )(x, mod, *weights)


def _ffn_sample_call(layer, x2d, mod, weights, past, final, batch, seq):
    rows = batch * seq

    def full(shape):
        zeros = (0,) * len(shape)
        return pl.BlockSpec(shape, lambda i: zeros)

    def of_layer(shape):
        index = (layer,) + (0,) * len(shape)
        return pl.BlockSpec((None,) + shape, lambda i: index)

    return pl.pallas_call(
        functools.partial(_ffn_kernel, True, final, batch, seq),
        grid=(1,),
        in_specs=[full((rows, D_MODEL)), of_layer((batch, 6, D_MODEL))] + _ffn_weight_specs(layer) + [
            of_layer((batch, CONV_W - 1, D_FF)),
        ],
        out_specs=[full((rows, D_MODEL)), full((batch, CONV_W - 1, D_FF))],
        out_shape=[
            jax.ShapeDtypeStruct((rows, D_MODEL), F32),
            jax.ShapeDtypeStruct((batch, CONV_W - 1, D_FF), F32),
        ],
        scratch_shapes=[
            pltpu.VMEM((rows, D_MODEL), BF16),
            pltpu.VMEM((batch * (seq + SUBLANE), FF_PART), F32),
            pltpu.VMEM((rows, D_FF), BF16),
        ],
        compiler_params=pltpu.CompilerParams(
            dimension_semantics=("arbitrary",), vmem_limit_bytes=VMEM_LIMIT),
        name="ffn_sample",
    )(x2d, mod, *weights, past)


def _pad_w_in_kernel(w_ref, o_ref):
    head = C_LR + GLA_RANK
    w = w_ref[...]
    o_ref[:, 0:C_LR] = w[:, 0:C_LR].astype(BF16)
    gate = jnp.concatenate(
        [w[:, C_LR:head], jnp.zeros((w.shape[0], LR_PAD - GLA_RANK), F32)], axis=1)
    o_ref[:, C_LR:C_QS] = gate.astype(BF16)
    o_ref[:, C_QS:D_IN_P] = w[:, head:D_IN].astype(BF16)


def _pad_w_in(w):
    depth = w.shape[0]
    n_blocks = D_MODEL // W_PREP_ROWS
    return pl.pallas_call(
        _pad_w_in_kernel,
        grid=(depth, n_blocks),
        in_specs=[pl.BlockSpec((None, W_PREP_ROWS, D_IN), lambda l, r: (l, r, 0))],
        out_specs=pl.BlockSpec((None, W_PREP_ROWS, D_IN_P), lambda l, r: (l, r, 0)),
        out_shape=jax.ShapeDtypeStruct((depth, D_MODEL, D_IN_P), BF16),
        compiler_params=pltpu.CompilerParams(
            dimension_semantics=("arbitrary", "arbitrary"), vmem_limit_bytes=VMEM_LIMIT),
        name="pad_w_in",
    )(w)


def kernel(x_prompt, x_sample, state_gla, cache_swa_k, cache_swa_v, state_conv, c_prompt, c_sample,
           w_ada, b_ada, g_attn, g_ffn, w_in, w_gk2, b_gk, g_gla, sinks, w_out, w_up, conv_w, conv_b,
           w_down, g_final):
    depth = w_ada.shape[0]
    batch, seq, _ = x_prompt.shape
    dec_batch, dec_seq, _ = x_sample.shape
    state_rows = GLA_HEADS * GLA_DK

    c_all = jnp.concatenate(
        [c_sample, c_prompt, jnp.zeros((ADA_ROWS - batch - dec_batch, D_MODEL), F32)], axis=0)
    mod_all = _ada_call(c_all, w_ada, b_ada).reshape(depth, ADA_ROWS, 6, D_MODEL)

    w_gk2_p = jnp.concatenate(
        [w_gk2, jnp.zeros((depth, LR_PAD - GLA_RANK, GLA_QK), F32)], axis=1).astype(BF16)
    mixer_w = (g_attn[:, None], _pad_w_in(w_in), w_gk2_p, b_gk[:, None], g_gla[:, None], sinks,
               w_out.astype(BF16))
    ffn_w = (g_ffn[:, None], w_up.astype(BF16), conv_w, conv_b[:, None], w_down.astype(BF16),
             g_final[None])
    s0_all = state_gla.reshape(depth, dec_batch, state_rows, GLA_DV)
    kc_all = cache_swa_k.reshape(depth, dec_batch, WINDOW, SWA_KV)
    vc_all = cache_swa_v.reshape(depth, dec_batch, WINDOW, SWA_KV)

    yp = x_prompt
    ys = x_sample.reshape(dec_batch * dec_seq, D_MODEL)
    outs = [[] for _ in range(8)]
    for i in range(depth):
        final = i == depth - 1

        yp, s_p, k_p, v_p = _mixer_prompt_call(i, yp, mod_all, dec_batch, mixer_w)
        yp, conv_p = _ffn_prompt_call(i, yp, mod_all, dec_batch, ffn_w, final)

        ys, s_s, k_s, v_s = _mixer_sample_call(
            i, ys.reshape(dec_batch, dec_seq, D_MODEL), mod_all, mixer_w, s0_all, kc_all, vc_all)
        ys, conv_s = _ffn_sample_call(i, ys, mod_all, ffn_w, state_conv, final, dec_batch, dec_seq)

        keep = min(WINDOW, seq)
        outs[0].append(s_p.reshape(batch, GLA_HEADS, GLA_DK, GLA_DV))
        outs[1].append(k_p.reshape(batch, keep, SWA_KV_HEADS, SWA_HD))
        outs[2].append(v_p.reshape(batch, keep, SWA_KV_HEADS, SWA_HD))
        outs[3].append(conv_p)
        outs[4].append(s_s.reshape(dec_batch, GLA_HEADS, GLA_DK, GLA_DV))
        outs[5].append(k_s.reshape(dec_batch, dec_seq, SWA_KV_HEADS, SWA_HD))
        outs[6].append(v_s.reshape(dec_batch, dec_seq, SWA_KV_HEADS, SWA_HD))
        outs[7].append(conv_s)

    return (yp, ys.reshape(dec_batch, dec_seq, D_MODEL)) + tuple(jnp.stack(o) for o in outs)
```

```python
import functools

import jax
import jax.numpy as jnp
from jax import lax
from jax.experimental import pallas as pl
from jax.experimental.pallas import tpu as pltpu

F32 = jnp.float32
BF16 = jnp.bfloat16

D_MODEL = 1024
CHUNK = 64
GLA_HEADS = 4
GLA_DK = 64
GLA_DV = 128
GLA_RANK = 16
GLA_NORMALIZER = 16.0
SWA_Q_HEADS = 8
SWA_KV_HEADS = 2
SWA_GROUP = SWA_Q_HEADS // SWA_KV_HEADS
SWA_HD = 64
WINDOW = 128
D_FF = 2816
CONV_W = 3
RMS_EPS = 1e-6

GLA_QK = GLA_HEADS * GLA_DK
GLA_V = GLA_HEADS * GLA_DV
SWA_Q = SWA_Q_HEADS * SWA_HD
SWA_KV = SWA_KV_HEADS * SWA_HD
D_MIX = GLA_V + SWA_Q
BAND = WINDOW + CHUNK

LANE = 128
SUBLANE = 8

C_QG = 0
C_KG = C_QG + GLA_QK
C_VG = C_KG + GLA_QK
C_OG = C_VG + GLA_V
C_LR = C_OG + GLA_V
LR_PAD = LANE
C_QS = C_LR + LR_PAD
C_KS = C_QS + SWA_Q
C_VS = C_KS + SWA_KV
D_IN_P = C_VS + SWA_KV
D_IN = 2 * GLA_QK + 2 * GLA_V + GLA_RANK + SWA_Q + 2 * SWA_KV

assert GLA_DK * 2 == LANE and SWA_HD * 2 == LANE and GLA_DV == LANE and SWA_KV == LANE
GLA_PAIRS = GLA_HEADS // 2
PAIR_V = 2 * GLA_DV
SWA_GRP_Q = SWA_GROUP * SWA_HD

ADA_TILE = 1536
ADA_ROWS = 16
PROMPT_TILE = 512
MIXER_TILE = 1024
SUB_CHUNKS = 4
DENSE_PIECE = 256
DENSE_LEAD = 2
W_PREP_ROWS = 256
MIXER_PREP_ROWS = 32
FFN_PREP_ROWS_UP = 16
FFN_PREP_ROWS_DOWN = 64
FF_PART = 768
FF_PARTS = tuple((lo, min(lo + FF_PART, D_FF)) for lo in range(0, D_FF, FF_PART))
VMEM_LIMIT = 56 * 1024 * 1024

NT_DIMS = (((1,), (1,)), ((), ()))
TN_DIMS = (((0,), (0,)), ((), ()))


def _dot(a, b):
    return jnp.dot(a, b, preferred_element_type=F32)


def _dot_nt(a, b):
    return lax.dot_general(a, b, NT_DIMS, preferred_element_type=F32)


def _dot_tn(a, b):
    return lax.dot_general(a, b, TN_DIMS, preferred_element_type=F32)


def _split_bf16(a):
    hi = a.astype(BF16)
    lo = (a - hi.astype(F32)).astype(BF16)
    return hi, lo


def _sigmoid(a):
    return 1.0 / (1.0 + jnp.exp(-a))


def _log_sigmoid(a):
    return jnp.minimum(a, 0.0) - jnp.log(1.0 + jnp.exp(-jnp.abs(a)))


def _rms_scale(a):
    return lax.rsqrt(jnp.mean(a * a, axis=-1, keepdims=True) + RMS_EPS)


def _ada_kernel(c_ref, w_ref, b_ref, o_ref):
    c = c_ref[...]
    a = (c * _sigmoid(c)).astype(BF16)
    o_ref[...] = _dot(a, w_ref[...].astype(BF16)) + b_ref[...]


def _ada_call(c_all, w_ada, b_ada):
    depth = w_ada.shape[0]
    n_tiles = (6 * D_MODEL) // ADA_TILE
    return pl.pallas_call(
        _ada_kernel,
        grid=(depth, n_tiles),
        in_specs=[
            pl.BlockSpec((ADA_ROWS, D_MODEL), lambda l, j: (0, 0)),
            pl.BlockSpec((None, D_MODEL, ADA_TILE), lambda l, j: (l, 0, j)),
            pl.BlockSpec((None, 1, ADA_TILE), lambda l, j: (l, 0, j)),
        ],
        out_specs=pl.BlockSpec((None, ADA_ROWS, ADA_TILE), lambda l, j: (l, 0, j)),
        out_shape=jax.ShapeDtypeStruct((depth, ADA_ROWS, 6 * D_MODEL), F32),
        compiler_params=pltpu.CompilerParams(
            dimension_semantics=("arbitrary", "arbitrary"), vmem_limit_bytes=VMEM_LIMIT),
        name="adaln_mod",
    )(c_all, w_ada, b_ada.reshape(depth, 1, 6 * D_MODEL))


def _dup_halves(a, low_half):
    swapped = pltpu.roll(a, LANE // 2, axis=1)
    return jnp.where(low_half, a, swapped), jnp.where(low_half, swapped, a)


def _mixer_kernel(sample, n_chunks, layer, preps, *refs):
    if sample:
        (x_ref, mod_ref, g_attn_ref, w_in_ref, w_gk2_ref, b_gk_ref, g_gla_ref, sinks_ref, w_out_ref,
         s0_ref, kc_ref, vc_ref,
         xo_ref, so_ref, ko_ref, vo_ref,
         h_scr, proj_scr, omix_scr, qm_scr, km_scr, qi_scr, ko_scr, vb_scr, gk_scr, dec_scr, u_scr, sb_scr,
         att_scr, pn_scr,
         kd_scr, vd_scr) = refs
        t = None
    else:
        n_in, n_out, n_prep = 9, 4, len(preps)
        prep_src = refs[n_in:n_in + n_prep]
        prep_dst = refs[n_in + n_prep + n_out:n_in + n_prep + n_out + n_prep]
        refs = refs[:n_in] + refs[n_in + n_prep:n_in + n_prep + n_out] + refs[n_in + 2 * n_prep + n_out:]
        (x_ref, mod_ref, g_attn_ref, w_in_ref, w_gk2_ref, b_gk_ref, g_gla_ref, sinks_ref, w_out_ref,
         xo_ref, so_ref, ko_ref, vo_ref,
         h_scr, proj_scr, omix_scr, qm_scr, km_scr, qi_scr, ko_scr, vb_scr, gk_scr, dec_scr, u_scr, sb_scr,
         att_scr, pn_scr,
         kd_scr, vd_scr, s_scr) = refs
        for body, src, dst in zip(preps, prep_src, prep_dst):
            body(src, dst)
        t = pl.program_id(1)
        n_t = pl.num_programs(1)

        @pl.when(t == 0)
        def _():
            s_scr[...] = jnp.zeros_like(s_scr)
            kd_scr[0:WINDOW, :] = jnp.zeros((WINDOW, 2 * LANE), BF16)
            vd_scr[0:WINDOW, :] = jnp.zeros((WINDOW, 2 * LANE), BF16)

    rows = n_chunks * CHUNK
    g_attn = g_attn_ref[...]

    def mod_row(c, idx):
        if sample:
            return mod_ref[c, idx:idx + 1, :]
        return mod_ref[idx:idx + 1, :]

    def chunk_rows(c):
        return slice(c * CHUNK, (c + 1) * CHUNK)

    sub_chunks = min(SUB_CHUNKS, n_chunks)
    n_sub = n_chunks // sub_chunks
    sub_rows = sub_chunks * CHUNK

    def sub_slice(s):
        return slice(s * sub_rows, (s + 1) * sub_rows)

    def norm_groups(s):
        if sample:
            return [(c, chunk_rows(c)) for c in range(s * sub_chunks, (s + 1) * sub_chunks)]
        return [(0, sub_slice(s))]

    r64 = lax.broadcasted_iota(jnp.int32, (CHUNK, CHUNK), 0)
    c64 = lax.broadcasted_iota(jnp.int32, (CHUNK, CHUNK), 1)
    tri = jnp.where(r64 >= c64, 1.0, 0.0).astype(BF16)
    tri2 = jnp.concatenate([tri, tri], axis=1)
    l_pair = lax.broadcasted_iota(jnp.int32, (CHUNK, LANE), 1)
    low_half = l_pair < LANE // 2
    causal_heads = (lax.broadcasted_iota(jnp.int32, (CHUNK, GLA_QK), 0)
                    >= (lax.broadcasted_iota(jnp.int32, (CHUNK, GLA_QK), 1) & (CHUNK - 1)))
    diag_heads = ((lax.broadcasted_iota(jnp.int32, (GLA_QK, GLA_QK), 0) // GLA_DK)
                  == (lax.broadcasted_iota(jnp.int32, (GLA_QK, GLA_QK), 1) // GLA_DK))
    diag_v = ((lax.broadcasted_iota(jnp.int32, (LANE, PAIR_V), 0) >= LANE // 2)
              == (lax.broadcasted_iota(jnp.int32, (LANE, PAIR_V), 1) >= GLA_DV))
    g_gla = g_gla_ref[...]
    low_sub = lax.broadcasted_iota(jnp.int32, (sub_rows, LANE), 1) < LANE // 2

    def project_steps(s):
        ss = sub_slice(s)

        def norm():
            for mi, rs in norm_groups(s):
                x = x_ref[rs, :]
                gain = g_attn * (1.0 + mod_row(mi, 1))
                h_scr[rs, :] = (x * _rms_scale(x) * gain + mod_row(mi, 0)).astype(BF16)

        def piece(lo, hi):
            proj_scr[ss, lo:hi] = _dot(h_scr[ss, :], w_in_ref[:, lo:hi])

        def gates():
            lr = proj_scr[ss, C_LR:C_LR + LR_PAD].astype(BF16)
            gk_scr[ss, :] = (_log_sigmoid(_dot(lr, w_gk2_ref[...]) + b_gk_ref[...])
                             * (1.0 / GLA_NORMALIZER))
            vb_scr[ss, :] = proj_scr[ss, C_VG:C_VG + GLA_V].astype(BF16)

        def bands():
            k_dup = _dup_halves(proj_scr[ss, C_KS:C_KS + SWA_KV], low_sub)
            v_dup = _dup_halves(proj_scr[ss, C_VS:C_VS + SWA_KV], low_sub)
            for kv in range(SWA_KV_HEADS):
                ls = slice(kv * LANE, (kv + 1) * LANE)
                if sample:
                    for c in range(s * sub_chunks, (s + 1) * sub_chunks):
                        local = slice((c - s * sub_chunks) * CHUNK, (c - s * sub_chunks + 1) * CHUNK)
                        kd_scr[c * BAND + WINDOW:(c + 1) * BAND, ls] = k_dup[kv][local, :].astype(BF16)
                        vd_scr[c * BAND + WINDOW:(c + 1) * BAND, ls] = v_dup[kv][local, :].astype(BF16)
                else:
                    band_rows = slice(WINDOW + s * sub_rows, WINDOW + (s + 1) * sub_rows)
                    kd_scr[band_rows, ls] = k_dup[kv].astype(BF16)
                    vd_scr[band_rows, ls] = v_dup[kv].astype(BF16)

        steps = [norm]
        for lo in range(0, D_IN_P, DENSE_PIECE):
            steps.append(functools.partial(piece, lo, min(lo + DENSE_PIECE, D_IN_P)))
        return steps + [gates, bands]

    def gla_prepare(c):
        rs = chunk_rows(c)
        gk_hi, gk_lo = _split_bf16(gk_scr[rs, :])
        cum = _dot(tri2, jnp.concatenate([gk_hi, gk_lo], axis=0))
        mid = cum[CHUNK // 2:CHUNK // 2 + 1, :]
        last = cum[CHUNK - 1:CHUNK, :]
        q = proj_scr[rs, C_QG:C_QG + GLA_QK] * (GLA_DK ** -0.5)
        k = proj_scr[rs, C_KG:C_KG + GLA_QK]
        qm_scr[rs, :] = (q * jnp.exp(cum - mid)).astype(BF16)
        km_scr[rs, :] = (k * jnp.exp(mid - cum)).astype(BF16)
        qi_scr[rs, :] = (q * jnp.exp(cum)).astype(BF16)
        ko_scr[rs, :] = (k * jnp.exp(last - cum)).astype(BF16)
        dec_scr[c] = jnp.broadcast_to(jnp.exp(last), (LANE, GLA_QK)).T

    def gla_increment(c):
        rs = chunk_rows(c)
        for p in range(GLA_PAIRS):
            upd = _dot_tn(ko_scr[rs, p * LANE:(p + 1) * LANE], vb_scr[rs, p * PAIR_V:(p + 1) * PAIR_V])
            u_scr[c, p * LANE:p * LANE + GLA_DK, :] = upd[0:GLA_DK, 0:GLA_DV]
            u_scr[c, p * LANE + GLA_DK:(p + 1) * LANE, :] = upd[GLA_DK:LANE, GLA_DV:PAIR_V]

    def gla_recurrence(s):
        state = None if sample else s_scr[...]
        for c in range(s * sub_chunks, (s + 1) * sub_chunks):
            if sample:
                state = s0_ref[c]
            sb_scr[c] = state.astype(BF16)
            state = dec_scr[c] * state + u_scr[c]
            if sample:
                so_ref[c] = state
        if not sample:
            s_scr[...] = state

    def gla_scores(c):
        rs = chunk_rows(c)
        km = km_scr[rs, :]
        k_bd = jnp.where(diag_heads, jnp.concatenate([km] * GLA_HEADS, axis=0), 0.0)
        att_scr[rs, :] = jnp.where(causal_heads, _dot_nt(qm_scr[rs, :], k_bd), 0.0).astype(BF16)

    def gla_output(c, p):
        rs = chunk_rows(c)
        ls = slice(p * LANE, (p + 1) * LANE)
        v_pair = vb_scr[rs, p * PAIR_V:(p + 1) * PAIR_V]
        s_pair = sb_scr[c, ls, :]
        w_top = jnp.where(diag_v, jnp.concatenate([v_pair, v_pair], axis=0), 0.0)
        w_bot = jnp.where(diag_v, jnp.concatenate([s_pair, s_pair], axis=1), 0.0)
        o_pair = _dot(jnp.concatenate([att_scr[rs, ls], qi_scr[rs, ls]], axis=1),
                      jnp.concatenate([w_top, w_bot], axis=0))
        for hh in range(2):
            h = 2 * p + hh
            o = o_pair[:, hh * GLA_DV:(hh + 1) * GLA_DV]
            og = proj_scr[rs, C_OG + h * GLA_DV:C_OG + (h + 1) * GLA_DV]
            y = o * _rms_scale(o) * g_gla * (og * _sigmoid(og))
            omix_scr[rs, h * GLA_DV:(h + 1) * GLA_DV] = y.astype(BF16)

    if sample:
        low_win = lax.broadcasted_iota(jnp.int32, (WINDOW, LANE), 1) < LANE // 2
        for c in range(n_chunks):
            kc_dup = _dup_halves(kc_ref[c], low_win)
            vc_dup = _dup_halves(vc_ref[c], low_win)
            for kv in range(SWA_KV_HEADS):
                ls = slice(kv * LANE, (kv + 1) * LANE)
                kd_scr[c * BAND:c * BAND + WINDOW, ls] = kc_dup[kv].astype(BF16)
                vd_scr[c * BAND:c * BAND + WINDOW, ls] = vc_dup[kv].astype(BF16)

    lane_q = lax.broadcasted_iota(jnp.int32, (1, SWA_GRP_Q), 1)
    key_ids = lax.broadcasted_iota(jnp.int32, (BAND, SWA_GRP_Q), 0)
    sink_vecs = []
    for kv in range(SWA_KV_HEADS):
        vec = jnp.full((1, SWA_GRP_Q), sinks_ref[layer, kv * SWA_GROUP + SWA_GROUP - 1], F32)
        for g in range(SWA_GROUP - 2, -1, -1):
            vec = jnp.where(lane_q < (g + 1) * SWA_HD, sinks_ref[layer, kv * SWA_GROUP + g], vec)
        sink_vecs.append(vec)

    def band_rows(c):
        return slice(c * BAND, (c + 1) * BAND) if sample else slice(c * CHUNK, c * CHUNK + BAND)

    def swa_scores(c, kv):
        rs = chunk_rows(c)
        band = band_rows(c)
        ls = slice(kv * LANE, (kv + 1) * LANE)
        qg = (proj_scr[rs, C_QS + kv * SWA_GRP_Q:C_QS + (kv + 1) * SWA_GRP_Q]
              * (SWA_HD ** -0.5)).astype(BF16)
        q_stack = jnp.concatenate(
            [jnp.where(low_half if hh == 0 else ~low_half, qg[:, pp * LANE:(pp + 1) * LANE], 0.0)
             for pp in range(SWA_GROUP // 2) for hh in range(2)], axis=0)
        s_t = _dot_nt(kd_scr[band, ls], q_stack)
        if not sample and c * CHUNK < WINDOW:
            first_valid = WINDOW - (t * rows + c * CHUNK)
            s_t = jnp.where(key_ids >= first_valid, s_t, -jnp.inf)
        sink = sink_vecs[kv]
        m = jnp.maximum(jnp.max(s_t, axis=0, keepdims=True), sink)
        p_t = jnp.exp(s_t - m)
        den = jnp.sum(p_t, axis=0, keepdims=True) + jnp.exp(sink - m)
        pn_scr[c * SWA_KV_HEADS + kv] = (p_t * (1.0 / den)).astype(BF16)

    def swa_output(c, kv):
        rs = chunk_rows(c)
        ls = slice(kv * LANE, (kv + 1) * LANE)
        o_t = _dot_tn(vd_scr[band_rows(c), ls], pn_scr[c * SWA_KV_HEADS + kv]).T
        for pp in range(SWA_GROUP // 2):
            o_pair = jnp.where(low_half, o_t[(2 * pp) * CHUNK:(2 * pp + 1) * CHUNK, :],
                               o_t[(2 * pp + 1) * CHUNK:(2 * pp + 2) * CHUNK, :])
            col = GLA_V + kv * SWA_GRP_Q + pp * LANE
            omix_scr[rs, col:col + LANE] = o_pair.astype(BF16)

    def out_steps(s):
        ss = sub_slice(s)
        base = s * sub_rows

        def piece(lo, hi):
            mix = _dot(omix_scr[ss, :], w_out_ref[:, lo:hi])
            for mi, rs in norm_groups(s):
                local = slice(rs.start - base, rs.stop - base)
                xo_ref[rs, lo:hi] = x_ref[rs, lo:hi] + mod_row(mi, 2)[:, lo:hi] * mix[local, :]

        return [functools.partial(piece, lo, lo + DENSE_PIECE) for lo in range(0, D_MODEL, DENSE_PIECE)]

    def block_steps(s):
        blocks = range(s * sub_chunks, (s + 1) * sub_chunks)
        pairs = [(c, p) for c in blocks for p in range(GLA_PAIRS)]
        groups = [(c, kv) for c in blocks for kv in range(SWA_KV_HEADS)]
        steps = [functools.partial(gla_prepare, c) for c in blocks]
        steps += [functools.partial(swa_scores, c, kv) for c, kv in groups]
        steps += [functools.partial(gla_increment, c) for c in blocks]
        steps += [functools.partial(gla_scores, c) for c in blocks]
        steps.append(functools.partial(gla_recurrence, s))
        steps += [functools.partial(swa_output, c, kv) for c, kv in groups]
        steps += [functools.partial(gla_output, c, p) for c, p in pairs]
        return steps

    for step in project_steps(0):
        step()
    for s in range(n_sub):
        dense = (project_steps(s + 1) if s + 1 < n_sub else []) + (out_steps(s - 1) if s > 0 else [])
        work = block_steps(s)
        issued = 0
        for i, step in enumerate(work):
            while issued < len(dense) and (issued - DENSE_LEAD) * len(work) < (i + 1) * len(dense):
                dense[issued]()
                issued += 1
            step()
        for step in dense[issued:]:
            step()
    for step in out_steps(n_sub - 1):
        step()

    if sample:
        ko_ref[...] = proj_scr[:, C_KS:C_KS + SWA_KV]
        vo_ref[...] = proj_scr[:, C_VS:C_VS + SWA_KV]
    else:
        kd_scr[0:WINDOW, :] = kd_scr[rows:rows + WINDOW, :]
        vd_scr[0:WINDOW, :] = vd_scr[rows:rows + WINDOW, :]

        @pl.when(t == n_t - 1)
        def _():
            so_ref[...] = s_scr[...]
            ko_ref[...] = proj_scr[rows - WINDOW:rows, C_KS:C_KS + SWA_KV]
            vo_ref[...] = proj_scr[rows - WINDOW:rows, C_VS:C_VS + SWA_KV]


def _pad_cast_w_in(w_ref, o_ref):
    head = C_LR + GLA_RANK
    w = w_ref[...]
    o_ref[:, 0:C_LR] = w[:, 0:C_LR].astype(BF16)
    gate = jnp.concatenate(
        [w[:, C_LR:head], jnp.zeros((w.shape[0], LR_PAD - GLA_RANK), F32)], axis=1)
    o_ref[:, C_LR:C_QS] = gate.astype(BF16)
    o_ref[:, C_QS:D_IN_P] = w[:, head:D_IN].astype(BF16)


def _cast_weight(w_ref, o_ref):
    o_ref[...] = w_ref[...].astype(BF16)


class _WeightPrep:
    def __init__(self, weight, layer, rows_per_step, out_cols, body):
        self.weight, self.layer, self.rows_per_step = weight, layer, rows_per_step
        self.rows, self.in_cols = weight.shape[1], weight.shape[2]
        self.out_cols, self.body = out_cols, body
        assert self.rows % rows_per_step == 0
        self.n_blocks = self.rows // rows_per_step

    def specs(self, n_t):
        def block(b, t):
            return jnp.minimum(b * n_t + t, self.n_blocks - 1)
        layer = self.layer
        return (pl.BlockSpec((None, self.rows_per_step, self.in_cols), lambda b, t: (layer, block(b, t), 0)),
                pl.BlockSpec((None, self.rows_per_step, self.out_cols), lambda b, t: (0, block(b, t), 0)),
                jax.ShapeDtypeStruct((1, self.rows, self.out_cols), BF16))


def _const_spec(shape):
    zeros = (0,) * len(shape)
    return pl.BlockSpec(shape, lambda *_: zeros, pipeline_mode=pl.Buffered(1))


def _layer_spec(layer, shape):
    index = (layer,) + (0,) * len(shape)
    return pl.BlockSpec((None,) + shape, lambda *_: index, pipeline_mode=pl.Buffered(1))


def _mixer_weight_specs(layer):
    return [
        _layer_spec(layer, (1, D_MODEL)),
        _layer_spec(0, (D_MODEL, D_IN_P)),
        _layer_spec(layer, (LR_PAD, GLA_QK)),
        _layer_spec(layer, (1, GLA_QK)),
        _layer_spec(layer, (1, GLA_DV)),
        pl.BlockSpec(memory_space=pltpu.SMEM),
        _layer_spec(0, (D_MIX, D_MODEL)),
    ]


def _mixer_scratch(rows, n_chunks, band_rows):
    state_rows = GLA_HEADS * GLA_DK
    return [
        pltpu.VMEM((rows, D_MODEL), BF16),
        pltpu.VMEM((rows, D_IN_P), F32),
        pltpu.VMEM((rows, D_MIX), BF16),
        pltpu.VMEM((rows, GLA_QK), BF16),
        pltpu.VMEM((rows, GLA_QK), BF16),
        pltpu.VMEM((rows, GLA_QK), BF16),
        pltpu.VMEM((rows, GLA_QK), BF16),
        pltpu.VMEM((rows, GLA_V), BF16),
        pltpu.VMEM((rows, GLA_QK), F32),
        pltpu.VMEM((n_chunks, state_rows, GLA_DV), F32),
        pltpu.VMEM((n_chunks, state_rows, GLA_DV), F32),
        pltpu.VMEM((n_chunks, state_rows, GLA_DV), BF16),
        pltpu.VMEM((rows, GLA_QK), BF16),
        pltpu.VMEM((n_chunks * SWA_KV_HEADS, BAND, SWA_GRP_Q), BF16),
        pltpu.VMEM((band_rows, 2 * LANE), BF16),
        pltpu.VMEM((band_rows, 2 * LANE), BF16),
    ]


def _mixer_prompt_call(layer, x, mod, mod_row0, weights, preps):
    batch, seq, _ = x.shape
    tile = min(MIXER_TILE, seq)
    assert seq % tile == 0 and tile % CHUNK == 0 and tile >= WINDOW
    n_chunks = tile // CHUNK
    n_t = seq // tile
    state_rows = GLA_HEADS * GLA_DK
    prep_specs = [p.specs(n_t) for p in preps]
    assert all(p.n_blocks <= batch * n_t for p in preps)
    return pl.pallas_call(
        functools.partial(_mixer_kernel, False, n_chunks, layer, tuple(p.body for p in preps)),
        grid=(batch, n_t),
        in_specs=[
            pl.BlockSpec((None, tile, D_MODEL), lambda b, t: (b, t, 0)),
            pl.BlockSpec((None, None, 6, D_MODEL), lambda b, t: (layer, mod_row0 + b, 0, 0)),
        ] + _mixer_weight_specs(layer) + [s[0] for s in prep_specs],
        out_specs=[
            pl.BlockSpec((None, tile, D_MODEL), lambda b, t: (b, t, 0)),
            pl.BlockSpec((None, state_rows, GLA_DV), lambda b, t: (b, 0, 0)),
            pl.BlockSpec((None, WINDOW, SWA_KV), lambda b, t: (b, 0, 0)),
            pl.BlockSpec((None, WINDOW, SWA_KV), lambda b, t: (b, 0, 0)),
        ] + [s[1] for s in prep_specs],
        out_shape=[
            jax.ShapeDtypeStruct((batch, seq, D_MODEL), F32),
            jax.ShapeDtypeStruct((batch, state_rows, GLA_DV), F32),
            jax.ShapeDtypeStruct((batch, WINDOW, SWA_KV), F32),
            jax.ShapeDtypeStruct((batch, WINDOW, SWA_KV), F32),
        ] + [s[2] for s in prep_specs],
        scratch_shapes=_mixer_scratch(tile, n_chunks, WINDOW + tile) + [
            pltpu.VMEM((state_rows, GLA_DV), F32),
        ],
        compiler_params=pltpu.CompilerParams(
            dimension_semantics=("arbitrary", "arbitrary"), vmem_limit_bytes=VMEM_LIMIT),
        name="mixer_prompt",
    )(x, mod, *weights, *[p.weight for p in preps])


def _mixer_sample_call(layer, x, mod, weights, s0, k_cache, v_cache):
    batch, seq, _ = x.shape
    assert seq == CHUNK
    rows = batch * seq
    state_rows = GLA_HEADS * GLA_DK

    def full(shape):
        zeros = (0,) * len(shape)
        return pl.BlockSpec(shape, lambda i: zeros)

    def of_layer(shape):
        index = (layer,) + (0,) * len(shape)
        return pl.BlockSpec((None,) + shape, lambda i: index)

    return pl.pallas_call(
        functools.partial(_mixer_kernel, True, batch, layer, ()),
        grid=(1,),
        in_specs=[full((rows, D_MODEL)), of_layer((batch, 6, D_MODEL))] + _mixer_weight_specs(layer) + [
            of_layer((batch, state_rows, GLA_DV)),
            of_layer((batch, WINDOW, SWA_KV)),
            of_layer((batch, WINDOW, SWA_KV)),
        ],
        out_specs=[
            full((rows, D_MODEL)),
            full((batch, state_rows, GLA_DV)),
            full((rows, SWA_KV)),
            full((rows, SWA_KV)),
        ],
        out_shape=[
            jax.ShapeDtypeStruct((rows, D_MODEL), F32),
            jax.ShapeDtypeStruct((batch, state_rows, GLA_DV), F32),
            jax.ShapeDtypeStruct((rows, SWA_KV), F32),
            jax.ShapeDtypeStruct((rows, SWA_KV), F32),
        ],
        scratch_shapes=_mixer_scratch(rows, batch, batch * BAND),
        compiler_params=pltpu.CompilerParams(
            dimension_semantics=("arbitrary",), vmem_limit_bytes=VMEM_LIMIT),
        name="mixer_sample",
    )(x.reshape(rows, D_MODEL), mod, *weights, s0, k_cache, v_cache)


def _ffn_kernel(sample, final, n_seg, seg_len, preps, *refs):
    if sample:
        (x_ref, mod_ref, g_ffn_ref, w_up_ref, conv_w_ref, conv_b_ref, w_down_ref, g_final_ref,
         past_ref, xo_ref, co_ref, h_scr, ub_scr, act_scr) = refs
    else:
        n_in, n_out, n_prep = 8, 2, len(preps)
        prep_src = refs[n_in:n_in + n_prep]
        prep_dst = refs[n_in + n_prep + n_out:n_in + n_prep + n_out + n_prep]
        refs = refs[:n_in] + refs[n_in + n_prep:n_in + n_prep + n_out] + refs[n_in + 2 * n_prep + n_out:]
        (x_ref, mod_ref, g_ffn_ref, w_up_ref, conv_w_ref, conv_b_ref, w_down_ref, g_final_ref,
         xo_ref, co_ref, h_scr, ub_scr, act_scr, past_scr) = refs
        for body, src, dst in zip(preps, prep_src, prep_dst):
            body(src, dst)
        t = pl.program_id(1)
        n_t = pl.num_programs(1)

        @pl.when(t == 0)
        def _():
            past_scr[...] = jnp.zeros_like(past_scr)

    stride = seg_len + SUBLANE
    g_ffn = g_ffn_ref[...]

    def mod_row(c, idx):
        if sample:
            return mod_ref[c, idx:idx + 1, :]
        return mod_ref[idx:idx + 1, :]

    for c in range(n_seg):
        rs = slice(c * seg_len, (c + 1) * seg_len)
        x = x_ref[rs, :]
        gain = g_ffn * (1.0 + mod_row(c, 4))
        h_scr[rs, :] = (x * _rms_scale(x) * gain + mod_row(c, 3)).astype(BF16)

    def up(j):
        lo, hi = FF_PARTS[j]
        h = h_scr[...]
        return _dot(h, w_up_ref[:, lo:hi]), _dot(h, w_up_ref[:, D_FF + lo:D_FF + hi])

    def activate(j, u, val):
        lo, hi = FF_PARTS[j]
        width = hi - lo
        w0 = conv_w_ref[0:1, lo:hi]
        w1 = conv_w_ref[1:2, lo:hi]
        w2 = conv_w_ref[2:3, lo:hi]
        cb = conv_b_ref[:, lo:hi]
        for c in range(n_seg):
            base = c * stride
            rs = slice(c * seg_len, (c + 1) * seg_len)
            if sample:
                ub_scr[base + SUBLANE - 2:base + SUBLANE, 0:width] = past_ref[c, :, lo:hi]
            else:
                ub_scr[base + SUBLANE - 2:base + SUBLANE, 0:width] = past_scr[:, lo:hi]
            u_seg = u[rs, :]
            ub_scr[base + SUBLANE:base + SUBLANE + seg_len, 0:width] = u_seg
            u1 = ub_scr[base + SUBLANE - 1:base + SUBLANE - 1 + seg_len, 0:width]
            u2 = ub_scr[base + SUBLANE - 2:base + SUBLANE - 2 + seg_len, 0:width]
            uc = w0 * u2 + w1 * u1 + w2 * u_seg + cb
            act_scr[rs, lo:hi] = (uc * _sigmoid(uc) * val[rs, :]).astype(BF16)
            tail = ub_scr[base + seg_len + SUBLANE - 2:base + seg_len + SUBLANE, 0:width]
            if sample:
                co_ref[c, :, lo:hi] = tail
            else:
                past_scr[:, lo:hi] = tail

    def down(j):
        lo, hi = FF_PARTS[j]
        return _dot(act_scr[:, lo:hi], w_down_ref[lo:hi, :])

    n_parts = len(FF_PARTS)
    pending = {0: up(0)}
    acc = None
    for j in range(n_parts):
        if j + 1 < n_parts:
            pending[j + 1] = up(j + 1)
        activate(j, *pending.pop(j))
        if j > 0:
            d = down(j - 1)
            acc = d if acc is None else acc + d
    acc = acc + down(n_parts - 1)

    for c in range(n_seg):
        rs = slice(c * seg_len, (c + 1) * seg_len)
        y = x_ref[rs, :] + mod_row(c, 5) * acc[rs, :]
        if final:
            y = y * _rms_scale(y) * g_final_ref[...]
        xo_ref[rs, :] = y

    if not sample:
        @pl.when(t == n_t - 1)
        def _():
            co_ref[...] = past_scr[...]


def _ffn_weight_specs(layer):
    return [
        _layer_spec(layer, (1, D_MODEL)),
        _layer_spec(0, (D_MODEL, 2 * D_FF)),
        _layer_spec(layer, (CONV_W, D_FF)),
        _layer_spec(layer, (1, D_FF)),
        _layer_spec(0, (D_FF, D_MODEL)),
        _const_spec((1, D_MODEL)),
    ]


def _ffn_prompt_call(layer, x, mod, mod_row0, weights, final, preps):
    batch, seq, _ = x.shape
    tile = min(PROMPT_TILE, seq)
    n_t = seq // tile
    prep_specs = [p.specs(n_t) for p in preps]
    assert all(p.n_blocks <= batch * n_t for p in preps)
    return pl.pallas_call(
        functools.partial(_ffn_kernel, False, final, 1, tile, tuple(p.body for p in preps)),
        grid=(batch, n_t),
        in_specs=[
            pl.BlockSpec((None, tile, D_MODEL), lambda b, t: (b, t, 0)),
            pl.BlockSpec((None, None, 6, D_MODEL), lambda b, t: (layer, mod_row0 + b, 0, 0)),
        ] + _ffn_weight_specs(layer) + [s[0] for s in prep_specs],
        out_specs=[
            pl.BlockSpec((None, tile, D_MODEL), lambda b, t: (b, t, 0)),
            pl.BlockSpec((None, CONV_W - 1, D_FF), lambda b, t: (b, 0, 0)),
        ] + [s[1] for s in prep_specs],
        out_shape=[
            jax.ShapeDtypeStruct((batch, seq, D_MODEL), F32),
            jax.ShapeDtypeStruct((batch, CONV_W - 1, D_FF), F32),
        ] + [s[2] for s in prep_specs],
        scratch_shapes=[
            pltpu.VMEM((tile, D_MODEL), BF16),
            pltpu.VMEM((tile + SUBLANE, FF_PART), F32),
            pltpu.VMEM((tile, D_FF), BF16),
            pltpu.VMEM((CONV_W - 1, D_FF), F32),
        ],
        compiler_params=pltpu.CompilerParams(
            dimension_semantics=("arbitrary", "arbitrary"), vmem_limit_bytes=VMEM_LIMIT),
        name="ffn_prompt",
    )(x, mod, *weights, *[p.weight for p in preps])


def _ffn_sample_call(layer, x2d, mod, weights, past, final, batch, seq):
    rows = batch * seq

    def full(shape):
        zeros = (0,) * len(shape)
        return pl.BlockSpec(shape, lambda i: zeros)

    def of_layer(shape):
        index = (layer,) + (0,) * len(shape)
        return pl.BlockSpec((None,) + shape, lambda i: index)

    return pl.pallas_call(
        functools.partial(_ffn_kernel, True, final, batch, seq, ()),
        grid=(1,),
        in_specs=[full((rows, D_MODEL)), of_layer((batch, 6, D_MODEL))] + _ffn_weight_specs(layer) + [
            of_layer((batch, CONV_W - 1, D_FF)),
        ],
        out_specs=[full((rows, D_MODEL)), full((batch, CONV_W - 1, D_FF))],
        out_shape=[
            jax.ShapeDtypeStruct((rows, D_MODEL), F32),
            jax.ShapeDtypeStruct((batch, CONV_W - 1, D_FF), F32),
        ],
        scratch_shapes=[
            pltpu.VMEM((rows, D_MODEL), BF16),
            pltpu.VMEM((batch * (seq + SUBLANE), FF_PART), F32),
            pltpu.VMEM((rows, D_FF), BF16),
        ],
        compiler_params=pltpu.CompilerParams(
            dimension_semantics=("arbitrary",), vmem_limit_bytes=VMEM_LIMIT),
        name="ffn_sample",
    )(x2d, mod, *weights, past)


def _pad_w_in_first(w):
    n_blocks = D_MODEL // W_PREP_ROWS
    return pl.pallas_call(
        _pad_cast_w_in,
        grid=(n_blocks,),
        in_specs=[pl.BlockSpec((None, W_PREP_ROWS, D_IN), lambda r: (0, r, 0))],
        out_specs=pl.BlockSpec((None, W_PREP_ROWS, D_IN_P), lambda r: (0, r, 0)),
        out_shape=jax.ShapeDtypeStruct((1, D_MODEL, D_IN_P), BF16),
        compiler_params=pltpu.CompilerParams(
            dimension_semantics=("arbitrary",), vmem_limit_bytes=VMEM_LIMIT),
        name="pad_w_in",
    )(w)


def kernel(x_prompt, x_sample, state_gla, cache_swa_k, cache_swa_v, state_conv, c_prompt, c_sample,
           w_ada, b_ada, g_attn, g_ffn, w_in, w_gk2, b_gk, g_gla, sinks, w_out, w_up, conv_w, conv_b,
           w_down, g_final):
    depth = w_ada.shape[0]
    batch, seq, _ = x_prompt.shape
    dec_batch, dec_seq, _ = x_sample.shape
    state_rows = GLA_HEADS * GLA_DK

    c_all = jnp.concatenate(
        [c_sample, c_prompt, jnp.zeros((ADA_ROWS - batch - dec_batch, D_MODEL), F32)], axis=0)
    mod_all = _ada_call(c_all, w_ada, b_ada).reshape(depth, ADA_ROWS, 6, D_MODEL)

    w_gk2_p = jnp.concatenate(
        [w_gk2, jnp.zeros((depth, LR_PAD - GLA_RANK, GLA_QK), F32)], axis=1).astype(BF16)
    w_in_b, w_out_b = _pad_w_in_first(w_in[:1]), w_out[:1].astype(BF16)
    w_up_b, w_down_b = w_up[:1].astype(BF16), w_down[:1].astype(BF16)
    mixer_steps = batch * (seq // min(MIXER_TILE, seq))
    ffn_steps = batch * (seq // min(PROMPT_TILE, seq))
    prep_ahead = (D_MODEL // MIXER_PREP_ROWS <= mixer_steps
                  and D_MODEL // FFN_PREP_ROWS_UP <= ffn_steps
                  and D_FF // FFN_PREP_ROWS_DOWN <= ffn_steps)
    s0_all = state_gla.reshape(depth, dec_batch, state_rows, GLA_DV)
    kc_all = cache_swa_k.reshape(depth, dec_batch, WINDOW, SWA_KV)
    vc_all = cache_swa_v.reshape(depth, dec_batch, WINDOW, SWA_KV)

    yp = x_prompt
    ys = x_sample.reshape(dec_batch * dec_seq, D_MODEL)
    outs = [[] for _ in range(8)]
    for i in range(depth):
        final = i == depth - 1
        mixer_w = (g_attn[:, None], w_in_b, w_gk2_p, b_gk[:, None], g_gla[:, None], sinks, w_out_b)
        ffn_w = (g_ffn[:, None], w_up_b, conv_w, conv_b[:, None], w_down_b, g_final[None])
        mixer_preps, ffn_preps = [], []
        if not final and not prep_ahead:
            next_mixer_w = [_pad_w_in_first(w_in[i + 1:i + 2]), w_out[i + 1:i + 2].astype(BF16)]
            next_ffn_w = [w_up[i + 1:i + 2].astype(BF16), w_down[i + 1:i + 2].astype(BF16)]
        elif not final:
            mixer_preps = [_WeightPrep(w_in, i + 1, MIXER_PREP_ROWS, D_IN_P, _pad_cast_w_in),
                           _WeightPrep(w_out, i + 1, MIXER_PREP_ROWS, D_MODEL, _cast_weight)]
            ffn_preps = [_WeightPrep(w_up, i + 1, FFN_PREP_ROWS_UP, 2 * D_FF, _cast_weight),
                         _WeightPrep(w_down, i + 1, FFN_PREP_ROWS_DOWN, D_MODEL, _cast_weight)]

        yp, s_p, k_p, v_p, *prepared = _mixer_prompt_call(i, yp, mod_all, dec_batch, mixer_w, mixer_preps)
        if mixer_preps:
            next_mixer_w = prepared
        yp, conv_p, *prepared = _ffn_prompt_call(i, yp, mod_all, dec_batch, ffn_w, final, ffn_preps)
        if ffn_preps:
            next_ffn_w = prepared

        ys, s_s, k_s, v_s = _mixer_sample_call(
            i, ys.reshape(dec_batch, dec_seq, D_MODEL), mod_all, mixer_w, s0_all, kc_all, vc_all)
        ys, conv_s = _ffn_sample_call(i, ys, mod_all, ffn_w, state_conv, final, dec_batch, dec_seq)
        if not final:
            (w_in_b, w_out_b), (w_up_b, w_down_b) = next_mixer_w, next_ffn_w

        keep = min(WINDOW, seq)
        outs[0].append(s_p.reshape(batch, GLA_HEADS, GLA_DK, GLA_DV))
        outs[1].append(k_p.reshape(batch, keep, SWA_KV_HEADS, SWA_HD))
        outs[2].append(v_p.reshape(batch, keep, SWA_KV_HEADS, SWA_HD))
        outs[3].append(conv_p)
        outs[4].append(s_s.reshape(dec_batch, GLA_HEADS, GLA_DK, GLA_DV))
        outs[5].append(k_s.reshape(dec_batch, dec_seq, SWA_KV_HEADS, SWA_HD))
        outs[6].append(v_s.reshape(dec_batch, dec_seq, SWA_KV_HEADS, SWA_HD))
        outs[7].append(conv_s)

    return (yp, ys.reshape(dec_batch, dec_seq, D_MODEL)) + tuple(jnp.stack(o) for o in outs)
```

```python
import functools

import jax
import jax.numpy as jnp
from jax import lax
from jax.experimental import pallas as pl
from jax.experimental.pallas import tpu as pltpu

F32 = jnp.float32
BF16 = jnp.bfloat16

D_MODEL = 1024
CHUNK = 64
GLA_HEADS = 4
GLA_DK = 64
GLA_DV = 128
GLA_RANK = 16
GLA_NORMALIZER = 16.0
SWA_Q_HEADS = 8
SWA_KV_HEADS = 2
SWA_GROUP = SWA_Q_HEADS // SWA_KV_HEADS
SWA_HD = 64
WINDOW = 128
D_FF = 2816
CONV_W = 3
RMS_EPS = 1e-6

GLA_QK = GLA_HEADS * GLA_DK
GLA_V = GLA_HEADS * GLA_DV
SWA_Q = SWA_Q_HEADS * SWA_HD
SWA_KV = SWA_KV_HEADS * SWA_HD
D_MIX = GLA_V + SWA_Q
BAND = WINDOW + CHUNK

LANE = 128
SUBLANE = 8

C_QG = 0
C_KG = C_QG + GLA_QK
C_VG = C_KG + GLA_QK
C_OG = C_VG + GLA_V
C_LR = C_OG + GLA_V
LR_PAD = LANE
C_QS = C_LR + LR_PAD
C_KS = C_QS + SWA_Q
C_VS = C_KS + SWA_KV
D_IN_P = C_VS + SWA_KV
D_IN = 2 * GLA_QK + 2 * GLA_V + GLA_RANK + SWA_Q + 2 * SWA_KV

assert GLA_DK * 2 == LANE and SWA_HD * 2 == LANE and GLA_DV == LANE and SWA_KV == LANE
GLA_PAIRS = GLA_HEADS // 2
PAIR_V = 2 * GLA_DV
SWA_GRP_Q = SWA_GROUP * SWA_HD

ADA_TILE = 1536
ADA_ROWS = 16
PROMPT_TILE = 512
MIXER_TILE = 1024
SUB_CHUNKS = 4
DENSE_PIECE = 256
DENSE_LEAD = 2
W_PREP_ROWS = 256
MIXER_PREP_ROWS = 32
FFN_PREP_ROWS_UP = 16
FFN_PREP_ROWS_DOWN = 64
FF_PART = 768
FF_PARTS = tuple((lo, min(lo + FF_PART, D_FF)) for lo in range(0, D_FF, FF_PART))
VMEM_LIMIT = 56 * 1024 * 1024

NT_DIMS = (((1,), (1,)), ((), ()))
TN_DIMS = (((0,), (0,)), ((), ()))


def _dot(a, b):
    return jnp.dot(a, b, preferred_element_type=F32)


def _dot_nt(a, b):
    return lax.dot_general(a, b, NT_DIMS, preferred_element_type=F32)


def _dot_tn(a, b):
    return lax.dot_general(a, b, TN_DIMS, preferred_element_type=F32)


def _split_bf16(a):
    hi = a.astype(BF16)
    lo = (a - hi.astype(F32)).astype(BF16)
    return hi, lo


def _sigmoid(a):
    return 1.0 / (1.0 + jnp.exp(-a))


def _log_sigmoid(a):
    return jnp.minimum(a, 0.0) - jnp.log(1.0 + jnp.exp(-jnp.abs(a)))


def _rms_scale(a):
    return lax.rsqrt(jnp.mean(a * a, axis=-1, keepdims=True) + RMS_EPS)


def _ada_kernel(c_ref, w_ref, b_ref, o_ref):
    c = c_ref[...]
    a = (c * _sigmoid(c)).astype(BF16)
    o_ref[...] = _dot(a, w_ref[...].astype(BF16)) + b_ref[...]


def _ada_call(c_all, w_ada, b_ada):
    depth = w_ada.shape[0]
    n_tiles = (6 * D_MODEL) // ADA_TILE
    return pl.pallas_call(
        _ada_kernel,
        grid=(depth, n_tiles),
        in_specs=[
            pl.BlockSpec((ADA_ROWS, D_MODEL), lambda l, j: (0, 0)),
            pl.BlockSpec((None, D_MODEL, ADA_TILE), lambda l, j: (l, 0, j)),
            pl.BlockSpec((None, 1, ADA_TILE), lambda l, j: (l, 0, j)),
        ],
        out_specs=pl.BlockSpec((None, ADA_ROWS, ADA_TILE), lambda l, j: (l, 0, j)),
        out_shape=jax.ShapeDtypeStruct((depth, ADA_ROWS, 6 * D_MODEL), F32),
        compiler_params=pltpu.CompilerParams(
            dimension_semantics=("arbitrary", "arbitrary"), vmem_limit_bytes=VMEM_LIMIT),
        name="adaln_mod",
    )(c_all, w_ada, b_ada.reshape(depth, 1, 6 * D_MODEL))


def _dup_halves(a, low_half):
    swapped = pltpu.roll(a, LANE // 2, axis=1)
    return jnp.where(low_half, a, swapped), jnp.where(low_half, swapped, a)


def _mixer_kernel(sample, n_chunks, layer, preps, *refs):
    if sample:
        (x_ref, mod_ref, g_attn_ref, w_in_ref, w_gk2_ref, b_gk_ref, g_gla_ref, sinks_ref, w_out_ref,
         s0_ref, kc_ref, vc_ref,
         xo_ref, so_ref, ko_ref, vo_ref,
         h_scr, proj_scr, omix_scr, qm_scr, km_scr, qi_scr, ko_scr, vb_scr, gk_scr, dec_scr, u_scr, sb_scr,
         att_scr, pn_scr,
         kd_scr, vd_scr) = refs
        t = None
    else:
        n_in, n_out, n_prep = 9, 4, len(preps)
        prep_src = refs[n_in:n_in + n_prep]
        prep_dst = refs[n_in + n_prep + n_out:n_in + n_prep + n_out + n_prep]
        refs = refs[:n_in] + refs[n_in + n_prep:n_in + n_prep + n_out] + refs[n_in + 2 * n_prep + n_out:]
        (x_ref, mod_ref, g_attn_ref, w_in_ref, w_gk2_ref, b_gk_ref, g_gla_ref, sinks_ref, w_out_ref,
         xo_ref, so_ref, ko_ref, vo_ref,
         h_scr, proj_scr, omix_scr, qm_scr, km_scr, qi_scr, ko_scr, vb_scr, gk_scr, dec_scr, u_scr, sb_scr,
         att_scr, pn_scr,
         kd_scr, vd_scr, s_scr) = refs
        for body, src, dst in zip(preps, prep_src, prep_dst):
            body(src, dst)
        t = pl.program_id(1)
        n_t = pl.num_programs(1)

        @pl.when(t == 0)
        def _():
            s_scr[...] = jnp.zeros_like(s_scr)
            kd_scr[0:WINDOW, :] = jnp.zeros((WINDOW, 2 * LANE), BF16)
            vd_scr[0:WINDOW, :] = jnp.zeros((WINDOW, 2 * LANE), BF16)

    rows = n_chunks * CHUNK
    g_attn = g_attn_ref[...]

    def mod_row(c, idx):
        if sample:
            return mod_ref[c, idx:idx + 1, :]
        return mod_ref[idx:idx + 1, :]

    def chunk_rows(c):
        return slice(c * CHUNK, (c + 1) * CHUNK)

    sub_chunks = min(SUB_CHUNKS, n_chunks)
    n_sub = n_chunks // sub_chunks
    sub_rows = sub_chunks * CHUNK

    def sub_slice(s):
        return slice(s * sub_rows, (s + 1) * sub_rows)

    def norm_groups(s):
        if sample:
            return [(c, chunk_rows(c)) for c in range(s * sub_chunks, (s + 1) * sub_chunks)]
        return [(0, sub_slice(s))]

    r64 = lax.broadcasted_iota(jnp.int32, (CHUNK, CHUNK), 0)
    c64 = lax.broadcasted_iota(jnp.int32, (CHUNK, CHUNK), 1)
    tri = jnp.where(r64 >= c64, 1.0, 0.0).astype(BF16)
    tri2 = jnp.concatenate([tri, tri], axis=1)
    l_pair = lax.broadcasted_iota(jnp.int32, (CHUNK, LANE), 1)
    low_half = l_pair < LANE // 2
    causal_heads = (lax.broadcasted_iota(jnp.int32, (CHUNK, GLA_QK), 0)
                    >= (lax.broadcasted_iota(jnp.int32, (CHUNK, GLA_QK), 1) & (CHUNK - 1)))
    diag_heads = ((lax.broadcasted_iota(jnp.int32, (GLA_QK, GLA_QK), 0) // GLA_DK)
                  == (lax.broadcasted_iota(jnp.int32, (GLA_QK, GLA_QK), 1) // GLA_DK))
    diag_v = ((lax.broadcasted_iota(jnp.int32, (LANE, PAIR_V), 0) >= LANE // 2)
              == (lax.broadcasted_iota(jnp.int32, (LANE, PAIR_V), 1) >= GLA_DV))
    g_gla = g_gla_ref[...]
    low_sub = lax.broadcasted_iota(jnp.int32, (sub_rows, LANE), 1) < LANE // 2

    def project_steps(s):
        ss = sub_slice(s)

        def norm():
            for mi, rs in norm_groups(s):
                x = x_ref[rs, :]
                gain = g_attn * (1.0 + mod_row(mi, 1))
                h_scr[rs, :] = (x * _rms_scale(x) * gain + mod_row(mi, 0)).astype(BF16)

        def piece(lo, hi):
            proj_scr[ss, lo:hi] = _dot(h_scr[ss, :], w_in_ref[:, lo:hi])

        def gates():
            lr = proj_scr[ss, C_LR:C_LR + LR_PAD].astype(BF16)
            gk_scr[ss, :] = (_log_sigmoid(_dot(lr, w_gk2_ref[...]) + b_gk_ref[...])
                             * (1.0 / GLA_NORMALIZER))
            vb_scr[ss, :] = proj_scr[ss, C_VG:C_VG + GLA_V].astype(BF16)

        def bands():
            k_dup = _dup_halves(proj_scr[ss, C_KS:C_KS + SWA_KV], low_sub)
            v_dup = _dup_halves(proj_scr[ss, C_VS:C_VS + SWA_KV], low_sub)
            for kv in range(SWA_KV_HEADS):
                ls = slice(kv * LANE, (kv + 1) * LANE)
                if sample:
                    for c in range(s * sub_chunks, (s + 1) * sub_chunks):
                        local = slice((c - s * sub_chunks) * CHUNK, (c - s * sub_chunks + 1) * CHUNK)
                        kd_scr[c * BAND + WINDOW:(c + 1) * BAND, ls] = k_dup[kv][local, :].astype(BF16)
                        vd_scr[c * BAND + WINDOW:(c + 1) * BAND, ls] = v_dup[kv][local, :].astype(BF16)
                else:
                    band_rows = slice(WINDOW + s * sub_rows, WINDOW + (s + 1) * sub_rows)
                    kd_scr[band_rows, ls] = k_dup[kv].astype(BF16)
                    vd_scr[band_rows, ls] = v_dup[kv].astype(BF16)

        steps = [norm]
        for lo in range(0, D_IN_P, DENSE_PIECE):
            steps.append(functools.partial(piece, lo, min(lo + DENSE_PIECE, D_IN_P)))
        return steps + [gates, bands]

    def gla_prepare(c):
        rs = chunk_rows(c)
        gk_hi, gk_lo = _split_bf16(gk_scr[rs, :])
        cum = _dot(tri2, jnp.concatenate([gk_hi, gk_lo], axis=0))
        mid = cum[CHUNK // 2:CHUNK // 2 + 1, :]
        last = cum[CHUNK - 1:CHUNK, :]
        q = proj_scr[rs, C_QG:C_QG + GLA_QK] * (GLA_DK ** -0.5)
        k = proj_scr[rs, C_KG:C_KG + GLA_QK]
        qm_scr[rs, :] = (q * jnp.exp(cum - mid)).astype(BF16)
        km_scr[rs, :] = (k * jnp.exp(mid - cum)).astype(BF16)
        qi_scr[rs, :] = (q * jnp.exp(cum)).astype(BF16)
        ko_scr[rs, :] = (k * jnp.exp(last - cum)).astype(BF16)
        dec_scr[c] = jnp.broadcast_to(jnp.exp(last), (LANE, GLA_QK)).T

    def gla_increment(c):
        rs = chunk_rows(c)
        for p in range(GLA_PAIRS):
            upd = _dot_tn(ko_scr[rs, p * LANE:(p + 1) * LANE], vb_scr[rs, p * PAIR_V:(p + 1) * PAIR_V])
            u_scr[c, p * LANE:p * LANE + GLA_DK, :] = upd[0:GLA_DK, 0:GLA_DV]
            u_scr[c, p * LANE + GLA_DK:(p + 1) * LANE, :] = upd[GLA_DK:LANE, GLA_DV:PAIR_V]

    def gla_recurrence(s):
        state = None if sample else s_scr[...]
        for c in range(s * sub_chunks, (s + 1) * sub_chunks):
            if sample:
                state = s0_ref[c]
            sb_scr[c] = state.astype(BF16)
            state = dec_scr[c] * state + u_scr[c]
            if sample:
                so_ref[c] = state
        if not sample:
            s_scr[...] = state

    def gla_scores(c):
        rs = chunk_rows(c)
        km = km_scr[rs, :]
        k_bd = jnp.where(diag_heads, jnp.concatenate([km] * GLA_HEADS, axis=0), 0.0)
        att_scr[rs, :] = jnp.where(causal_heads, _dot_nt(qm_scr[rs, :], k_bd), 0.0).astype(BF16)

    def gla_output(c, p):
        rs = chunk_rows(c)
        ls = slice(p * LANE, (p + 1) * LANE)
        v_pair = vb_scr[rs, p * PAIR_V:(p + 1) * PAIR_V]
        s_pair = sb_scr[c, ls, :]
        w_top = jnp.where(diag_v, jnp.concatenate([v_pair, v_pair], axis=0), 0.0)
        w_bot = jnp.where(diag_v, jnp.concatenate([s_pair, s_pair], axis=1), 0.0)
        o_pair = _dot(jnp.concatenate([att_scr[rs, ls], qi_scr[rs, ls]], axis=1),
                      jnp.concatenate([w_top, w_bot], axis=0))
        for hh in range(2):
            h = 2 * p + hh
            o = o_pair[:, hh * GLA_DV:(hh + 1) * GLA_DV]
            og = proj_scr[rs, C_OG + h * GLA_DV:C_OG + (h + 1) * GLA_DV]
            y = o * _rms_scale(o) * g_gla * (og * _sigmoid(og))
            omix_scr[rs, h * GLA_DV:(h + 1) * GLA_DV] = y.astype(BF16)

    if sample:
        low_win = lax.broadcasted_iota(jnp.int32, (WINDOW, LANE), 1) < LANE // 2
        for c in range(n_chunks):
            kc_dup = _dup_halves(kc_ref[c], low_win)
            vc_dup = _dup_halves(vc_ref[c], low_win)
            for kv in range(SWA_KV_HEADS):
                ls = slice(kv * LANE, (kv + 1) * LANE)
                kd_scr[c * BAND:c * BAND + WINDOW, ls] = kc_dup[kv].astype(BF16)
                vd_scr[c * BAND:c * BAND + WINDOW, ls] = vc_dup[kv].astype(BF16)

    lane_q = lax.broadcasted_iota(jnp.int32, (1, SWA_GRP_Q), 1)
    key_ids = lax.broadcasted_iota(jnp.int32, (BAND, SWA_GRP_Q), 0)
    sink_vecs = []
    for kv in range(SWA_KV_HEADS):
        vec = jnp.full((1, SWA_GRP_Q), sinks_ref[layer, kv * SWA_GROUP + SWA_GROUP - 1], F32)
        for g in range(SWA_GROUP - 2, -1, -1):
            vec = jnp.where(lane_q < (g + 1) * SWA_HD, sinks_ref[layer, kv * SWA_GROUP + g], vec)
        sink_vecs.append(vec)

    def band_rows(c):
        return slice(c * BAND, (c + 1) * BAND) if sample else slice(c * CHUNK, c * CHUNK + BAND)

    def swa_scores(c, kv):
        rs = chunk_rows(c)
        band = band_rows(c)
        ls = slice(kv * LANE, (kv + 1) * LANE)
        qg = (proj_scr[rs, C_QS + kv * SWA_GRP_Q:C_QS + (kv + 1) * SWA_GRP_Q]
              * (SWA_HD ** -0.5)).astype(BF16)
        q_stack = jnp.concatenate(
            [jnp.where(low_half if hh == 0 else ~low_half, qg[:, pp * LANE:(pp + 1) * LANE], 0.0)
             for pp in range(SWA_GROUP // 2) for hh in range(2)], axis=0)
        s_t = _dot_nt(kd_scr[band, ls], q_stack)
        if not sample and c * CHUNK < WINDOW:
            first_valid = WINDOW - (t * rows + c * CHUNK)
            s_t = jnp.where(key_ids >= first_valid, s_t, -jnp.inf)
        sink = sink_vecs[kv]
        m = jnp.maximum(jnp.max(s_t, axis=0, keepdims=True), sink)
        p_t = jnp.exp(s_t - m)
        den = jnp.sum(p_t, axis=0, keepdims=True) + jnp.exp(sink - m)
        pn_scr[c * SWA_KV_HEADS + kv] = (p_t * (1.0 / den)).astype(BF16)

    def swa_output(c, kv):
        rs = chunk_rows(c)
        ls = slice(kv * LANE, (kv + 1) * LANE)
        o_t = _dot_tn(vd_scr[band_rows(c), ls], pn_scr[c * SWA_KV_HEADS + kv]).T
        for pp in range(SWA_GROUP // 2):
            o_pair = jnp.where(low_half, o_t[(2 * pp) * CHUNK:(2 * pp + 1) * CHUNK, :],
                               o_t[(2 * pp + 1) * CHUNK:(2 * pp + 2) * CHUNK, :])
            col = GLA_V + kv * SWA_GRP_Q + pp * LANE
            omix_scr[rs, col:col + LANE] = o_pair.astype(BF16)

    def out_steps(s):
        ss = sub_slice(s)
        base = s * sub_rows

        def piece(lo, hi):
            mix = _dot(omix_scr[ss, :], w_out_ref[:, lo:hi])
            for mi, rs in norm_groups(s):
                local = slice(rs.start - base, rs.stop - base)
                xo_ref[rs, lo:hi] = x_ref[rs, lo:hi] + mod_row(mi, 2)[:, lo:hi] * mix[local, :]

        return [functools.partial(piece, lo, lo + DENSE_PIECE) for lo in range(0, D_MODEL, DENSE_PIECE)]

    def block_steps(s):
        blocks = range(s * sub_chunks, (s + 1) * sub_chunks)
        pairs = [(c, p) for c in blocks for p in range(GLA_PAIRS)]
        groups = [(c, kv) for c in blocks for kv in range(SWA_KV_HEADS)]
        steps = [functools.partial(gla_prepare, c) for c in blocks]
        steps += [functools.partial(swa_scores, c, kv) for c, kv in groups]
        steps += [functools.partial(gla_increment, c) for c in blocks]
        steps += [functools.partial(gla_scores, c) for c in blocks]
        steps.append(functools.partial(gla_recurrence, s))
        steps += [functools.partial(swa_output, c, kv) for c, kv in groups]
        steps += [functools.partial(gla_output, c, p) for c, p in pairs]
        return steps

    for step in project_steps(0):
        step()
    for s in range(n_sub):
        dense = (project_steps(s + 1) if s + 1 < n_sub else []) + (out_steps(s - 1) if s > 0 else [])
        work = block_steps(s)
        issued = 0
        for i, step in enumerate(work):
            while issued < len(dense) and (issued - DENSE_LEAD) * len(work) < (i + 1) * len(dense):
                dense[issued]()
                issued += 1
            step()
        for step in dense[issued:]:
            step()
    for step in out_steps(n_sub - 1):
        step()

    if sample:
        ko_ref[...] = proj_scr[:, C_KS:C_KS + SWA_KV]
        vo_ref[...] = proj_scr[:, C_VS:C_VS + SWA_KV]
    else:
        kd_scr[0:WINDOW, :] = kd_scr[rows:rows + WINDOW, :]
        vd_scr[0:WINDOW, :] = vd_scr[rows:rows + WINDOW, :]

        @pl.when(t == n_t - 1)
        def _():
            so_ref[...] = s_scr[...]
            ko_ref[...] = proj_scr[rows - WINDOW:rows, C_KS:C_KS + SWA_KV]
            vo_ref[...] = proj_scr[rows - WINDOW:rows, C_VS:C_VS + SWA_KV]


def _pad_cast_w_in(w_ref, o_ref):
    head = C_LR + GLA_RANK
    w = w_ref[...].astype(F32)
    o_ref[:, 0:C_LR] = w[:, 0:C_LR].astype(BF16)
    gate = jnp.concatenate(
        [w[:, C_LR:head], jnp.zeros((w.shape[0], LR_PAD - GLA_RANK), F32)], axis=1)
    o_ref[:, C_LR:C_QS] = gate.astype(BF16)
    o_ref[:, C_QS:D_IN_P] = w[:, head:D_IN].astype(BF16)


def _cast_weight(w_ref, o_ref):
    o_ref[...] = w_ref[...].astype(BF16)


class _WeightPrep:
    def __init__(self, weight, layer, rows_per_step, out_cols, body):
        self.weight, self.layer, self.rows_per_step = weight, layer, rows_per_step
        self.rows, self.in_cols = weight.shape[1], weight.shape[2]
        self.out_cols, self.body = out_cols, body
        assert self.rows % rows_per_step == 0
        self.n_blocks = self.rows // rows_per_step

    def specs(self, n_t):
        def block(b, t):
            return jnp.minimum(b * n_t + t, self.n_blocks - 1)
        layer = self.layer
        return (pl.BlockSpec((None, self.rows_per_step, self.in_cols), lambda b, t: (layer, block(b, t), 0)),
                pl.BlockSpec((None, self.rows_per_step, self.out_cols), lambda b, t: (0, block(b, t), 0)),
                jax.ShapeDtypeStruct((1, self.rows, self.out_cols), BF16))


def _const_spec(shape):
    zeros = (0,) * len(shape)
    return pl.BlockSpec(shape, lambda *_: zeros, pipeline_mode=pl.Buffered(1))


def _layer_spec(layer, shape):
    index = (layer,) + (0,) * len(shape)
    return pl.BlockSpec((None,) + shape, lambda *_: index, pipeline_mode=pl.Buffered(1))


def _mixer_weight_specs(layer):
    return [
        _layer_spec(layer, (1, D_MODEL)),
        _layer_spec(0, (D_MODEL, D_IN_P)),
        _layer_spec(layer, (LR_PAD, GLA_QK)),
        _layer_spec(layer, (1, GLA_QK)),
        _layer_spec(layer, (1, GLA_DV)),
        pl.BlockSpec(memory_space=pltpu.SMEM),
        _layer_spec(0, (D_MIX, D_MODEL)),
    ]


def _mixer_scratch(rows, n_chunks, band_rows):
    state_rows = GLA_HEADS * GLA_DK
    return [
        pltpu.VMEM((rows, D_MODEL), BF16),
        pltpu.VMEM((rows, D_IN_P), F32),
        pltpu.VMEM((rows, D_MIX), BF16),
        pltpu.VMEM((rows, GLA_QK), BF16),
        pltpu.VMEM((rows, GLA_QK), BF16),
        pltpu.VMEM((rows, GLA_QK), BF16),
        pltpu.VMEM((rows, GLA_QK), BF16),
        pltpu.VMEM((rows, GLA_V), BF16),
        pltpu.VMEM((rows, GLA_QK), F32),
        pltpu.VMEM((n_chunks, state_rows, GLA_DV), F32),
        pltpu.VMEM((n_chunks, state_rows, GLA_DV), F32),
        pltpu.VMEM((n_chunks, state_rows, GLA_DV), BF16),
        pltpu.VMEM((rows, GLA_QK), BF16),
        pltpu.VMEM((n_chunks * SWA_KV_HEADS, BAND, SWA_GRP_Q), BF16),
        pltpu.VMEM((band_rows, 2 * LANE), BF16),
        pltpu.VMEM((band_rows, 2 * LANE), BF16),
    ]


def _mixer_prompt_call(layer, x, mod, mod_row0, weights, preps):
    batch, seq, _ = x.shape
    tile = min(MIXER_TILE, seq)
    assert seq % tile == 0 and tile % CHUNK == 0 and tile >= WINDOW
    n_chunks = tile // CHUNK
    n_t = seq // tile
    state_rows = GLA_HEADS * GLA_DK
    prep_specs = [p.specs(n_t) for p in preps]
    assert all(p.n_blocks <= batch * n_t for p in preps)
    return pl.pallas_call(
        functools.partial(_mixer_kernel, False, n_chunks, layer, tuple(p.body for p in preps)),
        grid=(batch, n_t),
        in_specs=[
            pl.BlockSpec((None, tile, D_MODEL), lambda b, t: (b, t, 0)),
            pl.BlockSpec((None, None, 6, D_MODEL), lambda b, t: (layer, mod_row0 + b, 0, 0)),
        ] + _mixer_weight_specs(layer) + [s[0] for s in prep_specs],
        out_specs=[
            pl.BlockSpec((None, tile, D_MODEL), lambda b, t: (b, t, 0)),
            pl.BlockSpec((None, state_rows, GLA_DV), lambda b, t: (b, 0, 0)),
            pl.BlockSpec((None, WINDOW, SWA_KV), lambda b, t: (b, 0, 0)),
            pl.BlockSpec((None, WINDOW, SWA_KV), lambda b, t: (b, 0, 0)),
        ] + [s[1] for s in prep_specs],
        out_shape=[
            jax.ShapeDtypeStruct((batch, seq, D_MODEL), F32),
            jax.ShapeDtypeStruct((batch, state_rows, GLA_DV), F32),
            jax.ShapeDtypeStruct((batch, WINDOW, SWA_KV), F32),
            jax.ShapeDtypeStruct((batch, WINDOW, SWA_KV), F32),
        ] + [s[2] for s in prep_specs],
        scratch_shapes=_mixer_scratch(tile, n_chunks, WINDOW + tile) + [
            pltpu.VMEM((state_rows, GLA_DV), F32),
        ],
        compiler_params=pltpu.CompilerParams(
            dimension_semantics=("arbitrary", "arbitrary"), vmem_limit_bytes=VMEM_LIMIT),
        name="mixer_prompt",
    )(x, mod, *weights, *[p.weight for p in preps])


def _mixer_sample_call(layer, x, mod, weights, s0, k_cache, v_cache):
    batch, seq, _ = x.shape
    assert seq == CHUNK
    rows = batch * seq
    state_rows = GLA_HEADS * GLA_DK

    def full(shape):
        zeros = (0,) * len(shape)
        return pl.BlockSpec(shape, lambda i: zeros)

    def of_layer(shape):
        index = (layer,) + (0,) * len(shape)
        return pl.BlockSpec((None,) + shape, lambda i: index)

    return pl.pallas_call(
        functools.partial(_mixer_kernel, True, batch, layer, ()),
        grid=(1,),
        in_specs=[full((rows, D_MODEL)), of_layer((batch, 6, D_MODEL))] + _mixer_weight_specs(layer) + [
            of_layer((batch, state_rows, GLA_DV)),
            of_layer((batch, WINDOW, SWA_KV)),
            of_layer((batch, WINDOW, SWA_KV)),
        ],
        out_specs=[
            full((rows, D_MODEL)),
            full((batch, state_rows, GLA_DV)),
            full((rows, SWA_KV)),
            full((rows, SWA_KV)),
        ],
        out_shape=[
            jax.ShapeDtypeStruct((rows, D_MODEL), F32),
            jax.ShapeDtypeStruct((batch, state_rows, GLA_DV), F32),
            jax.ShapeDtypeStruct((rows, SWA_KV), F32),
            jax.ShapeDtypeStruct((rows, SWA_KV), F32),
        ],
        scratch_shapes=_mixer_scratch(rows, batch, batch * BAND),
        compiler_params=pltpu.CompilerParams(
            dimension_semantics=("arbitrary",), vmem_limit_bytes=VMEM_LIMIT),
        name="mixer_sample",
    )(x.reshape(rows, D_MODEL), mod, *weights, s0, k_cache, v_cache)


def _ffn_kernel(sample, final, n_seg, seg_len, preps, *refs):
    if sample:
        (x_ref, mod_ref, g_ffn_ref, w_up_ref, conv_w_ref, conv_b_ref, w_down_ref, g_final_ref,
         past_ref, xo_ref, co_ref, h_scr, ub_scr, act_scr) = refs
    else:
        n_in, n_out, n_prep = 8, 2, len(preps)
        prep_src = refs[n_in:n_in + n_prep]
        prep_dst = refs[n_in + n_prep + n_out:n_in + n_prep + n_out + n_prep]
        refs = refs[:n_in] + refs[n_in + n_prep:n_in + n_prep + n_out] + refs[n_in + 2 * n_prep + n_out:]
        (x_ref, mod_ref, g_ffn_ref, w_up_ref, conv_w_ref, conv_b_ref, w_down_ref, g_final_ref,
         xo_ref, co_ref, h_scr, ub_scr, act_scr, past_scr) = refs
        for body, src, dst in zip(preps, prep_src, prep_dst):
            body(src, dst)
        t = pl.program_id(1)
        n_t = pl.num_programs(1)

        @pl.when(t == 0)
        def _():
            past_scr[...] = jnp.zeros_like(past_scr)

    stride = seg_len + SUBLANE
    g_ffn = g_ffn_ref[...]

    def mod_row(c, idx):
        if sample:
            return mod_ref[c, idx:idx + 1, :]
        return mod_ref[idx:idx + 1, :]

    for c in range(n_seg):
        rs = slice(c * seg_len, (c + 1) * seg_len)
        x = x_ref[rs, :]
        gain = g_ffn * (1.0 + mod_row(c, 4))
        h_scr[rs, :] = (x * _rms_scale(x) * gain + mod_row(c, 3)).astype(BF16)

    def up(j):
        lo, hi = FF_PARTS[j]
        h = h_scr[...]
        return _dot(h, w_up_ref[:, lo:hi]), _dot(h, w_up_ref[:, D_FF + lo:D_FF + hi])

    def activate(j, u, val):
        lo, hi = FF_PARTS[j]
        width = hi - lo
        w0 = conv_w_ref[0:1, lo:hi]
        w1 = conv_w_ref[1:2, lo:hi]
        w2 = conv_w_ref[2:3, lo:hi]
        cb = conv_b_ref[:, lo:hi]
        for c in range(n_seg):
            base = c * stride
            rs = slice(c * seg_len, (c + 1) * seg_len)
            if sample:
                ub_scr[base + SUBLANE - 2:base + SUBLANE, 0:width] = past_ref[c, :, lo:hi]
            else:
                ub_scr[base + SUBLANE - 2:base + SUBLANE, 0:width] = past_scr[:, lo:hi]
            u_seg = u[rs, :]
            ub_scr[base + SUBLANE:base + SUBLANE + seg_len, 0:width] = u_seg
            u1 = ub_scr[base + SUBLANE - 1:base + SUBLANE - 1 + seg_len, 0:width]
            u2 = ub_scr[base + SUBLANE - 2:base + SUBLANE - 2 + seg_len, 0:width]
            uc = w0 * u2 + w1 * u1 + w2 * u_seg + cb
            act_scr[rs, lo:hi] = (uc * _sigmoid(uc) * val[rs, :]).astype(BF16)
            tail = ub_scr[base + seg_len + SUBLANE - 2:base + seg_len + SUBLANE, 0:width]
            if sample:
                co_ref[c, :, lo:hi] = tail
            else:
                past_scr[:, lo:hi] = tail

    def down(j):
        lo, hi = FF_PARTS[j]
        return _dot(act_scr[:, lo:hi], w_down_ref[lo:hi, :])

    n_parts = len(FF_PARTS)
    pending = {0: up(0)}
    acc = None
    for j in range(n_parts):
        if j + 1 < n_parts:
            pending[j + 1] = up(j + 1)
        activate(j, *pending.pop(j))
        if j > 0:
            d = down(j - 1)
            acc = d if acc is None else acc + d
    acc = acc + down(n_parts - 1)

    for c in range(n_seg):
        rs = slice(c * seg_len, (c + 1) * seg_len)
        y = x_ref[rs, :] + mod_row(c, 5) * acc[rs, :]
        if final:
            y = y * _rms_scale(y) * g_final_ref[...]
        xo_ref[rs, :] = y

    if not sample:
        @pl.when(t == n_t - 1)
        def _():
            co_ref[...] = past_scr[...]


def _ffn_weight_specs(layer):
    return [
        _layer_spec(layer, (1, D_MODEL)),
        _layer_spec(0, (D_MODEL, 2 * D_FF)),
        _layer_spec(layer, (CONV_W, D_FF)),
        _layer_spec(layer, (1, D_FF)),
        _layer_spec(0, (D_FF, D_MODEL)),
        _const_spec((1, D_MODEL)),
    ]


def _ffn_prompt_call(layer, x, mod, mod_row0, weights, final, preps):
    batch, seq, _ = x.shape
    tile = min(PROMPT_TILE, seq)
    n_t = seq // tile
    prep_specs = [p.specs(n_t) for p in preps]
    assert all(p.n_blocks <= batch * n_t for p in preps)
    return pl.pallas_call(
        functools.partial(_ffn_kernel, False, final, 1, tile, tuple(p.body for p in preps)),
        grid=(batch, n_t),
        in_specs=[
            pl.BlockSpec((None, tile, D_MODEL), lambda b, t: (b, t, 0)),
            pl.BlockSpec((None, None, 6, D_MODEL), lambda b, t: (layer, mod_row0 + b, 0, 0)),
        ] + _ffn_weight_specs(layer) + [s[0] for s in prep_specs],
        out_specs=[
            pl.BlockSpec((None, tile, D_MODEL), lambda b, t: (b, t, 0)),
            pl.BlockSpec((None, CONV_W - 1, D_FF), lambda b, t: (b, 0, 0)),
        ] + [s[1] for s in prep_specs],
        out_shape=[
            jax.ShapeDtypeStruct((batch, seq, D_MODEL), F32),
            jax.ShapeDtypeStruct((batch, CONV_W - 1, D_FF), F32),
        ] + [s[2] for s in prep_specs],
        scratch_shapes=[
            pltpu.VMEM((tile, D_MODEL), BF16),
            pltpu.VMEM((tile + SUBLANE, FF_PART), F32),
            pltpu.VMEM((tile, D_FF), BF16),
            pltpu.VMEM((CONV_W - 1, D_FF), F32),
        ],
        compiler_params=pltpu.CompilerParams(
            dimension_semantics=("arbitrary", "arbitrary"), vmem_limit_bytes=VMEM_LIMIT),
        name="ffn_prompt",
    )(x, mod, *weights, *[p.weight for p in preps])


def _ffn_sample_call(layer, x2d, mod, weights, past, final, batch, seq):
    rows = batch * seq

    def full(shape):
        zeros = (0,) * len(shape)
        return pl.BlockSpec(shape, lambda i: zeros)

    def of_layer(shape):
        index = (layer,) + (0,) * len(shape)
        return pl.BlockSpec((None,) + shape, lambda i: index)

    return pl.pallas_call(
        functools.partial(_ffn_kernel, True, final, batch, seq, ()),
        grid=(1,),
        in_specs=[full((rows, D_MODEL)), of_layer((batch, 6, D_MODEL))] + _ffn_weight_specs(layer) + [
            of_layer((batch, CONV_W - 1, D_FF)),
        ],
        out_specs=[full((rows, D_MODEL)), full((batch, CONV_W - 1, D_FF))],
        out_shape=[
            jax.ShapeDtypeStruct((rows, D_MODEL), F32),
            jax.ShapeDtypeStruct((batch, CONV_W - 1, D_FF), F32),
        ],
        scratch_shapes=[
            pltpu.VMEM((rows, D_MODEL), BF16),
            pltpu.VMEM((batch * (seq + SUBLANE), FF_PART), F32),
            pltpu.VMEM((rows, D_FF), BF16),
        ],
        compiler_params=pltpu.CompilerParams(
            dimension_semantics=("arbitrary",), vmem_limit_bytes=VMEM_LIMIT),
        name="ffn_sample",
    )(x2d, mod, *weights, past)


def _pad_w_in_first(w):
    n_blocks = D_MODEL // W_PREP_ROWS
    return pl.pallas_call(
        _pad_cast_w_in,
        grid=(n_blocks,),
        in_specs=[pl.BlockSpec((None, W_PREP_ROWS, D_IN), lambda r: (0, r, 0))],
        out_specs=pl.BlockSpec((None, W_PREP_ROWS, D_IN_P), lambda r: (0, r, 0)),
        out_shape=jax.ShapeDtypeStruct((1, D_MODEL, D_IN_P), BF16),
        compiler_params=pltpu.CompilerParams(
            dimension_semantics=("arbitrary",), vmem_limit_bytes=VMEM_LIMIT),
        name="pad_w_in",
    )(w)


def kernel(x_prompt, x_sample, state_gla, cache_swa_k, cache_swa_v, state_conv, c_prompt, c_sample,
           w_ada, b_ada, g_attn, g_ffn, w_in, w_gk2, b_gk, g_gla, sinks, w_out, w_up, conv_w, conv_b,
           w_down, g_final):
    depth = w_ada.shape[0]
    batch, seq, _ = x_prompt.shape
    dec_batch, dec_seq, _ = x_sample.shape
    state_rows = GLA_HEADS * GLA_DK

    c_all = jnp.concatenate(
        [c_sample, c_prompt, jnp.zeros((ADA_ROWS - batch - dec_batch, D_MODEL), F32)], axis=0)
    mod_all = _ada_call(c_all, w_ada, b_ada).reshape(depth, ADA_ROWS, 6, D_MODEL)

    w_gk2_p = jnp.concatenate(
        [w_gk2, jnp.zeros((depth, LR_PAD - GLA_RANK, GLA_QK), F32)], axis=1).astype(BF16)
    w_in16 = w_in.astype(BF16)
    w_in_b, w_out_b = _pad_w_in_first(w_in16), w_out[:1].astype(BF16)
    w_up_b, w_down_b = w_up[:1].astype(BF16), w_down[:1].astype(BF16)
    mixer_steps = batch * (seq // min(MIXER_TILE, seq))
    ffn_steps = batch * (seq // min(PROMPT_TILE, seq))
    prep_ahead = (D_MODEL // MIXER_PREP_ROWS <= mixer_steps
                  and D_MODEL // FFN_PREP_ROWS_UP <= ffn_steps
                  and D_FF // FFN_PREP_ROWS_DOWN <= ffn_steps)
    s0_all = state_gla.reshape(depth, dec_batch, state_rows, GLA_DV)
    kc_all = cache_swa_k.reshape(depth, dec_batch, WINDOW, SWA_KV)
    vc_all = cache_swa_v.reshape(depth, dec_batch, WINDOW, SWA_KV)

    yp = x_prompt
    ys = x_sample.reshape(dec_batch * dec_seq, D_MODEL)
    outs = [[] for _ in range(8)]
    for i in range(depth):
        final = i == depth - 1
        mixer_w = (g_attn[:, None], w_in_b, w_gk2_p, b_gk[:, None], g_gla[:, None], sinks, w_out_b)
        ffn_w = (g_ffn[:, None], w_up_b, conv_w, conv_b[:, None], w_down_b, g_final[None])
        mixer_preps, ffn_preps = [], []
        if not final and not prep_ahead:
            next_mixer_w = [_pad_w_in_first(w_in16[i + 1:i + 2]), w_out[i + 1:i + 2].astype(BF16)]
            next_ffn_w = [w_up[i + 1:i + 2].astype(BF16), w_down[i + 1:i + 2].astype(BF16)]
        elif not final:
            mixer_preps = [_WeightPrep(w_in16, i + 1, MIXER_PREP_ROWS, D_IN_P, _pad_cast_w_in),
                           _WeightPrep(w_out, i + 1, MIXER_PREP_ROWS, D_MODEL, _cast_weight)]
            ffn_preps = [_WeightPrep(w_up, i + 1, FFN_PREP_ROWS_UP, 2 * D_FF, _cast_weight),
                         _WeightPrep(w_down, i + 1, FFN_PREP_ROWS_DOWN, D_MODEL, _cast_weight)]

        yp, s_p, k_p, v_p, *prepared = _mixer_prompt_call(i, yp, mod_all, dec_batch, mixer_w, mixer_preps)
        if mixer_preps:
            next_mixer_w = prepared
        yp, conv_p, *prepared = _ffn_prompt_call(i, yp, mod_all, dec_batch, ffn_w, final, ffn_preps)
        if ffn_preps:
            next_ffn_w = prepared

        ys, s_s, k_s, v_s = _mixer_sample_call(
            i, ys.reshape(dec_batch, dec_seq, D_MODEL), mod_all, mixer_w, s0_all, kc_all, vc_all)
        ys, conv_s = _ffn_sample_call(i, ys, mod_all, ffn_w, state_conv, final, dec_batch, dec_seq)
        if not final:
            (w_in_b, w_out_b), (w_up_b, w_down_b) = next_mixer_w, next_ffn_w

        keep = min(WINDOW, seq)
        outs[0].append(s_p.reshape(batch, GLA_HEADS, GLA_DK, GLA_DV))
        outs[1].append(k_p.reshape(batch, keep, SWA_KV_HEADS, SWA_HD))
        outs[2].append(v_p.reshape(batch, keep, SWA_KV_HEADS, SWA_HD))
        outs[3].append(conv_p)
        outs[4].append(s_s.reshape(dec_batch, GLA_HEADS, GLA_DK, GLA_DV))
        outs[5].append(k_s.reshape(dec_batch, dec_seq, SWA_KV_HEADS, SWA_HD))
        outs[6].append(v_s.reshape(dec_batch, dec_seq, SWA_KV_HEADS, SWA_HD))
        outs[7].append(conv_s)

    return (yp, ys.reshape(dec_batch, dec_seq, D_MODEL)) + tuple(jnp.stack(o) for o in outs)
```

```python
import functools

import jax
import jax.numpy as jnp
from jax import lax
from jax.experimental import pallas as pl
from jax.experimental.pallas import tpu as pltpu

F32 = jnp.float32
BF16 = jnp.bfloat16

D_MODEL = 1024
CHUNK = 64
GLA_HEADS = 4
GLA_DK = 64
GLA_DV = 128
GLA_RANK = 16
GLA_NORMALIZER = 16.0
SWA_Q_HEADS = 8
SWA_KV_HEADS = 2
SWA_GROUP = SWA_Q_HEADS // SWA_KV_HEADS
SWA_HD = 64
WINDOW = 128
D_FF = 2816
CONV_W = 3
RMS_EPS = 1e-6

GLA_QK = GLA_HEADS * GLA_DK
GLA_V = GLA_HEADS * GLA_DV
SWA_Q = SWA_Q_HEADS * SWA_HD
SWA_KV = SWA_KV_HEADS * SWA_HD
D_MIX = GLA_V + SWA_Q
BAND = WINDOW + CHUNK

LANE = 128
SUBLANE = 8

C_QG = 0
C_KG = C_QG + GLA_QK
C_VG = C_KG + GLA_QK
C_OG = C_VG + GLA_V
C_LR = C_OG + GLA_V
LR_PAD = LANE
C_QS = C_LR + LR_PAD
C_KS = C_QS + SWA_Q
C_VS = C_KS + SWA_KV
D_IN_P = C_VS + SWA_KV
D_IN = 2 * GLA_QK + 2 * GLA_V + GLA_RANK + SWA_Q + 2 * SWA_KV

assert GLA_DK * 2 == LANE and SWA_HD * 2 == LANE and GLA_DV == LANE and SWA_KV == LANE
GLA_PAIRS = GLA_HEADS // 2
PAIR_V = 2 * GLA_DV
SWA_GRP_Q = SWA_GROUP * SWA_HD

ADA_TILE = 1536
ADA_ROWS = 16
PROMPT_TILE = 512
MIXER_TILE = 1024
SUB_CHUNKS = 4
DENSE_PIECE = 256
DENSE_LEAD = 2
W_PREP_ROWS = 256
MIXER_PREP_ROWS_UP = 32
MIXER_PREP_ROWS_DOWN = 128
FFN_PREP_ROWS = 16
FF_PART = 768
FF_PARTS = tuple((lo, min(lo + FF_PART, D_FF)) for lo in range(0, D_FF, FF_PART))
VMEM_LIMIT = 56 * 1024 * 1024

NT_DIMS = (((1,), (1,)), ((), ()))
TN_DIMS = (((0,), (0,)), ((), ()))


def _dot(a, b):
    return jnp.dot(a, b, preferred_element_type=F32)


def _dot_nt(a, b):
    return lax.dot_general(a, b, NT_DIMS, preferred_element_type=F32)


def _dot_tn(a, b):
    return lax.dot_general(a, b, TN_DIMS, preferred_element_type=F32)


def _split_bf16(a):
    hi = a.astype(BF16)
    lo = (a - hi.astype(F32)).astype(BF16)
    return hi, lo


def _sigmoid(a):
    return 1.0 / (1.0 + jnp.exp(-a))


def _log_sigmoid(a):
    return jnp.minimum(a, 0.0) - jnp.log(1.0 + jnp.exp(-jnp.abs(a)))


def _rms_scale(a):
    return lax.rsqrt(jnp.mean(a * a, axis=-1, keepdims=True) + RMS_EPS)


def _ada_kernel(c_ref, w_ref, b_ref, o_ref):
    c = c_ref[...]
    a = (c * _sigmoid(c)).astype(BF16)
    o_ref[...] = _dot(a, w_ref[...].astype(BF16)) + b_ref[...]


def _ada_call(c_all, w_ada, b_ada):
    depth = w_ada.shape[0]
    n_tiles = (6 * D_MODEL) // ADA_TILE
    return pl.pallas_call(
        _ada_kernel,
        grid=(depth, n_tiles),
        in_specs=[
            pl.BlockSpec((ADA_ROWS, D_MODEL), lambda l, j: (0, 0)),
            pl.BlockSpec((None, D_MODEL, ADA_TILE), lambda l, j: (l, 0, j)),
            pl.BlockSpec((None, 1, ADA_TILE), lambda l, j: (l, 0, j)),
        ],
        out_specs=pl.BlockSpec((None, ADA_ROWS, ADA_TILE), lambda l, j: (l, 0, j)),
        out_shape=jax.ShapeDtypeStruct((depth, ADA_ROWS, 6 * D_MODEL), F32),
        compiler_params=pltpu.CompilerParams(
            dimension_semantics=("arbitrary", "arbitrary"), vmem_limit_bytes=VMEM_LIMIT),
        name="adaln_mod",
    )(c_all, w_ada, b_ada.reshape(depth, 1, 6 * D_MODEL))


def _dup_halves(a, low_half):
    swapped = pltpu.roll(a, LANE // 2, axis=1)
    return jnp.where(low_half, a, swapped), jnp.where(low_half, swapped, a)


def _mixer_kernel(sample, n_chunks, layer, preps, *refs):
    if sample:
        (x_ref, mod_ref, g_attn_ref, w_in_ref, w_gk2_ref, b_gk_ref, g_gla_ref, sinks_ref, w_out_ref,
         s0_ref, kc_ref, vc_ref,
         xo_ref, so_ref, ko_ref, vo_ref,
         h_scr, proj_scr, omix_scr, qm_scr, km_scr, qi_scr, ko_scr, vb_scr, gk_scr, dec_scr, u_scr, sb_scr,
         att_scr, pn_scr,
         kd_scr, vd_scr) = refs
        t = None
    else:
        n_in, n_out, n_prep = 9, 4, len(preps)
        prep_src = refs[n_in:n_in + n_prep]
        prep_dst = refs[n_in + n_prep + n_out:n_in + n_prep + n_out + n_prep]
        refs = refs[:n_in] + refs[n_in + n_prep:n_in + n_prep + n_out] + refs[n_in + 2 * n_prep + n_out:]
        (x_ref, mod_ref, g_attn_ref, w_in_ref, w_gk2_ref, b_gk_ref, g_gla_ref, sinks_ref, w_out_ref,
         xo_ref, so_ref, ko_ref, vo_ref,
         h_scr, proj_scr, omix_scr, qm_scr, km_scr, qi_scr, ko_scr, vb_scr, gk_scr, dec_scr, u_scr, sb_scr,
         att_scr, pn_scr,
         kd_scr, vd_scr, s_scr) = refs
        for body, src, dst in zip(preps, prep_src, prep_dst):
            body(src, dst)
        t = pl.program_id(1)
        n_t = pl.num_programs(1)

        @pl.when(t == 0)
        def _():
            s_scr[...] = jnp.zeros_like(s_scr)
            kd_scr[0:WINDOW, :] = jnp.zeros((WINDOW, 2 * LANE), BF16)
            vd_scr[0:WINDOW, :] = jnp.zeros((WINDOW, 2 * LANE), BF16)

    rows = n_chunks * CHUNK
    g_attn = g_attn_ref[...]

    def mod_row(c, idx):
        if sample:
            return mod_ref[c, idx:idx + 1, :]
        return mod_ref[idx:idx + 1, :]

    def chunk_rows(c):
        return slice(c * CHUNK, (c + 1) * CHUNK)

    sub_chunks = min(SUB_CHUNKS, n_chunks)
    n_sub = n_chunks // sub_chunks
    sub_rows = sub_chunks * CHUNK

    def sub_slice(s):
        return slice(s * sub_rows, (s + 1) * sub_rows)

    def norm_groups(s):
        if sample:
            return [(c, chunk_rows(c)) for c in range(s * sub_chunks, (s + 1) * sub_chunks)]
        return [(0, sub_slice(s))]

    r64 = lax.broadcasted_iota(jnp.int32, (CHUNK, CHUNK), 0)
    c64 = lax.broadcasted_iota(jnp.int32, (CHUNK, CHUNK), 1)
    tri = jnp.where(r64 >= c64, 1.0, 0.0).astype(BF16)
    tri2 = jnp.concatenate([tri, tri], axis=1)
    l_pair = lax.broadcasted_iota(jnp.int32, (CHUNK, LANE), 1)
    low_half = l_pair < LANE // 2
    causal_heads = (lax.broadcasted_iota(jnp.int32, (CHUNK, GLA_QK), 0)
                    >= (lax.broadcasted_iota(jnp.int32, (CHUNK, GLA_QK), 1) & (CHUNK - 1)))
    diag_heads = ((lax.broadcasted_iota(jnp.int32, (GLA_QK, GLA_QK), 0) // GLA_DK)
                  == (lax.broadcasted_iota(jnp.int32, (GLA_QK, GLA_QK), 1) // GLA_DK))
    diag_v = ((lax.broadcasted_iota(jnp.int32, (LANE, PAIR_V), 0) >= LANE // 2)
              == (lax.broadcasted_iota(jnp.int32, (LANE, PAIR_V), 1) >= GLA_DV))
    g_gla = g_gla_ref[...]
    low_sub = lax.broadcasted_iota(jnp.int32, (sub_rows, LANE), 1) < LANE // 2

    def project_steps(s):
        ss = sub_slice(s)

        def norm():
            for mi, rs in norm_groups(s):
                x = x_ref[rs, :]
                gain = g_attn * (1.0 + mod_row(mi, 1))
                h_scr[rs, :] = (x * _rms_scale(x) * gain + mod_row(mi, 0)).astype(BF16)

        def piece(lo, hi):
            proj_scr[ss, lo:hi] = _dot(h_scr[ss, :], w_in_ref[:, lo:hi])

        def gates():
            lr = proj_scr[ss, C_LR:C_LR + LR_PAD].astype(BF16)
            gk_scr[ss, :] = (_log_sigmoid(_dot(lr, w_gk2_ref[...]) + b_gk_ref[...])
                             * (1.0 / GLA_NORMALIZER))
            vb_scr[ss, :] = proj_scr[ss, C_VG:C_VG + GLA_V].astype(BF16)

        def bands():
            k_dup = _dup_halves(proj_scr[ss, C_KS:C_KS + SWA_KV], low_sub)
            v_dup = _dup_halves(proj_scr[ss, C_VS:C_VS + SWA_KV], low_sub)
            for kv in range(SWA_KV_HEADS):
                ls = slice(kv * LANE, (kv + 1) * LANE)
                if sample:
                    for c in range(s * sub_chunks, (s + 1) * sub_chunks):
                        local = slice((c - s * sub_chunks) * CHUNK, (c - s * sub_chunks + 1) * CHUNK)
                        kd_scr[c * BAND + WINDOW:(c + 1) * BAND, ls] = k_dup[kv][local, :].astype(BF16)
                        vd_scr[c * BAND + WINDOW:(c + 1) * BAND, ls] = v_dup[kv][local, :].astype(BF16)
                else:
                    band_rows = slice(WINDOW + s * sub_rows, WINDOW + (s + 1) * sub_rows)
                    kd_scr[band_rows, ls] = k_dup[kv].astype(BF16)
                    vd_scr[band_rows, ls] = v_dup[kv].astype(BF16)

        steps = [norm]
        for lo in range(0, D_IN_P, DENSE_PIECE):
            steps.append(functools.partial(piece, lo, min(lo + DENSE_PIECE, D_IN_P)))
        return steps + [gates, bands]

    def gla_prepare(c):
        rs = chunk_rows(c)
        gk_hi, gk_lo = _split_bf16(gk_scr[rs, :])
        cum = _dot(tri2, jnp.concatenate([gk_hi, gk_lo], axis=0))
        mid = cum[CHUNK // 2:CHUNK // 2 + 1, :]
        last = cum[CHUNK - 1:CHUNK, :]
        q = proj_scr[rs, C_QG:C_QG + GLA_QK] * (GLA_DK ** -0.5)
        k = proj_scr[rs, C_KG:C_KG + GLA_QK]
        qm_scr[rs, :] = (q * jnp.exp(cum - mid)).astype(BF16)
        km_scr[rs, :] = (k * jnp.exp(mid - cum)).astype(BF16)
        qi_scr[rs, :] = (q * jnp.exp(cum)).astype(BF16)
        ko_scr[rs, :] = (k * jnp.exp(last - cum)).astype(BF16)
        dec_scr[c] = jnp.broadcast_to(jnp.exp(last), (LANE, GLA_QK)).T

    def gla_increment(c):
        rs = chunk_rows(c)
        for p in range(GLA_PAIRS):
            upd = _dot_tn(ko_scr[rs, p * LANE:(p + 1) * LANE], vb_scr[rs, p * PAIR_V:(p + 1) * PAIR_V])
            u_scr[c, p * LANE:p * LANE + GLA_DK, :] = upd[0:GLA_DK, 0:GLA_DV]
            u_scr[c, p * LANE + GLA_DK:(p + 1) * LANE, :] = upd[GLA_DK:LANE, GLA_DV:PAIR_V]

    def gla_recurrence(s):
        state = None if sample else s_scr[...]
        for c in range(s * sub_chunks, (s + 1) * sub_chunks):
            if sample:
                state = s0_ref[c]
            sb_scr[c] = state.astype(BF16)
            state = dec_scr[c] * state + u_scr[c]
            if sample:
                so_ref[c] = state
        if not sample:
            s_scr[...] = state

    def gla_scores(c):
        rs = chunk_rows(c)
        km = km_scr[rs, :]
        k_bd = jnp.where(diag_heads, jnp.concatenate([km] * GLA_HEADS, axis=0), 0.0)
        att_scr[rs, :] = jnp.where(causal_heads, _dot_nt(qm_scr[rs, :], k_bd), 0.0).astype(BF16)

    def gla_output(c, p):
        rs = chunk_rows(c)
        ls = slice(p * LANE, (p + 1) * LANE)
        v_pair = vb_scr[rs, p * PAIR_V:(p + 1) * PAIR_V]
        s_pair = sb_scr[c, ls, :]
        w_top = jnp.where(diag_v, jnp.concatenate([v_pair, v_pair], axis=0), 0.0)
        w_bot = jnp.where(diag_v, jnp.concatenate([s_pair, s_pair], axis=1), 0.0)
        o_pair = _dot(jnp.concatenate([att_scr[rs, ls], qi_scr[rs, ls]], axis=1),
                      jnp.concatenate([w_top, w_bot], axis=0))
        for hh in range(2):
            h = 2 * p + hh
            o = o_pair[:, hh * GLA_DV:(hh + 1) * GLA_DV]
            og = proj_scr[rs, C_OG + h * GLA_DV:C_OG + (h + 1) * GLA_DV]
            y = o * _rms_scale(o) * g_gla * (og * _sigmoid(og))
            omix_scr[rs, h * GLA_DV:(h + 1) * GLA_DV] = y.astype(BF16)

    if sample:
        low_win = lax.broadcasted_iota(jnp.int32, (WINDOW, LANE), 1) < LANE // 2
        for c in range(n_chunks):
            kc_dup = _dup_halves(kc_ref[c], low_win)
            vc_dup = _dup_halves(vc_ref[c], low_win)
            for kv in range(SWA_KV_HEADS):
                ls = slice(kv * LANE, (kv + 1) * LANE)
                kd_scr[c * BAND:c * BAND + WINDOW, ls] = kc_dup[kv].astype(BF16)
                vd_scr[c * BAND:c * BAND + WINDOW, ls] = vc_dup[kv].astype(BF16)

    lane_q = lax.broadcasted_iota(jnp.int32, (1, SWA_GRP_Q), 1)
    key_ids = lax.broadcasted_iota(jnp.int32, (BAND, SWA_GRP_Q), 0)
    sink_vecs = []
    for kv in range(SWA_KV_HEADS):
        vec = jnp.full((1, SWA_GRP_Q), sinks_ref[layer, kv * SWA_GROUP + SWA_GROUP - 1], F32)
        for g in range(SWA_GROUP - 2, -1, -1):
            vec = jnp.where(lane_q < (g + 1) * SWA_HD, sinks_ref[layer, kv * SWA_GROUP + g], vec)
        sink_vecs.append(vec)

    def band_rows(c):
        return slice(c * BAND, (c + 1) * BAND) if sample else slice(c * CHUNK, c * CHUNK + BAND)

    def swa_scores(c, kv):
        rs = chunk_rows(c)
        band = band_rows(c)
        ls = slice(kv * LANE, (kv + 1) * LANE)
        qg = (proj_scr[rs, C_QS + kv * SWA_GRP_Q:C_QS + (kv + 1) * SWA_GRP_Q]
              * (SWA_HD ** -0.5)).astype(BF16)
        q_stack = jnp.concatenate(
            [jnp.where(low_half if hh == 0 else ~low_half, qg[:, pp * LANE:(pp + 1) * LANE], 0.0)
             for pp in range(SWA_GROUP // 2) for hh in range(2)], axis=0)
        s_t = _dot_nt(kd_scr[band, ls], q_stack)
        if not sample and c * CHUNK < WINDOW:
            first_valid = WINDOW - (t * rows + c * CHUNK)
            s_t = jnp.where(key_ids >= first_valid, s_t, -jnp.inf)
        sink = sink_vecs[kv]
        m = jnp.maximum(jnp.max(s_t, axis=0, keepdims=True), sink)
        p_t = jnp.exp(s_t - m)
        den = jnp.sum(p_t, axis=0, keepdims=True) + jnp.exp(sink - m)
        pn_scr[c * SWA_KV_HEADS + kv] = (p_t * (1.0 / den)).astype(BF16)

    def swa_output(c, kv):
        rs = chunk_rows(c)
        ls = slice(kv * LANE, (kv + 1) * LANE)
        o_t = _dot_tn(vd_scr[band_rows(c), ls], pn_scr[c * SWA_KV_HEADS + kv]).T
        for pp in range(SWA_GROUP // 2):
            o_pair = jnp.where(low_half, o_t[(2 * pp) * CHUNK:(2 * pp + 1) * CHUNK, :],
                               o_t[(2 * pp + 1) * CHUNK:(2 * pp + 2) * CHUNK, :])
            col = GLA_V + kv * SWA_GRP_Q + pp * LANE
            omix_scr[rs, col:col + LANE] = o_pair.astype(BF16)

    def out_steps(s):
        ss = sub_slice(s)
        base = s * sub_rows

        def piece(lo, hi):
            mix = _dot(omix_scr[ss, :], w_out_ref[:, lo:hi])
            for mi, rs in norm_groups(s):
                local = slice(rs.start - base, rs.stop - base)
                xo_ref[rs, lo:hi] = x_ref[rs, lo:hi] + mod_row(mi, 2)[:, lo:hi] * mix[local, :]

        return [functools.partial(piece, lo, lo + DENSE_PIECE) for lo in range(0, D_MODEL, DENSE_PIECE)]

    def block_steps(s):
        blocks = range(s * sub_chunks, (s + 1) * sub_chunks)
        pairs = [(c, p) for c in blocks for p in range(GLA_PAIRS)]
        groups = [(c, kv) for c in blocks for kv in range(SWA_KV_HEADS)]
        steps = [functools.partial(gla_prepare, c) for c in blocks]
        steps += [functools.partial(swa_scores, c, kv) for c, kv in groups]
        steps += [functools.partial(gla_increment, c) for c in blocks]
        steps += [functools.partial(gla_scores, c) for c in blocks]
        steps.append(functools.partial(gla_recurrence, s))
        steps += [functools.partial(swa_output, c, kv) for c, kv in groups]
        steps += [functools.partial(gla_output, c, p) for c, p in pairs]
        return steps

    for step in project_steps(0):
        step()
    for s in range(n_sub):
        dense = (project_steps(s + 1) if s + 1 < n_sub else []) + (out_steps(s - 1) if s > 0 else [])
        work = block_steps(s)
        issued = 0
        for i, step in enumerate(work):
            while issued < len(dense) and (issued - DENSE_LEAD) * len(work) < (i + 1) * len(dense):
                dense[issued]()
                issued += 1
            step()
        for step in dense[issued:]:
            step()
    for step in out_steps(n_sub - 1):
        step()

    if sample:
        ko_ref[...] = proj_scr[:, C_KS:C_KS + SWA_KV]
        vo_ref[...] = proj_scr[:, C_VS:C_VS + SWA_KV]
    else:
        kd_scr[0:WINDOW, :] = kd_scr[rows:rows + WINDOW, :]
        vd_scr[0:WINDOW, :] = vd_scr[rows:rows + WINDOW, :]

        @pl.when(t == n_t - 1)
        def _():
            so_ref[...] = s_scr[...]
            ko_ref[...] = proj_scr[rows - WINDOW:rows, C_KS:C_KS + SWA_KV]
            vo_ref[...] = proj_scr[rows - WINDOW:rows, C_VS:C_VS + SWA_KV]


def _pad_cast_w_in(w_ref, o_ref):
    head = C_LR + GLA_RANK
    w = w_ref[...].astype(F32)
    o_ref[:, 0:C_LR] = w[:, 0:C_LR].astype(BF16)
    gate = jnp.concatenate(
        [w[:, C_LR:head], jnp.zeros((w.shape[0], LR_PAD - GLA_RANK), F32)], axis=1)
    o_ref[:, C_LR:C_QS] = gate.astype(BF16)
    o_ref[:, C_QS:D_IN_P] = w[:, head:D_IN].astype(BF16)


def _cast_weight(w_ref, o_ref):
    o_ref[...] = w_ref[...].astype(BF16)


class _WeightPrep:
    def __init__(self, weight, layer, rows_per_step, out_cols, body):
        self.weight, self.layer, self.rows_per_step = weight, layer, rows_per_step
        self.rows, self.in_cols = weight.shape[1], weight.shape[2]
        self.out_cols, self.body = out_cols, body
        assert self.rows % rows_per_step == 0
        self.n_blocks = self.rows // rows_per_step

    def specs(self, n_t):
        def block(b, t):
            return jnp.minimum(b * n_t + t, self.n_blocks - 1)
        layer = self.layer
        return (pl.BlockSpec((None, self.rows_per_step, self.in_cols), lambda b, t: (layer, block(b, t), 0)),
                pl.BlockSpec((None, self.rows_per_step, self.out_cols), lambda b, t: (0, block(b, t), 0)),
                jax.ShapeDtypeStruct((1, self.rows, self.out_cols), BF16))


def _const_spec(shape):
    zeros = (0,) * len(shape)
    return pl.BlockSpec(shape, lambda *_: zeros, pipeline_mode=pl.Buffered(1))


def _layer_spec(layer, shape):
    index = (layer,) + (0,) * len(shape)
    return pl.BlockSpec((None,) + shape, lambda *_: index, pipeline_mode=pl.Buffered(1))


def _mixer_weight_specs(layer):
    return [
        _layer_spec(layer, (1, D_MODEL)),
        _layer_spec(0, (D_MODEL, D_IN_P)),
        _layer_spec(layer, (LR_PAD, GLA_QK)),
        _layer_spec(layer, (1, GLA_QK)),
        _layer_spec(layer, (1, GLA_DV)),
        pl.BlockSpec(memory_space=pltpu.SMEM),
        _layer_spec(0, (D_MIX, D_MODEL)),
    ]


def _mixer_scratch(rows, n_chunks, band_rows):
    state_rows = GLA_HEADS * GLA_DK
    return [
        pltpu.VMEM((rows, D_MODEL), BF16),
        pltpu.VMEM((rows, D_IN_P), F32),
        pltpu.VMEM((rows, D_MIX), BF16),
        pltpu.VMEM((rows, GLA_QK), BF16),
        pltpu.VMEM((rows, GLA_QK), BF16),
        pltpu.VMEM((rows, GLA_QK), BF16),
        pltpu.VMEM((rows, GLA_QK), BF16),
        pltpu.VMEM((rows, GLA_V), BF16),
        pltpu.VMEM((rows, GLA_QK), F32),
        pltpu.VMEM((n_chunks, state_rows, GLA_DV), F32),
        pltpu.VMEM((n_chunks, state_rows, GLA_DV), F32),
        pltpu.VMEM((n_chunks, state_rows, GLA_DV), BF16),
        pltpu.VMEM((rows, GLA_QK), BF16),
        pltpu.VMEM((n_chunks * SWA_KV_HEADS, BAND, SWA_GRP_Q), BF16),
        pltpu.VMEM((band_rows, 2 * LANE), BF16),
        pltpu.VMEM((band_rows, 2 * LANE), BF16),
    ]


def _mixer_prompt_call(layer, x, mod, mod_row0, weights, preps):
    batch, seq, _ = x.shape
    tile = min(MIXER_TILE, seq)
    assert seq % tile == 0 and tile % CHUNK == 0 and tile >= WINDOW
    n_chunks = tile // CHUNK
    n_t = seq // tile
    state_rows = GLA_HEADS * GLA_DK
    prep_specs = [p.specs(n_t) for p in preps]
    assert all(p.n_blocks <= batch * n_t for p in preps)
    return pl.pallas_call(
        functools.partial(_mixer_kernel, False, n_chunks, layer, tuple(p.body for p in preps)),
        grid=(batch, n_t),
        in_specs=[
            pl.BlockSpec((None, tile, D_MODEL), lambda b, t: (b, t, 0)),
            pl.BlockSpec((None, None, 6, D_MODEL), lambda b, t: (layer, mod_row0 + b, 0, 0)),
        ] + _mixer_weight_specs(layer) + [s[0] for s in prep_specs],
        out_specs=[
            pl.BlockSpec((None, tile, D_MODEL), lambda b, t: (b, t, 0)),
            pl.BlockSpec((None, state_rows, GLA_DV), lambda b, t: (b, 0, 0)),
            pl.BlockSpec((None, WINDOW, SWA_KV), lambda b, t: (b, 0, 0)),
            pl.BlockSpec((None, WINDOW, SWA_KV), lambda b, t: (b, 0, 0)),
        ] + [s[1] for s in prep_specs],
        out_shape=[
            jax.ShapeDtypeStruct((batch, seq, D_MODEL), F32),
            jax.ShapeDtypeStruct((batch, state_rows, GLA_DV), F32),
            jax.ShapeDtypeStruct((batch, WINDOW, SWA_KV), F32),
            jax.ShapeDtypeStruct((batch, WINDOW, SWA_KV), F32),
        ] + [s[2] for s in prep_specs],
        scratch_shapes=_mixer_scratch(tile, n_chunks, WINDOW + tile) + [
            pltpu.VMEM((state_rows, GLA_DV), F32),
        ],
        compiler_params=pltpu.CompilerParams(
            dimension_semantics=("arbitrary", "arbitrary"), vmem_limit_bytes=VMEM_LIMIT),
        name="mixer_prompt",
    )(x, mod, *weights, *[p.weight for p in preps])


def _mixer_sample_call(layer, x, mod, weights, s0, k_cache, v_cache):
    batch, seq, _ = x.shape
    assert seq == CHUNK
    rows = batch * seq
    state_rows = GLA_HEADS * GLA_DK

    def full(shape):
        zeros = (0,) * len(shape)
        return pl.BlockSpec(shape, lambda i: zeros)

    def of_layer(shape):
        index = (layer,) + (0,) * len(shape)
        return pl.BlockSpec((None,) + shape, lambda i: index)

    return pl.pallas_call(
        functools.partial(_mixer_kernel, True, batch, layer, ()),
        grid=(1,),
        in_specs=[full((rows, D_MODEL)), of_layer((batch, 6, D_MODEL))] + _mixer_weight_specs(layer) + [
            of_layer((batch, state_rows, GLA_DV)),
            of_layer((batch, WINDOW, SWA_KV)),
            of_layer((batch, WINDOW, SWA_KV)),
        ],
        out_specs=[
            full((rows, D_MODEL)),
            full((batch, state_rows, GLA_DV)),
            full((rows, SWA_KV)),
            full((rows, SWA_KV)),
        ],
        out_shape=[
            jax.ShapeDtypeStruct((rows, D_MODEL), F32),
            jax.ShapeDtypeStruct((batch, state_rows, GLA_DV), F32),
            jax.ShapeDtypeStruct((rows, SWA_KV), F32),
            jax.ShapeDtypeStruct((rows, SWA_KV), F32),
        ],
        scratch_shapes=_mixer_scratch(rows, batch, batch * BAND),
        compiler_params=pltpu.CompilerParams(
            dimension_semantics=("arbitrary",), vmem_limit_bytes=VMEM_LIMIT),
        name="mixer_sample",
    )(x.reshape(rows, D_MODEL), mod, *weights, s0, k_cache, v_cache)


def _ffn_kernel(sample, final, n_seg, seg_len, preps, *refs):
    if sample:
        (x_ref, mod_ref, g_ffn_ref, w_up_ref, conv_w_ref, conv_b_ref, w_down_ref, g_final_ref,
         past_ref, xo_ref, co_ref, h_scr, ub_scr, act_scr) = refs
    else:
        n_in, n_out, n_prep = 8, 2, len(preps)
        prep_src = refs[n_in:n_in + n_prep]
        prep_dst = refs[n_in + n_prep + n_out:n_in + n_prep + n_out + n_prep]
        refs = refs[:n_in] + refs[n_in + n_prep:n_in + n_prep + n_out] + refs[n_in + 2 * n_prep + n_out:]
        (x_ref, mod_ref, g_ffn_ref, w_up_ref, conv_w_ref, conv_b_ref, w_down_ref, g_final_ref,
         xo_ref, co_ref, h_scr, ub_scr, act_scr, past_scr) = refs
        for body, src, dst in zip(preps, prep_src, prep_dst):
            body(src, dst)
        t = pl.program_id(1)
        n_t = pl.num_programs(1)

        @pl.when(t == 0)
        def _():
            past_scr[...] = jnp.zeros_like(past_scr)

    stride = seg_len + SUBLANE
    g_ffn = g_ffn_ref[...]

    def mod_row(c, idx):
        if sample:
            return mod_ref[c, idx:idx + 1, :]
        return mod_ref[idx:idx + 1, :]

    for c in range(n_seg):
        rs = slice(c * seg_len, (c + 1) * seg_len)
        x = x_ref[rs, :]
        gain = g_ffn * (1.0 + mod_row(c, 4))
        h_scr[rs, :] = (x * _rms_scale(x) * gain + mod_row(c, 3)).astype(BF16)

    def up(j):
        lo, hi = FF_PARTS[j]
        h = h_scr[...]
        return _dot(h, w_up_ref[:, lo:hi]), _dot(h, w_up_ref[:, D_FF + lo:D_FF + hi])

    def activate(j, u, val):
        lo, hi = FF_PARTS[j]
        width = hi - lo
        w0 = conv_w_ref[0:1, lo:hi]
        w1 = conv_w_ref[1:2, lo:hi]
        w2 = conv_w_ref[2:3, lo:hi]
        cb = conv_b_ref[:, lo:hi]
        for c in range(n_seg):
            base = c * stride
            rs = slice(c * seg_len, (c + 1) * seg_len)
            if sample:
                ub_scr[base + SUBLANE - 2:base + SUBLANE, 0:width] = past_ref[c, :, lo:hi]
            else:
                ub_scr[base + SUBLANE - 2:base + SUBLANE, 0:width] = past_scr[:, lo:hi]
            u_seg = u[rs, :]
            ub_scr[base + SUBLANE:base + SUBLANE + seg_len, 0:width] = u_seg
            u1 = ub_scr[base + SUBLANE - 1:base + SUBLANE - 1 + seg_len, 0:width]
            u2 = ub_scr[base + SUBLANE - 2:base + SUBLANE - 2 + seg_len, 0:width]
            uc = w0 * u2 + w1 * u1 + w2 * u_seg + cb
            act_scr[rs, lo:hi] = (uc * _sigmoid(uc) * val[rs, :]).astype(BF16)
            tail = ub_scr[base + seg_len + SUBLANE - 2:base + seg_len + SUBLANE, 0:width]
            if sample:
                co_ref[c, :, lo:hi] = tail
            else:
                past_scr[:, lo:hi] = tail

    def down(j):
        lo, hi = FF_PARTS[j]
        return _dot(act_scr[:, lo:hi], w_down_ref[lo:hi, :])

    n_parts = len(FF_PARTS)
    pending = {0: up(0)}
    acc = None
    for j in range(n_parts):
        if j + 1 < n_parts:
            pending[j + 1] = up(j + 1)
        activate(j, *pending.pop(j))
        if j > 0:
            d = down(j - 1)
            acc = d if acc is None else acc + d
    acc = acc + down(n_parts - 1)

    for c in range(n_seg):
        rs = slice(c * seg_len, (c + 1) * seg_len)
        y = x_ref[rs, :] + mod_row(c, 5) * acc[rs, :]
        if final:
            y = y * _rms_scale(y) * g_final_ref[...]
        xo_ref[rs, :] = y

    if not sample:
        @pl.when(t == n_t - 1)
        def _():
            co_ref[...] = past_scr[...]


def _ffn_weight_specs(layer):
    return [
        _layer_spec(layer, (1, D_MODEL)),
        _layer_spec(0, (D_MODEL, 2 * D_FF)),
        _layer_spec(layer, (CONV_W, D_FF)),
        _layer_spec(layer, (1, D_FF)),
        _layer_spec(0, (D_FF, D_MODEL)),
        _const_spec((1, D_MODEL)),
    ]


def _ffn_prompt_call(layer, x, mod, mod_row0, weights, final, preps):
    batch, seq, _ = x.shape
    tile = min(PROMPT_TILE, seq)
    n_t = seq // tile
    prep_specs = [p.specs(n_t) for p in preps]
    assert all(p.n_blocks <= batch * n_t for p in preps)
    return pl.pallas_call(
        functools.partial(_ffn_kernel, False, final, 1, tile, tuple(p.body for p in preps)),
        grid=(batch, n_t),
        in_specs=[
            pl.BlockSpec((None, tile, D_MODEL), lambda b, t: (b, t, 0)),
            pl.BlockSpec((None, None, 6, D_MODEL), lambda b, t: (layer, mod_row0 + b, 0, 0)),
        ] + _ffn_weight_specs(layer) + [s[0] for s in prep_specs],
        out_specs=[
            pl.BlockSpec((None, tile, D_MODEL), lambda b, t: (b, t, 0)),
            pl.BlockSpec((None, CONV_W - 1, D_FF), lambda b, t: (b, 0, 0)),
        ] + [s[1] for s in prep_specs],
        out_shape=[
            jax.ShapeDtypeStruct((batch, seq, D_MODEL), F32),
            jax.ShapeDtypeStruct((batch, CONV_W - 1, D_FF), F32),
        ] + [s[2] for s in prep_specs],
        scratch_shapes=[
            pltpu.VMEM((tile, D_MODEL), BF16),
            pltpu.VMEM((tile + SUBLANE, FF_PART), F32),
            pltpu.VMEM((tile, D_FF), BF16),
            pltpu.VMEM((CONV_W - 1, D_FF), F32),
        ],
        compiler_params=pltpu.CompilerParams(
            dimension_semantics=("arbitrary", "arbitrary"), vmem_limit_bytes=VMEM_LIMIT),
        name="ffn_prompt",
    )(x, mod, *weights, *[p.weight for p in preps])


def _ffn_sample_call(layer, x2d, mod, weights, past, final, batch, seq):
    rows = batch * seq

    def full(shape):
        zeros = (0,) * len(shape)
        return pl.BlockSpec(shape, lambda i: zeros)

    def of_layer(shape):
        index = (layer,) + (0,) * len(shape)
        return pl.BlockSpec((None,) + shape, lambda i: index)

    return pl.pallas_call(
        functools.partial(_ffn_kernel, True, final, batch, seq, ()),
        grid=(1,),
        in_specs=[full((rows, D_MODEL)), of_layer((batch, 6, D_MODEL))] + _ffn_weight_specs(layer) + [
            of_layer((batch, CONV_W - 1, D_FF)),
        ],
        out_specs=[full((rows, D_MODEL)), full((batch, CONV_W - 1, D_FF))],
        out_shape=[
            jax.ShapeDtypeStruct((rows, D_MODEL), F32),
            jax.ShapeDtypeStruct((batch, CONV_W - 1, D_FF), F32),
        ],
        scratch_shapes=[
            pltpu.VMEM((rows, D_MODEL), BF16),
            pltpu.VMEM((batch * (seq + SUBLANE), FF_PART), F32),
            pltpu.VMEM((rows, D_FF), BF16),
        ],
        compiler_params=pltpu.CompilerParams(
            dimension_semantics=("arbitrary",), vmem_limit_bytes=VMEM_LIMIT),
        name="ffn_sample",
    )(x2d, mod, *weights, past)


def _pad_w_in_first(w):
    n_blocks = D_MODEL // W_PREP_ROWS
    return pl.pallas_call(
        _pad_cast_w_in,
        grid=(n_blocks,),
        in_specs=[pl.BlockSpec((None, W_PREP_ROWS, D_IN), lambda r: (0, r, 0))],
        out_specs=pl.BlockSpec((None, W_PREP_ROWS, D_IN_P), lambda r: (0, r, 0)),
        out_shape=jax.ShapeDtypeStruct((1, D_MODEL, D_IN_P), BF16),
        compiler_params=pltpu.CompilerParams(
            dimension_semantics=("arbitrary",), vmem_limit_bytes=VMEM_LIMIT),
        name="pad_w_in",
    )(w)


def kernel(x_prompt, x_sample, state_gla, cache_swa_k, cache_swa_v, state_conv, c_prompt, c_sample,
           w_ada, b_ada, g_attn, g_ffn, w_in, w_gk2, b_gk, g_gla, sinks, w_out, w_up, conv_w, conv_b,
           w_down, g_final):
    depth = w_ada.shape[0]
    batch, seq, _ = x_prompt.shape
    dec_batch, dec_seq, _ = x_sample.shape
    state_rows = GLA_HEADS * GLA_DK

    c_all = jnp.concatenate(
        [c_sample, c_prompt, jnp.zeros((ADA_ROWS - batch - dec_batch, D_MODEL), F32)], axis=0)
    mod_all = _ada_call(c_all, w_ada, b_ada).reshape(depth, ADA_ROWS, 6, D_MODEL)

    w_gk2_p = jnp.concatenate(
        [w_gk2, jnp.zeros((depth, LR_PAD - GLA_RANK, GLA_QK), F32)], axis=1).astype(BF16)
    w_in16 = w_in.astype(BF16)
    w_in_b, w_out_b = _pad_w_in_first(w_in16), w_out[:1].astype(BF16)
    mixer_steps = batch * (seq // min(MIXER_TILE, seq))
    ffn_steps = batch * (seq // min(PROMPT_TILE, seq))
    prep_ahead = (D_MODEL // MIXER_PREP_ROWS_UP <= mixer_steps
                  and D_FF // MIXER_PREP_ROWS_DOWN <= mixer_steps
                  and D_MODEL // FFN_PREP_ROWS <= ffn_steps)
    s0_all = state_gla.reshape(depth, dec_batch, state_rows, GLA_DV)
    kc_all = cache_swa_k.reshape(depth, dec_batch, WINDOW, SWA_KV)
    vc_all = cache_swa_v.reshape(depth, dec_batch, WINDOW, SWA_KV)

    yp = x_prompt
    ys = x_sample.reshape(dec_batch * dec_seq, D_MODEL)
    outs = [[] for _ in range(8)]
    for i in range(depth):
        final = i == depth - 1
        mixer_w = (g_attn[:, None], w_in_b, w_gk2_p, b_gk[:, None], g_gla[:, None], sinks, w_out_b)
        mixer_preps, ffn_preps = [], []
        if prep_ahead:
            mixer_preps = [_WeightPrep(w_up, i, MIXER_PREP_ROWS_UP, 2 * D_FF, _cast_weight),
                           _WeightPrep(w_down, i, MIXER_PREP_ROWS_DOWN, D_MODEL, _cast_weight)]
            if not final:
                ffn_preps = [_WeightPrep(w_in16, i + 1, FFN_PREP_ROWS, D_IN_P, _pad_cast_w_in),
                             _WeightPrep(w_out, i + 1, FFN_PREP_ROWS, D_MODEL, _cast_weight)]

        yp, s_p, k_p, v_p, *prepared = _mixer_prompt_call(i, yp, mod_all, dec_batch, mixer_w, mixer_preps)
        w_up_b, w_down_b = prepared or (w_up[i:i + 1].astype(BF16), w_down[i:i + 1].astype(BF16))
        ffn_w = (g_ffn[:, None], w_up_b, conv_w, conv_b[:, None], w_down_b, g_final[None])
        yp, conv_p, *prepared = _ffn_prompt_call(i, yp, mod_all, dec_batch, ffn_w, final, ffn_preps)

        ys, s_s, k_s, v_s = _mixer_sample_call(
            i, ys.reshape(dec_batch, dec_seq, D_MODEL), mod_all, mixer_w, s0_all, kc_all, vc_all)
        ys, conv_s = _ffn_sample_call(i, ys, mod_all, ffn_w, state_conv, final, dec_batch, dec_seq)
        if not final:
            w_in_b, w_out_b = prepared or (
                _pad_w_in_first(w_in16[i + 1:i + 2]), w_out[i + 1:i + 2].astype(BF16))

        keep = min(WINDOW, seq)
        outs[0].append(s_p.reshape(batch, GLA_HEADS, GLA_DK, GLA_DV))
        outs[1].append(k_p.reshape(batch, keep, SWA_KV_HEADS, SWA_HD))
        outs[2].append(v_p.reshape(batch, keep, SWA_KV_HEADS, SWA_HD))
        outs[3].append(conv_p)
        outs[4].append(s_s.reshape(dec_batch, GLA_HEADS, GLA_DK, GLA_DV))
        outs[5].append(k_s.reshape(dec_batch, dec_seq, SWA_KV_HEADS, SWA_HD))
        outs[6].append(v_s.reshape(dec_batch, dec_seq, SWA_KV_HEADS, SWA_HD))
        outs[7].append(conv_s)

    return (yp, ys.reshape(dec_batch, dec_seq, D_MODEL)) + tuple(jnp.stack(o) for o in outs)
```

```python
import functools

import jax
import jax.numpy as jnp
from jax import lax
from jax.experimental import pallas as pl
from jax.experimental.pallas import tpu as pltpu

F32 = jnp.float32
BF16 = jnp.bfloat16

D_MODEL = 1024
CHUNK = 64
GLA_HEADS = 4
GLA_DK = 64
GLA_DV = 128
GLA_RANK = 16
GLA_NORMALIZER = 16.0
SWA_Q_HEADS = 8
SWA_KV_HEADS = 2
SWA_GROUP = SWA_Q_HEADS // SWA_KV_HEADS
SWA_HD = 64
WINDOW = 128
D_FF = 2816
CONV_W = 3
RMS_EPS = 1e-6

GLA_QK = GLA_HEADS * GLA_DK
GLA_V = GLA_HEADS * GLA_DV
SWA_Q = SWA_Q_HEADS * SWA_HD
SWA_KV = SWA_KV_HEADS * SWA_HD
D_MIX = GLA_V + SWA_Q
BAND = WINDOW + CHUNK

LANE = 128
SUBLANE = 8

C_QG = 0
C_KG = C_QG + GLA_QK
C_VG = C_KG + GLA_QK
C_OG = C_VG + GLA_V
C_LR = C_OG + GLA_V
LR_PAD = LANE
C_QS = C_LR + LR_PAD
C_KS = C_QS + SWA_Q
C_VS = C_KS + SWA_KV
D_IN_P = C_VS + SWA_KV
D_IN = 2 * GLA_QK + 2 * GLA_V + GLA_RANK + SWA_Q + 2 * SWA_KV

assert GLA_DK * 2 == LANE and SWA_HD * 2 == LANE and GLA_DV == LANE and SWA_KV == LANE
GLA_PAIRS = GLA_HEADS // 2
PAIR_V = 2 * GLA_DV
SWA_GRP_Q = SWA_GROUP * SWA_HD

ADA_TILE = 1536
ADA_ROWS = 16
PROMPT_TILE = 1024
FFN_PASS = 512
NORM_AHEAD_PART = 1
MIXER_TILE = 1024
SUB_CHUNKS = 4
DENSE_PIECE = 256
DENSE_LEAD = 2
W_PREP_ROWS = 256
MIXER_PREP_ROWS_UP = 32
MIXER_PREP_ROWS_DOWN = 128
FFN_PREP_ROWS = 32
FF_PART = 768
FF_PARTS = tuple((lo, min(lo + FF_PART, D_FF)) for lo in range(0, D_FF, FF_PART))
VMEM_LIMIT = 56 * 1024 * 1024

NT_DIMS = (((1,), (1,)), ((), ()))
TN_DIMS = (((0,), (0,)), ((), ()))


def _dot(a, b):
    return jnp.dot(a, b, preferred_element_type=F32)


def _dot_nt(a, b):
    return lax.dot_general(a, b, NT_DIMS, preferred_element_type=F32)


def _dot_tn(a, b):
    return lax.dot_general(a, b, TN_DIMS, preferred_element_type=F32)


def _split_bf16(a):
    hi = a.astype(BF16)
    lo = (a - hi.astype(F32)).astype(BF16)
    return hi, lo


def _sigmoid(a):
    return 1.0 / (1.0 + jnp.exp(-a))


def _log_sigmoid(a):
    return jnp.minimum(a, 0.0) - jnp.log(1.0 + jnp.exp(-jnp.abs(a)))


def _rms_scale(a):
    return lax.rsqrt(jnp.mean(a * a, axis=-1, keepdims=True) + RMS_EPS)


def _ada_kernel(c_ref, w_ref, b_ref, o_ref):
    c = c_ref[...]
    a = (c * _sigmoid(c)).astype(BF16)
    o_ref[...] = _dot(a, w_ref[...].astype(BF16)) + b_ref[...]


def _ada_call(c_all, w_ada, b_ada):
    depth = w_ada.shape[0]
    n_tiles = (6 * D_MODEL) // ADA_TILE
    return pl.pallas_call(
        _ada_kernel,
        grid=(depth, n_tiles),
        in_specs=[
            pl.BlockSpec((ADA_ROWS, D_MODEL), lambda l, j: (0, 0)),
            pl.BlockSpec((None, D_MODEL, ADA_TILE), lambda l, j: (l, 0, j)),
            pl.BlockSpec((None, 1, ADA_TILE), lambda l, j: (l, 0, j)),
        ],
        out_specs=pl.BlockSpec((None, ADA_ROWS, ADA_TILE), lambda l, j: (l, 0, j)),
        out_shape=jax.ShapeDtypeStruct((depth, ADA_ROWS, 6 * D_MODEL), F32),
        compiler_params=pltpu.CompilerParams(
            dimension_semantics=("arbitrary", "arbitrary"), vmem_limit_bytes=VMEM_LIMIT),
        name="adaln_mod",
    )(c_all, w_ada, b_ada.reshape(depth, 1, 6 * D_MODEL))


def _dup_halves(a, low_half):
    swapped = pltpu.roll(a, LANE // 2, axis=1)
    return jnp.where(low_half, a, swapped), jnp.where(low_half, swapped, a)


def _mixer_kernel(sample, n_chunks, layer, preps, *refs):
    if sample:
        (x_ref, mod_ref, g_attn_ref, w_in_ref, w_gk2_ref, b_gk_ref, g_gla_ref, sinks_ref, w_out_ref,
         s0_ref, kc_ref, vc_ref,
         xo_ref, so_ref, ko_ref, vo_ref,
         h_scr, proj_scr, omix_scr, qm_scr, km_scr, qi_scr, ko_scr, vb_scr, gk_scr, dec_scr, u_scr, sb_scr,
         att_scr, pn_scr,
         kd_scr, vd_scr) = refs
        t = None
    else:
        n_in, n_out, n_prep = 9, 4, len(preps)
        prep_src = refs[n_in:n_in + n_prep]
        prep_dst = refs[n_in + n_prep + n_out:n_in + n_prep + n_out + n_prep]
        refs = refs[:n_in] + refs[n_in + n_prep:n_in + n_prep + n_out] + refs[n_in + 2 * n_prep + n_out:]
        (x_ref, mod_ref, g_attn_ref, w_in_ref, w_gk2_ref, b_gk_ref, g_gla_ref, sinks_ref, w_out_ref,
         xo_ref, so_ref, ko_ref, vo_ref,
         h_scr, proj_scr, omix_scr, qm_scr, km_scr, qi_scr, ko_scr, vb_scr, gk_scr, dec_scr, u_scr, sb_scr,
         att_scr, pn_scr,
         kd_scr, vd_scr, s_scr) = refs
        for body, src, dst in zip(preps, prep_src, prep_dst):
            body(src, dst)
        t = pl.program_id(1)
        n_t = pl.num_programs(1)

        @pl.when(t == 0)
        def _():
            s_scr[...] = jnp.zeros_like(s_scr)
            kd_scr[0:WINDOW, :] = jnp.zeros((WINDOW, 2 * LANE), BF16)
            vd_scr[0:WINDOW, :] = jnp.zeros((WINDOW, 2 * LANE), BF16)

    rows = n_chunks * CHUNK
    g_attn = g_attn_ref[...]

    def mod_row(c, idx):
        if sample:
            return mod_ref[c, idx:idx + 1, :]
        return mod_ref[idx:idx + 1, :]

    def chunk_rows(c):
        return slice(c * CHUNK, (c + 1) * CHUNK)

    sub_chunks = min(SUB_CHUNKS, n_chunks)
    n_sub = n_chunks // sub_chunks
    sub_rows = sub_chunks * CHUNK

    def sub_slice(s):
        return slice(s * sub_rows, (s + 1) * sub_rows)

    def norm_groups(s):
        if sample:
            return [(c, chunk_rows(c)) for c in range(s * sub_chunks, (s + 1) * sub_chunks)]
        return [(0, sub_slice(s))]

    r64 = lax.broadcasted_iota(jnp.int32, (CHUNK, CHUNK), 0)
    c64 = lax.broadcasted_iota(jnp.int32, (CHUNK, CHUNK), 1)
    tri = jnp.where(r64 >= c64, 1.0, 0.0).astype(BF16)
    tri2 = jnp.concatenate([tri, tri], axis=1)
    l_pair = lax.broadcasted_iota(jnp.int32, (CHUNK, LANE), 1)
    low_half = l_pair < LANE // 2
    causal_heads = (lax.broadcasted_iota(jnp.int32, (CHUNK, GLA_QK), 0)
                    >= (lax.broadcasted_iota(jnp.int32, (CHUNK, GLA_QK), 1) & (CHUNK - 1)))
    diag_heads = ((lax.broadcasted_iota(jnp.int32, (GLA_QK, GLA_QK), 0) // GLA_DK)
                  == (lax.broadcasted_iota(jnp.int32, (GLA_QK, GLA_QK), 1) // GLA_DK))
    diag_v = ((lax.broadcasted_iota(jnp.int32, (LANE, PAIR_V), 0) >= LANE // 2)
              == (lax.broadcasted_iota(jnp.int32, (LANE, PAIR_V), 1) >= GLA_DV))
    g_gla = g_gla_ref[...]
    low_sub = lax.broadcasted_iota(jnp.int32, (sub_rows, LANE), 1) < LANE // 2

    def project_steps(s):
        ss = sub_slice(s)

        def norm():
            for mi, rs in norm_groups(s):
                x = x_ref[rs, :]
                gain = g_attn * (1.0 + mod_row(mi, 1))
                h_scr[rs, :] = (x * _rms_scale(x) * gain + mod_row(mi, 0)).astype(BF16)

        def piece(lo, hi):
            proj_scr[ss, lo:hi] = _dot(h_scr[ss, :], w_in_ref[:, lo:hi])

        def gates():
            lr = proj_scr[ss, C_LR:C_LR + LR_PAD].astype(BF16)
            gk_scr[ss, :] = (_log_sigmoid(_dot(lr, w_gk2_ref[...]) + b_gk_ref[...])
                             * (1.0 / GLA_NORMALIZER))
            vb_scr[ss, :] = proj_scr[ss, C_VG:C_VG + GLA_V].astype(BF16)

        def bands():
            k_dup = _dup_halves(proj_scr[ss, C_KS:C_KS + SWA_KV], low_sub)
            v_dup = _dup_halves(proj_scr[ss, C_VS:C_VS + SWA_KV], low_sub)
            for kv in range(SWA_KV_HEADS):
                ls = slice(kv * LANE, (kv + 1) * LANE)
                if sample:
                    for c in range(s * sub_chunks, (s + 1) * sub_chunks):
                        local = slice((c - s * sub_chunks) * CHUNK, (c - s * sub_chunks + 1) * CHUNK)
                        kd_scr[c * BAND + WINDOW:(c + 1) * BAND, ls] = k_dup[kv][local, :].astype(BF16)
                        vd_scr[c * BAND + WINDOW:(c + 1) * BAND, ls] = v_dup[kv][local, :].astype(BF16)
                else:
                    band_rows = slice(WINDOW + s * sub_rows, WINDOW + (s + 1) * sub_rows)
                    kd_scr[band_rows, ls] = k_dup[kv].astype(BF16)
                    vd_scr[band_rows, ls] = v_dup[kv].astype(BF16)

        steps = [norm]
        for lo in range(0, D_IN_P, DENSE_PIECE):
            steps.append(functools.partial(piece, lo, min(lo + DENSE_PIECE, D_IN_P)))
        return steps + [gates, bands]

    def gla_prepare(c):
        rs = chunk_rows(c)
        gk_hi, gk_lo = _split_bf16(gk_scr[rs, :])
        cum = _dot(tri2, jnp.concatenate([gk_hi, gk_lo], axis=0))
        mid = cum[CHUNK // 2:CHUNK // 2 + 1, :]
        last = cum[CHUNK - 1:CHUNK, :]
        q = proj_scr[rs, C_QG:C_QG + GLA_QK] * (GLA_DK ** -0.5)
        k = proj_scr[rs, C_KG:C_KG + GLA_QK]
        qm_scr[rs, :] = (q * jnp.exp(cum - mid)).astype(BF16)
        km_scr[rs, :] = (k * jnp.exp(mid - cum)).astype(BF16)
        qi_scr[rs, :] = (q * jnp.exp(cum)).astype(BF16)
        ko_scr[rs, :] = (k * jnp.exp(last - cum)).astype(BF16)
        dec_scr[c] = jnp.broadcast_to(jnp.exp(last), (LANE, GLA_QK)).T

    def gla_increment(c):
        rs = chunk_rows(c)
        for p in range(GLA_PAIRS):
            upd = _dot_tn(ko_scr[rs, p * LANE:(p + 1) * LANE], vb_scr[rs, p * PAIR_V:(p + 1) * PAIR_V])
            u_scr[c, p * LANE:p * LANE + GLA_DK, :] = upd[0:GLA_DK, 0:GLA_DV]
            u_scr[c, p * LANE + GLA_DK:(p + 1) * LANE, :] = upd[GLA_DK:LANE, GLA_DV:PAIR_V]

    def gla_recurrence(s):
        state = None if sample else s_scr[...]
        for c in range(s * sub_chunks, (s + 1) * sub_chunks):
            if sample:
                state = s0_ref[c]
            sb_scr[c] = state.astype(BF16)
            state = dec_scr[c] * state + u_scr[c]
            if sample:
                so_ref[c] = state
        if not sample:
            s_scr[...] = state

    def gla_scores(c):
        rs = chunk_rows(c)
        km = km_scr[rs, :]
        k_bd = jnp.where(diag_heads, jnp.concatenate([km] * GLA_HEADS, axis=0), 0.0)
        att_scr[rs, :] = jnp.where(causal_heads, _dot_nt(qm_scr[rs, :], k_bd), 0.0).astype(BF16)

    def gla_output(c, p):
        rs = chunk_rows(c)
        ls = slice(p * LANE, (p + 1) * LANE)
        v_pair = vb_scr[rs, p * PAIR_V:(p + 1) * PAIR_V]
        s_pair = sb_scr[c, ls, :]
        w_top = jnp.where(diag_v, jnp.concatenate([v_pair, v_pair], axis=0), 0.0)
        w_bot = jnp.where(diag_v, jnp.concatenate([s_pair, s_pair], axis=1), 0.0)
        o_pair = _dot(jnp.concatenate([att_scr[rs, ls], qi_scr[rs, ls]], axis=1),
                      jnp.concatenate([w_top, w_bot], axis=0))
        for hh in range(2):
            h = 2 * p + hh
            o = o_pair[:, hh * GLA_DV:(hh + 1) * GLA_DV]
            og = proj_scr[rs, C_OG + h * GLA_DV:C_OG + (h + 1) * GLA_DV]
            y = o * _rms_scale(o) * g_gla * (og * _sigmoid(og))
            omix_scr[rs, h * GLA_DV:(h + 1) * GLA_DV] = y.astype(BF16)

    if sample:
        low_win = lax.broadcasted_iota(jnp.int32, (WINDOW, LANE), 1) < LANE // 2
        for c in range(n_chunks):
            kc_dup = _dup_halves(kc_ref[c], low_win)
            vc_dup = _dup_halves(vc_ref[c], low_win)
            for kv in range(SWA_KV_HEADS):
                ls = slice(kv * LANE, (kv + 1) * LANE)
                kd_scr[c * BAND:c * BAND + WINDOW, ls] = kc_dup[kv].astype(BF16)
                vd_scr[c * BAND:c * BAND + WINDOW, ls] = vc_dup[kv].astype(BF16)

    lane_q = lax.broadcasted_iota(jnp.int32, (1, SWA_GRP_Q), 1)
    key_ids = lax.broadcasted_iota(jnp.int32, (BAND, SWA_GRP_Q), 0)
    sink_vecs = []
    for kv in range(SWA_KV_HEADS):
        vec = jnp.full((1, SWA_GRP_Q), sinks_ref[layer, kv * SWA_GROUP + SWA_GROUP - 1], F32)
        for g in range(SWA_GROUP - 2, -1, -1):
            vec = jnp.where(lane_q < (g + 1) * SWA_HD, sinks_ref[layer, kv * SWA_GROUP + g], vec)
        sink_vecs.append(vec)

    def band_rows(c):
        return slice(c * BAND, (c + 1) * BAND) if sample else slice(c * CHUNK, c * CHUNK + BAND)

    def swa_scores(c, kv):
        rs = chunk_rows(c)
        band = band_rows(c)
        ls = slice(kv * LANE, (kv + 1) * LANE)
        qg = (proj_scr[rs, C_QS + kv * SWA_GRP_Q:C_QS + (kv + 1) * SWA_GRP_Q]
              * (SWA_HD ** -0.5)).astype(BF16)
        q_stack = jnp.concatenate(
            [jnp.where(low_half if hh == 0 else ~low_half, qg[:, pp * LANE:(pp + 1) * LANE], 0.0)
             for pp in range(SWA_GROUP // 2) for hh in range(2)], axis=0)
        s_t = _dot_nt(kd_scr[band, ls], q_stack)
        if not sample and c * CHUNK < WINDOW:
            first_valid = WINDOW - (t * rows + c * CHUNK)
            s_t = jnp.where(key_ids >= first_valid, s_t, -jnp.inf)
        sink = sink_vecs[kv]
        m = jnp.maximum(jnp.max(s_t, axis=0, keepdims=True), sink)
        p_t = jnp.exp(s_t - m)
        den = jnp.sum(p_t, axis=0, keepdims=True) + jnp.exp(sink - m)
        pn_scr[c * SWA_KV_HEADS + kv] = (p_t * (1.0 / den)).astype(BF16)

    def swa_output(c, kv):
        rs = chunk_rows(c)
        ls = slice(kv * LANE, (kv + 1) * LANE)
        o_t = _dot_tn(vd_scr[band_rows(c), ls], pn_scr[c * SWA_KV_HEADS + kv]).T
        for pp in range(SWA_GROUP // 2):
            o_pair = jnp.where(low_half, o_t[(2 * pp) * CHUNK:(2 * pp + 1) * CHUNK, :],
                               o_t[(2 * pp + 1) * CHUNK:(2 * pp + 2) * CHUNK, :])
            col = GLA_V + kv * SWA_GRP_Q + pp * LANE
            omix_scr[rs, col:col + LANE] = o_pair.astype(BF16)

    def out_steps(s):
        ss = sub_slice(s)
        base = s * sub_rows

        def piece(lo, hi):
            mix = _dot(omix_scr[ss, :], w_out_ref[:, lo:hi])
            for mi, rs in norm_groups(s):
                local = slice(rs.start - base, rs.stop - base)
                xo_ref[rs, lo:hi] = x_ref[rs, lo:hi] + mod_row(mi, 2)[:, lo:hi] * mix[local, :]

        return [functools.partial(piece, lo, lo + DENSE_PIECE) for lo in range(0, D_MODEL, DENSE_PIECE)]

    def block_steps(s):
        blocks = range(s * sub_chunks, (s + 1) * sub_chunks)
        pairs = [(c, p) for c in blocks for p in range(GLA_PAIRS)]
        groups = [(c, kv) for c in blocks for kv in range(SWA_KV_HEADS)]
        steps = [functools.partial(gla_prepare, c) for c in blocks]
        steps += [functools.partial(swa_scores, c, kv) for c, kv in groups]
        steps += [functools.partial(gla_increment, c) for c in blocks]
        steps += [functools.partial(gla_scores, c) for c in blocks]
        steps.append(functools.partial(gla_recurrence, s))
        steps += [functools.partial(swa_output, c, kv) for c, kv in groups]
        steps += [functools.partial(gla_output, c, p) for c, p in pairs]
        return steps

    for step in project_steps(0):
        step()
    for s in range(n_sub):
        dense = (project_steps(s + 1) if s + 1 < n_sub else []) + (out_steps(s - 1) if s > 0 else [])
        work = block_steps(s)
        issued = 0
        for i, step in enumerate(work):
            while issued < len(dense) and (issued - DENSE_LEAD) * len(work) < (i + 1) * len(dense):
                dense[issued]()
                issued += 1
            step()
        for step in dense[issued:]:
            step()
    for step in out_steps(n_sub - 1):
        step()

    if sample:
        ko_ref[...] = proj_scr[:, C_KS:C_KS + SWA_KV]
        vo_ref[...] = proj_scr[:, C_VS:C_VS + SWA_KV]
    else:
        kd_scr[0:WINDOW, :] = kd_scr[rows:rows + WINDOW, :]
        vd_scr[0:WINDOW, :] = vd_scr[rows:rows + WINDOW, :]

        @pl.when(t == n_t - 1)
        def _():
            so_ref[...] = s_scr[...]
            ko_ref[...] = proj_scr[rows - WINDOW:rows, C_KS:C_KS + SWA_KV]
            vo_ref[...] = proj_scr[rows - WINDOW:rows, C_VS:C_VS + SWA_KV]


def _pad_cast_w_in(w_ref, o_ref):
    head = C_LR + GLA_RANK
    w = w_ref[...].astype(F32)
    o_ref[:, 0:C_LR] = w[:, 0:C_LR].astype(BF16)
    gate = jnp.concatenate(
        [w[:, C_LR:head], jnp.zeros((w.shape[0], LR_PAD - GLA_RANK), F32)], axis=1)
    o_ref[:, C_LR:C_QS] = gate.astype(BF16)
    o_ref[:, C_QS:D_IN_P] = w[:, head:D_IN].astype(BF16)


def _cast_weight(w_ref, o_ref):
    o_ref[...] = w_ref[...].astype(BF16)


class _WeightPrep:
    def __init__(self, weight, layer, rows_per_step, out_cols, body):
        self.weight, self.layer, self.rows_per_step = weight, layer, rows_per_step
        self.rows, self.in_cols = weight.shape[1], weight.shape[2]
        self.out_cols, self.body = out_cols, body
        assert self.rows % rows_per_step == 0
        self.n_blocks = self.rows // rows_per_step

    def specs(self, n_t):
        def block(b, t):
            return jnp.minimum(b * n_t + t, self.n_blocks - 1)
        layer = self.layer
        return (pl.BlockSpec((None, self.rows_per_step, self.in_cols), lambda b, t: (layer, block(b, t), 0)),
                pl.BlockSpec((None, self.rows_per_step, self.out_cols), lambda b, t: (0, block(b, t), 0)),
                jax.ShapeDtypeStruct((1, self.rows, self.out_cols), BF16))


def _const_spec(shape):
    zeros = (0,) * len(shape)
    return pl.BlockSpec(shape, lambda *_: zeros, pipeline_mode=pl.Buffered(1))


def _layer_spec(layer, shape):
    index = (layer,) + (0,) * len(shape)
    return pl.BlockSpec((None,) + shape, lambda *_: index, pipeline_mode=pl.Buffered(1))


def _mixer_weight_specs(layer):
    return [
        _layer_spec(layer, (1, D_MODEL)),
        _layer_spec(0, (D_MODEL, D_IN_P)),
        _layer_spec(layer, (LR_PAD, GLA_QK)),
        _layer_spec(layer, (1, GLA_QK)),
        _layer_spec(layer, (1, GLA_DV)),
        pl.BlockSpec(memory_space=pltpu.SMEM),
        _layer_spec(0, (D_MIX, D_MODEL)),
    ]


def _mixer_scratch(rows, n_chunks, band_rows):
    state_rows = GLA_HEADS * GLA_DK
    return [
        pltpu.VMEM((rows, D_MODEL), BF16),
        pltpu.VMEM((rows, D_IN_P), F32),
        pltpu.VMEM((rows, D_MIX), BF16),
        pltpu.VMEM((rows, GLA_QK), BF16),
        pltpu.VMEM((rows, GLA_QK), BF16),
        pltpu.VMEM((rows, GLA_QK), BF16),
        pltpu.VMEM((rows, GLA_QK), BF16),
        pltpu.VMEM((rows, GLA_V), BF16),
        pltpu.VMEM((rows, GLA_QK), F32),
        pltpu.VMEM((n_chunks, state_rows, GLA_DV), F32),
        pltpu.VMEM((n_chunks, state_rows, GLA_DV), F32),
        pltpu.VMEM((n_chunks, state_rows, GLA_DV), BF16),
        pltpu.VMEM((rows, GLA_QK), BF16),
        pltpu.VMEM((n_chunks * SWA_KV_HEADS, BAND, SWA_GRP_Q), BF16),
        pltpu.VMEM((band_rows, 2 * LANE), BF16),
        pltpu.VMEM((band_rows, 2 * LANE), BF16),
    ]


def _mixer_prompt_call(layer, x, mod, mod_row0, weights, preps):
    batch, seq, _ = x.shape
    tile = min(MIXER_TILE, seq)
    assert seq % tile == 0 and tile % CHUNK == 0 and tile >= WINDOW
    n_chunks = tile // CHUNK
    n_t = seq // tile
    state_rows = GLA_HEADS * GLA_DK
    prep_specs = [p.specs(n_t) for p in preps]
    assert all(p.n_blocks <= batch * n_t for p in preps)
    return pl.pallas_call(
        functools.partial(_mixer_kernel, False, n_chunks, layer, tuple(p.body for p in preps)),
        grid=(batch, n_t),
        in_specs=[
            pl.BlockSpec((None, tile, D_MODEL), lambda b, t: (b, t, 0)),
            pl.BlockSpec((None, None, 6, D_MODEL), lambda b, t: (layer, mod_row0 + b, 0, 0)),
        ] + _mixer_weight_specs(layer) + [s[0] for s in prep_specs],
        out_specs=[
            pl.BlockSpec((None, tile, D_MODEL), lambda b, t: (b, t, 0)),
            pl.BlockSpec((None, state_rows, GLA_DV), lambda b, t: (b, 0, 0)),
            pl.BlockSpec((None, WINDOW, SWA_KV), lambda b, t: (b, 0, 0)),
            pl.BlockSpec((None, WINDOW, SWA_KV), lambda b, t: (b, 0, 0)),
        ] + [s[1] for s in prep_specs],
        out_shape=[
            jax.ShapeDtypeStruct((batch, seq, D_MODEL), F32),
            jax.ShapeDtypeStruct((batch, state_rows, GLA_DV), F32),
            jax.ShapeDtypeStruct((batch, WINDOW, SWA_KV), F32),
            jax.ShapeDtypeStruct((batch, WINDOW, SWA_KV), F32),
        ] + [s[2] for s in prep_specs],
        scratch_shapes=_mixer_scratch(tile, n_chunks, WINDOW + tile) + [
            pltpu.VMEM((state_rows, GLA_DV), F32),
        ],
        compiler_params=pltpu.CompilerParams(
            dimension_semantics=("arbitrary", "arbitrary"), vmem_limit_bytes=VMEM_LIMIT),
        name="mixer_prompt",
    )(x, mod, *weights, *[p.weight for p in preps])


def _mixer_sample_call(layer, x, mod, weights, s0, k_cache, v_cache):
    batch, seq, _ = x.shape
    assert seq == CHUNK
    rows = batch * seq
    state_rows = GLA_HEADS * GLA_DK

    def full(shape):
        zeros = (0,) * len(shape)
        return pl.BlockSpec(shape, lambda i: zeros)

    def of_layer(shape):
        index = (layer,) + (0,) * len(shape)
        return pl.BlockSpec((None,) + shape, lambda i: index)

    return pl.pallas_call(
        functools.partial(_mixer_kernel, True, batch, layer, ()),
        grid=(1,),
        in_specs=[full((rows, D_MODEL)), of_layer((batch, 6, D_MODEL))] + _mixer_weight_specs(layer) + [
            of_layer((batch, state_rows, GLA_DV)),
            of_layer((batch, WINDOW, SWA_KV)),
            of_layer((batch, WINDOW, SWA_KV)),
        ],
        out_specs=[
            full((rows, D_MODEL)),
            full((batch, state_rows, GLA_DV)),
            full((rows, SWA_KV)),
            full((rows, SWA_KV)),
        ],
        out_shape=[
            jax.ShapeDtypeStruct((rows, D_MODEL), F32),
            jax.ShapeDtypeStruct((batch, state_rows, GLA_DV), F32),
            jax.ShapeDtypeStruct((rows, SWA_KV), F32),
            jax.ShapeDtypeStruct((rows, SWA_KV), F32),
        ],
        scratch_shapes=_mixer_scratch(rows, batch, batch * BAND),
        compiler_params=pltpu.CompilerParams(
            dimension_semantics=("arbitrary",), vmem_limit_bytes=VMEM_LIMIT),
        name="mixer_sample",
    )(x.reshape(rows, D_MODEL), mod, *weights, s0, k_cache, v_cache)


def _ffn_kernel(sample, final, n_seg, seg_len, pass_rows, preps, *refs):
    if sample:
        (x_ref, mod_ref, g_ffn_ref, w_up_ref, conv_w_ref, conv_b_ref, w_down_ref, g_final_ref,
         past_ref, xo_ref, co_ref, h_scr, ub_scr, act_scr) = refs
    else:
        n_in, n_out, n_prep = 8, 2, len(preps)
        prep_src = refs[n_in:n_in + n_prep]
        prep_dst = refs[n_in + n_prep + n_out:n_in + n_prep + n_out + n_prep]
        refs = refs[:n_in] + refs[n_in + n_prep:n_in + n_prep + n_out] + refs[n_in + 2 * n_prep + n_out:]
        (x_ref, mod_ref, g_ffn_ref, w_up_ref, conv_w_ref, conv_b_ref, w_down_ref, g_final_ref,
         xo_ref, co_ref, h_scr, ub_scr, act_scr, past_scr) = refs
        for body, src, dst in zip(preps, prep_src, prep_dst):
            body(src, dst)
        t = pl.program_id(1)
        n_t = pl.num_programs(1)

        @pl.when(t == 0)
        def _():
            past_scr[...] = jnp.zeros_like(past_scr)

    stride = seg_len + SUBLANE
    g_ffn = g_ffn_ref[...]
    n_pass = (n_seg * seg_len) // pass_rows

    def mod_row(c, idx):
        if sample:
            return mod_ref[c, idx:idx + 1, :]
        return mod_ref[idx:idx + 1, :]

    def pass_slice(p):
        return slice(p * pass_rows, (p + 1) * pass_rows)

    def segments(p):
        if sample:
            return [(c, slice(c * seg_len, (c + 1) * seg_len), c * stride + SUBLANE) for c in range(n_seg)]
        return [(0, pass_slice(p), SUBLANE)]

    def normalise(p):
        for c, rs, _ in segments(p):
            x = x_ref[rs, :]
            gain = g_ffn * (1.0 + mod_row(c, 4))
            h_scr[rs, :] = (x * _rms_scale(x) * gain + mod_row(c, 3)).astype(BF16)

    def up(j, p):
        lo, hi = FF_PARTS[j]
        h = h_scr[pass_slice(p), :]
        return _dot(h, w_up_ref[:, lo:hi]), _dot(h, w_up_ref[:, D_FF + lo:D_FF + hi])

    def activate(j, p, u, val):
        lo, hi = FF_PARTS[j]
        width = hi - lo
        w0 = conv_w_ref[0:1, lo:hi]
        w1 = conv_w_ref[1:2, lo:hi]
        w2 = conv_w_ref[2:3, lo:hi]
        cb = conv_b_ref[:, lo:hi]
        first = pass_slice(p).start
        for c, rs, base in segments(p):
            n = rs.stop - rs.start
            local = slice(rs.start - first, rs.stop - first)
            if sample:
                ub_scr[base - 2:base, 0:width] = past_ref[c, :, lo:hi]
            else:
                ub_scr[base - 2:base, 0:width] = past_scr[:, lo:hi]
            u_seg = u[local, :]
            ub_scr[base:base + n, 0:width] = u_seg
            u1 = ub_scr[base - 1:base - 1 + n, 0:width]
            u2 = ub_scr[base - 2:base - 2 + n, 0:width]
            uc = w0 * u2 + w1 * u1 + w2 * u_seg + cb
            act_scr[rs, lo:hi] = (uc * _sigmoid(uc) * val[local, :]).astype(BF16)
            tail = ub_scr[base + n - 2:base + n, 0:width]
            if sample:
                co_ref[c, :, lo:hi] = tail
            else:
                past_scr[:, lo:hi] = tail

    def down(j, p):
        lo, hi = FF_PARTS[j]
        return _dot(act_scr[pass_slice(p), lo:hi], w_down_ref[lo:hi, :])

    def finish(p, acc):
        first = pass_slice(p).start
        for c, rs, _ in segments(p):
            y = x_ref[rs, :] + mod_row(c, 5) * acc[rs.start - first:rs.stop - first, :]
            if final:
                y = y * _rms_scale(y) * g_final_ref[...]
            xo_ref[rs, :] = y

    n_parts = len(FF_PARTS)
    normalise(0)
    for p in range(n_pass):
        pending = {0: up(0, p)}
        acc = None
        for j in range(n_parts):
            if j + 1 < n_parts:
                pending[j + 1] = up(j + 1, p)
            activate(j, p, *pending.pop(j))
            if j == NORM_AHEAD_PART and p + 1 < n_pass:
                normalise(p + 1)
            if j > 0:
                d = down(j - 1, p)
                acc = d if acc is None else acc + d
        finish(p, acc + down(n_parts - 1, p))

    if not sample:
        @pl.when(t == n_t - 1)
        def _():
            co_ref[...] = past_scr[...]


def _ffn_weight_specs(layer):
    return [
        _layer_spec(layer, (1, D_MODEL)),
        _layer_spec(0, (D_MODEL, 2 * D_FF)),
        _layer_spec(layer, (CONV_W, D_FF)),
        _layer_spec(layer, (1, D_FF)),
        _layer_spec(0, (D_FF, D_MODEL)),
        _const_spec((1, D_MODEL)),
    ]


def _ffn_prompt_call(layer, x, mod, mod_row0, weights, final, preps):
    batch, seq, _ = x.shape
    tile = min(PROMPT_TILE, seq)
    n_t = seq // tile
    prep_specs = [p.specs(n_t) for p in preps]
    assert all(p.n_blocks <= batch * n_t for p in preps)
    return pl.pallas_call(
        functools.partial(_ffn_kernel, False, final, 1, tile, min(FFN_PASS, tile),
                          tuple(p.body for p in preps)),
        grid=(batch, n_t),
        in_specs=[
            pl.BlockSpec((None, tile, D_MODEL), lambda b, t: (b, t, 0)),
            pl.BlockSpec((None, None, 6, D_MODEL), lambda b, t: (layer, mod_row0 + b, 0, 0)),
        ] + _ffn_weight_specs(layer) + [s[0] for s in prep_specs],
        out_specs=[
            pl.BlockSpec((None, tile, D_MODEL), lambda b, t: (b, t, 0)),
            pl.BlockSpec((None, CONV_W - 1, D_FF), lambda b, t: (b, 0, 0)),
        ] + [s[1] for s in prep_specs],
        out_shape=[
            jax.ShapeDtypeStruct((batch, seq, D_MODEL), F32),
            jax.ShapeDtypeStruct((batch, CONV_W - 1, D_FF), F32),
        ] + [s[2] for s in prep_specs],
        scratch_shapes=[
            pltpu.VMEM((tile, D_MODEL), BF16),
            pltpu.VMEM((min(FFN_PASS, tile) + SUBLANE, FF_PART), F32),
            pltpu.VMEM((tile, D_FF), BF16),
            pltpu.VMEM((CONV_W - 1, D_FF), F32),
        ],
        compiler_params=pltpu.CompilerParams(
            dimension_semantics=("arbitrary", "arbitrary"), vmem_limit_bytes=VMEM_LIMIT),
        name="ffn_prompt",
    )(x, mod, *weights, *[p.weight for p in preps])


def _ffn_sample_call(layer, x2d, mod, weights, past, final, batch, seq):
    rows = batch * seq

    def full(shape):
        zeros = (0,) * len(shape)
        return pl.BlockSpec(shape, lambda i: zeros)

    def of_layer(shape):
        index = (layer,) + (0,) * len(shape)
        return pl.BlockSpec((None,) + shape, lambda i: index)

    return pl.pallas_call(
        functools.partial(_ffn_kernel, True, final, batch, seq, rows, ()),
        grid=(1,),
        in_specs=[full((rows, D_MODEL)), of_layer((batch, 6, D_MODEL))] + _ffn_weight_specs(layer) + [
            of_layer((batch, CONV_W - 1, D_FF)),
        ],
        out_specs=[full((rows, D_MODEL)), full((batch, CONV_W - 1, D_FF))],
        out_shape=[
            jax.ShapeDtypeStruct((rows, D_MODEL), F32),
            jax.ShapeDtypeStruct((batch, CONV_W - 1, D_FF), F32),
        ],
        scratch_shapes=[
            pltpu.VMEM((rows, D_MODEL), BF16),
            pltpu.VMEM((batch * (seq + SUBLANE), FF_PART), F32),
            pltpu.VMEM((rows, D_FF), BF16),
        ],
        compiler_params=pltpu.CompilerParams(
            dimension_semantics=("arbitrary",), vmem_limit_bytes=VMEM_LIMIT),
        name="ffn_sample",
    )(x2d, mod, *weights, past)


def _pad_w_in_first(w):
    n_blocks = D_MODEL // W_PREP_ROWS
    return pl.pallas_call(
        _pad_cast_w_in,
        grid=(n_blocks,),
        in_specs=[pl.BlockSpec((None, W_PREP_ROWS, D_IN), lambda r: (0, r, 0))],
        out_specs=pl.BlockSpec((None, W_PREP_ROWS, D_IN_P), lambda r: (0, r, 0)),
        out_shape=jax.ShapeDtypeStruct((1, D_MODEL, D_IN_P), BF16),
        compiler_params=pltpu.CompilerParams(
            dimension_semantics=("arbitrary",), vmem_limit_bytes=VMEM_LIMIT),
        name="pad_w_in",
    )(w)


def kernel(x_prompt, x_sample, state_gla, cache_swa_k, cache_swa_v, state_conv, c_prompt, c_sample,
           w_ada, b_ada, g_attn, g_ffn, w_in, w_gk2, b_gk, g_gla, sinks, w_out, w_up, conv_w, conv_b,
           w_down, g_final):
    depth = w_ada.shape[0]
    batch, seq, _ = x_prompt.shape
    dec_batch, dec_seq, _ = x_sample.shape
    state_rows = GLA_HEADS * GLA_DK

    c_all = jnp.concatenate(
        [c_sample, c_prompt, jnp.zeros((ADA_ROWS - batch - dec_batch, D_MODEL), F32)], axis=0)
    mod_all = _ada_call(c_all, w_ada, b_ada).reshape(depth, ADA_ROWS, 6, D_MODEL)

    w_gk2_p = jnp.concatenate(
        [w_gk2, jnp.zeros((depth, LR_PAD - GLA_RANK, GLA_QK), F32)], axis=1).astype(BF16)
    w_in16 = w_in.astype(BF16)
    w_in_b, w_out_b = _pad_w_in_first(w_in16), w_out[:1].astype(BF16)
    mixer_steps = batch * (seq // min(MIXER_TILE, seq))
    ffn_steps = batch * (seq // min(PROMPT_TILE, seq))
    prep_ahead = (D_MODEL // MIXER_PREP_ROWS_UP <= mixer_steps
                  and D_FF // MIXER_PREP_ROWS_DOWN <= mixer_steps
                  and D_MODEL // FFN_PREP_ROWS <= ffn_steps)
    s0_all = state_gla.reshape(depth, dec_batch, state_rows, GLA_DV)
    kc_all = cache_swa_k.reshape(depth, dec_batch, WINDOW, SWA_KV)
    vc_all = cache_swa_v.reshape(depth, dec_batch, WINDOW, SWA_KV)

    yp = x_prompt
    ys = x_sample.reshape(dec_batch * dec_seq, D_MODEL)
    outs = [[] for _ in range(8)]
    for i in range(depth):
        final = i == depth - 1
        mixer_w = (g_attn[:, None], w_in_b, w_gk2_p, b_gk[:, None], g_gla[:, None], sinks, w_out_b)
        mixer_preps, ffn_preps = [], []
        if prep_ahead:
            mixer_preps = [_WeightPrep(w_up, i, MIXER_PREP_ROWS_UP, 2 * D_FF, _cast_weight),
                           _WeightPrep(w_down, i, MIXER_PREP_ROWS_DOWN, D_MODEL, _cast_weight)]
            if not final:
                ffn_preps = [_WeightPrep(w_in16, i + 1, FFN_PREP_ROWS, D_IN_P, _pad_cast_w_in),
                             _WeightPrep(w_out, i + 1, FFN_PREP_ROWS, D_MODEL, _cast_weight)]

        yp, s_p, k_p, v_p, *prepared = _mixer_prompt_call(i, yp, mod_all, dec_batch, mixer_w, mixer_preps)
        w_up_b, w_down_b = prepared or (w_up[i:i + 1].astype(BF16), w_down[i:i + 1].astype(BF16))
        ffn_w = (g_ffn[:, None], w_up_b, conv_w, conv_b[:, None], w_down_b, g_final[None])
        yp, conv_p, *prepared = _ffn_prompt_call(i, yp, mod_all, dec_batch, ffn_w, final, ffn_preps)

        ys, s_s, k_s, v_s = _mixer_sample_call(
            i, ys.reshape(dec_batch, dec_seq, D_MODEL), mod_all, mixer_w, s0_all, kc_all, vc_all)
        ys, conv_s = _ffn_sample_call(i, ys, mod_all, ffn_w, state_conv, final, dec_batch, dec_seq)
        if not final:
            w_in_b, w_out_b = prepared or (
                _pad_w_in_first(w_in16[i + 1:i + 2]), w_out[i + 1:i + 2].astype(BF16))

        keep = min(WINDOW, seq)
        outs[0].append(s_p.reshape(batch, GLA_HEADS, GLA_DK, GLA_DV))
        outs[1].append(k_p.reshape(batch, keep, SWA_KV_HEADS, SWA_HD))
        outs[2].append(v_p.reshape(batch, keep, SWA_KV_HEADS, SWA_HD))
        outs[3].append(conv_p)
        outs[4].append(s_s.reshape(dec_batch, GLA_HEADS, GLA_DK, GLA_DV))
        outs[5].append(k_s.reshape(dec_batch, dec_seq, SWA_KV_HEADS, SWA_HD))
        outs[6].append(v_s.reshape(dec_batch, dec_seq, SWA_KV_HEADS, SWA_HD))
        outs[7].append(conv_s)

    return (yp, ys.reshape(dec_batch, dec_seq, D_MODEL)) + tuple(jnp.stack(o) for o in outs)
```

```python
import functools

import jax
import jax.numpy as jnp
from jax import lax
from jax.experimental import pallas as pl
from jax.experimental.pallas import tpu as pltpu

F32 = jnp.float32
BF16 = jnp.bfloat16

D_MODEL = 1024
CHUNK = 64
GLA_HEADS = 4
GLA_DK = 64
GLA_DV = 128
GLA_RANK = 16
GLA_NORMALIZER = 16.0
SWA_Q_HEADS = 8
SWA_KV_HEADS = 2
SWA_GROUP = SWA_Q_HEADS // SWA_KV_HEADS
SWA_HD = 64
WINDOW = 128
D_FF = 2816
CONV_W = 3
RMS_EPS = 1e-6

GLA_QK = GLA_HEADS * GLA_DK
GLA_V = GLA_HEADS * GLA_DV
SWA_Q = SWA_Q_HEADS * SWA_HD
SWA_KV = SWA_KV_HEADS * SWA_HD
D_MIX = GLA_V + SWA_Q
BAND = WINDOW + CHUNK

LANE = 128
SUBLANE = 8

C_QG = 0
C_KG = C_QG + GLA_QK
C_VG = C_KG + GLA_QK
C_OG = C_VG + GLA_V
C_LR = C_OG + GLA_V
LR_PAD = LANE
C_QS = C_LR + LR_PAD
C_KS = C_QS + SWA_Q
C_VS = C_KS + SWA_KV
D_IN_P = C_VS + SWA_KV
D_IN = 2 * GLA_QK + 2 * GLA_V + GLA_RANK + SWA_Q + 2 * SWA_KV

assert GLA_DK * 2 == LANE and SWA_HD * 2 == LANE and GLA_DV == LANE and SWA_KV == LANE
GLA_PAIRS = GLA_HEADS // 2
PAIR_V = 2 * GLA_DV
SWA_GRP_Q = SWA_GROUP * SWA_HD

ADA_TILE = 1536
ADA_ROWS = 16
PROMPT_TILE = 512
MIXER_TILE = 512
SUB_CHUNKS = 4
DENSE_PIECE = 256
DENSE_LEAD = 2
W_PREP_ROWS = 256
MIXER_PREP_ROWS_UP = 32
MIXER_PREP_ROWS_DOWN = 128
FFN_PREP_ROWS = 16
FF_PART = 768
FF_PARTS = tuple((lo, min(lo + FF_PART, D_FF)) for lo in range(0, D_FF, FF_PART))
VMEM_LIMIT = 56 * 1024 * 1024

NT_DIMS = (((1,), (1,)), ((), ()))
TN_DIMS = (((0,), (0,)), ((), ()))


def _dot(a, b):
    return jnp.dot(a, b, preferred_element_type=F32)


def _dot_nt(a, b):
    return lax.dot_general(a, b, NT_DIMS, preferred_element_type=F32)


def _dot_tn(a, b):
    return lax.dot_general(a, b, TN_DIMS, preferred_element_type=F32)


def _split_bf16(a):
    hi = a.astype(BF16)
    lo = (a - hi.astype(F32)).astype(BF16)
    return hi, lo


def _sigmoid(a):
    return 1.0 / (1.0 + jnp.exp(-a))


def _log_sigmoid(a):
    return jnp.minimum(a, 0.0) - jnp.log(1.0 + jnp.exp(-jnp.abs(a)))


def _rms_scale(a):
    return lax.rsqrt(jnp.mean(a * a, axis=-1, keepdims=True) + RMS_EPS)


def _ada_kernel(c_ref, w_ref, b_ref, o_ref):
    c = c_ref[...]
    a = (c * _sigmoid(c)).astype(BF16)
    o_ref[...] = _dot(a, w_ref[...].astype(BF16)) + b_ref[...]


def _ada_call(c_all, w_ada, b_ada):
    depth = w_ada.shape[0]
    n_tiles = (6 * D_MODEL) // ADA_TILE
    return pl.pallas_call(
        _ada_kernel,
        grid=(depth, n_tiles),
        in_specs=[
            pl.BlockSpec((ADA_ROWS, D_MODEL), lambda l, j: (0, 0)),
            pl.BlockSpec((None, D_MODEL, ADA_TILE), lambda l, j: (l, 0, j)),
            pl.BlockSpec((None, 1, ADA_TILE), lambda l, j: (l, 0, j)),
        ],
        out_specs=pl.BlockSpec((None, ADA_ROWS, ADA_TILE), lambda l, j: (l, 0, j)),
        out_shape=jax.ShapeDtypeStruct((depth, ADA_ROWS, 6 * D_MODEL), F32),
        compiler_params=pltpu.CompilerParams(
            dimension_semantics=("arbitrary", "arbitrary"), vmem_limit_bytes=VMEM_LIMIT),
        name="adaln_mod",
    )(c_all, w_ada, b_ada.reshape(depth, 1, 6 * D_MODEL))


def _dup_halves(a, low_half):
    swapped = pltpu.roll(a, LANE // 2, axis=1)
    return jnp.where(low_half, a, swapped), jnp.where(low_half, swapped, a)


def _mixer_kernel(sample, n_chunks, layer, preps, *refs):
    if sample:
        (x_ref, mod_ref, g_attn_ref, w_in_ref, w_gk2_ref, b_gk_ref, g_gla_ref, sinks_ref, w_out_ref,
         s0_ref, kc_ref, vc_ref,
         xo_ref, so_ref, ko_ref, vo_ref,
         h_scr, proj_scr, omix_scr, qm_scr, km_scr, qi_scr, ko_scr, vb_scr, gk_scr, dec_scr, u_scr, sb_scr,
         att_scr, pn_scr,
         kd_scr, vd_scr) = refs
        t = None
    else:
        n_in, n_out, n_prep = 9, 4, len(preps)
        prep_src = refs[n_in:n_in + n_prep]
        prep_dst = refs[n_in + n_prep + n_out:n_in + n_prep + n_out + n_prep]
        refs = refs[:n_in] + refs[n_in + n_prep:n_in + n_prep + n_out] + refs[n_in + 2 * n_prep + n_out:]
        (x_ref, mod_ref, g_attn_ref, w_in_ref, w_gk2_ref, b_gk_ref, g_gla_ref, sinks_ref, w_out_ref,
         xo_ref, so_ref, ko_ref, vo_ref,
         h_scr, proj_scr, omix_scr, qm_scr, km_scr, qi_scr, ko_scr, vb_scr, gk_scr, dec_scr, u_scr, sb_scr,
         att_scr, pn_scr,
         kd_scr, vd_scr, s_scr) = refs
        for body, src, dst in zip(preps, prep_src, prep_dst):
            body(src, dst)
        t = pl.program_id(1)
        n_t = pl.num_programs(1)

        @pl.when(t == 0)
        def _():
            s_scr[...] = jnp.zeros_like(s_scr)
            kd_scr[0:WINDOW, :] = jnp.zeros((WINDOW, 2 * LANE), BF16)
            vd_scr[0:WINDOW, :] = jnp.zeros((WINDOW, 2 * LANE), BF16)

    rows = n_chunks * CHUNK
    g_attn = g_attn_ref[...]

    def mod_row(c, idx):
        if sample:
            return mod_ref[c, idx:idx + 1, :]
        return mod_ref[idx:idx + 1, :]

    def chunk_rows(c):
        return slice(c * CHUNK, (c + 1) * CHUNK)

    sub_chunks = min(SUB_CHUNKS, n_chunks)
    n_sub = n_chunks // sub_chunks
    sub_rows = sub_chunks * CHUNK

    def sub_slice(s):
        return slice(s * sub_rows, (s + 1) * sub_rows)

    def norm_groups(s):
        if sample:
            return [(c, chunk_rows(c)) for c in range(s * sub_chunks, (s + 1) * sub_chunks)]
        return [(0, sub_slice(s))]

    r64 = lax.broadcasted_iota(jnp.int32, (CHUNK, CHUNK), 0)
    c64 = lax.broadcasted_iota(jnp.int32, (CHUNK, CHUNK), 1)
    tri = jnp.where(r64 >= c64, 1.0, 0.0).astype(BF16)
    tri2 = jnp.concatenate([tri, tri], axis=1)
    l_pair = lax.broadcasted_iota(jnp.int32, (CHUNK, LANE), 1)
    low_half = l_pair < LANE // 2
    causal_heads = (lax.broadcasted_iota(jnp.int32, (CHUNK, GLA_QK), 0)
                    >= (lax.broadcasted_iota(jnp.int32, (CHUNK, GLA_QK), 1) & (CHUNK - 1)))
    diag_heads = ((lax.broadcasted_iota(jnp.int32, (GLA_QK, GLA_QK), 0) // GLA_DK)
                  == (lax.broadcasted_iota(jnp.int32, (GLA_QK, GLA_QK), 1) // GLA_DK))
    diag_v = ((lax.broadcasted_iota(jnp.int32, (LANE, PAIR_V), 0) >= LANE // 2)
              == (lax.broadcasted_iota(jnp.int32, (LANE, PAIR_V), 1) >= GLA_DV))
    g_gla = g_gla_ref[...]
    low_sub = lax.broadcasted_iota(jnp.int32, (sub_rows, LANE), 1) < LANE // 2

    def project_steps(s):
        ss = sub_slice(s)

        def norm():
            for mi, rs in norm_groups(s):
                x = x_ref[rs, :]
                gain = g_attn * (1.0 + mod_row(mi, 1))
                h_scr[rs, :] = (x * _rms_scale(x) * gain + mod_row(mi, 0)).astype(BF16)

        def piece(lo, hi):
            proj_scr[ss, lo:hi] = _dot(h_scr[ss, :], w_in_ref[:, lo:hi])

        def gates():
            lr = proj_scr[ss, C_LR:C_LR + LR_PAD].astype(BF16)
            gk_scr[ss, :] = (_log_sigmoid(_dot(lr, w_gk2_ref[...]) + b_gk_ref[...])
                             * (1.0 / GLA_NORMALIZER))
            vb_scr[ss, :] = proj_scr[ss, C_VG:C_VG + GLA_V].astype(BF16)

        def bands():
            k_dup = _dup_halves(proj_scr[ss, C_KS:C_KS + SWA_KV], low_sub)
            v_dup = _dup_halves(proj_scr[ss, C_VS:C_VS + SWA_KV], low_sub)
            for kv in range(SWA_KV_HEADS):
                ls = slice(kv * LANE, (kv + 1) * LANE)
                if sample:
                    for c in range(s * sub_chunks, (s + 1) * sub_chunks):
                        local = slice((c - s * sub_chunks) * CHUNK, (c - s * sub_chunks + 1) * CHUNK)
                        kd_scr[c * BAND + WINDOW:(c + 1) * BAND, ls] = k_dup[kv][local, :].astype(BF16)
                        vd_scr[c * BAND + WINDOW:(c + 1) * BAND, ls] = v_dup[kv][local, :].astype(BF16)
                else:
                    band_rows = slice(WINDOW + s * sub_rows, WINDOW + (s + 1) * sub_rows)
                    kd_scr[band_rows, ls] = k_dup[kv].astype(BF16)
                    vd_scr[band_rows, ls] = v_dup[kv].astype(BF16)

        steps = [norm]
        for lo in range(0, D_IN_P, DENSE_PIECE):
            steps.append(functools.partial(piece, lo, min(lo + DENSE_PIECE, D_IN_P)))
        return steps + [gates, bands]

    def gla_prepare(c):
        rs = chunk_rows(c)
        gk_hi, gk_lo = _split_bf16(gk_scr[rs, :])
        cum = _dot(tri2, jnp.concatenate([gk_hi, gk_lo], axis=0))
        mid = cum[CHUNK // 2:CHUNK // 2 + 1, :]
        last = cum[CHUNK - 1:CHUNK, :]
        q = proj_scr[rs, C_QG:C_QG + GLA_QK] * (GLA_DK ** -0.5)
        k = proj_scr[rs, C_KG:C_KG + GLA_QK]
        qm_scr[rs, :] = (q * jnp.exp(cum - mid)).astype(BF16)
        km_scr[rs, :] = (k * jnp.exp(mid - cum)).astype(BF16)
        qi_scr[rs, :] = (q * jnp.exp(cum)).astype(BF16)
        ko_scr[rs, :] = (k * jnp.exp(last - cum)).astype(BF16)
        dec_scr[c] = jnp.broadcast_to(jnp.exp(last), (LANE, GLA_QK)).T

    def gla_increment(c):
        rs = chunk_rows(c)
        for p in range(GLA_PAIRS):
            upd = _dot_tn(ko_scr[rs, p * LANE:(p + 1) * LANE], vb_scr[rs, p * PAIR_V:(p + 1) * PAIR_V])
            u_scr[c, p * LANE:p * LANE + GLA_DK, :] = upd[0:GLA_DK, 0:GLA_DV]
            u_scr[c, p * LANE + GLA_DK:(p + 1) * LANE, :] = upd[GLA_DK:LANE, GLA_DV:PAIR_V]

    def gla_recurrence(s):
        state = None if sample else s_scr[...]
        for c in range(s * sub_chunks, (s + 1) * sub_chunks):
            if sample:
                state = s0_ref[c]
            sb_scr[c] = state.astype(BF16)
            state = dec_scr[c] * state + u_scr[c]
            if sample:
                so_ref[c] = state
        if not sample:
            s_scr[...] = state

    def gla_scores(c):
        rs = chunk_rows(c)
        km = km_scr[rs, :]
        k_bd = jnp.where(diag_heads, jnp.concatenate([km] * GLA_HEADS, axis=0), 0.0)
        att_scr[rs, :] = jnp.where(causal_heads, _dot_nt(qm_scr[rs, :], k_bd), 0.0).astype(BF16)

    def gla_output(c, p):
        rs = chunk_rows(c)
        ls = slice(p * LANE, (p + 1) * LANE)
        v_pair = vb_scr[rs, p * PAIR_V:(p + 1) * PAIR_V]
        s_pair = sb_scr[c, ls, :]
        w_top = jnp.where(diag_v, jnp.concatenate([v_pair, v_pair], axis=0), 0.0)
        w_bot = jnp.where(diag_v, jnp.concatenate([s_pair, s_pair], axis=1), 0.0)
        o_pair = _dot(jnp.concatenate([att_scr[rs, ls], qi_scr[rs, ls]], axis=1),
                      jnp.concatenate([w_top, w_bot], axis=0))
        for hh in range(2):
            h = 2 * p + hh
            o = o_pair[:, hh * GLA_DV:(hh + 1) * GLA_DV]
            og = proj_scr[rs, C_OG + h * GLA_DV:C_OG + (h + 1) * GLA_DV]
            y = o * _rms_scale(o) * g_gla * (og * _sigmoid(og))
            omix_scr[rs, h * GLA_DV:(h + 1) * GLA_DV] = y.astype(BF16)

    if sample:
        low_win = lax.broadcasted_iota(jnp.int32, (WINDOW, LANE), 1) < LANE // 2
        for c in range(n_chunks):
            kc_dup = _dup_halves(kc_ref[c], low_win)
            vc_dup = _dup_halves(vc_ref[c], low_win)
            for kv in range(SWA_KV_HEADS):
                ls = slice(kv * LANE, (kv + 1) * LANE)
                kd_scr[c * BAND:c * BAND + WINDOW, ls] = kc_dup[kv].astype(BF16)
                vd_scr[c * BAND:c * BAND + WINDOW, ls] = vc_dup[kv].astype(BF16)

    lane_q = lax.broadcasted_iota(jnp.int32, (1, SWA_GRP_Q), 1)
    key_ids = lax.broadcasted_iota(jnp.int32, (BAND, SWA_GRP_Q), 0)
    sink_vecs = []
    for kv in range(SWA_KV_HEADS):
        vec = jnp.full((1, SWA_GRP_Q), sinks_ref[layer, kv * SWA_GROUP + SWA_GROUP - 1], F32)
        for g in range(SWA_GROUP - 2, -1, -1):
            vec = jnp.where(lane_q < (g + 1) * SWA_HD, sinks_ref[layer, kv * SWA_GROUP + g], vec)
        sink_vecs.append(vec)

    def band_rows(c):
        return slice(c * BAND, (c + 1) * BAND) if sample else slice(c * CHUNK, c * CHUNK + BAND)

    def swa_scores(c, kv):
        rs = chunk_rows(c)
        band = band_rows(c)
        ls = slice(kv * LANE, (kv + 1) * LANE)
        qg = (proj_scr[rs, C_QS + kv * SWA_GRP_Q:C_QS + (kv + 1) * SWA_GRP_Q]
              * (SWA_HD ** -0.5)).astype(BF16)
        q_stack = jnp.concatenate(
            [jnp.where(low_half if hh == 0 else ~low_half, qg[:, pp * LANE:(pp + 1) * LANE], 0.0)
             for pp in range(SWA_GROUP // 2) for hh in range(2)], axis=0)
        s_t = _dot_nt(kd_scr[band, ls], q_stack)
        if not sample and c * CHUNK < WINDOW:
            first_valid = WINDOW - (t * rows + c * CHUNK)
            s_t = jnp.where(key_ids >= first_valid, s_t, -jnp.inf)
        sink = sink_vecs[kv]
        m = jnp.maximum(jnp.max(s_t, axis=0, keepdims=True), sink)
        p_t = jnp.exp(s_t - m)
        den = jnp.sum(p_t, axis=0, keepdims=True) + jnp.exp(sink - m)
        pn_scr[c * SWA_KV_HEADS + kv] = (p_t * (1.0 / den)).astype(BF16)

    def swa_output(c, kv):
        rs = chunk_rows(c)
        ls = slice(kv * LANE, (kv + 1) * LANE)
        o_t = _dot_tn(vd_scr[band_rows(c), ls], pn_scr[c * SWA_KV_HEADS + kv]).T
        for pp in range(SWA_GROUP // 2):
            o_pair = jnp.where(low_half, o_t[(2 * pp) * CHUNK:(2 * pp + 1) * CHUNK, :],
                               o_t[(2 * pp + 1) * CHUNK:(2 * pp + 2) * CHUNK, :])
            col = GLA_V + kv * SWA_GRP_Q + pp * LANE
            omix_scr[rs, col:col + LANE] = o_pair.astype(BF16)

    def out_steps(s):
        ss = sub_slice(s)
        base = s * sub_rows

        def piece(lo, hi):
            mix = _dot(omix_scr[ss, :], w_out_ref[:, lo:hi])
            for mi, rs in norm_groups(s):
                local = slice(rs.start - base, rs.stop - base)
                xo_ref[rs, lo:hi] = x_ref[rs, lo:hi] + mod_row(mi, 2)[:, lo:hi] * mix[local, :]

        return [functools.partial(piece, lo, lo + DENSE_PIECE) for lo in range(0, D_MODEL, DENSE_PIECE)]

    def block_steps(s):
        blocks = range(s * sub_chunks, (s + 1) * sub_chunks)
        pairs = [(c, p) for c in blocks for p in range(GLA_PAIRS)]
        groups = [(c, kv) for c in blocks for kv in range(SWA_KV_HEADS)]
        steps = [functools.partial(gla_prepare, c) for c in blocks]
        steps += [functools.partial(swa_scores, c, kv) for c, kv in groups]
        steps += [functools.partial(gla_increment, c) for c in blocks]
        steps += [functools.partial(gla_scores, c) for c in blocks]
        steps.append(functools.partial(gla_recurrence, s))
        steps += [functools.partial(swa_output, c, kv) for c, kv in groups]
        steps += [functools.partial(gla_output, c, p) for c, p in pairs]
        return steps

    for step in project_steps(0):
        step()
    for s in range(n_sub):
        dense = (project_steps(s + 1) if s + 1 < n_sub else []) + (out_steps(s - 1) if s > 0 else [])
        work = block_steps(s)
        issued = 0
        for i, step in enumerate(work):
            while issued < len(dense) and (issued - DENSE_LEAD) * len(work) < (i + 1) * len(dense):
                dense[issued]()
                issued += 1
            step()
        for step in dense[issued:]:
            step()
    for step in out_steps(n_sub - 1):
        step()

    if sample:
        ko_ref[...] = proj_scr[:, C_KS:C_KS + SWA_KV]
        vo_ref[...] = proj_scr[:, C_VS:C_VS + SWA_KV]
    else:
        kd_scr[0:WINDOW, :] = kd_scr[rows:rows + WINDOW, :]
        vd_scr[0:WINDOW, :] = vd_scr[rows:rows + WINDOW, :]

        @pl.when(t == n_t - 1)
        def _():
            so_ref[...] = s_scr[...]
            ko_ref[...] = proj_scr[rows - WINDOW:rows, C_KS:C_KS + SWA_KV]
            vo_ref[...] = proj_scr[rows - WINDOW:rows, C_VS:C_VS + SWA_KV]


def _pad_cast_w_in(w_ref, o_ref):
    head = C_LR + GLA_RANK
    w = w_ref[...].astype(F32)
    o_ref[:, 0:C_LR] = w[:, 0:C_LR].astype(BF16)
    gate = jnp.concatenate(
        [w[:, C_LR:head], jnp.zeros((w.shape[0], LR_PAD - GLA_RANK), F32)], axis=1)
    o_ref[:, C_LR:C_QS] = gate.astype(BF16)
    o_ref[:, C_QS:D_IN_P] = w[:, head:D_IN].astype(BF16)


def _cast_weight(w_ref, o_ref):
    o_ref[...] = w_ref[...].astype(BF16)


class _WeightPrep:
    def __init__(self, weight, layer, rows_per_step, out_cols, body):
        self.weight, self.layer, self.rows_per_step = weight, layer, rows_per_step
        self.rows, self.in_cols = weight.shape[1], weight.shape[2]
        self.out_cols, self.body = out_cols, body
        assert self.rows % rows_per_step == 0
        self.n_blocks = self.rows // rows_per_step

    def specs(self, n_t):
        def block(b, t):
            return jnp.minimum(b * n_t + t, self.n_blocks - 1)
        layer = self.layer
        return (pl.BlockSpec((None, self.rows_per_step, self.in_cols), lambda b, t: (layer, block(b, t), 0)),
                pl.BlockSpec((None, self.rows_per_step, self.out_cols), lambda b, t: (0, block(b, t), 0)),
                jax.ShapeDtypeStruct((1, self.rows, self.out_cols), BF16))


def _const_spec(shape):
    zeros = (0,) * len(shape)
    return pl.BlockSpec(shape, lambda *_: zeros, pipeline_mode=pl.Buffered(1))


def _layer_spec(layer, shape):
    index = (layer,) + (0,) * len(shape)
    return pl.BlockSpec((None,) + shape, lambda *_: index, pipeline_mode=pl.Buffered(1))


def _mixer_weight_specs(layer):
    return [
        _layer_spec(layer, (1, D_MODEL)),
        _layer_spec(0, (D_MODEL, D_IN_P)),
        _layer_spec(layer, (LR_PAD, GLA_QK)),
        _layer_spec(layer, (1, GLA_QK)),
        _layer_spec(layer, (1, GLA_DV)),
        pl.BlockSpec(memory_space=pltpu.SMEM),
        _layer_spec(0, (D_MIX, D_MODEL)),
    ]


def _mixer_scratch(rows, n_chunks, band_rows):
    state_rows = GLA_HEADS * GLA_DK
    return [
        pltpu.VMEM((rows, D_MODEL), BF16),
        pltpu.VMEM((rows, D_IN_P), F32),
        pltpu.VMEM((rows, D_MIX), BF16),
        pltpu.VMEM((rows, GLA_QK), BF16),
        pltpu.VMEM((rows, GLA_QK), BF16),
        pltpu.VMEM((rows, GLA_QK), BF16),
        pltpu.VMEM((rows, GLA_QK), BF16),
        pltpu.VMEM((rows, GLA_V), BF16),
        pltpu.VMEM((rows, GLA_QK), F32),
        pltpu.VMEM((n_chunks, state_rows, GLA_DV), F32),
        pltpu.VMEM((n_chunks, state_rows, GLA_DV), F32),
        pltpu.VMEM((n_chunks, state_rows, GLA_DV), BF16),
        pltpu.VMEM((rows, GLA_QK), BF16),
        pltpu.VMEM((n_chunks * SWA_KV_HEADS, BAND, SWA_GRP_Q), BF16),
        pltpu.VMEM((band_rows, 2 * LANE), BF16),
        pltpu.VMEM((band_rows, 2 * LANE), BF16),
    ]


def _mixer_prompt_call(layer, x, mod, mod_row0, weights, preps):
    batch, seq, _ = x.shape
    tile = min(MIXER_TILE, seq)
    assert seq % tile == 0 and tile % CHUNK == 0 and tile >= WINDOW
    n_chunks = tile // CHUNK
    n_t = seq // tile
    state_rows = GLA_HEADS * GLA_DK
    prep_specs = [p.specs(n_t) for p in preps]
    assert all(p.n_blocks <= batch * n_t for p in preps)
    return pl.pallas_call(
        functools.partial(_mixer_kernel, False, n_chunks, layer, tuple(p.body for p in preps)),
        grid=(batch, n_t),
        in_specs=[
            pl.BlockSpec((None, tile, D_MODEL), lambda b, t: (b, t, 0)),
            pl.BlockSpec((None, None, 6, D_MODEL), lambda b, t: (layer, mod_row0 + b, 0, 0)),
        ] + _mixer_weight_specs(layer) + [s[0] for s in prep_specs],
        out_specs=[
            pl.BlockSpec((None, tile, D_MODEL), lambda b, t: (b, t, 0)),
            pl.BlockSpec((None, state_rows, GLA_DV), lambda b, t: (b, 0, 0)),
            pl.BlockSpec((None, WINDOW, SWA_KV), lambda b, t: (b, 0, 0)),
            pl.BlockSpec((None, WINDOW, SWA_KV), lambda b, t: (b, 0, 0)),
        ] + [s[1] for s in prep_specs],
        out_shape=[
            jax.ShapeDtypeStruct((batch, seq, D_MODEL), F32),
            jax.ShapeDtypeStruct((batch, state_rows, GLA_DV), F32),
            jax.ShapeDtypeStruct((batch, WINDOW, SWA_KV), F32),
            jax.ShapeDtypeStruct((batch, WINDOW, SWA_KV), F32),
        ] + [s[2] for s in prep_specs],
        scratch_shapes=_mixer_scratch(tile, n_chunks, WINDOW + tile) + [
            pltpu.VMEM((state_rows, GLA_DV), F32),
        ],
        compiler_params=pltpu.CompilerParams(
            dimension_semantics=("arbitrary", "arbitrary"), vmem_limit_bytes=VMEM_LIMIT),
        name="mixer_prompt",
    )(x, mod, *weights, *[p.weight for p in preps])


def _mixer_sample_call(layer, x, mod, weights, s0, k_cache, v_cache):
    batch, seq, _ = x.shape
    assert seq == CHUNK
    rows = batch * seq
    state_rows = GLA_HEADS * GLA_DK

    def full(shape):
        zeros = (0,) * len(shape)
        return pl.BlockSpec(shape, lambda i: zeros)

    def of_layer(shape):
        index = (layer,) + (0,) * len(shape)
        return pl.BlockSpec((None,) + shape, lambda i: index)

    return pl.pallas_call(
        functools.partial(_mixer_kernel, True, batch, layer, ()),
        grid=(1,),
        in_specs=[full((rows, D_MODEL)), of_layer((batch, 6, D_MODEL))] + _mixer_weight_specs(layer) + [
            of_layer((batch, state_rows, GLA_DV)),
            of_layer((batch, WINDOW, SWA_KV)),
            of_layer((batch, WINDOW, SWA_KV)),
        ],
        out_specs=[
            full((rows, D_MODEL)),
            full((batch, state_rows, GLA_DV)),
            full((rows, SWA_KV)),
            full((rows, SWA_KV)),
        ],
        out_shape=[
            jax.ShapeDtypeStruct((rows, D_MODEL), F32),
            jax.ShapeDtypeStruct((batch, state_rows, GLA_DV), F32),
            jax.ShapeDtypeStruct((rows, SWA_KV), F32),
            jax.ShapeDtypeStruct((rows, SWA_KV), F32),
        ],
        scratch_shapes=_mixer_scratch(rows, batch, batch * BAND),
        compiler_params=pltpu.CompilerParams(
            dimension_semantics=("arbitrary",), vmem_limit_bytes=VMEM_LIMIT),
        name="mixer_sample",
    )(x.reshape(rows, D_MODEL), mod, *weights, s0, k_cache, v_cache)


def _ffn_kernel(sample, final, n_seg, seg_len, preps, *refs):
    if sample:
        (x_ref, mod_ref, g_ffn_ref, w_up_ref, conv_w_ref, conv_b_ref, w_down_ref, g_final_ref,
         past_ref, xo_ref, co_ref, h_scr, ub_scr, act_scr) = refs
    else:
        n_in, n_out, n_prep = 8, 2, len(preps)
        prep_src = refs[n_in:n_in + n_prep]
        prep_dst = refs[n_in + n_prep + n_out:n_in + n_prep + n_out + n_prep]
        refs = refs[:n_in] + refs[n_in + n_prep:n_in + n_prep + n_out] + refs[n_in + 2 * n_prep + n_out:]
        (x_ref, mod_ref, g_ffn_ref, w_up_ref, conv_w_ref, conv_b_ref, w_down_ref, g_final_ref,
         xo_ref, co_ref, h_scr, ub_scr, act_scr, past_scr) = refs
        for body, src, dst in zip(preps, prep_src, prep_dst):
            body(src, dst)
        t = pl.program_id(1)
        n_t = pl.num_programs(1)

        @pl.when(t == 0)
        def _():
            past_scr[...] = jnp.zeros_like(past_scr)

    stride = seg_len + SUBLANE
    g_ffn = g_ffn_ref[...]

    def mod_row(c, idx):
        if sample:
            return mod_ref[c, idx:idx + 1, :]
        return mod_ref[idx:idx + 1, :]

    for c in range(n_seg):
        rs = slice(c * seg_len, (c + 1) * seg_len)
        x = x_ref[rs, :]
        gain = g_ffn * (1.0 + mod_row(c, 4))
        h_scr[rs, :] = (x * _rms_scale(x) * gain + mod_row(c, 3)).astype(BF16)

    def up(j):
        lo, hi = FF_PARTS[j]
        h = h_scr[...]
        return _dot(h, w_up_ref[:, lo:hi]), _dot(h, w_up_ref[:, D_FF + lo:D_FF + hi])

    def activate(j, u, val):
        lo, hi = FF_PARTS[j]
        width = hi - lo
        w0 = conv_w_ref[0:1, lo:hi]
        w1 = conv_w_ref[1:2, lo:hi]
        w2 = conv_w_ref[2:3, lo:hi]
        cb = conv_b_ref[:, lo:hi]
        for c in range(n_seg):
            base = c * stride
            rs = slice(c * seg_len, (c + 1) * seg_len)
            if sample:
                ub_scr[base + SUBLANE - 2:base + SUBLANE, 0:width] = past_ref[c, :, lo:hi]
            else:
                ub_scr[base + SUBLANE - 2:base + SUBLANE, 0:width] = past_scr[:, lo:hi]
            u_seg = u[rs, :]
            ub_scr[base + SUBLANE:base + SUBLANE + seg_len, 0:width] = u_seg
            u1 = ub_scr[base + SUBLANE - 1:base + SUBLANE - 1 + seg_len, 0:width]
            u2 = ub_scr[base + SUBLANE - 2:base + SUBLANE - 2 + seg_len, 0:width]
            uc = w0 * u2 + w1 * u1 + w2 * u_seg + cb
            act_scr[rs, lo:hi] = (uc * _sigmoid(uc) * val[rs, :]).astype(BF16)
            tail = ub_scr[base + seg_len + SUBLANE - 2:base + seg_len + SUBLANE, 0:width]
            if sample:
                co_ref[c, :, lo:hi] = tail
            else:
                past_scr[:, lo:hi] = tail

    def down(j):
        lo, hi = FF_PARTS[j]
        return _dot(act_scr[:, lo:hi], w_down_ref[lo:hi, :])

    n_parts = len(FF_PARTS)
    pending = {0: up(0)}
    acc = None
    for j in range(n_parts):
        if j + 1 < n_parts:
            pending[j + 1] = up(j + 1)
        activate(j, *pending.pop(j))
        if j > 0:
            d = down(j - 1)
            acc = d if acc is None else acc + d
    acc = acc + down(n_parts - 1)

    for c in range(n_seg):
        rs = slice(c * seg_len, (c + 1) * seg_len)
        y = x_ref[rs, :] + mod_row(c, 5) * acc[rs, :]
        if final:
            y = y * _rms_scale(y) * g_final_ref[...]
        xo_ref[rs, :] = y

    if not sample:
        @pl.when(t == n_t - 1)
        def _():
            co_ref[...] = past_scr[...]


def _ffn_weight_specs(layer):
    return [
        _layer_spec(layer, (1, D_MODEL)),
        _layer_spec(0, (D_MODEL, 2 * D_FF)),
        _layer_spec(layer, (CONV_W, D_FF)),
        _layer_spec(layer, (1, D_FF)),
        _layer_spec(0, (D_FF, D_MODEL)),
        _const_spec((1, D_MODEL)),
    ]


def _ffn_prompt_call(layer, x, mod, mod_row0, weights, final, preps):
    batch, seq, _ = x.shape
    tile = min(PROMPT_TILE, seq)
    n_t = seq // tile
    prep_specs = [p.specs(n_t) for p in preps]
    assert all(p.n_blocks <= batch * n_t for p in preps)
    return pl.pallas_call(
        functools.partial(_ffn_kernel, False, final, 1, tile, tuple(p.body for p in preps)),
        grid=(batch, n_t),
        in_specs=[
            pl.BlockSpec((None, tile, D_MODEL), lambda b, t: (b, t, 0)),
            pl.BlockSpec((None, None, 6, D_MODEL), lambda b, t: (layer, mod_row0 + b, 0, 0)),
        ] + _ffn_weight_specs(layer) + [s[0] for s in prep_specs],
        out_specs=[
            pl.BlockSpec((None, tile, D_MODEL), lambda b, t: (b, t, 0)),
            pl.BlockSpec((None, CONV_W - 1, D_FF), lambda b, t: (b, 0, 0)),
        ] + [s[1] for s in prep_specs],
        out_shape=[
            jax.ShapeDtypeStruct((batch, seq, D_MODEL), F32),
            jax.ShapeDtypeStruct((batch, CONV_W - 1, D_FF), F32),
        ] + [s[2] for s in prep_specs],
        scratch_shapes=[
            pltpu.VMEM((tile, D_MODEL), BF16),
            pltpu.VMEM((tile + SUBLANE, FF_PART), F32),
            pltpu.VMEM((tile, D_FF), BF16),
            pltpu.VMEM((CONV_W - 1, D_FF), F32),
        ],
        compiler_params=pltpu.CompilerParams(
            dimension_semantics=("arbitrary", "arbitrary"), vmem_limit_bytes=VMEM_LIMIT),
        name="ffn_prompt",
    )(x, mod, *weights, *[p.weight for p in preps])


def _ffn_sample_call(layer, x2d, mod, weights, past, final, batch, seq):
    rows = batch * seq

    def full(shape):
        zeros = (0,) * len(shape)
        return pl.BlockSpec(shape, lambda i: zeros)

    def of_layer(shape):
        index = (layer,) + (0,) * len(shape)
        return pl.BlockSpec((None,) + shape, lambda i: index)

    return pl.pallas_call(
        functools.partial(_ffn_kernel, True, final, batch, seq, ()),
        grid=(1,),
        in_specs=[full((rows, D_MODEL)), of_layer((batch, 6, D_MODEL))] + _ffn_weight_specs(layer) + [
            of_layer((batch, CONV_W - 1, D_FF)),
        ],
        out_specs=[full((rows, D_MODEL)), full((batch, CONV_W - 1, D_FF))],
        out_shape=[
            jax.ShapeDtypeStruct((rows, D_MODEL), F32),
            jax.ShapeDtypeStruct((batch, CONV_W - 1, D_FF), F32),
        ],
        scratch_shapes=[
            pltpu.VMEM((rows, D_MODEL), BF16),
            pltpu.VMEM((batch * (seq + SUBLANE), FF_PART), F32),
            pltpu.VMEM((rows, D_FF), BF16),
        ],
        compiler_params=pltpu.CompilerParams(
            dimension_semantics=("arbitrary",), vmem_limit_bytes=VMEM_LIMIT),
        name="ffn_sample",
    )(x2d, mod, *weights, past)


def _pad_w_in_first(w):
    n_blocks = D_MODEL // W_PREP_ROWS
    return pl.pallas_call(
        _pad_cast_w_in,
        grid=(n_blocks,),
        in_specs=[pl.BlockSpec((None, W_PREP_ROWS, D_IN), lambda r: (0, r, 0))],
        out_specs=pl.BlockSpec((None, W_PREP_ROWS, D_IN_P), lambda r: (0, r, 0)),
        out_shape=jax.ShapeDtypeStruct((1, D_MODEL, D_IN_P), BF16),
        compiler_params=pltpu.CompilerParams(
            dimension_semantics=("arbitrary",), vmem_limit_bytes=VMEM_LIMIT),
        name="pad_w_in",
    )(w)


def kernel(x_prompt, x_sample, state_gla, cache_swa_k, cache_swa_v, state_conv, c_prompt, c_sample,
           w_ada, b_ada, g_attn, g_ffn, w_in, w_gk2, b_gk, g_gla, sinks, w_out, w_up, conv_w, conv_b,
           w_down, g_final):
    depth = w_ada.shape[0]
    batch, seq, _ = x_prompt.shape
    dec_batch, dec_seq, _ = x_sample.shape
    state_rows = GLA_HEADS * GLA_DK

    c_all = jnp.concatenate(
        [c_sample, c_prompt, jnp.zeros((ADA_ROWS - batch - dec_batch, D_MODEL), F32)], axis=0)
    mod_all = _ada_call(c_all, w_ada, b_ada).reshape(depth, ADA_ROWS, 6, D_MODEL)

    w_gk2_p = jnp.concatenate(
        [w_gk2, jnp.zeros((depth, LR_PAD - GLA_RANK, GLA_QK), F32)], axis=1).astype(BF16)
    w_in16 = w_in.astype(BF16)
    w_in_b, w_out_b = _pad_w_in_first(w_in16), w_out[:1].astype(BF16)
    mixer_steps = batch * (seq // min(MIXER_TILE, seq))
    ffn_steps = batch * (seq // min(PROMPT_TILE, seq))
    prep_ahead = (D_MODEL // MIXER_PREP_ROWS_UP <= mixer_steps
                  and D_FF // MIXER_PREP_ROWS_DOWN <= mixer_steps
                  and D_MODEL // FFN_PREP_ROWS <= ffn_steps)
    s0_all = state_gla.reshape(depth, dec_batch, state_rows, GLA_DV)
    kc_all = cache_swa_k.reshape(depth, dec_batch, WINDOW, SWA_KV)
    vc_all = cache_swa_v.reshape(depth, dec_batch, WINDOW, SWA_KV)

    yp = x_prompt
    ys = x_sample.reshape(dec_batch * dec_seq, D_MODEL)
    outs = [[] for _ in range(8)]
    for i in range(depth):
        final = i == depth - 1
        mixer_w = (g_attn[:, None], w_in_b, w_gk2_p, b_gk[:, None], g_gla[:, None], sinks, w_out_b)
        mixer_preps, ffn_preps = [], []
        if prep_ahead:
            mixer_preps = [_WeightPrep(w_up, i, MIXER_PREP_ROWS_UP, 2 * D_FF, _cast_weight),
                           _WeightPrep(w_down, i, MIXER_PREP_ROWS_DOWN, D_MODEL, _cast_weight)]
            if not final:
                ffn_preps = [_WeightPrep(w_in16, i + 1, FFN_PREP_ROWS, D_IN_P, _pad_cast_w_in),
                             _WeightPrep(w_out, i + 1, FFN_PREP_ROWS, D_MODEL, _cast_weight)]

        yp, s_p, k_p, v_p, *prepared = _mixer_prompt_call(i, yp, mod_all, dec_batch, mixer_w, mixer_preps)
        w_up_b, w_down_b = prepared or (w_up[i:i + 1].astype(BF16), w_down[i:i + 1].astype(BF16))
        ffn_w = (g_ffn[:, None], w_up_b, conv_w, conv_b[:, None], w_down_b, g_final[None])
        yp, conv_p, *prepared = _ffn_prompt_call(i, yp, mod_all, dec_batch, ffn_w, final, ffn_preps)

        ys, s_s, k_s, v_s = _mixer_sample_call(
            i, ys.reshape(dec_batch, dec_seq, D_MODEL), mod_all, mixer_w, s0_all, kc_all, vc_all)
        ys, conv_s = _ffn_sample_call(i, ys, mod_all, ffn_w, state_conv, final, dec_batch, dec_seq)
        if not final:
            w_in_b, w_out_b = prepared or (
                _pad_w_in_first(w_in16[i + 1:i + 2]), w_out[i + 1:i + 2].astype(BF16))

        keep = min(WINDOW, seq)
        outs[0].append(s_p.reshape(batch, GLA_HEADS, GLA_DK, GLA_DV))
        outs[1].append(k_p.reshape(batch, keep, SWA_KV_HEADS, SWA_HD))
        outs[2].append(v_p.reshape(batch, keep, SWA_KV_HEADS, SWA_HD))
        outs[3].append(conv_p)
        outs[4].append(s_s.reshape(dec_batch, GLA_HEADS, GLA_DK, GLA_DV))
        outs[5].append(k_s.reshape(dec_batch, dec_seq, SWA_KV_HEADS, SWA_HD))
        outs[6].append(v_s.reshape(dec_batch, dec_seq, SWA_KV_HEADS, SWA_HD))
        outs[7].append(conv_s)

    return (yp, ys.reshape(dec_batch, dec_seq, D_MODEL)) + tuple(jnp.stack(o) for o in outs)
```

```python
import functools

import jax
import jax.numpy as jnp
from jax import lax
from jax.experimental import pallas as pl
from jax.experimental.pallas import tpu as pltpu

F32 = jnp.float32
BF16 = jnp.bfloat16

D_MODEL = 1024
CHUNK = 64
GLA_HEADS = 4
GLA_DK = 64
GLA_DV = 128
GLA_RANK = 16
GLA_NORMALIZER = 16.0
SWA_Q_HEADS = 8
SWA_KV_HEADS = 2
SWA_GROUP = SWA_Q_HEADS // SWA_KV_HEADS
SWA_HD = 64
WINDOW = 128
D_FF = 2816
CONV_W = 3
RMS_EPS = 1e-6

GLA_QK = GLA_HEADS * GLA_DK
GLA_V = GLA_HEADS * GLA_DV
SWA_Q = SWA_Q_HEADS * SWA_HD
SWA_KV = SWA_KV_HEADS * SWA_HD
D_MIX = GLA_V + SWA_Q
BAND = WINDOW + CHUNK

LANE = 128
SUBLANE = 8

C_QG = 0
C_KG = C_QG + GLA_QK
C_VG = C_KG + GLA_QK
C_OG = C_VG + GLA_V
C_LR = C_OG + GLA_V
LR_PAD = LANE
C_QS = C_LR + LR_PAD
C_KS = C_QS + SWA_Q
C_VS = C_KS + SWA_KV
D_IN_P = C_VS + SWA_KV
D_IN = 2 * GLA_QK + 2 * GLA_V + GLA_RANK + SWA_Q + 2 * SWA_KV

assert GLA_DK * 2 == LANE and SWA_HD * 2 == LANE and GLA_DV == LANE and SWA_KV == LANE
GLA_PAIRS = GLA_HEADS // 2
PAIR_V = 2 * GLA_DV
SWA_GRP_Q = SWA_GROUP * SWA_HD

ADA_TILE = 1536
ADA_ROWS = 16
PROMPT_TILE = 512
MIXER_TILE = 1024
SUB_CHUNKS = 4
DENSE_PIECE = 256
DENSE_LEAD = 2
W_PREP_ROWS = 256
MIXER_PREP_ROWS_UP = 32
MIXER_PREP_ROWS_DOWN = 128
FFN_PREP_ROWS = 16
FF_PART = 768
FF_PARTS = tuple((lo, min(lo + FF_PART, D_FF)) for lo in range(0, D_FF, FF_PART))
VMEM_LIMIT = 56 * 1024 * 1024

LOG2E = 1.4426950408889634
NT_DIMS = (((1,), (1,)), ((), ()))
TN_DIMS = (((0,), (0,)), ((), ()))


def _dot(a, b):
    return jnp.dot(a, b, preferred_element_type=F32)


def _dot_nt(a, b):
    return lax.dot_general(a, b, NT_DIMS, preferred_element_type=F32)


def _dot_tn(a, b):
    return lax.dot_general(a, b, TN_DIMS, preferred_element_type=F32)


def _split_bf16(a):
    hi = a.astype(BF16)
    lo = (a - hi.astype(F32)).astype(BF16)
    return hi, lo


def _sigmoid(a):
    return 1.0 / (1.0 + jnp.exp2(a * (-LOG2E)))


def _log_sigmoid(a):
    return jnp.minimum(a, 0.0) - jnp.log(1.0 + jnp.exp2(jnp.abs(a) * (-LOG2E)))


def _rms_scale(a):
    return lax.rsqrt(jnp.mean(a * a, axis=-1, keepdims=True) + RMS_EPS)


def _ada_kernel(c_ref, w_ref, b_ref, o_ref):
    c = c_ref[...]
    a = (c * _sigmoid(c)).astype(BF16)
    o_ref[...] = _dot(a, w_ref[...].astype(BF16)) + b_ref[...]


def _ada_call(c_all, w_ada, b_ada):
    depth = w_ada.shape[0]
    n_tiles = (6 * D_MODEL) // ADA_TILE
    return pl.pallas_call(
        _ada_kernel,
        grid=(depth, n_tiles),
        in_specs=[
            pl.BlockSpec((ADA_ROWS, D_MODEL), lambda l, j: (0, 0)),
            pl.BlockSpec((None, D_MODEL, ADA_TILE), lambda l, j: (l, 0, j)),
            pl.BlockSpec((None, 1, ADA_TILE), lambda l, j: (l, 0, j)),
        ],
        out_specs=pl.BlockSpec((None, ADA_ROWS, ADA_TILE), lambda l, j: (l, 0, j)),
        out_shape=jax.ShapeDtypeStruct((depth, ADA_ROWS, 6 * D_MODEL), F32),
        compiler_params=pltpu.CompilerParams(
            dimension_semantics=("arbitrary", "arbitrary"), vmem_limit_bytes=VMEM_LIMIT),
        name="adaln_mod",
    )(c_all, w_ada, b_ada.reshape(depth, 1, 6 * D_MODEL))


def _dup_halves(a, low_half):
    swapped = pltpu.roll(a, LANE // 2, axis=1)
    return jnp.where(low_half, a, swapped), jnp.where(low_half, swapped, a)


def _mixer_kernel(sample, n_chunks, layer, preps, *refs):
    if sample:
        (x_ref, mod_ref, g_attn_ref, w_in_ref, w_gk2_ref, b_gk_ref, g_gla_ref, sinks_ref, w_out_ref,
         s0_ref, kc_ref, vc_ref,
         xo_ref, so_ref, ko_ref, vo_ref,
         h_scr, proj_scr, omix_scr, qm_scr, km_scr, qi_scr, ko_scr, vb_scr, gk_scr, dec_scr, u_scr, sb_scr,
         att_scr, pn_scr,
         kd_scr, vd_scr) = refs
        t = None
    else:
        n_in, n_out, n_prep = 9, 4, len(preps)
        prep_src = refs[n_in:n_in + n_prep]
        prep_dst = refs[n_in + n_prep + n_out:n_in + n_prep + n_out + n_prep]
        refs = refs[:n_in] + refs[n_in + n_prep:n_in + n_prep + n_out] + refs[n_in + 2 * n_prep + n_out:]
        (x_ref, mod_ref, g_attn_ref, w_in_ref, w_gk2_ref, b_gk_ref, g_gla_ref, sinks_ref, w_out_ref,
         xo_ref, so_ref, ko_ref, vo_ref,
         h_scr, proj_scr, omix_scr, qm_scr, km_scr, qi_scr, ko_scr, vb_scr, gk_scr, dec_scr, u_scr, sb_scr,
         att_scr, pn_scr,
         kd_scr, vd_scr, s_scr) = refs
        for body, src, dst in zip(preps, prep_src, prep_dst):
            body(src, dst)
        t = pl.program_id(1)
        n_t = pl.num_programs(1)

        @pl.when(t == 0)
        def _():
            s_scr[...] = jnp.zeros_like(s_scr)
            kd_scr[0:WINDOW, :] = jnp.zeros((WINDOW, 2 * LANE), BF16)
            vd_scr[0:WINDOW, :] = jnp.zeros((WINDOW, 2 * LANE), BF16)

    rows = n_chunks * CHUNK
    g_attn = g_attn_ref[...]

    def mod_row(c, idx):
        if sample:
            return mod_ref[c, idx:idx + 1, :]
        return mod_ref[idx:idx + 1, :]

    def chunk_rows(c):
        return slice(c * CHUNK, (c + 1) * CHUNK)

    sub_chunks = min(SUB_CHUNKS, n_chunks)
    n_sub = n_chunks // sub_chunks
    sub_rows = sub_chunks * CHUNK

    def sub_slice(s):
        return slice(s * sub_rows, (s + 1) * sub_rows)

    def norm_groups(s):
        if sample:
            return [(c, chunk_rows(c)) for c in range(s * sub_chunks, (s + 1) * sub_chunks)]
        return [(0, sub_slice(s))]

    r64 = lax.broadcasted_iota(jnp.int32, (CHUNK, CHUNK), 0)
    c64 = lax.broadcasted_iota(jnp.int32, (CHUNK, CHUNK), 1)
    tri = jnp.where(r64 >= c64, 1.0, 0.0).astype(BF16)
    tri2 = jnp.concatenate([tri, tri], axis=1)
    l_pair = lax.broadcasted_iota(jnp.int32, (CHUNK, LANE), 1)
    low_half = l_pair < LANE // 2
    causal_heads = (lax.broadcasted_iota(jnp.int32, (CHUNK, GLA_QK), 0)
                    >= (lax.broadcasted_iota(jnp.int32, (CHUNK, GLA_QK), 1) & (CHUNK - 1)))
    diag_heads = ((lax.broadcasted_iota(jnp.int32, (GLA_QK, GLA_QK), 0) // GLA_DK)
                  == (lax.broadcasted_iota(jnp.int32, (GLA_QK, GLA_QK), 1) // GLA_DK))
    diag_v = ((lax.broadcasted_iota(jnp.int32, (LANE, PAIR_V), 0) >= LANE // 2)
              == (lax.broadcasted_iota(jnp.int32, (LANE, PAIR_V), 1) >= GLA_DV))
    g_gla = g_gla_ref[...]
    low_sub = lax.broadcasted_iota(jnp.int32, (sub_rows, LANE), 1) < LANE // 2

    def project_steps(s):
        ss = sub_slice(s)

        def norm():
            for mi, rs in norm_groups(s):
                x = x_ref[rs, :]
                gain = g_attn * (1.0 + mod_row(mi, 1))
                h_scr[rs, :] = (x * _rms_scale(x) * gain + mod_row(mi, 0)).astype(BF16)

        def piece(lo, hi):
            proj_scr[ss, lo:hi] = _dot(h_scr[ss, :], w_in_ref[:, lo:hi])

        def gates():
            lr = proj_scr[ss, C_LR:C_LR + LR_PAD].astype(BF16)
            gk_scr[ss, :] = (_log_sigmoid(_dot(lr, w_gk2_ref[...]) + b_gk_ref[...])
                             * (LOG2E / GLA_NORMALIZER))
            vb_scr[ss, :] = proj_scr[ss, C_VG:C_VG + GLA_V].astype(BF16)

        def bands():
            k_dup = _dup_halves(proj_scr[ss, C_KS:C_KS + SWA_KV], low_sub)
            v_dup = _dup_halves(proj_scr[ss, C_VS:C_VS + SWA_KV], low_sub)
            for kv in range(SWA_KV_HEADS):
                ls = slice(kv * LANE, (kv + 1) * LANE)
                if sample:
                    for c in range(s * sub_chunks, (s + 1) * sub_chunks):
                        local = slice((c - s * sub_chunks) * CHUNK, (c - s * sub_chunks + 1) * CHUNK)
                        kd_scr[c * BAND + WINDOW:(c + 1) * BAND, ls] = k_dup[kv][local, :].astype(BF16)
                        vd_scr[c * BAND + WINDOW:(c + 1) * BAND, ls] = v_dup[kv][local, :].astype(BF16)
                else:
                    band_rows = slice(WINDOW + s * sub_rows, WINDOW + (s + 1) * sub_rows)
                    kd_scr[band_rows, ls] = k_dup[kv].astype(BF16)
                    vd_scr[band_rows, ls] = v_dup[kv].astype(BF16)

        steps = [norm]
        for lo in range(0, D_IN_P, DENSE_PIECE):
            steps.append(functools.partial(piece, lo, min(lo + DENSE_PIECE, D_IN_P)))
        return steps + [gates, bands]

    def gla_prepare(c):
        rs = chunk_rows(c)
        gk_hi, gk_lo = _split_bf16(gk_scr[rs, :])
        cum = _dot(tri2, jnp.concatenate([gk_hi, gk_lo], axis=0))
        mid = cum[CHUNK // 2:CHUNK // 2 + 1, :]
        last = cum[CHUNK - 1:CHUNK, :]
        q = proj_scr[rs, C_QG:C_QG + GLA_QK] * (GLA_DK ** -0.5)
        k = proj_scr[rs, C_KG:C_KG + GLA_QK]
        qm_scr[rs, :] = (q * jnp.exp2(cum - mid)).astype(BF16)
        km_scr[rs, :] = (k * jnp.exp2(mid - cum)).astype(BF16)
        qi_scr[rs, :] = (q * jnp.exp2(cum)).astype(BF16)
        ko_scr[rs, :] = (k * jnp.exp2(last - cum)).astype(BF16)
        dec_scr[c] = jnp.broadcast_to(jnp.exp2(last), (LANE, GLA_QK)).T

    def gla_increment(c):
        rs = chunk_rows(c)
        for p in range(GLA_PAIRS):
            upd = _dot_tn(ko_scr[rs, p * LANE:(p + 1) * LANE], vb_scr[rs, p * PAIR_V:(p + 1) * PAIR_V])
            u_scr[c, p * LANE:p * LANE + GLA_DK, :] = upd[0:GLA_DK, 0:GLA_DV]
            u_scr[c, p * LANE + GLA_DK:(p + 1) * LANE, :] = upd[GLA_DK:LANE, GLA_DV:PAIR_V]

    def gla_recurrence(s):
        state = None if sample else s_scr[...]
        for c in range(s * sub_chunks, (s + 1) * sub_chunks):
            if sample:
                state = s0_ref[c]
            sb_scr[c] = state.astype(BF16)
            state = dec_scr[c] * state + u_scr[c]
            if sample:
                so_ref[c] = state
        if not sample:
            s_scr[...] = state

    def gla_scores(c):
        rs = chunk_rows(c)
        km = km_scr[rs, :]
        k_bd = jnp.where(diag_heads, jnp.concatenate([km] * GLA_HEADS, axis=0), 0.0)
        att_scr[rs, :] = jnp.where(causal_heads, _dot_nt(qm_scr[rs, :], k_bd), 0.0).astype(BF16)

    def gla_output(c, p):
        rs = chunk_rows(c)
        ls = slice(p * LANE, (p + 1) * LANE)
        v_pair = vb_scr[rs, p * PAIR_V:(p + 1) * PAIR_V]
        s_pair = sb_scr[c, ls, :]
        w_top = jnp.where(diag_v, jnp.concatenate([v_pair, v_pair], axis=0), 0.0)
        w_bot = jnp.where(diag_v, jnp.concatenate([s_pair, s_pair], axis=1), 0.0)
        o_pair = _dot(jnp.concatenate([att_scr[rs, ls], qi_scr[rs, ls]], axis=1),
                      jnp.concatenate([w_top, w_bot], axis=0))
        for hh in range(2):
            h = 2 * p + hh
            o = o_pair[:, hh * GLA_DV:(hh + 1) * GLA_DV]
            og = proj_scr[rs, C_OG + h * GLA_DV:C_OG + (h + 1) * GLA_DV]
            y = o * _rms_scale(o) * g_gla * (og * _sigmoid(og))
            omix_scr[rs, h * GLA_DV:(h + 1) * GLA_DV] = y.astype(BF16)

    if sample:
        low_win = lax.broadcasted_iota(jnp.int32, (WINDOW, LANE), 1) < LANE // 2
        for c in range(n_chunks):
            kc_dup = _dup_halves(kc_ref[c], low_win)
            vc_dup = _dup_halves(vc_ref[c], low_win)
            for kv in range(SWA_KV_HEADS):
                ls = slice(kv * LANE, (kv + 1) * LANE)
                kd_scr[c * BAND:c * BAND + WINDOW, ls] = kc_dup[kv].astype(BF16)
                vd_scr[c * BAND:c * BAND + WINDOW, ls] = vc_dup[kv].astype(BF16)

    lane_q = lax.broadcasted_iota(jnp.int32, (1, SWA_GRP_Q), 1)
    key_ids = lax.broadcasted_iota(jnp.int32, (BAND, SWA_GRP_Q), 0)
    sink_vecs = []
    for kv in range(SWA_KV_HEADS):
        vec = jnp.full((1, SWA_GRP_Q), sinks_ref[layer, kv * SWA_GROUP + SWA_GROUP - 1], F32)
        for g in range(SWA_GROUP - 2, -1, -1):
            vec = jnp.where(lane_q < (g + 1) * SWA_HD, sinks_ref[layer, kv * SWA_GROUP + g], vec)
        sink_vecs.append(vec * LOG2E)

    def band_rows(c):
        return slice(c * BAND, (c + 1) * BAND) if sample else slice(c * CHUNK, c * CHUNK + BAND)

    def swa_scores(c, kv):
        rs = chunk_rows(c)
        band = band_rows(c)
        ls = slice(kv * LANE, (kv + 1) * LANE)
        qg = (proj_scr[rs, C_QS + kv * SWA_GRP_Q:C_QS + (kv + 1) * SWA_GRP_Q]
              * (SWA_HD ** -0.5 * LOG2E)).astype(BF16)
        q_stack = jnp.concatenate(
            [jnp.where(low_half if hh == 0 else ~low_half, qg[:, pp * LANE:(pp + 1) * LANE], 0.0)
             for pp in range(SWA_GROUP // 2) for hh in range(2)], axis=0)
        s_t = _dot_nt(kd_scr[band, ls], q_stack)
        if not sample and c * CHUNK < WINDOW:
            first_valid = WINDOW - (t * rows + c * CHUNK)
            s_t = jnp.where(key_ids >= first_valid, s_t, -jnp.inf)
        sink = sink_vecs[kv]
        m = jnp.maximum(jnp.max(s_t, axis=0, keepdims=True), sink)
        p_t = jnp.exp2(s_t - m)
        den = jnp.sum(p_t, axis=0, keepdims=True) + jnp.exp2(sink - m)
        pn_scr[c * SWA_KV_HEADS + kv] = (p_t * (1.0 / den)).astype(BF16)

    def swa_output(c, kv):
        rs = chunk_rows(c)
        ls = slice(kv * LANE, (kv + 1) * LANE)
        o_t = _dot_tn(vd_scr[band_rows(c), ls], pn_scr[c * SWA_KV_HEADS + kv]).T
        for pp in range(SWA_GROUP // 2):
            o_pair = jnp.where(low_half, o_t[(2 * pp) * CHUNK:(2 * pp + 1) * CHUNK, :],
                               o_t[(2 * pp + 1) * CHUNK:(2 * pp + 2) * CHUNK, :])
            col = GLA_V + kv * SWA_GRP_Q + pp * LANE
            omix_scr[rs, col:col + LANE] = o_pair.astype(BF16)

    def out_steps(s):
        ss = sub_slice(s)
        base = s * sub_rows

        def piece(lo, hi):
            mix = _dot(omix_scr[ss, :], w_out_ref[:, lo:hi])
            for mi, rs in norm_groups(s):
                local = slice(rs.start - base, rs.stop - base)
                xo_ref[rs, lo:hi] = x_ref[rs, lo:hi] + mod_row(mi, 2)[:, lo:hi] * mix[local, :]

        return [functools.partial(piece, lo, lo + DENSE_PIECE) for lo in range(0, D_MODEL, DENSE_PIECE)]

    def block_steps(s):
        blocks = range(s * sub_chunks, (s + 1) * sub_chunks)
        pairs = [(c, p) for c in blocks for p in range(GLA_PAIRS)]
        groups = [(c, kv) for c in blocks for kv in range(SWA_KV_HEADS)]
        steps = [functools.partial(gla_prepare, c) for c in blocks]
        steps += [functools.partial(swa_scores, c, kv) for c, kv in groups]
        steps += [functools.partial(gla_increment, c) for c in blocks]
        steps += [functools.partial(gla_scores, c) for c in blocks]
        steps.append(functools.partial(gla_recurrence, s))
        for (c, p), (c2, kv) in zip(pairs, groups):
            steps.append(functools.partial(gla_output, c, p))
            steps.append(functools.partial(swa_output, c2, kv))
        return steps

    for step in project_steps(0):
        step()
    for s in range(n_sub):
        dense = (project_steps(s + 1) if s + 1 < n_sub else []) + (out_steps(s - 1) if s > 0 else [])
        work = block_steps(s)
        issued = 0
        for i, step in enumerate(work):
            while issued < len(dense) and (issued - DENSE_LEAD) * len(work) < (i + 1) * len(dense):
                dense[issued]()
                issued += 1
            step()
        for step in dense[issued:]:
            step()
    for step in out_steps(n_sub - 1):
        step()

    if sample:
        ko_ref[...] = proj_scr[:, C_KS:C_KS + SWA_KV]
        vo_ref[...] = proj_scr[:, C_VS:C_VS + SWA_KV]
    else:
        kd_scr[0:WINDOW, :] = kd_scr[rows:rows + WINDOW, :]
        vd_scr[0:WINDOW, :] = vd_scr[rows:rows + WINDOW, :]

        @pl.when(t == n_t - 1)
        def _():
            so_ref[...] = s_scr[...]
            ko_ref[...] = proj_scr[rows - WINDOW:rows, C_KS:C_KS + SWA_KV]
            vo_ref[...] = proj_scr[rows - WINDOW:rows, C_VS:C_VS + SWA_KV]


def _pad_cast_w_in(w_ref, o_ref):
    head = C_LR + GLA_RANK
    w = w_ref[...].astype(F32)
    o_ref[:, 0:C_LR] = w[:, 0:C_LR].astype(BF16)
    gate = jnp.concatenate(
        [w[:, C_LR:head], jnp.zeros((w.shape[0], LR_PAD - GLA_RANK), F32)], axis=1)
    o_ref[:, C_LR:C_QS] = gate.astype(BF16)
    o_ref[:, C_QS:D_IN_P] = w[:, head:D_IN].astype(BF16)


def _cast_weight(w_ref, o_ref):
    o_ref[...] = w_ref[...].astype(BF16)


class _WeightPrep:
    def __init__(self, weight, layer, rows_per_step, out_cols, body):
        self.weight, self.layer, self.rows_per_step = weight, layer, rows_per_step
        self.rows, self.in_cols = weight.shape[1], weight.shape[2]
        self.out_cols, self.body = out_cols, body
        assert self.rows % rows_per_step == 0
        self.n_blocks = self.rows // rows_per_step

    def specs(self, n_t):
        def block(b, t):
            return jnp.minimum(b * n_t + t, self.n_blocks - 1)
        layer = self.layer
        return (pl.BlockSpec((None, self.rows_per_step, self.in_cols), lambda b, t: (layer, block(b, t), 0)),
                pl.BlockSpec((None, self.rows_per_step, self.out_cols), lambda b, t: (0, block(b, t), 0)),
                jax.ShapeDtypeStruct((1, self.rows, self.out_cols), BF16))


def _const_spec(shape):
    zeros = (0,) * len(shape)
    return pl.BlockSpec(shape, lambda *_: zeros, pipeline_mode=pl.Buffered(1))


def _layer_spec(layer, shape):
    index = (layer,) + (0,) * len(shape)
    return pl.BlockSpec((None,) + shape, lambda *_: index, pipeline_mode=pl.Buffered(1))


def _mixer_weight_specs(layer):
    return [
        _layer_spec(layer, (1, D_MODEL)),
        _layer_spec(0, (D_MODEL, D_IN_P)),
        _layer_spec(layer, (LR_PAD, GLA_QK)),
        _layer_spec(layer, (1, GLA_QK)),
        _layer_spec(layer, (1, GLA_DV)),
        pl.BlockSpec(memory_space=pltpu.SMEM),
        _layer_spec(0, (D_MIX, D_MODEL)),
    ]


def _mixer_scratch(rows, n_chunks, band_rows):
    state_rows = GLA_HEADS * GLA_DK
    return [
        pltpu.VMEM((rows, D_MODEL), BF16),
        pltpu.VMEM((rows, D_IN_P), F32),
        pltpu.VMEM((rows, D_MIX), BF16),
        pltpu.VMEM((rows, GLA_QK), BF16),
        pltpu.VMEM((rows, GLA_QK), BF16),
        pltpu.VMEM((rows, GLA_QK), BF16),
        pltpu.VMEM((rows, GLA_QK), BF16),
        pltpu.VMEM((rows, GLA_V), BF16),
        pltpu.VMEM((rows, GLA_QK), F32),
        pltpu.VMEM((n_chunks, state_rows, GLA_DV), F32),
        pltpu.VMEM((n_chunks, state_rows, GLA_DV), F32),
        pltpu.VMEM((n_chunks, state_rows, GLA_DV), BF16),
        pltpu.VMEM((rows, GLA_QK), BF16),
        pltpu.VMEM((n_chunks * SWA_KV_HEADS, BAND, SWA_GRP_Q), BF16),
        pltpu.VMEM((band_rows, 2 * LANE), BF16),
        pltpu.VMEM((band_rows, 2 * LANE), BF16),
    ]


def _mixer_prompt_call(layer, x, mod, mod_row0, weights, preps):
    batch, seq, _ = x.shape
    tile = min(MIXER_TILE, seq)
    assert seq % tile == 0 and tile % CHUNK == 0 and tile >= WINDOW
    n_chunks = tile // CHUNK
    n_t = seq // tile
    state_rows = GLA_HEADS * GLA_DK
    prep_specs = [p.specs(n_t) for p in preps]
    assert all(p.n_blocks <= batch * n_t for p in preps)
    return pl.pallas_call(
        functools.partial(_mixer_kernel, False, n_chunks, layer, tuple(p.body for p in preps)),
        grid=(batch, n_t),
        in_specs=[
            pl.BlockSpec((None, tile, D_MODEL), lambda b, t: (b, t, 0)),
            pl.BlockSpec((None, None, 6, D_MODEL), lambda b, t: (layer, mod_row0 + b, 0, 0)),
        ] + _mixer_weight_specs(layer) + [s[0] for s in prep_specs],
        out_specs=[
            pl.BlockSpec((None, tile, D_MODEL), lambda b, t: (b, t, 0)),
            pl.BlockSpec((None, state_rows, GLA_DV), lambda b, t: (b, 0, 0)),
            pl.BlockSpec((None, WINDOW, SWA_KV), lambda b, t: (b, 0, 0)),
            pl.BlockSpec((None, WINDOW, SWA_KV), lambda b, t: (b, 0, 0)),
        ] + [s[1] for s in prep_specs],
        out_shape=[
            jax.ShapeDtypeStruct((batch, seq, D_MODEL), F32),
            jax.ShapeDtypeStruct((batch, state_rows, GLA_DV), F32),
            jax.ShapeDtypeStruct((batch, WINDOW, SWA_KV), F32),
            jax.ShapeDtypeStruct((batch, WINDOW, SWA_KV), F32),
        ] + [s[2] for s in prep_specs],
        scratch_shapes=_mixer_scratch(tile, n_chunks, WINDOW + tile) + [
            pltpu.VMEM((state_rows, GLA_DV), F32),
        ],
        compiler_params=pltpu.CompilerParams(
            dimension_semantics=("arbitrary", "arbitrary"), vmem_limit_bytes=VMEM_LIMIT),
        name="mixer_prompt",
    )(x, mod, *weights, *[p.weight for p in preps])


def _mixer_sample_call(layer, x, mod, weights, s0, k_cache, v_cache):
    batch, seq, _ = x.shape
    assert seq == CHUNK
    rows = batch * seq
    state_rows = GLA_HEADS * GLA_DK

    def full(shape):
        zeros = (0,) * len(shape)
        return pl.BlockSpec(shape, lambda i: zeros)

    def of_layer(shape):
        index = (layer,) + (0,) * len(shape)
        return pl.BlockSpec((None,) + shape, lambda i: index)

    return pl.pallas_call(
        functools.partial(_mixer_kernel, True, batch, layer, ()),
        grid=(1,),
        in_specs=[full((rows, D_MODEL)), of_layer((batch, 6, D_MODEL))] + _mixer_weight_specs(layer) + [
            of_layer((batch, state_rows, GLA_DV)),
            of_layer((batch, WINDOW, SWA_KV)),
            of_layer((batch, WINDOW, SWA_KV)),
        ],
        out_specs=[
            full((rows, D_MODEL)),
            full((batch, state_rows, GLA_DV)),
            full((rows, SWA_KV)),
            full((rows, SWA_KV)),
        ],
        out_shape=[
            jax.ShapeDtypeStruct((rows, D_MODEL), F32),
            jax.ShapeDtypeStruct((batch, state_rows, GLA_DV), F32),
            jax.ShapeDtypeStruct((rows, SWA_KV), F32),
            jax.ShapeDtypeStruct((rows, SWA_KV), F32),
        ],
        scratch_shapes=_mixer_scratch(rows, batch, batch * BAND),
        compiler_params=pltpu.CompilerParams(
            dimension_semantics=("arbitrary",), vmem_limit_bytes=VMEM_LIMIT),
        name="mixer_sample",
    )(x.reshape(rows, D_MODEL), mod, *weights, s0, k_cache, v_cache)


def _ffn_kernel(sample, final, n_seg, seg_len, preps, *refs):
    if sample:
        (x_ref, mod_ref, g_ffn_ref, w_up_ref, conv_w_ref, conv_b_ref, w_down_ref, g_final_ref,
         past_ref, xo_ref, co_ref, h_scr, ub_scr, act_scr) = refs
    else:
        n_in, n_out, n_prep = 8, 2, len(preps)
        prep_src = refs[n_in:n_in + n_prep]
        prep_dst = refs[n_in + n_prep + n_out:n_in + n_prep + n_out + n_prep]
        refs = refs[:n_in] + refs[n_in + n_prep:n_in + n_prep + n_out] + refs[n_in + 2 * n_prep + n_out:]
        (x_ref, mod_ref, g_ffn_ref, w_up_ref, conv_w_ref, conv_b_ref, w_down_ref, g_final_ref,
         xo_ref, co_ref, h_scr, ub_scr, act_scr, past_scr) = refs
        for body, src, dst in zip(preps, prep_src, prep_dst):
            body(src, dst)
        t = pl.program_id(1)
        n_t = pl.num_programs(1)

        @pl.when(t == 0)
        def _():
            past_scr[...] = jnp.zeros_like(past_scr)

    stride = seg_len + SUBLANE
    g_ffn = g_ffn_ref[...]

    def mod_row(c, idx):
        if sample:
            return mod_ref[c, idx:idx + 1, :]
        return mod_ref[idx:idx + 1, :]

    for c in range(n_seg):
        rs = slice(c * seg_len, (c + 1) * seg_len)
        x = x_ref[rs, :]
        gain = g_ffn * (1.0 + mod_row(c, 4))
        h_scr[rs, :] = (x * _rms_scale(x) * gain + mod_row(c, 3)).astype(BF16)

    def up(j):
        lo, hi = FF_PARTS[j]
        h = h_scr[...]
        return _dot(h, w_up_ref[:, lo:hi]), _dot(h, w_up_ref[:, D_FF + lo:D_FF + hi])

    def activate(j, u, val):
        lo, hi = FF_PARTS[j]
        width = hi - lo
        w0 = conv_w_ref[0:1, lo:hi]
        w1 = conv_w_ref[1:2, lo:hi]
        w2 = conv_w_ref[2:3, lo:hi]
        cb = conv_b_ref[:, lo:hi]
        for c in range(n_seg):
            base = c * stride
            rs = slice(c * seg_len, (c + 1) * seg_len)
            if sample:
                ub_scr[base + SUBLANE - 2:base + SUBLANE, 0:width] = past_ref[c, :, lo:hi]
            else:
                ub_scr[base + SUBLANE - 2:base + SUBLANE, 0:width] = past_scr[:, lo:hi]
            u_seg = u[rs, :]
            ub_scr[base + SUBLANE:base + SUBLANE + seg_len, 0:width] = u_seg
            u1 = ub_scr[base + SUBLANE - 1:base + SUBLANE - 1 + seg_len, 0:width]
            u2 = ub_scr[base + SUBLANE - 2:base + SUBLANE - 2 + seg_len, 0:width]
            uc = w0 * u2 + w1 * u1 + w2 * u_seg + cb
            act_scr[rs, lo:hi] = (uc * _sigmoid(uc) * val[rs, :]).astype(BF16)
            tail = ub_scr[base + seg_len + SUBLANE - 2:base + seg_len + SUBLANE, 0:width]
            if sample:
                co_ref[c, :, lo:hi] = tail
            else:
                past_scr[:, lo:hi] = tail

    def down(j):
        lo, hi = FF_PARTS[j]
        return _dot(act_scr[:, lo:hi], w_down_ref[lo:hi, :])

    n_parts = len(FF_PARTS)
    pending = {0: up(0)}
    acc = None
    for j in range(n_parts):
        if j + 1 < n_parts:
            pending[j + 1] = up(j + 1)
        activate(j, *pending.pop(j))
        if j > 0:
            d = down(j - 1)
            acc = d if acc is None else acc + d
    acc = acc + down(n_parts - 1)

    for c in range(n_seg):
        rs = slice(c * seg_len, (c + 1) * seg_len)
        y = x_ref[rs, :] + mod_row(c, 5) * acc[rs, :]
        if final:
            y = y * _rms_scale(y) * g_final_ref[...]
        xo_ref[rs, :] = y

    if not sample:
        @pl.when(t == n_t - 1)
        def _():
            co_ref[...] = past_scr[...]


def _ffn_weight_specs(layer):
    return [
        _layer_spec(layer, (1, D_MODEL)),
        _layer_spec(0, (D_MODEL, 2 * D_FF)),
        _layer_spec(layer, (CONV_W, D_FF)),
        _layer_spec(layer, (1, D_FF)),
        _layer_spec(0, (D_FF, D_MODEL)),
        _const_spec((1, D_MODEL)),
    ]


def _ffn_prompt_call(layer, x, mod, mod_row0, weights, final, preps):
    batch, seq, _ = x.shape
    tile = min(PROMPT_TILE, seq)
    n_t = seq // tile
    prep_specs = [p.specs(n_t) for p in preps]
    assert all(p.n_blocks <= batch * n_t for p in preps)
    return pl.pallas_call(
        functools.partial(_ffn_kernel, False, final, 1, tile, tuple(p.body for p in preps)),
        grid=(batch, n_t),
        in_specs=[
            pl.BlockSpec((None, tile, D_MODEL), lambda b, t: (b, t, 0)),
            pl.BlockSpec((None, None, 6, D_MODEL), lambda b, t: (layer, mod_row0 + b, 0, 0)),
        ] + _ffn_weight_specs(layer) + [s[0] for s in prep_specs],
        out_specs=[
            pl.BlockSpec((None, tile, D_MODEL), lambda b, t: (b, t, 0)),
            pl.BlockSpec((None, CONV_W - 1, D_FF), lambda b, t: (b, 0, 0)),
        ] + [s[1] for s in prep_specs],
        out_shape=[
            jax.ShapeDtypeStruct((batch, seq, D_MODEL), F32),
            jax.ShapeDtypeStruct((batch, CONV_W - 1, D_FF), F32),
        ] + [s[2] for s in prep_specs],
        scratch_shapes=[
            pltpu.VMEM((tile, D_MODEL), BF16),
            pltpu.VMEM((tile + SUBLANE, FF_PART), F32),
            pltpu.VMEM((tile, D_FF), BF16),
            pltpu.VMEM((CONV_W - 1, D_FF), F32),
        ],
        compiler_params=pltpu.CompilerParams(
            dimension_semantics=("arbitrary", "arbitrary"), vmem_limit_bytes=VMEM_LIMIT),
        name="ffn_prompt",
    )(x, mod, *weights, *[p.weight for p in preps])


def _ffn_sample_call(layer, x2d, mod, weights, past, final, batch, seq):
    rows = batch * seq

    def full(shape):
        zeros = (0,) * len(shape)
        return pl.BlockSpec(shape, lambda i: zeros)

    def of_layer(shape):
        index = (layer,) + (0,) * len(shape)
        return pl.BlockSpec((None,) + shape, lambda i: index)

    return pl.pallas_call(
        functools.partial(_ffn_kernel, True, final, batch, seq, ()),
        grid=(1,),
        in_specs=[full((rows, D_MODEL)), of_layer((batch, 6, D_MODEL))] + _ffn_weight_specs(layer) + [
            of_layer((batch, CONV_W - 1, D_FF)),
        ],
        out_specs=[full((rows, D_MODEL)), full((batch, CONV_W - 1, D_FF))],
        out_shape=[
            jax.ShapeDtypeStruct((rows, D_MODEL), F32),
            jax.ShapeDtypeStruct((batch, CONV_W - 1, D_FF), F32),
        ],
        scratch_shapes=[
            pltpu.VMEM((rows, D_MODEL), BF16),
            pltpu.VMEM((batch * (seq + SUBLANE), FF_PART), F32),
            pltpu.VMEM((rows, D_FF), BF16),
        ],
        compiler_params=pltpu.CompilerParams(
            dimension_semantics=("arbitrary",), vmem_limit_bytes=VMEM_LIMIT),
        name="ffn_sample",
    )(x2d, mod, *weights, past)


def _pad_w_in_first(w):
    n_blocks = D_MODEL // W_PREP_ROWS
    return pl.pallas_call(
        _pad_cast_w_in,
        grid=(n_blocks,),
        in_specs=[pl.BlockSpec((None, W_PREP_ROWS, D_IN), lambda r: (0, r, 0))],
        out_specs=pl.BlockSpec((None, W_PREP_ROWS, D_IN_P), lambda r: (0, r, 0)),
        out_shape=jax.ShapeDtypeStruct((1, D_MODEL, D_IN_P), BF16),
        compiler_params=pltpu.CompilerParams(
            dimension_semantics=("arbitrary",), vmem_limit_bytes=VMEM_LIMIT),
        name="pad_w_in",
    )(w)


def kernel(x_prompt, x_sample, state_gla, cache_swa_k, cache_swa_v, state_conv, c_prompt, c_sample,
           w_ada, b_ada, g_attn, g_ffn, w_in, w_gk2, b_gk, g_gla, sinks, w_out, w_up, conv_w, conv_b,
           w_down, g_final):
    depth = w_ada.shape[0]
    batch, seq, _ = x_prompt.shape
    dec_batch, dec_seq, _ = x_sample.shape
    state_rows = GLA_HEADS * GLA_DK

    c_all = jnp.concatenate(
        [c_sample, c_prompt, jnp.zeros((ADA_ROWS - batch - dec_batch, D_MODEL), F32)], axis=0)
    mod_all = _ada_call(c_all, w_ada, b_ada).reshape(depth, ADA_ROWS, 6, D_MODEL)

    w_gk2_p = jnp.concatenate(
        [w_gk2, jnp.zeros((depth, LR_PAD - GLA_RANK, GLA_QK), F32)], axis=1).astype(BF16)
    w_in16 = w_in.astype(BF16)
    w_in_b, w_out_b = _pad_w_in_first(w_in16), w_out[:1].astype(BF16)
    mixer_steps = batch * (seq // min(MIXER_TILE, seq))
    ffn_steps = batch * (seq // min(PROMPT_TILE, seq))
    prep_ahead = (D_MODEL // MIXER_PREP_ROWS_UP <= mixer_steps
                  and D_FF // MIXER_PREP_ROWS_DOWN <= mixer_steps
                  and D_MODEL // FFN_PREP_ROWS <= ffn_steps)
    s0_all = state_gla.reshape(depth, dec_batch, state_rows, GLA_DV)
    kc_all = cache_swa_k.reshape(depth, dec_batch, WINDOW, SWA_KV)
    vc_all = cache_swa_v.reshape(depth, dec_batch, WINDOW, SWA_KV)

    yp = x_prompt
    ys = x_sample.reshape(dec_batch * dec_seq, D_MODEL)
    outs = [[] for _ in range(8)]
    for i in range(depth):
        final = i == depth - 1
        mixer_w = (g_attn[:, None], w_in_b, w_gk2_p, b_gk[:, None], g_gla[:, None], sinks, w_out_b)
        mixer_preps, ffn_preps = [], []
        if prep_ahead:
            mixer_preps = [_WeightPrep(w_up, i, MIXER_PREP_ROWS_UP, 2 * D_FF, _cast_weight),
                           _WeightPrep(w_down, i, MIXER_PREP_ROWS_DOWN, D_MODEL, _cast_weight)]
            if not final:
                ffn_preps = [_WeightPrep(w_in16, i + 1, FFN_PREP_ROWS, D_IN_P, _pad_cast_w_in),
                             _WeightPrep(w_out, i + 1, FFN_PREP_ROWS, D_MODEL, _cast_weight)]

        yp, s_p, k_p, v_p, *prepared = _mixer_prompt_call(i, yp, mod_all, dec_batch, mixer_w, mixer_preps)
        w_up_b, w_down_b = prepared or (w_up[i:i + 1].astype(BF16), w_down[i:i + 1].astype(BF16))
        ffn_w = (g_ffn[:, None], w_up_b, conv_w, conv_b[:, None], w_down_b, g_final[None])
        yp, conv_p, *prepared = _ffn_prompt_call(i, yp, mod_all, dec_batch, ffn_w, final, ffn_preps)

        ys, s_s, k_s, v_s = _mixer_sample_call(
            i, ys.reshape(dec_batch, dec_seq, D_MODEL), mod_all, mixer_w, s0_all, kc_all, vc_all)
        ys, conv_s = _ffn_sample_call(i, ys, mod_all, ffn_w, state_conv, final, dec_batch, dec_seq)
        if not final:
            w_in_b, w_out_b = prepared or (
                _pad_w_in_first(w_in16[i + 1:i + 2]), w_out[i + 1:i + 2].astype(BF16))

        keep = min(WINDOW, seq)
        outs[0].append(s_p.reshape(batch, GLA_HEADS, GLA_DK, GLA_DV))
        outs[1].append(k_p.reshape(batch, keep, SWA_KV_HEADS, SWA_HD))
        outs[2].append(v_p.reshape(batch, keep, SWA_KV_HEADS, SWA_HD))
        outs[3].append(conv_p)
        outs[4].append(s_s.reshape(dec_batch, GLA_HEADS, GLA_DK, GLA_DV))
        outs[5].append(k_s.reshape(dec_batch, dec_seq, SWA_KV_HEADS, SWA_HD))
        outs[6].append(v_s.reshape(dec_batch, dec_seq, SWA_KV_HEADS, SWA_HD))
        outs[7].append(conv_s)

    return (yp, ys.reshape(dec_batch, dec_seq, D_MODEL)) + tuple(jnp.stack(o) for o in outs)
```

```python
import functools

import jax
import jax.numpy as jnp
from jax import lax
from jax.experimental import pallas as pl
from jax.experimental.pallas import tpu as pltpu

F32 = jnp.float32
BF16 = jnp.bfloat16

D_MODEL = 1024
CHUNK = 64
GLA_HEADS = 4
GLA_DK = 64
GLA_DV = 128
GLA_RANK = 16
GLA_NORMALIZER = 16.0
SWA_Q_HEADS = 8
SWA_KV_HEADS = 2
SWA_GROUP = SWA_Q_HEADS // SWA_KV_HEADS
SWA_HD = 64
WINDOW = 128
D_FF = 2816
CONV_W = 3
RMS_EPS = 1e-6

GLA_QK = GLA_HEADS * GLA_DK
GLA_V = GLA_HEADS * GLA_DV
SWA_Q = SWA_Q_HEADS * SWA_HD
SWA_KV = SWA_KV_HEADS * SWA_HD
D_MIX = GLA_V + SWA_Q
BAND = WINDOW + CHUNK

LANE = 128
SUBLANE = 8

C_QG = 0
C_KG = C_QG + GLA_QK
C_VG = C_KG + GLA_QK
C_OG = C_VG + GLA_V
C_LR = C_OG + GLA_V
LR_PAD = LANE
C_QS = C_LR + LR_PAD
C_KS = C_QS + SWA_Q
C_VS = C_KS + SWA_KV
D_IN_P = C_VS + SWA_KV
D_IN = 2 * GLA_QK + 2 * GLA_V + GLA_RANK + SWA_Q + 2 * SWA_KV

assert GLA_DK * 2 == LANE and SWA_HD * 2 == LANE and GLA_DV == LANE and SWA_KV == LANE
GLA_PAIRS = GLA_HEADS // 2
PAIR_V = 2 * GLA_DV
SWA_GRP_Q = SWA_GROUP * SWA_HD

ADA_TILE = 1536
ADA_ROWS = 16
PROMPT_TILE = 512
MIXER_TILE = 1024
SUB_CHUNKS = 4
DENSE_PIECE = 256
DENSE_LEAD = 2
W_PREP_ROWS = 256
MIXER_PREP_ROWS_UP = 32
MIXER_PREP_ROWS_DOWN = 128
FFN_PREP_ROWS = 16
FF_PART = 768
FF_PARTS = tuple((lo, min(lo + FF_PART, D_FF)) for lo in range(0, D_FF, FF_PART))
VMEM_LIMIT = 56 * 1024 * 1024

LOG2E = 1.4426950408889634
NT_DIMS = (((1,), (1,)), ((), ()))
TN_DIMS = (((0,), (0,)), ((), ()))


def _dot(a, b):
    return jnp.dot(a, b, preferred_element_type=F32)


def _dot_nt(a, b):
    return lax.dot_general(a, b, NT_DIMS, preferred_element_type=F32)


def _dot_tn(a, b):
    return lax.dot_general(a, b, TN_DIMS, preferred_element_type=F32)


def _split_bf16(a):
    hi = a.astype(BF16)
    lo = (a - hi.astype(F32)).astype(BF16)
    return hi, lo


def _silu(a):
    half = 0.5 * a
    return half + half * jnp.tanh(half)


def _log_sigmoid(a):
    return jnp.minimum(a, 0.0) - jnp.log(1.0 + jnp.exp2(jnp.abs(a) * (-LOG2E)))


def _rms_scale(a):
    return lax.rsqrt(jnp.mean(a * a, axis=-1, keepdims=True) + RMS_EPS)


def _ada_kernel(c_ref, w_ref, b_ref, o_ref):
    c = c_ref[...]
    a = _silu(c).astype(BF16)
    o_ref[...] = _dot(a, w_ref[...].astype(BF16)) + b_ref[...]


def _ada_call(c_all, w_ada, b_ada):
    depth = w_ada.shape[0]
    n_tiles = (6 * D_MODEL) // ADA_TILE
    return pl.pallas_call(
        _ada_kernel,
        grid=(depth, n_tiles),
        in_specs=[
            pl.BlockSpec((ADA_ROWS, D_MODEL), lambda l, j: (0, 0)),
            pl.BlockSpec((None, D_MODEL, ADA_TILE), lambda l, j: (l, 0, j)),
            pl.BlockSpec((None, 1, ADA_TILE), lambda l, j: (l, 0, j)),
        ],
        out_specs=pl.BlockSpec((None, ADA_ROWS, ADA_TILE), lambda l, j: (l, 0, j)),
        out_shape=jax.ShapeDtypeStruct((depth, ADA_ROWS, 6 * D_MODEL), F32),
        compiler_params=pltpu.CompilerParams(
            dimension_semantics=("arbitrary", "arbitrary"), vmem_limit_bytes=VMEM_LIMIT),
        name="adaln_mod",
    )(c_all, w_ada, b_ada.reshape(depth, 1, 6 * D_MODEL))


def _dup_halves(a, low_half):
    swapped = pltpu.roll(a, LANE // 2, axis=1)
    return jnp.where(low_half, a, swapped), jnp.where(low_half, swapped, a)


def _mixer_kernel(sample, n_chunks, layer, preps, *refs):
    if sample:
        (x_ref, mod_ref, g_attn_ref, w_in_ref, w_gk2_ref, b_gk_ref, g_gla_ref, sinks_ref, w_out_ref,
         s0_ref, kc_ref, vc_ref,
         xo_ref, so_ref, ko_ref, vo_ref,
         h_scr, proj_scr, omix_scr, qm_scr, km_scr, qi_scr, ko_scr, vb_scr, gk_scr, dec_scr, u_scr, sb_scr,
         att_scr, pn_scr,
         kd_scr, vd_scr) = refs
        t = None
    else:
        n_in, n_out, n_prep = 9, 4, len(preps)
        prep_src = refs[n_in:n_in + n_prep]
        prep_dst = refs[n_in + n_prep + n_out:n_in + n_prep + n_out + n_prep]
        refs = refs[:n_in] + refs[n_in + n_prep:n_in + n_prep + n_out] + refs[n_in + 2 * n_prep + n_out:]
        (x_ref, mod_ref, g_attn_ref, w_in_ref, w_gk2_ref, b_gk_ref, g_gla_ref, sinks_ref, w_out_ref,
         xo_ref, so_ref, ko_ref, vo_ref,
         h_scr, proj_scr, omix_scr, qm_scr, km_scr, qi_scr, ko_scr, vb_scr, gk_scr, dec_scr, u_scr, sb_scr,
         att_scr, pn_scr,
         kd_scr, vd_scr, s_scr) = refs
        for body, src, dst in zip(preps, prep_src, prep_dst):
            body(src, dst)
        t = pl.program_id(1)
        n_t = pl.num_programs(1)

        @pl.when(t == 0)
        def _():
            s_scr[...] = jnp.zeros_like(s_scr)
            kd_scr[0:WINDOW, :] = jnp.zeros((WINDOW, 2 * LANE), BF16)
            vd_scr[0:WINDOW, :] = jnp.zeros((WINDOW, 2 * LANE), BF16)

    rows = n_chunks * CHUNK
    g_attn = g_attn_ref[...]

    def mod_row(c, idx):
        if sample:
            return mod_ref[c, idx:idx + 1, :]
        return mod_ref[idx:idx + 1, :]

    def chunk_rows(c):
        return slice(c * CHUNK, (c + 1) * CHUNK)

    sub_chunks = min(SUB_CHUNKS, n_chunks)
    n_sub = n_chunks // sub_chunks
    sub_rows = sub_chunks * CHUNK

    def sub_slice(s):
        return slice(s * sub_rows, (s + 1) * sub_rows)

    def norm_groups(s):
        if sample:
            return [(c, chunk_rows(c)) for c in range(s * sub_chunks, (s + 1) * sub_chunks)]
        return [(0, sub_slice(s))]

    r64 = lax.broadcasted_iota(jnp.int32, (CHUNK, CHUNK), 0)
    c64 = lax.broadcasted_iota(jnp.int32, (CHUNK, CHUNK), 1)
    tri = jnp.where(r64 >= c64, 1.0, 0.0).astype(BF16)
    tri2 = jnp.concatenate([tri, tri], axis=1)
    l_pair = lax.broadcasted_iota(jnp.int32, (CHUNK, LANE), 1)
    low_half = l_pair < LANE // 2
    causal_heads = (lax.broadcasted_iota(jnp.int32, (CHUNK, GLA_QK), 0)
                    >= (lax.broadcasted_iota(jnp.int32, (CHUNK, GLA_QK), 1) & (CHUNK - 1)))
    diag_heads = ((lax.broadcasted_iota(jnp.int32, (GLA_QK, GLA_QK), 0) // GLA_DK)
                  == (lax.broadcasted_iota(jnp.int32, (GLA_QK, GLA_QK), 1) // GLA_DK))
    diag_v = ((lax.broadcasted_iota(jnp.int32, (LANE, PAIR_V), 0) >= LANE // 2)
              == (lax.broadcasted_iota(jnp.int32, (LANE, PAIR_V), 1) >= GLA_DV))
    g_gla = g_gla_ref[...]
    low_sub = lax.broadcasted_iota(jnp.int32, (sub_rows, LANE), 1) < LANE // 2

    def project_steps(s):
        ss = sub_slice(s)

        def norm():
            for mi, rs in norm_groups(s):
                x = x_ref[rs, :]
                gain = g_attn * (1.0 + mod_row(mi, 1))
                h_scr[rs, :] = (x * _rms_scale(x) * gain + mod_row(mi, 0)).astype(BF16)

        def piece(lo, hi):
            proj_scr[ss, lo:hi] = _dot(h_scr[ss, :], w_in_ref[:, lo:hi])

        def gates():
            lr = proj_scr[ss, C_LR:C_LR + LR_PAD].astype(BF16)
            gk_scr[ss, :] = (_log_sigmoid(_dot(lr, w_gk2_ref[...]) + b_gk_ref[...])
                             * (LOG2E / GLA_NORMALIZER))
            vb_scr[ss, :] = proj_scr[ss, C_VG:C_VG + GLA_V].astype(BF16)

        def bands():
            k_dup = _dup_halves(proj_scr[ss, C_KS:C_KS + SWA_KV], low_sub)
            v_dup = _dup_halves(proj_scr[ss, C_VS:C_VS + SWA_KV], low_sub)
            for kv in range(SWA_KV_HEADS):
                ls = slice(kv * LANE, (kv + 1) * LANE)
                if sample:
                    for c in range(s * sub_chunks, (s + 1) * sub_chunks):
                        local = slice((c - s * sub_chunks) * CHUNK, (c - s * sub_chunks + 1) * CHUNK)
                        kd_scr[c * BAND + WINDOW:(c + 1) * BAND, ls] = k_dup[kv][local, :].astype(BF16)
                        vd_scr[c * BAND + WINDOW:(c + 1) * BAND, ls] = v_dup[kv][local, :].astype(BF16)
                else:
                    band_rows = slice(WINDOW + s * sub_rows, WINDOW + (s + 1) * sub_rows)
                    kd_scr[band_rows, ls] = k_dup[kv].astype(BF16)
                    vd_scr[band_rows, ls] = v_dup[kv].astype(BF16)

        steps = [norm]
        for lo in range(0, D_IN_P, DENSE_PIECE):
            steps.append(functools.partial(piece, lo, min(lo + DENSE_PIECE, D_IN_P)))
        return steps + [gates, bands]

    def gla_prepare(c):
        rs = chunk_rows(c)
        gk_hi, gk_lo = _split_bf16(gk_scr[rs, :])
        cum = _dot(tri2, jnp.concatenate([gk_hi, gk_lo], axis=0))
        mid = cum[CHUNK // 2:CHUNK // 2 + 1, :]
        last = cum[CHUNK - 1:CHUNK, :]
        q = proj_scr[rs, C_QG:C_QG + GLA_QK] * (GLA_DK ** -0.5)
        k = proj_scr[rs, C_KG:C_KG + GLA_QK]
        qm_scr[rs, :] = (q * jnp.exp2(cum - mid)).astype(BF16)
        km_scr[rs, :] = (k * jnp.exp2(mid - cum)).astype(BF16)
        qi_scr[rs, :] = (q * jnp.exp2(cum)).astype(BF16)
        ko_scr[rs, :] = (k * jnp.exp2(last - cum)).astype(BF16)
        dec_scr[c] = jnp.broadcast_to(jnp.exp2(last), (LANE, GLA_QK)).T

    def gla_increment(c):
        rs = chunk_rows(c)
        for p in range(GLA_PAIRS):
            upd = _dot_tn(ko_scr[rs, p * LANE:(p + 1) * LANE], vb_scr[rs, p * PAIR_V:(p + 1) * PAIR_V])
            u_scr[c, p * LANE:p * LANE + GLA_DK, :] = upd[0:GLA_DK, 0:GLA_DV]
            u_scr[c, p * LANE + GLA_DK:(p + 1) * LANE, :] = upd[GLA_DK:LANE, GLA_DV:PAIR_V]

    def gla_recurrence(s):
        state = None if sample else s_scr[...]
        for c in range(s * sub_chunks, (s + 1) * sub_chunks):
            if sample:
                state = s0_ref[c]
            sb_scr[c] = state.astype(BF16)
            state = dec_scr[c] * state + u_scr[c]
            if sample:
                so_ref[c] = state
        if not sample:
            s_scr[...] = state

    def gla_scores(c):
        rs = chunk_rows(c)
        km = km_scr[rs, :]
        k_bd = jnp.where(diag_heads, jnp.concatenate([km] * GLA_HEADS, axis=0), 0.0)
        att_scr[rs, :] = jnp.where(causal_heads, _dot_nt(qm_scr[rs, :], k_bd), 0.0).astype(BF16)

    def gla_output(c, p):
        rs = chunk_rows(c)
        ls = slice(p * LANE, (p + 1) * LANE)
        v_pair = vb_scr[rs, p * PAIR_V:(p + 1) * PAIR_V]
        s_pair = sb_scr[c, ls, :]
        w_top = jnp.where(diag_v, jnp.concatenate([v_pair, v_pair], axis=0), 0.0)
        w_bot = jnp.where(diag_v, jnp.concatenate([s_pair, s_pair], axis=1), 0.0)
        o_pair = _dot(jnp.concatenate([att_scr[rs, ls], qi_scr[rs, ls]], axis=1),
                      jnp.concatenate([w_top, w_bot], axis=0))
        for hh in range(2):
            h = 2 * p + hh
            o = o_pair[:, hh * GLA_DV:(hh + 1) * GLA_DV]
            og = proj_scr[rs, C_OG + h * GLA_DV:C_OG + (h + 1) * GLA_DV]
            y = o * _rms_scale(o) * g_gla * _silu(og)
            omix_scr[rs, h * GLA_DV:(h + 1) * GLA_DV] = y.astype(BF16)

    if sample:
        low_win = lax.broadcasted_iota(jnp.int32, (WINDOW, LANE), 1) < LANE // 2
        for c in range(n_chunks):
            kc_dup = _dup_halves(kc_ref[c], low_win)
            vc_dup = _dup_halves(vc_ref[c], low_win)
            for kv in range(SWA_KV_HEADS):
                ls = slice(kv * LANE, (kv + 1) * LANE)
                kd_scr[c * BAND:c * BAND + WINDOW, ls] = kc_dup[kv].astype(BF16)
                vd_scr[c * BAND:c * BAND + WINDOW, ls] = vc_dup[kv].astype(BF16)

    lane_q = lax.broadcasted_iota(jnp.int32, (1, SWA_GRP_Q), 1)
    key_ids = lax.broadcasted_iota(jnp.int32, (BAND, SWA_GRP_Q), 0)
    sink_vecs = []
    for kv in range(SWA_KV_HEADS):
        vec = jnp.full((1, SWA_GRP_Q), sinks_ref[layer, kv * SWA_GROUP + SWA_GROUP - 1], F32)
        for g in range(SWA_GROUP - 2, -1, -1):
            vec = jnp.where(lane_q < (g + 1) * SWA_HD, sinks_ref[layer, kv * SWA_GROUP + g], vec)
        sink_vecs.append(vec * LOG2E)

    def band_rows(c):
        return slice(c * BAND, (c + 1) * BAND) if sample else slice(c * CHUNK, c * CHUNK + BAND)

    def swa_scores(c, kv):
        rs = chunk_rows(c)
        band = band_rows(c)
        ls = slice(kv * LANE, (kv + 1) * LANE)
        qg = (proj_scr[rs, C_QS + kv * SWA_GRP_Q:C_QS + (kv + 1) * SWA_GRP_Q]
              * (SWA_HD ** -0.5 * LOG2E)).astype(BF16)
        q_stack = jnp.concatenate(
            [jnp.where(low_half if hh == 0 else ~low_half, qg[:, pp * LANE:(pp + 1) * LANE], 0.0)
             for pp in range(SWA_GROUP // 2) for hh in range(2)], axis=0)
        s_t = _dot_nt(kd_scr[band, ls], q_stack)
        if not sample and c * CHUNK < WINDOW:
            first_valid = WINDOW - (t * rows + c * CHUNK)
            s_t = jnp.where(key_ids >= first_valid, s_t, -jnp.inf)
        sink = sink_vecs[kv]
        m = jnp.maximum(jnp.max(s_t, axis=0, keepdims=True), sink)
        p_t = jnp.exp2(s_t - m)
        den = jnp.sum(p_t, axis=0, keepdims=True) + jnp.exp2(sink - m)
        pn_scr[c * SWA_KV_HEADS + kv] = (p_t * (1.0 / den)).astype(BF16)

    def swa_output(c, kv):
        rs = chunk_rows(c)
        ls = slice(kv * LANE, (kv + 1) * LANE)
        o_t = _dot_tn(vd_scr[band_rows(c), ls], pn_scr[c * SWA_KV_HEADS + kv]).T
        for pp in range(SWA_GROUP // 2):
            o_pair = jnp.where(low_half, o_t[(2 * pp) * CHUNK:(2 * pp + 1) * CHUNK, :],
                               o_t[(2 * pp + 1) * CHUNK:(2 * pp + 2) * CHUNK, :])
            col = GLA_V + kv * SWA_GRP_Q + pp * LANE
            omix_scr[rs, col:col + LANE] = o_pair.astype(BF16)

    def out_steps(s):
        ss = sub_slice(s)
        base = s * sub_rows

        def piece(lo, hi):
            mix = _dot(omix_scr[ss, :], w_out_ref[:, lo:hi])
            for mi, rs in norm_groups(s):
                local = slice(rs.start - base, rs.stop - base)
                xo_ref[rs, lo:hi] = x_ref[rs, lo:hi] + mod_row(mi, 2)[:, lo:hi] * mix[local, :]

        return [functools.partial(piece, lo, lo + DENSE_PIECE) for lo in range(0, D_MODEL, DENSE_PIECE)]

    def block_steps(s):
        blocks = range(s * sub_chunks, (s + 1) * sub_chunks)
        pairs = [(c, p) for c in blocks for p in range(GLA_PAIRS)]
        groups = [(c, kv) for c in blocks for kv in range(SWA_KV_HEADS)]
        steps = [functools.partial(gla_prepare, c) for c in blocks]
        steps += [functools.partial(swa_scores, c, kv) for c, kv in groups]
        steps += [functools.partial(gla_increment, c) for c in blocks]
        steps += [functools.partial(gla_scores, c) for c in blocks]
        steps.append(functools.partial(gla_recurrence, s))
        for (c, p), (c2, kv) in zip(pairs, groups):
            steps.append(functools.partial(gla_output, c, p))
            steps.append(functools.partial(swa_output, c2, kv))
        return steps

    for step in project_steps(0):
        step()
    for s in range(n_sub):
        dense = (project_steps(s + 1) if s + 1 < n_sub else []) + (out_steps(s - 1) if s > 0 else [])
        work = block_steps(s)
        issued = 0
        for i, step in enumerate(work):
            while issued < len(dense) and (issued - DENSE_LEAD) * len(work) < (i + 1) * len(dense):
                dense[issued]()
                issued += 1
            step()
        for step in dense[issued:]:
            step()
    for step in out_steps(n_sub - 1):
        step()

    if sample:
        ko_ref[...] = proj_scr[:, C_KS:C_KS + SWA_KV]
        vo_ref[...] = proj_scr[:, C_VS:C_VS + SWA_KV]
    else:
        kd_scr[0:WINDOW, :] = kd_scr[rows:rows + WINDOW, :]
        vd_scr[0:WINDOW, :] = vd_scr[rows:rows + WINDOW, :]

        @pl.when(t == n_t - 1)
        def _():
            so_ref[...] = s_scr[...]
            ko_ref[...] = proj_scr[rows - WINDOW:rows, C_KS:C_KS + SWA_KV]
            vo_ref[...] = proj_scr[rows - WINDOW:rows, C_VS:C_VS + SWA_KV]


def _pad_cast_w_in(w_ref, o_ref):
    head = C_LR + GLA_RANK
    w = w_ref[...].astype(F32)
    o_ref[:, 0:C_LR] = w[:, 0:C_LR].astype(BF16)
    gate = jnp.concatenate(
        [w[:, C_LR:head], jnp.zeros((w.shape[0], LR_PAD - GLA_RANK), F32)], axis=1)
    o_ref[:, C_LR:C_QS] = gate.astype(BF16)
    o_ref[:, C_QS:D_IN_P] = w[:, head:D_IN].astype(BF16)


def _cast_weight(w_ref, o_ref):
    o_ref[...] = w_ref[...].astype(BF16)


class _WeightPrep:
    def __init__(self, weight, layer, rows_per_step, out_cols, body):
        self.weight, self.layer, self.rows_per_step = weight, layer, rows_per_step
        self.rows, self.in_cols = weight.shape[1], weight.shape[2]
        self.out_cols, self.body = out_cols, body
        assert self.rows % rows_per_step == 0
        self.n_blocks = self.rows // rows_per_step

    def specs(self, n_t):
        def block(b, t):
            return jnp.minimum(b * n_t + t, self.n_blocks - 1)
        layer = self.layer
        return (pl.BlockSpec((None, self.rows_per_step, self.in_cols), lambda b, t: (layer, block(b, t), 0)),
                pl.BlockSpec((None, self.rows_per_step, self.out_cols), lambda b, t: (0, block(b, t), 0)),
                jax.ShapeDtypeStruct((1, self.rows, self.out_cols), BF16))


def _const_spec(shape):
    zeros = (0,) * len(shape)
    return pl.BlockSpec(shape, lambda *_: zeros, pipeline_mode=pl.Buffered(1))


def _layer_spec(layer, shape):
    index = (layer,) + (0,) * len(shape)
    return pl.BlockSpec((None,) + shape, lambda *_: index, pipeline_mode=pl.Buffered(1))


def _mixer_weight_specs(layer):
    return [
        _layer_spec(layer, (1, D_MODEL)),
        _layer_spec(0, (D_MODEL, D_IN_P)),
        _layer_spec(layer, (LR_PAD, GLA_QK)),
        _layer_spec(layer, (1, GLA_QK)),
        _layer_spec(layer, (1, GLA_DV)),
        pl.BlockSpec(memory_space=pltpu.SMEM),
        _layer_spec(0, (D_MIX, D_MODEL)),
    ]


def _mixer_scratch(rows, n_chunks, band_rows):
    state_rows = GLA_HEADS * GLA_DK
    return [
        pltpu.VMEM((rows, D_MODEL), BF16),
        pltpu.VMEM((rows, D_IN_P), F32),
        pltpu.VMEM((rows, D_MIX), BF16),
        pltpu.VMEM((rows, GLA_QK), BF16),
        pltpu.VMEM((rows, GLA_QK), BF16),
        pltpu.VMEM((rows, GLA_QK), BF16),
        pltpu.VMEM((rows, GLA_QK), BF16),
        pltpu.VMEM((rows, GLA_V), BF16),
        pltpu.VMEM((rows, GLA_QK), F32),
        pltpu.VMEM((n_chunks, state_rows, GLA_DV), F32),
        pltpu.VMEM((n_chunks, state_rows, GLA_DV), F32),
        pltpu.VMEM((n_chunks, state_rows, GLA_DV), BF16),
        pltpu.VMEM((rows, GLA_QK), BF16),
        pltpu.VMEM((n_chunks * SWA_KV_HEADS, BAND, SWA_GRP_Q), BF16),
        pltpu.VMEM((band_rows, 2 * LANE), BF16),
        pltpu.VMEM((band_rows, 2 * LANE), BF16),
    ]


def _mixer_prompt_call(layer, x, mod, mod_row0, weights, preps):
    batch, seq, _ = x.shape
    tile = min(MIXER_TILE, seq)
    assert seq % tile == 0 and tile % CHUNK == 0 and tile >= WINDOW
    n_chunks = tile // CHUNK
    n_t = seq // tile
    state_rows = GLA_HEADS * GLA_DK
    prep_specs = [p.specs(n_t) for p in preps]
    assert all(p.n_blocks <= batch * n_t for p in preps)
    return pl.pallas_call(
        functools.partial(_mixer_kernel, False, n_chunks, layer, tuple(p.body for p in preps)),
        grid=(batch, n_t),
        in_specs=[
            pl.BlockSpec((None, tile, D_MODEL), lambda b, t: (b, t, 0)),
            pl.BlockSpec((None, None, 6, D_MODEL), lambda b, t: (layer, mod_row0 + b, 0, 0)),
        ] + _mixer_weight_specs(layer) + [s[0] for s in prep_specs],
        out_specs=[
            pl.BlockSpec((None, tile, D_MODEL), lambda b, t: (b, t, 0)),
            pl.BlockSpec((None, state_rows, GLA_DV), lambda b, t: (b, 0, 0)),
            pl.BlockSpec((None, WINDOW, SWA_KV), lambda b, t: (b, 0, 0)),
            pl.BlockSpec((None, WINDOW, SWA_KV), lambda b, t: (b, 0, 0)),
        ] + [s[1] for s in prep_specs],
        out_shape=[
            jax.ShapeDtypeStruct((batch, seq, D_MODEL), F32),
            jax.ShapeDtypeStruct((batch, state_rows, GLA_DV), F32),
            jax.ShapeDtypeStruct((batch, WINDOW, SWA_KV), F32),
            jax.ShapeDtypeStruct((batch, WINDOW, SWA_KV), F32),
        ] + [s[2] for s in prep_specs],
        scratch_shapes=_mixer_scratch(tile, n_chunks, WINDOW + tile) + [
            pltpu.VMEM((state_rows, GLA_DV), F32),
        ],
        compiler_params=pltpu.CompilerParams(
            dimension_semantics=("arbitrary", "arbitrary"), vmem_limit_bytes=VMEM_LIMIT),
        name="mixer_prompt",
    )(x, mod, *weights, *[p.weight for p in preps])


def _mixer_sample_call(layer, x, mod, weights, s0, k_cache, v_cache):
    batch, seq, _ = x.shape
    assert seq == CHUNK
    rows = batch * seq
    state_rows = GLA_HEADS * GLA_DK

    def full(shape):
        zeros = (0,) * len(shape)
        return pl.BlockSpec(shape, lambda i: zeros)

    def of_layer(shape):
        index = (layer,) + (0,) * len(shape)
        return pl.BlockSpec((None,) + shape, lambda i: index)

    return pl.pallas_call(
        functools.partial(_mixer_kernel, True, batch, layer, ()),
        grid=(1,),
        in_specs=[full((rows, D_MODEL)), of_layer((batch, 6, D_MODEL))] + _mixer_weight_specs(layer) + [
            of_layer((batch, state_rows, GLA_DV)),
            of_layer((batch, WINDOW, SWA_KV)),
            of_layer((batch, WINDOW, SWA_KV)),
        ],
        out_specs=[
            full((rows, D_MODEL)),
            full((batch, state_rows, GLA_DV)),
            full((rows, SWA_KV)),
            full((rows, SWA_KV)),
        ],
        out_shape=[
            jax.ShapeDtypeStruct((rows, D_MODEL), F32),
            jax.ShapeDtypeStruct((batch, state_rows, GLA_DV), F32),
            jax.ShapeDtypeStruct((rows, SWA_KV), F32),
            jax.ShapeDtypeStruct((rows, SWA_KV), F32),
        ],
        scratch_shapes=_mixer_scratch(rows, batch, batch * BAND),
        compiler_params=pltpu.CompilerParams(
            dimension_semantics=("arbitrary",), vmem_limit_bytes=VMEM_LIMIT),
        name="mixer_sample",
    )(x.reshape(rows, D_MODEL), mod, *weights, s0, k_cache, v_cache)


def _ffn_kernel(sample, final, n_seg, seg_len, preps, *refs):
    if sample:
        (x_ref, mod_ref, g_ffn_ref, w_up_ref, conv_w_ref, conv_b_ref, w_down_ref, g_final_ref,
         past_ref, xo_ref, co_ref, h_scr, ub_scr, act_scr) = refs
    else:
        n_in, n_out, n_prep = 8, 2, len(preps)
        prep_src = refs[n_in:n_in + n_prep]
        prep_dst = refs[n_in + n_prep + n_out:n_in + n_prep + n_out + n_prep]
        refs = refs[:n_in] + refs[n_in + n_prep:n_in + n_prep + n_out] + refs[n_in + 2 * n_prep + n_out:]
        (x_ref, mod_ref, g_ffn_ref, w_up_ref, conv_w_ref, conv_b_ref, w_down_ref, g_final_ref,
         xo_ref, co_ref, h_scr, ub_scr, act_scr, past_scr) = refs
        for body, src, dst in zip(preps, prep_src, prep_dst):
            body(src, dst)
        t = pl.program_id(1)
        n_t = pl.num_programs(1)

        @pl.when(t == 0)
        def _():
            past_scr[...] = jnp.zeros_like(past_scr)

    stride = seg_len + SUBLANE
    g_ffn = g_ffn_ref[...]

    def mod_row(c, idx):
        if sample:
            return mod_ref[c, idx:idx + 1, :]
        return mod_ref[idx:idx + 1, :]

    for c in range(n_seg):
        rs = slice(c * seg_len, (c + 1) * seg_len)
        x = x_ref[rs, :]
        gain = g_ffn * (1.0 + mod_row(c, 4))
        h_scr[rs, :] = (x * _rms_scale(x) * gain + mod_row(c, 3)).astype(BF16)

    def up(j):
        lo, hi = FF_PARTS[j]
        h = h_scr[...]
        return _dot(h, w_up_ref[:, lo:hi]), _dot(h, w_up_ref[:, D_FF + lo:D_FF + hi])

    def activate(j, u, val):
        lo, hi = FF_PARTS[j]
        width = hi - lo
        w0 = conv_w_ref[0:1, lo:hi]
        w1 = conv_w_ref[1:2, lo:hi]
        w2 = conv_w_ref[2:3, lo:hi]
        cb = conv_b_ref[:, lo:hi]
        for c in range(n_seg):
            base = c * stride
            rs = slice(c * seg_len, (c + 1) * seg_len)
            if sample:
                ub_scr[base + SUBLANE - 2:base + SUBLANE, 0:width] = past_ref[c, :, lo:hi]
            else:
                ub_scr[base + SUBLANE - 2:base + SUBLANE, 0:width] = past_scr[:, lo:hi]
            u_seg = u[rs, :]
            ub_scr[base + SUBLANE:base + SUBLANE + seg_len, 0:width] = u_seg
            u1 = ub_scr[base + SUBLANE - 1:base + SUBLANE - 1 + seg_len, 0:width]
            u2 = ub_scr[base + SUBLANE - 2:base + SUBLANE - 2 + seg_len, 0:width]
            uc = w0 * u2 + w1 * u1 + w2 * u_seg + cb
            act_scr[rs, lo:hi] = (_silu(uc) * val[rs, :]).astype(BF16)
            tail = ub_scr[base + seg_len + SUBLANE - 2:base + seg_len + SUBLANE, 0:width]
            if sample:
                co_ref[c, :, lo:hi] = tail
            else:
                past_scr[:, lo:hi] = tail

    def down(j):
        lo, hi = FF_PARTS[j]
        return _dot(act_scr[:, lo:hi], w_down_ref[lo:hi, :])

    n_parts = len(FF_PARTS)
    pending = {0: up(0)}
    acc = None
    for j in range(n_parts):
        if j + 1 < n_parts:
            pending[j + 1] = up(j + 1)
        activate(j, *pending.pop(j))
        if j > 0:
            d = down(j - 1)
            acc = d if acc is None else acc + d
    acc = acc + down(n_parts - 1)

    for c in range(n_seg):
        rs = slice(c * seg_len, (c + 1) * seg_len)
        y = x_ref[rs, :] + mod_row(c, 5) * acc[rs, :]
        if final:
            y = y * _rms_scale(y) * g_final_ref[...]
        xo_ref[rs, :] = y

    if not sample:
        @pl.when(t == n_t - 1)
        def _():
            co_ref[...] = past_scr[...]


def _ffn_weight_specs(layer):
    return [
        _layer_spec(layer, (1, D_MODEL)),
        _layer_spec(0, (D_MODEL, 2 * D_FF)),
        _layer_spec(layer, (CONV_W, D_FF)),
        _layer_spec(layer, (1, D_FF)),
        _layer_spec(0, (D_FF, D_MODEL)),
        _const_spec((1, D_MODEL)),
    ]


def _ffn_prompt_call(layer, x, mod, mod_row0, weights, final, preps):
    batch, seq, _ = x.shape
    tile = min(PROMPT_TILE, seq)
    n_t = seq // tile
    prep_specs = [p.specs(n_t) for p in preps]
    assert all(p.n_blocks <= batch * n_t for p in preps)
    return pl.pallas_call(
        functools.partial(_ffn_kernel, False, final, 1, tile, tuple(p.body for p in preps)),
        grid=(batch, n_t),
        in_specs=[
            pl.BlockSpec((None, tile, D_MODEL), lambda b, t: (b, t, 0)),
            pl.BlockSpec((None, None, 6, D_MODEL), lambda b, t: (layer, mod_row0 + b, 0, 0)),
        ] + _ffn_weight_specs(layer) + [s[0] for s in prep_specs],
        out_specs=[
            pl.BlockSpec((None, tile, D_MODEL), lambda b, t: (b, t, 0)),
            pl.BlockSpec((None, CONV_W - 1, D_FF), lambda b, t: (b, 0, 0)),
        ] + [s[1] for s in prep_specs],
        out_shape=[
            jax.ShapeDtypeStruct((batch, seq, D_MODEL), F32),
            jax.ShapeDtypeStruct((batch, CONV_W - 1, D_FF), F32),
        ] + [s[2] for s in prep_specs],
        scratch_shapes=[
            pltpu.VMEM((tile, D_MODEL), BF16),
            pltpu.VMEM((tile + SUBLANE, FF_PART), F32),
            pltpu.VMEM((tile, D_FF), BF16),
            pltpu.VMEM((CONV_W - 1, D_FF), F32),
        ],
        compiler_params=pltpu.CompilerParams(
            dimension_semantics=("arbitrary", "arbitrary"), vmem_limit_bytes=VMEM_LIMIT),
        name="ffn_prompt",
    )(x, mod, *weights, *[p.weight for p in preps])


def _ffn_sample_call(layer, x2d, mod, weights, past, final, batch, seq):
    rows = batch * seq

    def full(shape):
        zeros = (0,) * len(shape)
        return pl.BlockSpec(shape, lambda i: zeros)

    def of_layer(shape):
        index = (layer,) + (0,) * len(shape)
        return pl.BlockSpec((None,) + shape, lambda i: index)

    return pl.pallas_call(
        functools.partial(_ffn_kernel, True, final, batch, seq, ()),
        grid=(1,),
        in_specs=[full((rows, D_MODEL)), of_layer((batch, 6, D_MODEL))] + _ffn_weight_specs(layer) + [
            of_layer((batch, CONV_W - 1, D_FF)),
        ],
        out_specs=[full((rows, D_MODEL)), full((batch, CONV_W - 1, D_FF))],
        out_shape=[
            jax.ShapeDtypeStruct((rows, D_MODEL), F32),
            jax.ShapeDtypeStruct((batch, CONV_W - 1, D_FF), F32),
        ],
        scratch_shapes=[
            pltpu.VMEM((rows, D_MODEL), BF16),
            pltpu.VMEM((batch * (seq + SUBLANE), FF_PART), F32),
            pltpu.VMEM((rows, D_FF), BF16),
        ],
        compiler_params=pltpu.CompilerParams(
            dimension_semantics=("arbitrary",), vmem_limit_bytes=VMEM_LIMIT),
        name="ffn_sample",
    )(x2d, mod, *weights, past)


def _pad_w_in_first(w):
    n_blocks = D_MODEL // W_PREP_ROWS
    return pl.pallas_call(
        _pad_cast_w_in,
        grid=(n_blocks,),
        in_specs=[pl.BlockSpec((None, W_PREP_ROWS, D_IN), lambda r: (0, r, 0))],
        out_specs=pl.BlockSpec((None, W_PREP_ROWS, D_IN_P), lambda r: (0, r, 0)),
        out_shape=jax.ShapeDtypeStruct((1, D_MODEL, D_IN_P), BF16),
        compiler_params=pltpu.CompilerParams(
            dimension_semantics=("arbitrary",), vmem_limit_bytes=VMEM_LIMIT),
        name="pad_w_in",
    )(w)


def kernel(x_prompt, x_sample, state_gla, cache_swa_k, cache_swa_v, state_conv, c_prompt, c_sample,
           w_ada, b_ada, g_attn, g_ffn, w_in, w_gk2, b_gk, g_gla, sinks, w_out, w_up, conv_w, conv_b,
           w_down, g_final):
    depth = w_ada.shape[0]
    batch, seq, _ = x_prompt.shape
    dec_batch, dec_seq, _ = x_sample.shape
    state_rows = GLA_HEADS * GLA_DK

    c_all = jnp.concatenate(
        [c_sample, c_prompt, jnp.zeros((ADA_ROWS - batch - dec_batch, D_MODEL), F32)], axis=0)
    mod_all = _ada_call(c_all, w_ada, b_ada).reshape(depth, ADA_ROWS, 6, D_MODEL)

    w_gk2_p = jnp.concatenate(
        [w_gk2, jnp.zeros((depth, LR_PAD - GLA_RANK, GLA_QK), F32)], axis=1).astype(BF16)
    w_in16 = w_in.astype(BF16)
    w_in_b, w_out_b = _pad_w_in_first(w_in16), w_out[:1].astype(BF16)
    mixer_steps = batch * (seq // min(MIXER_TILE, seq))
    ffn_steps = batch * (seq // min(PROMPT_TILE, seq))
    prep_ahead = (D_MODEL // MIXER_PREP_ROWS_UP <= mixer_steps
                  and D_FF // MIXER_PREP_ROWS_DOWN <= mixer_steps
                  and D_MODEL // FFN_PREP_ROWS <= ffn_steps)
    s0_all = state_gla.reshape(depth, dec_batch, state_rows, GLA_DV)
    kc_all = cache_swa_k.reshape(depth, dec_batch, WINDOW, SWA_KV)
    vc_all = cache_swa_v.reshape(depth, dec_batch, WINDOW, SWA_KV)

    yp = x_prompt
    ys = x_sample.reshape(dec_batch * dec_seq, D_MODEL)
    outs = [[] for _ in range(8)]
    for i in range(depth):
        final = i == depth - 1
        mixer_w = (g_attn[:, None], w_in_b, w_gk2_p, b_gk[:, None], g_gla[:, None], sinks, w_out_b)
        mixer_preps, ffn_preps = [], []
        if prep_ahead:
            mixer_preps = [_WeightPrep(w_up, i, MIXER_PREP_ROWS_UP, 2 * D_FF, _cast_weight),
                           _WeightPrep(w_down, i, MIXER_PREP_ROWS_DOWN, D_MODEL, _cast_weight)]
            if not final:
                ffn_preps = [_WeightPrep(w_in16, i + 1, FFN_PREP_ROWS, D_IN_P, _pad_cast_w_in),
                             _WeightPrep(w_out, i + 1, FFN_PREP_ROWS, D_MODEL, _cast_weight)]

        yp, s_p, k_p, v_p, *prepared = _mixer_prompt_call(i, yp, mod_all, dec_batch, mixer_w, mixer_preps)
        w_up_b, w_down_b = prepared or (w_up[i:i + 1].astype(BF16), w_down[i:i + 1].astype(BF16))
        ffn_w = (g_ffn[:, None], w_up_b, conv_w, conv_b[:, None], w_down_b, g_final[None])
        yp, conv_p, *prepared = _ffn_prompt_call(i, yp, mod_all, dec_batch, ffn_w, final, ffn_preps)

        ys, s_s, k_s, v_s = _mixer_sample_call(
            i, ys.reshape(dec_batch, dec_seq, D_MODEL), mod_all, mixer_w, s0_all, kc_all, vc_all)
        ys, conv_s = _ffn_sample_call(i, ys, mod_all, ffn_w, state_conv, final, dec_batch, dec_seq)
        if not final:
            w_in_b, w_out_b = prepared or (
                _pad_w_in_first(w_in16[i + 1:i + 2]), w_out[i + 1:i + 2].astype(BF16))

        keep = min(WINDOW, seq)
        outs[0].append(s_p.reshape(batch, GLA_HEADS, GLA_DK, GLA_DV))
        outs[1].append(k_p.reshape(batch, keep, SWA_KV_HEADS, SWA_HD))
        outs[2].append(v_p.reshape(batch, keep, SWA_KV_HEADS, SWA_HD))
        outs[3].append(conv_p)
        outs[4].append(s_s.reshape(dec_batch, GLA_HEADS, GLA_DK, GLA_DV))
        outs[5].append(k_s.reshape(dec_batch, dec_seq, SWA_KV_HEADS, SWA_HD))
        outs[6].append(v_s.reshape(dec_batch, dec_seq, SWA_KV_HEADS, SWA_HD))
        outs[7].append(conv_s)

    return (yp, ys.reshape(dec_batch, dec_seq, D_MODEL)) + tuple(jnp.stack(o) for o in outs)
```

```python
import functools

import jax
import jax.numpy as jnp
from jax import lax
from jax.experimental import pallas as pl
from jax.experimental.pallas import tpu as pltpu

F32 = jnp.float32
BF16 = jnp.bfloat16

D_MODEL = 1024
CHUNK = 64
GLA_HEADS = 4
GLA_DK = 64
GLA_DV = 128
GLA_RANK = 16
GLA_NORMALIZER = 16.0
SWA_Q_HEADS = 8
SWA_KV_HEADS = 2
SWA_GROUP = SWA_Q_HEADS // SWA_KV_HEADS
SWA_HD = 64
WINDOW = 128
D_FF = 2816
CONV_W = 3
RMS_EPS = 1e-6

GLA_QK = GLA_HEADS * GLA_DK
GLA_V = GLA_HEADS * GLA_DV
SWA_Q = SWA_Q_HEADS * SWA_HD
SWA_KV = SWA_KV_HEADS * SWA_HD
D_MIX = GLA_V + SWA_Q
BAND = WINDOW + CHUNK

LANE = 128
SUBLANE = 8

C_QG = 0
C_KG = C_QG + GLA_QK
C_VG = C_KG + GLA_QK
C_OG = C_VG + GLA_V
C_LR = C_OG + GLA_V
LR_PAD = LANE
C_QS = C_LR + LR_PAD
C_KS = C_QS + SWA_Q
C_VS = C_KS + SWA_KV
D_IN_P = C_VS + SWA_KV
D_IN = 2 * GLA_QK + 2 * GLA_V + GLA_RANK + SWA_Q + 2 * SWA_KV

assert GLA_DK * 2 == LANE and SWA_HD * 2 == LANE and GLA_DV == LANE and SWA_KV == LANE
GLA_PAIRS = GLA_HEADS // 2
PAIR_V = 2 * GLA_DV
SWA_GRP_Q = SWA_GROUP * SWA_HD

ADA_TILE = 1536
ADA_ROWS = 16
PROMPT_TILE = 512
MIXER_TILE = 1024
SUB_CHUNKS = 4
DENSE_PIECE = 256
DENSE_LEAD = 2
W_PREP_ROWS = 256
MIXER_PREP_ROWS_UP = 32
MIXER_PREP_ROWS_DOWN = 128
FFN_PREP_ROWS = 16
FF_PART = 768
FF_PARTS = tuple((lo, min(lo + FF_PART, D_FF)) for lo in range(0, D_FF, FF_PART))
VMEM_LIMIT = 56 * 1024 * 1024

LOG2E = 1.4426950408889634
NT_DIMS = (((1,), (1,)), ((), ()))
TN_DIMS = (((0,), (0,)), ((), ()))


def _dot(a, b):
    return jnp.dot(a, b, preferred_element_type=F32)


def _dot_nt(a, b):
    return lax.dot_general(a, b, NT_DIMS, preferred_element_type=F32)


def _dot_tn(a, b):
    return lax.dot_general(a, b, TN_DIMS, preferred_element_type=F32)


def _split_bf16(a):
    hi = a.astype(BF16)
    lo = (a - hi.astype(F32)).astype(BF16)
    return hi, lo


def _silu_of_half(half):
    return half + half * jnp.tanh(half)


def _silu(a):
    return _silu_of_half(0.5 * a)


def _log_sigmoid(a):
    return jnp.minimum(a, 0.0) - jnp.log(1.0 + jnp.exp2(jnp.abs(a) * (-LOG2E)))


def _rms_scale(a):
    return lax.rsqrt(jnp.mean(a * a, axis=-1, keepdims=True) + RMS_EPS)


def _ada_kernel(c_ref, w_ref, b_ref, o_ref):
    c = c_ref[...]
    a = _silu(c).astype(BF16)
    o_ref[...] = _dot(a, w_ref[...].astype(BF16)) + b_ref[...]


def _ada_call(c_all, w_ada, b_ada):
    depth = w_ada.shape[0]
    n_tiles = (6 * D_MODEL) // ADA_TILE
    return pl.pallas_call(
        _ada_kernel,
        grid=(depth, n_tiles),
        in_specs=[
            pl.BlockSpec((ADA_ROWS, D_MODEL), lambda l, j: (0, 0)),
            pl.BlockSpec((None, D_MODEL, ADA_TILE), lambda l, j: (l, 0, j)),
            pl.BlockSpec((None, 1, ADA_TILE), lambda l, j: (l, 0, j)),
        ],
        out_specs=pl.BlockSpec((None, ADA_ROWS, ADA_TILE), lambda l, j: (l, 0, j)),
        out_shape=jax.ShapeDtypeStruct((depth, ADA_ROWS, 6 * D_MODEL), F32),
        compiler_params=pltpu.CompilerParams(
            dimension_semantics=("arbitrary", "arbitrary"), vmem_limit_bytes=VMEM_LIMIT),
        name="adaln_mod",
    )(c_all, w_ada, b_ada.reshape(depth, 1, 6 * D_MODEL))


def _dup_halves(a, low_half):
    swapped = pltpu.roll(a, LANE // 2, axis=1)
    return jnp.where(low_half, a, swapped), jnp.where(low_half, swapped, a)


def _mixer_kernel(sample, n_chunks, layer, preps, *refs):
    if sample:
        (x_ref, mod_ref, g_attn_ref, w_in_ref, w_gk2_ref, b_gk_ref, g_gla_ref, sinks_ref, w_out_ref,
         s0_ref, kc_ref, vc_ref,
         xo_ref, so_ref, ko_ref, vo_ref,
         h_scr, proj_scr, omix_scr, qm_scr, km_scr, qi_scr, ko_scr, vb_scr, gk_scr, dec_scr, u_scr, sb_scr,
         att_scr, pn_scr,
         kd_scr, vd_scr) = refs
        t = None
    else:
        n_in, n_out, n_prep = 9, 4, len(preps)
        prep_src = refs[n_in:n_in + n_prep]
        prep_dst = refs[n_in + n_prep + n_out:n_in + n_prep + n_out + n_prep]
        refs = refs[:n_in] + refs[n_in + n_prep:n_in + n_prep + n_out] + refs[n_in + 2 * n_prep + n_out:]
        (x_ref, mod_ref, g_attn_ref, w_in_ref, w_gk2_ref, b_gk_ref, g_gla_ref, sinks_ref, w_out_ref,
         xo_ref, so_ref, ko_ref, vo_ref,
         h_scr, proj_scr, omix_scr, qm_scr, km_scr, qi_scr, ko_scr, vb_scr, gk_scr, dec_scr, u_scr, sb_scr,
         att_scr, pn_scr,
         kd_scr, vd_scr, s_scr) = refs
        for body, src, dst in zip(preps, prep_src, prep_dst):
            body(src, dst)
        t = pl.program_id(1)
        n_t = pl.num_programs(1)

        @pl.when(t == 0)
        def _():
            s_scr[...] = jnp.zeros_like(s_scr)
            kd_scr[0:WINDOW, :] = jnp.zeros((WINDOW, 2 * LANE), BF16)
            vd_scr[0:WINDOW, :] = jnp.zeros((WINDOW, 2 * LANE), BF16)

    rows = n_chunks * CHUNK
    g_attn = g_attn_ref[...]

    def mod_row(c, idx):
        if sample:
            return mod_ref[c, idx:idx + 1, :]
        return mod_ref[idx:idx + 1, :]

    def chunk_rows(c):
        return slice(c * CHUNK, (c + 1) * CHUNK)

    sub_chunks = min(SUB_CHUNKS, n_chunks)
    n_sub = n_chunks // sub_chunks
    sub_rows = sub_chunks * CHUNK

    def sub_slice(s):
        return slice(s * sub_rows, (s + 1) * sub_rows)

    def norm_groups(s):
        if sample:
            return [(c, chunk_rows(c)) for c in range(s * sub_chunks, (s + 1) * sub_chunks)]
        return [(0, sub_slice(s))]

    r64 = lax.broadcasted_iota(jnp.int32, (CHUNK, CHUNK), 0)
    c64 = lax.broadcasted_iota(jnp.int32, (CHUNK, CHUNK), 1)
    tri = jnp.where(r64 >= c64, 1.0, 0.0).astype(BF16)
    tri2 = jnp.concatenate([tri, tri], axis=1)
    l_pair = lax.broadcasted_iota(jnp.int32, (CHUNK, LANE), 1)
    low_half = l_pair < LANE // 2
    causal_heads = (lax.broadcasted_iota(jnp.int32, (CHUNK, GLA_QK), 0)
                    >= (lax.broadcasted_iota(jnp.int32, (CHUNK, GLA_QK), 1) & (CHUNK - 1)))
    diag_heads = ((lax.broadcasted_iota(jnp.int32, (GLA_QK, GLA_QK), 0) // GLA_DK)
                  == (lax.broadcasted_iota(jnp.int32, (GLA_QK, GLA_QK), 1) // GLA_DK))
    diag_v = ((lax.broadcasted_iota(jnp.int32, (LANE, PAIR_V), 0) >= LANE // 2)
              == (lax.broadcasted_iota(jnp.int32, (LANE, PAIR_V), 1) >= GLA_DV))
    g_gla = g_gla_ref[...]
    low_sub = lax.broadcasted_iota(jnp.int32, (sub_rows, LANE), 1) < LANE // 2

    def project_steps(s):
        ss = sub_slice(s)

        def norm():
            for mi, rs in norm_groups(s):
                x = x_ref[rs, :]
                gain = g_attn * (1.0 + mod_row(mi, 1))
                h_scr[rs, :] = (x * _rms_scale(x) * gain + mod_row(mi, 0)).astype(BF16)

        def piece(lo, hi):
            proj_scr[ss, lo:hi] = _dot(h_scr[ss, :], w_in_ref[:, lo:hi])

        def gates():
            lr = proj_scr[ss, C_LR:C_LR + LR_PAD].astype(BF16)
            gk_scr[ss, :] = (_log_sigmoid(_dot(lr, w_gk2_ref[...]) + b_gk_ref[...])
                             * (LOG2E / GLA_NORMALIZER))
            vb_scr[ss, :] = proj_scr[ss, C_VG:C_VG + GLA_V].astype(BF16)

        def bands():
            k_dup = _dup_halves(proj_scr[ss, C_KS:C_KS + SWA_KV], low_sub)
            v_dup = _dup_halves(proj_scr[ss, C_VS:C_VS + SWA_KV], low_sub)
            for kv in range(SWA_KV_HEADS):
                ls = slice(kv * LANE, (kv + 1) * LANE)
                if sample:
                    for c in range(s * sub_chunks, (s + 1) * sub_chunks):
                        local = slice((c - s * sub_chunks) * CHUNK, (c - s * sub_chunks + 1) * CHUNK)
                        kd_scr[c * BAND + WINDOW:(c + 1) * BAND, ls] = k_dup[kv][local, :].astype(BF16)
                        vd_scr[c * BAND + WINDOW:(c + 1) * BAND, ls] = v_dup[kv][local, :].astype(BF16)
                else:
                    band_rows = slice(WINDOW + s * sub_rows, WINDOW + (s + 1) * sub_rows)
                    kd_scr[band_rows, ls] = k_dup[kv].astype(BF16)
                    vd_scr[band_rows, ls] = v_dup[kv].astype(BF16)

        steps = [norm]
        for lo in range(0, D_IN_P, DENSE_PIECE):
            steps.append(functools.partial(piece, lo, min(lo + DENSE_PIECE, D_IN_P)))
        return steps + [gates, bands]

    def gla_prepare(c):
        rs = chunk_rows(c)
        gk_hi, gk_lo = _split_bf16(gk_scr[rs, :])
        cum = _dot(tri2, jnp.concatenate([gk_hi, gk_lo], axis=0))
        mid = cum[CHUNK // 2:CHUNK // 2 + 1, :]
        last = cum[CHUNK - 1:CHUNK, :]
        q = proj_scr[rs, C_QG:C_QG + GLA_QK] * (GLA_DK ** -0.5)
        k = proj_scr[rs, C_KG:C_KG + GLA_QK]
        qm_scr[rs, :] = (q * jnp.exp2(cum - mid)).astype(BF16)
        km_scr[rs, :] = (k * jnp.exp2(mid - cum)).astype(BF16)
        qi_scr[rs, :] = (q * jnp.exp2(cum)).astype(BF16)
        ko_scr[rs, :] = (k * jnp.exp2(last - cum)).astype(BF16)
        dec_scr[c] = jnp.broadcast_to(jnp.exp2(last), (LANE, GLA_QK)).T

    def gla_increment(c):
        rs = chunk_rows(c)
        for p in range(GLA_PAIRS):
            upd = _dot_tn(ko_scr[rs, p * LANE:(p + 1) * LANE], vb_scr[rs, p * PAIR_V:(p + 1) * PAIR_V])
            u_scr[c, p * LANE:p * LANE + GLA_DK, :] = upd[0:GLA_DK, 0:GLA_DV]
            u_scr[c, p * LANE + GLA_DK:(p + 1) * LANE, :] = upd[GLA_DK:LANE, GLA_DV:PAIR_V]

    def gla_recurrence(s):
        state = None if sample else s_scr[...]
        for c in range(s * sub_chunks, (s + 1) * sub_chunks):
            if sample:
                state = s0_ref[c]
            sb_scr[c] = state.astype(BF16)
            state = dec_scr[c] * state + u_scr[c]
            if sample:
                so_ref[c] = state
        if not sample:
            s_scr[...] = state

    def gla_scores(c):
        rs = chunk_rows(c)
        km = km_scr[rs, :]
        k_bd = jnp.where(diag_heads, jnp.concatenate([km] * GLA_HEADS, axis=0), 0.0)
        att_scr[rs, :] = jnp.where(causal_heads, _dot_nt(qm_scr[rs, :], k_bd), 0.0).astype(BF16)

    def gla_output(c, p):
        rs = chunk_rows(c)
        ls = slice(p * LANE, (p + 1) * LANE)
        v_pair = vb_scr[rs, p * PAIR_V:(p + 1) * PAIR_V]
        s_pair = sb_scr[c, ls, :]
        w_top = jnp.where(diag_v, jnp.concatenate([v_pair, v_pair], axis=0), 0.0)
        w_bot = jnp.where(diag_v, jnp.concatenate([s_pair, s_pair], axis=1), 0.0)
        o_pair = _dot(jnp.concatenate([att_scr[rs, ls], qi_scr[rs, ls]], axis=1),
                      jnp.concatenate([w_top, w_bot], axis=0))
        for hh in range(2):
            h = 2 * p + hh
            o = o_pair[:, hh * GLA_DV:(hh + 1) * GLA_DV]
            og = proj_scr[rs, C_OG + h * GLA_DV:C_OG + (h + 1) * GLA_DV]
            y = o * _rms_scale(o) * g_gla * _silu(og)
            omix_scr[rs, h * GLA_DV:(h + 1) * GLA_DV] = y.astype(BF16)

    if sample:
        low_win = lax.broadcasted_iota(jnp.int32, (WINDOW, LANE), 1) < LANE // 2
        for c in range(n_chunks):
            kc_dup = _dup_halves(kc_ref[c], low_win)
            vc_dup = _dup_halves(vc_ref[c], low_win)
            for kv in range(SWA_KV_HEADS):
                ls = slice(kv * LANE, (kv + 1) * LANE)
                kd_scr[c * BAND:c * BAND + WINDOW, ls] = kc_dup[kv].astype(BF16)
                vd_scr[c * BAND:c * BAND + WINDOW, ls] = vc_dup[kv].astype(BF16)

    lane_q = lax.broadcasted_iota(jnp.int32, (1, SWA_GRP_Q), 1)
    key_ids = lax.broadcasted_iota(jnp.int32, (BAND, SWA_GRP_Q), 0)
    sink_vecs = []
    for kv in range(SWA_KV_HEADS):
        vec = jnp.full((1, SWA_GRP_Q), sinks_ref[layer, kv * SWA_GROUP + SWA_GROUP - 1], F32)
        for g in range(SWA_GROUP - 2, -1, -1):
            vec = jnp.where(lane_q < (g + 1) * SWA_HD, sinks_ref[layer, kv * SWA_GROUP + g], vec)
        sink_vecs.append(vec * LOG2E)

    def band_rows(c):
        return slice(c * BAND, (c + 1) * BAND) if sample else slice(c * CHUNK, c * CHUNK + BAND)

    def swa_scores(c, kv):
        rs = chunk_rows(c)
        band = band_rows(c)
        ls = slice(kv * LANE, (kv + 1) * LANE)
        qg = (proj_scr[rs, C_QS + kv * SWA_GRP_Q:C_QS + (kv + 1) * SWA_GRP_Q]
              * (SWA_HD ** -0.5 * LOG2E)).astype(BF16)
        q_stack = jnp.concatenate(
            [jnp.where(low_half if hh == 0 else ~low_half, qg[:, pp * LANE:(pp + 1) * LANE], 0.0)
             for pp in range(SWA_GROUP // 2) for hh in range(2)], axis=0)
        s_t = _dot_nt(kd_scr[band, ls], q_stack)
        if not sample and c * CHUNK < WINDOW:
            first_valid = WINDOW - (t * rows + c * CHUNK)
            s_t = jnp.where(key_ids >= first_valid, s_t, -jnp.inf)
        sink = sink_vecs[kv]
        m = jnp.maximum(jnp.max(s_t, axis=0, keepdims=True), sink)
        p_t = jnp.exp2(s_t - m)
        den = jnp.sum(p_t, axis=0, keepdims=True) + jnp.exp2(sink - m)
        pn_scr[c * SWA_KV_HEADS + kv] = (p_t * (1.0 / den)).astype(BF16)

    def swa_output(c, kv):
        rs = chunk_rows(c)
        ls = slice(kv * LANE, (kv + 1) * LANE)
        o_t = _dot_tn(vd_scr[band_rows(c), ls], pn_scr[c * SWA_KV_HEADS + kv]).T
        for pp in range(SWA_GROUP // 2):
            o_pair = jnp.where(low_half, o_t[(2 * pp) * CHUNK:(2 * pp + 1) * CHUNK, :],
                               o_t[(2 * pp + 1) * CHUNK:(2 * pp + 2) * CHUNK, :])
            col = GLA_V + kv * SWA_GRP_Q + pp * LANE
            omix_scr[rs, col:col + LANE] = o_pair.astype(BF16)

    def out_steps(s):
        ss = sub_slice(s)
        base = s * sub_rows

        def piece(lo, hi):
            mix = _dot(omix_scr[ss, :], w_out_ref[:, lo:hi])
            for mi, rs in norm_groups(s):
                local = slice(rs.start - base, rs.stop - base)
                xo_ref[rs, lo:hi] = x_ref[rs, lo:hi] + mod_row(mi, 2)[:, lo:hi] * mix[local, :]

        return [functools.partial(piece, lo, lo + DENSE_PIECE) for lo in range(0, D_MODEL, DENSE_PIECE)]

    def block_steps(s):
        blocks = range(s * sub_chunks, (s + 1) * sub_chunks)
        pairs = [(c, p) for c in blocks for p in range(GLA_PAIRS)]
        groups = [(c, kv) for c in blocks for kv in range(SWA_KV_HEADS)]
        steps = [functools.partial(gla_prepare, c) for c in blocks]
        steps += [functools.partial(swa_scores, c, kv) for c, kv in groups]
        steps += [functools.partial(gla_increment, c) for c in blocks]
        steps += [functools.partial(gla_scores, c) for c in blocks]
        steps.append(functools.partial(gla_recurrence, s))
        for (c, p), (c2, kv) in zip(pairs, groups):
            steps.append(functools.partial(gla_output, c, p))
            steps.append(functools.partial(swa_output, c2, kv))
        return steps

    for step in project_steps(0):
        step()
    for s in range(n_sub):
        dense = (project_steps(s + 1) if s + 1 < n_sub else []) + (out_steps(s - 1) if s > 0 else [])
        work = block_steps(s)
        issued = 0
        for i, step in enumerate(work):
            while issued < len(dense) and (issued - DENSE_LEAD) * len(work) < (i + 1) * len(dense):
                dense[issued]()
                issued += 1
            step()
        for step in dense[issued:]:
            step()
    for step in out_steps(n_sub - 1):
        step()

    if sample:
        ko_ref[...] = proj_scr[:, C_KS:C_KS + SWA_KV]
        vo_ref[...] = proj_scr[:, C_VS:C_VS + SWA_KV]
    else:
        kd_scr[0:WINDOW, :] = kd_scr[rows:rows + WINDOW, :]
        vd_scr[0:WINDOW, :] = vd_scr[rows:rows + WINDOW, :]

        @pl.when(t == n_t - 1)
        def _():
            so_ref[...] = s_scr[...]
            ko_ref[...] = proj_scr[rows - WINDOW:rows, C_KS:C_KS + SWA_KV]
            vo_ref[...] = proj_scr[rows - WINDOW:rows, C_VS:C_VS + SWA_KV]


def _pad_cast_w_in(w_ref, o_ref):
    head = C_LR + GLA_RANK
    w = w_ref[...].astype(F32)
    o_ref[:, 0:C_LR] = w[:, 0:C_LR].astype(BF16)
    gate = jnp.concatenate(
        [w[:, C_LR:head], jnp.zeros((w.shape[0], LR_PAD - GLA_RANK), F32)], axis=1)
    o_ref[:, C_LR:C_QS] = gate.astype(BF16)
    o_ref[:, C_QS:D_IN_P] = w[:, head:D_IN].astype(BF16)


def _cast_weight(w_ref, o_ref):
    o_ref[...] = w_ref[...].astype(BF16)


class _WeightPrep:
    def __init__(self, weight, layer, rows_per_step, out_cols, body):
        self.weight, self.layer, self.rows_per_step = weight, layer, rows_per_step
        self.rows, self.in_cols = weight.shape[1], weight.shape[2]
        self.out_cols, self.body = out_cols, body
        assert self.rows % rows_per_step == 0
        self.n_blocks = self.rows // rows_per_step

    def specs(self, n_t):
        def block(b, t):
            return jnp.minimum(b * n_t + t, self.n_blocks - 1)
        layer = self.layer
        return (pl.BlockSpec((None, self.rows_per_step, self.in_cols), lambda b, t: (layer, block(b, t), 0)),
                pl.BlockSpec((None, self.rows_per_step, self.out_cols), lambda b, t: (0, block(b, t), 0)),
                jax.ShapeDtypeStruct((1, self.rows, self.out_cols), BF16))


def _const_spec(shape):
    zeros = (0,) * len(shape)
    return pl.BlockSpec(shape, lambda *_: zeros, pipeline_mode=pl.Buffered(1))


def _layer_spec(layer, shape):
    index = (layer,) + (0,) * len(shape)
    return pl.BlockSpec((None,) + shape, lambda *_: index, pipeline_mode=pl.Buffered(1))


def _mixer_weight_specs(layer):
    return [
        _layer_spec(layer, (1, D_MODEL)),
        _layer_spec(0, (D_MODEL, D_IN_P)),
        _layer_spec(layer, (LR_PAD, GLA_QK)),
        _layer_spec(layer, (1, GLA_QK)),
        _layer_spec(layer, (1, GLA_DV)),
        pl.BlockSpec(memory_space=pltpu.SMEM),
        _layer_spec(0, (D_MIX, D_MODEL)),
    ]


def _mixer_scratch(rows, n_chunks, band_rows):
    state_rows = GLA_HEADS * GLA_DK
    return [
        pltpu.VMEM((rows, D_MODEL), BF16),
        pltpu.VMEM((rows, D_IN_P), F32),
        pltpu.VMEM((rows, D_MIX), BF16),
        pltpu.VMEM((rows, GLA_QK), BF16),
        pltpu.VMEM((rows, GLA_QK), BF16),
        pltpu.VMEM((rows, GLA_QK), BF16),
        pltpu.VMEM((rows, GLA_QK), BF16),
        pltpu.VMEM((rows, GLA_V), BF16),
        pltpu.VMEM((rows, GLA_QK), F32),
        pltpu.VMEM((n_chunks, state_rows, GLA_DV), F32),
        pltpu.VMEM((n_chunks, state_rows, GLA_DV), F32),
        pltpu.VMEM((n_chunks, state_rows, GLA_DV), BF16),
        pltpu.VMEM((rows, GLA_QK), BF16),
        pltpu.VMEM((n_chunks * SWA_KV_HEADS, BAND, SWA_GRP_Q), BF16),
        pltpu.VMEM((band_rows, 2 * LANE), BF16),
        pltpu.VMEM((band_rows, 2 * LANE), BF16),
    ]


def _mixer_prompt_call(layer, x, mod, mod_row0, weights, preps):
    batch, seq, _ = x.shape
    tile = min(MIXER_TILE, seq)
    assert seq % tile == 0 and tile % CHUNK == 0 and tile >= WINDOW
    n_chunks = tile // CHUNK
    n_t = seq // tile
    state_rows = GLA_HEADS * GLA_DK
    prep_specs = [p.specs(n_t) for p in preps]
    assert all(p.n_blocks <= batch * n_t for p in preps)
    return pl.pallas_call(
        functools.partial(_mixer_kernel, False, n_chunks, layer, tuple(p.body for p in preps)),
        grid=(batch, n_t),
        in_specs=[
            pl.BlockSpec((None, tile, D_MODEL), lambda b, t: (b, t, 0)),
            pl.BlockSpec((None, None, 6, D_MODEL), lambda b, t: (layer, mod_row0 + b, 0, 0)),
        ] + _mixer_weight_specs(layer) + [s[0] for s in prep_specs],
        out_specs=[
            pl.BlockSpec((None, tile, D_MODEL), lambda b, t: (b, t, 0)),
            pl.BlockSpec((None, state_rows, GLA_DV), lambda b, t: (b, 0, 0)),
            pl.BlockSpec((None, WINDOW, SWA_KV), lambda b, t: (b, 0, 0)),
            pl.BlockSpec((None, WINDOW, SWA_KV), lambda b, t: (b, 0, 0)),
        ] + [s[1] for s in prep_specs],
        out_shape=[
            jax.ShapeDtypeStruct((batch, seq, D_MODEL), F32),
            jax.ShapeDtypeStruct((batch, state_rows, GLA_DV), F32),
            jax.ShapeDtypeStruct((batch, WINDOW, SWA_KV), F32),
            jax.ShapeDtypeStruct((batch, WINDOW, SWA_KV), F32),
        ] + [s[2] for s in prep_specs],
        scratch_shapes=_mixer_scratch(tile, n_chunks, WINDOW + tile) + [
            pltpu.VMEM((state_rows, GLA_DV), F32),
        ],
        compiler_params=pltpu.CompilerParams(
            dimension_semantics=("arbitrary", "arbitrary"), vmem_limit_bytes=VMEM_LIMIT),
        name="mixer_prompt",
    )(x, mod, *weights, *[p.weight for p in preps])


def _mixer_sample_call(layer, x, mod, weights, s0, k_cache, v_cache):
    batch, seq, _ = x.shape
    assert seq == CHUNK
    rows = batch * seq
    state_rows = GLA_HEADS * GLA_DK

    def full(shape):
        zeros = (0,) * len(shape)
        return pl.BlockSpec(shape, lambda i: zeros)

    def of_layer(shape):
        index = (layer,) + (0,) * len(shape)
        return pl.BlockSpec((None,) + shape, lambda i: index)

    return pl.pallas_call(
        functools.partial(_mixer_kernel, True, batch, layer, ()),
        grid=(1,),
        in_specs=[full((rows, D_MODEL)), of_layer((batch, 6, D_MODEL))] + _mixer_weight_specs(layer) + [
            of_layer((batch, state_rows, GLA_DV)),
            of_layer((batch, WINDOW, SWA_KV)),
            of_layer((batch, WINDOW, SWA_KV)),
        ],
        out_specs=[
            full((rows, D_MODEL)),
            full((batch, state_rows, GLA_DV)),
            full((rows, SWA_KV)),
            full((rows, SWA_KV)),
        ],
        out_shape=[
            jax.ShapeDtypeStruct((rows, D_MODEL), F32),
            jax.ShapeDtypeStruct((batch, state_rows, GLA_DV), F32),
            jax.ShapeDtypeStruct((rows, SWA_KV), F32),
            jax.ShapeDtypeStruct((rows, SWA_KV), F32),
        ],
        scratch_shapes=_mixer_scratch(rows, batch, batch * BAND),
        compiler_params=pltpu.CompilerParams(
            dimension_semantics=("arbitrary",), vmem_limit_bytes=VMEM_LIMIT),
        name="mixer_sample",
    )(x.reshape(rows, D_MODEL), mod, *weights, s0, k_cache, v_cache)


def _ffn_kernel(sample, final, n_seg, seg_len, preps, *refs):
    if sample:
        (x_ref, mod_ref, g_ffn_ref, w_up_ref, conv_w_ref, conv_b_ref, w_down_ref, g_final_ref,
         past_ref, xo_ref, co_ref, h_scr, ub_scr, act_scr) = refs
    else:
        n_in, n_out, n_prep = 8, 2, len(preps)
        prep_src = refs[n_in:n_in + n_prep]
        prep_dst = refs[n_in + n_prep + n_out:n_in + n_prep + n_out + n_prep]
        refs = refs[:n_in] + refs[n_in + n_prep:n_in + n_prep + n_out] + refs[n_in + 2 * n_prep + n_out:]
        (x_ref, mod_ref, g_ffn_ref, w_up_ref, conv_w_ref, conv_b_ref, w_down_ref, g_final_ref,
         xo_ref, co_ref, h_scr, ub_scr, act_scr, past_scr) = refs
        for body, src, dst in zip(preps, prep_src, prep_dst):
            body(src, dst)
        t = pl.program_id(1)
        n_t = pl.num_programs(1)

        @pl.when(t == 0)
        def _():
            past_scr[...] = jnp.zeros_like(past_scr)

    stride = seg_len + SUBLANE
    g_ffn = g_ffn_ref[...]

    def mod_row(c, idx):
        if sample:
            return mod_ref[c, idx:idx + 1, :]
        return mod_ref[idx:idx + 1, :]

    for c in range(n_seg):
        rs = slice(c * seg_len, (c + 1) * seg_len)
        x = x_ref[rs, :]
        gain = g_ffn * (1.0 + mod_row(c, 4))
        h_scr[rs, :] = (x * _rms_scale(x) * gain + mod_row(c, 3)).astype(BF16)

    def up(j):
        lo, hi = FF_PARTS[j]
        h = h_scr[...]
        return _dot(h, w_up_ref[:, lo:hi]), _dot(h, w_up_ref[:, D_FF + lo:D_FF + hi])

    def activate(j, u, val):
        lo, hi = FF_PARTS[j]
        width = hi - lo
        w0 = 0.5 * conv_w_ref[0:1, lo:hi]
        w1 = 0.5 * conv_w_ref[1:2, lo:hi]
        w2 = 0.5 * conv_w_ref[2:3, lo:hi]
        cb = 0.5 * conv_b_ref[:, lo:hi]
        for c in range(n_seg):
            base = c * stride
            rs = slice(c * seg_len, (c + 1) * seg_len)
            if sample:
                ub_scr[base + SUBLANE - 2:base + SUBLANE, 0:width] = past_ref[c, :, lo:hi]
            else:
                ub_scr[base + SUBLANE - 2:base + SUBLANE, 0:width] = past_scr[:, lo:hi]
            u_seg = u[rs, :]
            ub_scr[base + SUBLANE:base + SUBLANE + seg_len, 0:width] = u_seg
            u1 = ub_scr[base + SUBLANE - 1:base + SUBLANE - 1 + seg_len, 0:width]
            u2 = ub_scr[base + SUBLANE - 2:base + SUBLANE - 2 + seg_len, 0:width]
            half_uc = w0 * u2 + w1 * u1 + w2 * u_seg + cb
            act_scr[rs, lo:hi] = (_silu_of_half(half_uc) * val[rs, :]).astype(BF16)
            tail = ub_scr[base + seg_len + SUBLANE - 2:base + seg_len + SUBLANE, 0:width]
            if sample:
                co_ref[c, :, lo:hi] = tail
            else:
                past_scr[:, lo:hi] = tail

    def down(j):
        lo, hi = FF_PARTS[j]
        return _dot(act_scr[:, lo:hi], w_down_ref[lo:hi, :])

    n_parts = len(FF_PARTS)
    pending = {0: up(0)}
    acc = None
    for j in range(n_parts):
        if j + 1 < n_parts:
            pending[j + 1] = up(j + 1)
        activate(j, *pending.pop(j))
        if j > 0:
            d = down(j - 1)
            acc = d if acc is None else acc + d
    acc = acc + down(n_parts - 1)

    for c in range(n_seg):
        rs = slice(c * seg_len, (c + 1) * seg_len)
        y = x_ref[rs, :] + mod_row(c, 5) * acc[rs, :]
        if final:
            y = y * _rms_scale(y) * g_final_ref[...]
        xo_ref[rs, :] = y

    if not sample:
        @pl.when(t == n_t - 1)
        def _():
            co_ref[...] = past_scr[...]


def _ffn_weight_specs(layer):
    return [
        _layer_spec(layer, (1, D_MODEL)),
        _layer_spec(0, (D_MODEL, 2 * D_FF)),
        _layer_spec(layer, (CONV_W, D_FF)),
        _layer_spec(layer, (1, D_FF)),
        _layer_spec(0, (D_FF, D_MODEL)),
        _const_spec((1, D_MODEL)),
    ]


def _ffn_prompt_call(layer, x, mod, mod_row0, weights, final, preps):
    batch, seq, _ = x.shape
    tile = min(PROMPT_TILE, seq)
    n_t = seq // tile
    prep_specs = [p.specs(n_t) for p in preps]
    assert all(p.n_blocks <= batch * n_t for p in preps)
    return pl.pallas_call(
        functools.partial(_ffn_kernel, False, final, 1, tile, tuple(p.body for p in preps)),
        grid=(batch, n_t),
        in_specs=[
            pl.BlockSpec((None, tile, D_MODEL), lambda b, t: (b, t, 0)),
            pl.BlockSpec((None, None, 6, D_MODEL), lambda b, t: (layer, mod_row0 + b, 0, 0)),
        ] + _ffn_weight_specs(layer) + [s[0] for s in prep_specs],
        out_specs=[
            pl.BlockSpec((None, tile, D_MODEL), lambda b, t: (b, t, 0)),
            pl.BlockSpec((None, CONV_W - 1, D_FF), lambda b, t: (b, 0, 0)),
        ] + [s[1] for s in prep_specs],
        out_shape=[
            jax.ShapeDtypeStruct((batch, seq, D_MODEL), F32),
            jax.ShapeDtypeStruct((batch, CONV_W - 1, D_FF), F32),
        ] + [s[2] for s in prep_specs],
        scratch_shapes=[
            pltpu.VMEM((tile, D_MODEL), BF16),
            pltpu.VMEM((tile + SUBLANE, FF_PART), F32),
            pltpu.VMEM((tile, D_FF), BF16),
            pltpu.VMEM((CONV_W - 1, D_FF), F32),
        ],
        compiler_params=pltpu.CompilerParams(
            dimension_semantics=("arbitrary", "arbitrary"), vmem_limit_bytes=VMEM_LIMIT),
        name="ffn_prompt",
    )(x, mod, *weights, *[p.weight for p in preps])


def _ffn_sample_call(layer, x2d, mod, weights, past, final, batch, seq):
    rows = batch * seq

    def full(shape):
        zeros = (0,) * len(shape)
        return pl.BlockSpec(shape, lambda i: zeros)

    def of_layer(shape):
        index = (layer,) + (0,) * len(shape)
        return pl.BlockSpec((None,) + shape, lambda i: index)

    return pl.pallas_call(
        functools.partial(_ffn_kernel, True, final, batch, seq, ()),
        grid=(1,),
        in_specs=[full((rows, D_MODEL)), of_layer((batch, 6, D_MODEL))] + _ffn_weight_specs(layer) + [
            of_layer((batch, CONV_W - 1, D_FF)),
        ],
        out_specs=[full((rows, D_MODEL)), full((batch, CONV_W - 1, D_FF))],
        out_shape=[
            jax.ShapeDtypeStruct((rows, D_MODEL), F32),
            jax.ShapeDtypeStruct((batch, CONV_W - 1, D_FF), F32),
        ],
        scratch_shapes=[
            pltpu.VMEM((rows, D_MODEL), BF16),
            pltpu.VMEM((batch * (seq + SUBLANE), FF_PART), F32),
            pltpu.VMEM((rows, D_FF), BF16),
        ],
        compiler_params=pltpu.CompilerParams(
            dimension_semantics=("arbitrary",), vmem_limit_bytes=VMEM_LIMIT),
        name="ffn_sample",
    )(x2d, mod, *weights, past)


def _pad_w_in_first(w):
    n_blocks = D_MODEL // W_PREP_ROWS
    return pl.pallas_call(
        _pad_cast_w_in,
        grid=(n_blocks,),
        in_specs=[pl.BlockSpec((None, W_PREP_ROWS, D_IN), lambda r: (0, r, 0))],
        out_specs=pl.BlockSpec((None, W_PREP_ROWS, D_IN_P), lambda r: (0, r, 0)),
        out_shape=jax.ShapeDtypeStruct((1, D_MODEL, D_IN_P), BF16),
        compiler_params=pltpu.CompilerParams(
            dimension_semantics=("arbitrary",), vmem_limit_bytes=VMEM_LIMIT),
        name="pad_w_in",
    )(w)


def kernel(x_prompt, x_sample, state_gla, cache_swa_k, cache_swa_v, state_conv, c_prompt, c_sample,
           w_ada, b_ada, g_attn, g_ffn, w_in, w_gk2, b_gk, g_gla, sinks, w_out, w_up, conv_w, conv_b,
           w_down, g_final):
    depth = w_ada.shape[0]
    batch, seq, _ = x_prompt.shape
    dec_batch, dec_seq, _ = x_sample.shape
    state_rows = GLA_HEADS * GLA_DK

    c_all = jnp.concatenate(
        [c_sample, c_prompt, jnp.zeros((ADA_ROWS - batch - dec_batch, D_MODEL), F32)], axis=0)
    mod_all = _ada_call(c_all, w_ada, b_ada).reshape(depth, ADA_ROWS, 6, D_MODEL)

    w_gk2_p = jnp.concatenate(
        [w_gk2, jnp.zeros((depth, LR_PAD - GLA_RANK, GLA_QK), F32)], axis=1).astype(BF16)
    w_in16 = w_in.astype(BF16)
    w_in_b, w_out_b = _pad_w_in_first(w_in16), w_out[:1].astype(BF16)
    mixer_steps = batch * (seq // min(MIXER_TILE, seq))
    ffn_steps = batch * (seq // min(PROMPT_TILE, seq))
    prep_ahead = (D_MODEL // MIXER_PREP_ROWS_UP <= mixer_steps
                  and D_FF // MIXER_PREP_ROWS_DOWN <= mixer_steps
                  and D_MODEL // FFN_PREP_ROWS <= ffn_steps)
    s0_all = state_gla.reshape(depth, dec_batch, state_rows, GLA_DV)
    kc_all = cache_swa_k.reshape(depth, dec_batch, WINDOW, SWA_KV)
    vc_all = cache_swa_v.reshape(depth, dec_batch, WINDOW, SWA_KV)

    yp = x_prompt
    ys = x_sample.reshape(dec_batch * dec_seq, D_MODEL)
    outs = [[] for _ in range(8)]
    for i in range(depth):
        final = i == depth - 1
        mixer_w = (g_attn[:, None], w_in_b, w_gk2_p, b_gk[:, None], g_gla[:, None], sinks, w_out_b)
        mixer_preps, ffn_preps = [], []
        if prep_ahead:
            mixer_preps = [_WeightPrep(w_up, i, MIXER_PREP_ROWS_UP, 2 * D_FF, _cast_weight),
                           _WeightPrep(w_down, i, MIXER_PREP_ROWS_DOWN, D_MODEL, _cast_weight)]
            if not final:
                ffn_preps = [_WeightPrep(w_in16, i + 1, FFN_PREP_ROWS, D_IN_P, _pad_cast_w_in),
                             _WeightPrep(w_out, i + 1, FFN_PREP_ROWS, D_MODEL, _cast_weight)]

        yp, s_p, k_p, v_p, *prepared = _mixer_prompt_call(i, yp, mod_all, dec_batch, mixer_w, mixer_preps)
        w_up_b, w_down_b = prepared or (w_up[i:i + 1].astype(BF16), w_down[i:i + 1].astype(BF16))
        ffn_w = (g_ffn[:, None], w_up_b, conv_w, conv_b[:, None], w_down_b, g_final[None])
        yp, conv_p, *prepared = _ffn_prompt_call(i, yp, mod_all, dec_batch, ffn_w, final, ffn_preps)

        ys, s_s, k_s, v_s = _mixer_sample_call(
            i, ys.reshape(dec_batch, dec_seq, D_MODEL), mod_all, mixer_w, s0_all, kc_all, vc_all)
        ys, conv_s = _ffn_sample_call(i, ys, mod_all, ffn_w, state_conv, final, dec_batch, dec_seq)
        if not final:
            w_in_b, w_out_b = prepared or (
                _pad_w_in_first(w_in16[i + 1:i + 2]), w_out[i + 1:i + 2].astype(BF16))

        keep = min(WINDOW, seq)
        outs[0].append(s_p.reshape(batch, GLA_HEADS, GLA_DK, GLA_DV))
        outs[1].append(k_p.reshape(batch, keep, SWA_KV_HEADS, SWA_HD))
        outs[2].append(v_p.reshape(batch, keep, SWA_KV_HEADS, SWA_HD))
        outs[3].append(conv_p)
        outs[4].append(s_s.reshape(dec_batch, GLA_HEADS, GLA_DK, GLA_DV))
        outs[5].append(k_s.reshape(dec_batch, dec_seq, SWA_KV_HEADS, SWA_HD))
        outs[6].append(v_s.reshape(dec_batch, dec_seq, SWA_KV_HEADS, SWA_HD))
        outs[7].append(conv_s)

    return (yp, ys.reshape(dec_batch, dec_seq, D_MODEL)) + tuple(jnp.stack(o) for o in outs)
```

```python
import functools

import jax
import jax.numpy as jnp
from jax import lax
from jax.experimental import pallas as pl
from jax.experimental.pallas import tpu as pltpu

F32 = jnp.float32
BF16 = jnp.bfloat16

D_MODEL = 1024
CHUNK = 64
GLA_HEADS = 4
GLA_DK = 64
GLA_DV = 128
GLA_RANK = 16
GLA_NORMALIZER = 16.0
SWA_Q_HEADS = 8
SWA_KV_HEADS = 2
SWA_GROUP = SWA_Q_HEADS // SWA_KV_HEADS
SWA_HD = 64
WINDOW = 128
D_FF = 2816
CONV_W = 3
RMS_EPS = 1e-6

GLA_QK = GLA_HEADS * GLA_DK
GLA_V = GLA_HEADS * GLA_DV
SWA_Q = SWA_Q_HEADS * SWA_HD
SWA_KV = SWA_KV_HEADS * SWA_HD
D_MIX = GLA_V + SWA_Q
BAND = WINDOW + CHUNK

LANE = 128
SUBLANE = 8

C_QG = 0
C_KG = C_QG + GLA_QK
C_VG = C_KG + GLA_QK
C_OG = C_VG + GLA_V
C_LR = C_OG + GLA_V
LR_PAD = LANE
C_QS = C_LR + LR_PAD
C_KS = C_QS + SWA_Q
C_VS = C_KS + SWA_KV
D_IN_P = C_VS + SWA_KV
D_IN = 2 * GLA_QK + 2 * GLA_V + GLA_RANK + SWA_Q + 2 * SWA_KV

assert GLA_DK * 2 == LANE and SWA_HD * 2 == LANE and GLA_DV == LANE and SWA_KV == LANE
GLA_PAIRS = GLA_HEADS // 2
PAIR_V = 2 * GLA_DV
SWA_GRP_Q = SWA_GROUP * SWA_HD

ADA_TILE = 1536
ADA_ROWS = 16
PROMPT_TILE = 512
MIXER_TILE = 1024
SUB_CHUNKS = 4
DENSE_PIECE = 256
DENSE_LEAD = 2
W_PREP_ROWS = 256
MIXER_PREP_ROWS_UP = 32
MIXER_PREP_ROWS_DOWN = 128
FFN_PREP_ROWS = 16
FF_PART = 768
FF_PARTS = tuple((lo, min(lo + FF_PART, D_FF)) for lo in range(0, D_FF, FF_PART))
VMEM_LIMIT = 56 * 1024 * 1024

LOG2E = 1.4426950408889634
NT_DIMS = (((1,), (1,)), ((), ()))
TN_DIMS = (((0,), (0,)), ((), ()))


def _dot(a, b):
    return jnp.dot(a, b, preferred_element_type=F32)


def _dot_nt(a, b):
    return lax.dot_general(a, b, NT_DIMS, preferred_element_type=F32)


def _dot_tn(a, b):
    return lax.dot_general(a, b, TN_DIMS, preferred_element_type=F32)


def _split_bf16(a):
    hi = a.astype(BF16)
    lo = (a - hi.astype(F32)).astype(BF16)
    return hi, lo


def _silu(a):
    half = 0.5 * a
    return half + half * jnp.tanh(half)


def _log_sigmoid(a):
    return jnp.minimum(a, 0.0) - jnp.log(1.0 + jnp.exp2(jnp.abs(a) * (-LOG2E)))


def _rms_scale(a):
    return lax.rsqrt(jnp.mean(a * a, axis=-1, keepdims=True) + RMS_EPS)


def _ada_kernel(c_ref, w_ref, b_ref, o_ref):
    c = c_ref[...]
    a = _silu(c).astype(BF16)
    o_ref[...] = _dot(a, w_ref[...].astype(BF16)) + b_ref[...]


def _ada_call(c_all, w_ada, b_ada):
    depth = w_ada.shape[0]
    n_tiles = (6 * D_MODEL) // ADA_TILE
    return pl.pallas_call(
        _ada_kernel,
        grid=(depth, n_tiles),
        in_specs=[
            pl.BlockSpec((ADA_ROWS, D_MODEL), lambda l, j: (0, 0)),
            pl.BlockSpec((None, D_MODEL, ADA_TILE), lambda l, j: (l, 0, j)),
            pl.BlockSpec((None, 1, ADA_TILE), lambda l, j: (l, 0, j)),
        ],
        out_specs=pl.BlockSpec((None, ADA_ROWS, ADA_TILE), lambda l, j: (l, 0, j)),
        out_shape=jax.ShapeDtypeStruct((depth, ADA_ROWS, 6 * D_MODEL), F32),
        compiler_params=pltpu.CompilerParams(
            dimension_semantics=("arbitrary", "arbitrary"), vmem_limit_bytes=VMEM_LIMIT),
        name="adaln_mod",
    )(c_all, w_ada, b_ada.reshape(depth, 1, 6 * D_MODEL))


def _dup_halves(a, low_half):
    swapped = pltpu.roll(a, LANE // 2, axis=1)
    return jnp.where(low_half, a, swapped), jnp.where(low_half, swapped, a)


def _mixer_kernel(sample, n_chunks, layer, preps, *refs):
    if sample:
        (x_ref, mod_ref, g_attn_ref, w_in_ref, w_gk2_ref, b_gk_ref, g_gla_ref, sinks_ref, w_out_ref,
         s0_ref, kc_ref, vc_ref,
         xo_ref, so_ref, ko_ref, vo_ref,
         h_scr, proj_scr, omix_scr, qm_scr, km_scr, qi_scr, ko_scr, vb_scr, gk_scr, dec_scr, u_scr, sb_scr,
         att_scr, pn_scr,
         kd_scr, vd_scr) = refs
        t = None
    else:
        n_in, n_out, n_prep = 9, 4, len(preps)
        prep_src = refs[n_in:n_in + n_prep]
        prep_dst = refs[n_in + n_prep + n_out:n_in + n_prep + n_out + n_prep]
        refs = refs[:n_in] + refs[n_in + n_prep:n_in + n_prep + n_out] + refs[n_in + 2 * n_prep + n_out:]
        (x_ref, mod_ref, g_attn_ref, w_in_ref, w_gk2_ref, b_gk_ref, g_gla_ref, sinks_ref, w_out_ref,
         xo_ref, so_ref, ko_ref, vo_ref,
         h_scr, proj_scr, omix_scr, qm_scr, km_scr, qi_scr, ko_scr, vb_scr, gk_scr, dec_scr, u_scr, sb_scr,
         att_scr, pn_scr,
         kd_scr, vd_scr, s_scr) = refs
        for body, src, dst in zip(preps, prep_src, prep_dst):
            body(src, dst)
        t = pl.program_id(1)
        n_t = pl.num_programs(1)

        @pl.when(t == 0)
        def _():
            s_scr[...] = jnp.zeros_like(s_scr)
            kd_scr[0:WINDOW, :] = jnp.zeros((WINDOW, 2 * LANE), BF16)
            vd_scr[0:WINDOW, :] = jnp.zeros((WINDOW, 2 * LANE), BF16)

    rows = n_chunks * CHUNK
    g_attn = g_attn_ref[...]

    def mod_row(c, idx):
        if sample:
            return mod_ref[c, idx:idx + 1, :]
        return mod_ref[idx:idx + 1, :]

    def chunk_rows(c):
        return slice(c * CHUNK, (c + 1) * CHUNK)

    sub_chunks = min(SUB_CHUNKS, n_chunks)
    n_sub = n_chunks // sub_chunks
    sub_rows = sub_chunks * CHUNK

    def sub_slice(s):
        return slice(s * sub_rows, (s + 1) * sub_rows)

    def norm_groups(s):
        if sample:
            return [(c, chunk_rows(c)) for c in range(s * sub_chunks, (s + 1) * sub_chunks)]
        return [(0, sub_slice(s))]

    r64 = lax.broadcasted_iota(jnp.int32, (CHUNK, CHUNK), 0)
    c64 = lax.broadcasted_iota(jnp.int32, (CHUNK, CHUNK), 1)
    tri = jnp.where(r64 >= c64, 1.0, 0.0).astype(BF16)
    tri2 = jnp.concatenate([tri, tri], axis=1)
    l_pair = lax.broadcasted_iota(jnp.int32, (CHUNK, LANE), 1)
    low_half = l_pair < LANE // 2
    causal_heads = (lax.broadcasted_iota(jnp.int32, (CHUNK, GLA_QK), 0)
                    >= (lax.broadcasted_iota(jnp.int32, (CHUNK, GLA_QK), 1) & (CHUNK - 1)))
    diag_heads = ((lax.broadcasted_iota(jnp.int32, (GLA_QK, GLA_QK), 0) // GLA_DK)
                  == (lax.broadcasted_iota(jnp.int32, (GLA_QK, GLA_QK), 1) // GLA_DK))
    diag_v = ((lax.broadcasted_iota(jnp.int32, (LANE, PAIR_V), 0) >= LANE // 2)
              == (lax.broadcasted_iota(jnp.int32, (LANE, PAIR_V), 1) >= GLA_DV))
    g_gla = g_gla_ref[...]
    low_sub = lax.broadcasted_iota(jnp.int32, (sub_rows, LANE), 1) < LANE // 2

    def project_steps(s):
        ss = sub_slice(s)

        def norm():
            for mi, rs in norm_groups(s):
                x = x_ref[rs, :]
                gain = g_attn * (1.0 + mod_row(mi, 1))
                h_scr[rs, :] = (x * _rms_scale(x) * gain + mod_row(mi, 0)).astype(BF16)

        def piece(lo, hi):
            proj_scr[ss, lo:hi] = _dot(h_scr[ss, :], w_in_ref[:, lo:hi])

        def gates():
            lr = proj_scr[ss, C_LR:C_LR + LR_PAD].astype(BF16)
            gk_scr[ss, :] = (_log_sigmoid(_dot(lr, w_gk2_ref[...]) + b_gk_ref[...])
                             * (LOG2E / GLA_NORMALIZER))
            vb_scr[ss, :] = proj_scr[ss, C_VG:C_VG + GLA_V].astype(BF16)

        def bands():
            k_dup = _dup_halves(proj_scr[ss, C_KS:C_KS + SWA_KV], low_sub)
            v_dup = _dup_halves(proj_scr[ss, C_VS:C_VS + SWA_KV], low_sub)
            for kv in range(SWA_KV_HEADS):
                ls = slice(kv * LANE, (kv + 1) * LANE)
                if sample:
                    for c in range(s * sub_chunks, (s + 1) * sub_chunks):
                        local = slice((c - s * sub_chunks) * CHUNK, (c - s * sub_chunks + 1) * CHUNK)
                        kd_scr[c * BAND + WINDOW:(c + 1) * BAND, ls] = k_dup[kv][local, :].astype(BF16)
                        vd_scr[c * BAND + WINDOW:(c + 1) * BAND, ls] = v_dup[kv][local, :].astype(BF16)
                else:
                    band_rows = slice(WINDOW + s * sub_rows, WINDOW + (s + 1) * sub_rows)
                    kd_scr[band_rows, ls] = k_dup[kv].astype(BF16)
                    vd_scr[band_rows, ls] = v_dup[kv].astype(BF16)

        steps = [norm]
        for lo in range(0, D_IN_P, DENSE_PIECE):
            steps.append(functools.partial(piece, lo, min(lo + DENSE_PIECE, D_IN_P)))
        return steps + [gates, bands]

    def gla_prepare(c):
        rs = chunk_rows(c)
        gk_hi, gk_lo = _split_bf16(gk_scr[rs, :])
        cum = _dot(tri2, jnp.concatenate([gk_hi, gk_lo], axis=0))
        mid = cum[CHUNK // 2:CHUNK // 2 + 1, :]
        last = cum[CHUNK - 1:CHUNK, :]
        q = proj_scr[rs, C_QG:C_QG + GLA_QK] * (GLA_DK ** -0.5)
        k = proj_scr[rs, C_KG:C_KG + GLA_QK]
        qm_scr[rs, :] = (q * jnp.exp2(cum - mid)).astype(BF16)
        km_scr[rs, :] = (k * jnp.exp2(mid - cum)).astype(BF16)
        qi_scr[rs, :] = (q * jnp.exp2(cum)).astype(BF16)
        ko_scr[rs, :] = (k * jnp.exp2(last - cum)).astype(BF16)
        dec_scr[c] = jnp.broadcast_to(jnp.exp2(last), (LANE, GLA_QK)).T

    def gla_increment(c):
        rs = chunk_rows(c)
        for p in range(GLA_PAIRS):
            upd = _dot_tn(ko_scr[rs, p * LANE:(p + 1) * LANE], vb_scr[rs, p * PAIR_V:(p + 1) * PAIR_V])
            u_scr[c, p * LANE:p * LANE + GLA_DK, :] = upd[0:GLA_DK, 0:GLA_DV]
            u_scr[c, p * LANE + GLA_DK:(p + 1) * LANE, :] = upd[GLA_DK:LANE, GLA_DV:PAIR_V]

    def gla_recurrence(s):
        state = None if sample else s_scr[...]
        for c in range(s * sub_chunks, (s + 1) * sub_chunks):
            if sample:
                state = s0_ref[c]
            sb_scr[c] = state.astype(BF16)
            state = dec_scr[c] * state + u_scr[c]
            if sample:
                so_ref[c] = state
        if not sample:
            s_scr[...] = state

    def gla_scores(c):
        rs = chunk_rows(c)
        km = km_scr[rs, :]
        k_bd = jnp.where(diag_heads, jnp.concatenate([km] * GLA_HEADS, axis=0), 0.0)
        att_scr[rs, :] = jnp.where(causal_heads, _dot_nt(qm_scr[rs, :], k_bd), 0.0).astype(BF16)

    def gla_output(c, p):
        rs = chunk_rows(c)
        ls = slice(p * LANE, (p + 1) * LANE)
        v_pair = vb_scr[rs, p * PAIR_V:(p + 1) * PAIR_V]
        s_pair = sb_scr[c, ls, :]
        w_top = jnp.where(diag_v, jnp.concatenate([v_pair, v_pair], axis=0), 0.0)
        w_bot = jnp.where(diag_v, jnp.concatenate([s_pair, s_pair], axis=1), 0.0)
        o_pair = _dot(jnp.concatenate([att_scr[rs, ls], qi_scr[rs, ls]], axis=1),
                      jnp.concatenate([w_top, w_bot], axis=0))
        for hh in range(2):
            h = 2 * p + hh
            o = o_pair[:, hh * GLA_DV:(hh + 1) * GLA_DV]
            og = proj_scr[rs, C_OG + h * GLA_DV:C_OG + (h + 1) * GLA_DV]
            y = o * _rms_scale(o) * g_gla * _silu(og)
            omix_scr[rs, h * GLA_DV:(h + 1) * GLA_DV] = y.astype(BF16)

    if sample:
        low_win = lax.broadcasted_iota(jnp.int32, (WINDOW, LANE), 1) < LANE // 2
        for c in range(n_chunks):
            kc_dup = _dup_halves(kc_ref[c], low_win)
            vc_dup = _dup_halves(vc_ref[c], low_win)
            for kv in range(SWA_KV_HEADS):
                ls = slice(kv * LANE, (kv + 1) * LANE)
                kd_scr[c * BAND:c * BAND + WINDOW, ls] = kc_dup[kv].astype(BF16)
                vd_scr[c * BAND:c * BAND + WINDOW, ls] = vc_dup[kv].astype(BF16)

    lane_q = lax.broadcasted_iota(jnp.int32, (1, SWA_GRP_Q), 1)
    key_ids = lax.broadcasted_iota(jnp.int32, (BAND, SWA_GRP_Q), 0)
    sink_vecs = []
    for kv in range(SWA_KV_HEADS):
        vec = jnp.full((1, SWA_GRP_Q), sinks_ref[layer, kv * SWA_GROUP + SWA_GROUP - 1], F32)
        for g in range(SWA_GROUP - 2, -1, -1):
            vec = jnp.where(lane_q < (g + 1) * SWA_HD, sinks_ref[layer, kv * SWA_GROUP + g], vec)
        sink_vecs.append(vec * LOG2E)

    def band_rows(c):
        return slice(c * BAND, (c + 1) * BAND) if sample else slice(c * CHUNK, c * CHUNK + BAND)

    def swa_scores(c, kv):
        rs = chunk_rows(c)
        band = band_rows(c)
        ls = slice(kv * LANE, (kv + 1) * LANE)
        qg = (proj_scr[rs, C_QS + kv * SWA_GRP_Q:C_QS + (kv + 1) * SWA_GRP_Q]
              * (SWA_HD ** -0.5 * LOG2E)).astype(BF16)
        q_stack = jnp.concatenate(
            [jnp.where(low_half if hh == 0 else ~low_half, qg[:, pp * LANE:(pp + 1) * LANE], 0.0)
             for pp in range(SWA_GROUP // 2) for hh in range(2)], axis=0)
        s_t = _dot_nt(kd_scr[band, ls], q_stack)
        if not sample and c * CHUNK < WINDOW:
            first_valid = WINDOW - (t * rows + c * CHUNK)
            s_t = jnp.where(key_ids >= first_valid, s_t, -jnp.inf)
        sink = sink_vecs[kv]
        m = jnp.maximum(jnp.max(s_t, axis=0, keepdims=True), sink)
        p_t = jnp.exp2(s_t - m)
        den = jnp.sum(p_t, axis=0, keepdims=True) + jnp.exp2(sink - m)
        pn_scr[c * SWA_KV_HEADS + kv] = (p_t * (1.0 / den)).astype(BF16)

    def swa_output(c, kv):
        rs = chunk_rows(c)
        ls = slice(kv * LANE, (kv + 1) * LANE)
        o_t = _dot_tn(vd_scr[band_rows(c), ls], pn_scr[c * SWA_KV_HEADS + kv]).T
        for pp in range(SWA_GROUP // 2):
            o_pair = jnp.where(low_half, o_t[(2 * pp) * CHUNK:(2 * pp + 1) * CHUNK, :],
                               o_t[(2 * pp + 1) * CHUNK:(2 * pp + 2) * CHUNK, :])
            col = GLA_V + kv * SWA_GRP_Q + pp * LANE
            omix_scr[rs, col:col + LANE] = o_pair.astype(BF16)

    def out_steps(s):
        ss = sub_slice(s)
        base = s * sub_rows

        def piece(lo, hi):
            mix = _dot(omix_scr[ss, :], w_out_ref[:, lo:hi])
            for mi, rs in norm_groups(s):
                local = slice(rs.start - base, rs.stop - base)
                xo_ref[rs, lo:hi] = x_ref[rs, lo:hi] + mod_row(mi, 2)[:, lo:hi] * mix[local, :]

        return [functools.partial(piece, lo, lo + DENSE_PIECE) for lo in range(0, D_MODEL, DENSE_PIECE)]

    def block_steps(s):
        blocks = range(s * sub_chunks, (s + 1) * sub_chunks)
        pairs = [(c, p) for c in blocks for p in range(GLA_PAIRS)]
        groups = [(c, kv) for c in blocks for kv in range(SWA_KV_HEADS)]
        steps = [functools.partial(gla_prepare, c) for c in blocks]
        steps += [functools.partial(swa_scores, c, kv) for c, kv in groups]
        steps += [functools.partial(gla_increment, c) for c in blocks]
        steps += [functools.partial(gla_scores, c) for c in blocks]
        steps.append(functools.partial(gla_recurrence, s))
        for (c, p), (c2, kv) in zip(pairs, groups):
            steps.append(functools.partial(gla_output, c, p))
            steps.append(functools.partial(swa_output, c2, kv))
        return steps

    for step in project_steps(0):
        step()
    for s in range(n_sub):
        dense = (project_steps(s + 1) if s + 1 < n_sub else []) + (out_steps(s - 1) if s > 0 else [])
        work = block_steps(s)
        issued = 0
        for i, step in enumerate(work):
            while issued < len(dense) and (issued - DENSE_LEAD) * len(work) < (i + 1) * len(dense):
                dense[issued]()
                issued += 1
            step()
        for step in dense[issued:]:
            step()
    for step in out_steps(n_sub - 1):
        step()

    if sample:
        ko_ref[...] = proj_scr[:, C_KS:C_KS + SWA_KV]
        vo_ref[...] = proj_scr[:, C_VS:C_VS + SWA_KV]
    else:
        kd_scr[0:WINDOW, :] = kd_scr[rows:rows + WINDOW, :]
        vd_scr[0:WINDOW, :] = vd_scr[rows:rows + WINDOW, :]

        @pl.when(t == n_t - 1)
        def _():
            so_ref[...] = s_scr[...]
            ko_ref[...] = proj_scr[rows - WINDOW:rows, C_KS:C_KS + SWA_KV]
            vo_ref[...] = proj_scr[rows - WINDOW:rows, C_VS:C_VS + SWA_KV]


def _pad_cast_w_in(w_ref, o_ref):
    head = C_LR + GLA_RANK
    w = w_ref[...].astype(F32)
    o_ref[:, 0:C_LR] = w[:, 0:C_LR].astype(BF16)
    gate = jnp.concatenate(
        [w[:, C_LR:head], jnp.zeros((w.shape[0], LR_PAD - GLA_RANK), F32)], axis=1)
    o_ref[:, C_LR:C_QS] = gate.astype(BF16)
    o_ref[:, C_QS:D_IN_P] = w[:, head:D_IN].astype(BF16)


def _cast_weight(w_ref, o_ref):
    o_ref[...] = w_ref[...].astype(BF16)


class _WeightPrep:
    def __init__(self, weight, layer, rows_per_step, out_cols, body):
        self.weight, self.layer, self.rows_per_step = weight, layer, rows_per_step
        self.rows, self.in_cols = weight.shape[1], weight.shape[2]
        self.out_cols, self.body = out_cols, body
        assert self.rows % rows_per_step == 0
        self.n_blocks = self.rows // rows_per_step

    def specs(self, n_t):
        def block(b, t):
            return jnp.minimum(b * n_t + t, self.n_blocks - 1)
        layer = self.layer
        return (pl.BlockSpec((None, self.rows_per_step, self.in_cols), lambda b, t: (layer, block(b, t), 0)),
                pl.BlockSpec((None, self.rows_per_step, self.out_cols), lambda b, t: (0, block(b, t), 0)),
                jax.ShapeDtypeStruct((1, self.rows, self.out_cols), BF16))


def _const_spec(shape):
    zeros = (0,) * len(shape)
    return pl.BlockSpec(shape, lambda *_: zeros, pipeline_mode=pl.Buffered(1))


def _layer_spec(layer, shape):
    index = (layer,) + (0,) * len(shape)
    return pl.BlockSpec((None,) + shape, lambda *_: index, pipeline_mode=pl.Buffered(1))


def _mixer_weight_specs(layer):
    return [
        _layer_spec(layer, (1, D_MODEL)),
        _layer_spec(0, (D_MODEL, D_IN_P)),
        _layer_spec(layer, (LR_PAD, GLA_QK)),
        _layer_spec(layer, (1, GLA_QK)),
        _layer_spec(layer, (1, GLA_DV)),
        pl.BlockSpec(memory_space=pltpu.SMEM),
        _layer_spec(0, (D_MIX, D_MODEL)),
    ]


def _mixer_scratch(rows, n_chunks, band_rows):
    state_rows = GLA_HEADS * GLA_DK
    return [
        pltpu.VMEM((rows, D_MODEL), BF16),
        pltpu.VMEM((rows, D_IN_P), F32),
        pltpu.VMEM((rows, D_MIX), BF16),
        pltpu.VMEM((rows, GLA_QK), BF16),
        pltpu.VMEM((rows, GLA_QK), BF16),
        pltpu.VMEM((rows, GLA_QK), BF16),
        pltpu.VMEM((rows, GLA_QK), BF16),
        pltpu.VMEM((rows, GLA_V), BF16),
        pltpu.VMEM((rows, GLA_QK), F32),
        pltpu.VMEM((n_chunks, state_rows, GLA_DV), F32),
        pltpu.VMEM((n_chunks, state_rows, GLA_DV), F32),
        pltpu.VMEM((n_chunks, state_rows, GLA_DV), BF16),
        pltpu.VMEM((rows, GLA_QK), BF16),
        pltpu.VMEM((n_chunks * SWA_KV_HEADS, BAND, SWA_GRP_Q), BF16),
        pltpu.VMEM((band_rows, 2 * LANE), BF16),
        pltpu.VMEM((band_rows, 2 * LANE), BF16),
    ]


def _mixer_prompt_call(layer, x, mod, mod_row0, weights, preps):
    batch, seq, _ = x.shape
    tile = min(MIXER_TILE, seq)
    assert seq % tile == 0 and tile % CHUNK == 0 and tile >= WINDOW
    n_chunks = tile // CHUNK
    n_t = seq // tile
    state_rows = GLA_HEADS * GLA_DK
    prep_specs = [p.specs(n_t) for p in preps]
    assert all(p.n_blocks <= batch * n_t for p in preps)
    return pl.pallas_call(
        functools.partial(_mixer_kernel, False, n_chunks, layer, tuple(p.body for p in preps)),
        grid=(batch, n_t),
        in_specs=[
            pl.BlockSpec((None, tile, D_MODEL), lambda b, t: (b, t, 0)),
            pl.BlockSpec((None, None, 6, D_MODEL), lambda b, t: (layer, mod_row0 + b, 0, 0)),
        ] + _mixer_weight_specs(layer) + [s[0] for s in prep_specs],
        out_specs=[
            pl.BlockSpec((None, tile, D_MODEL), lambda b, t: (b, t, 0)),
            pl.BlockSpec((None, state_rows, GLA_DV), lambda b, t: (b, 0, 0)),
            pl.BlockSpec((None, WINDOW, SWA_KV), lambda b, t: (b, 0, 0)),
            pl.BlockSpec((None, WINDOW, SWA_KV), lambda b, t: (b, 0, 0)),
        ] + [s[1] for s in prep_specs],
        out_shape=[
            jax.ShapeDtypeStruct((batch, seq, D_MODEL), F32),
            jax.ShapeDtypeStruct((batch, state_rows, GLA_DV), F32),
            jax.ShapeDtypeStruct((batch, WINDOW, SWA_KV), F32),
            jax.ShapeDtypeStruct((batch, WINDOW, SWA_KV), F32),
        ] + [s[2] for s in prep_specs],
        scratch_shapes=_mixer_scratch(tile, n_chunks, WINDOW + tile) + [
            pltpu.VMEM((state_rows, GLA_DV), F32),
        ],
        compiler_params=pltpu.CompilerParams(
            dimension_semantics=("arbitrary", "arbitrary"), vmem_limit_bytes=VMEM_LIMIT),
        name="mixer_prompt",
    )(x, mod, *weights, *[p.weight for p in preps])


def _mixer_sample_call(layer, x, mod, weights, s0, k_cache, v_cache):
    batch, seq, _ = x.shape
    assert seq == CHUNK
    rows = batch * seq
    state_rows = GLA_HEADS * GLA_DK

    def full(shape):
        zeros = (0,) * len(shape)
        return pl.BlockSpec(shape, lambda i: zeros)

    def of_layer(shape):
        index = (layer,) + (0,) * len(shape)
        return pl.BlockSpec((None,) + shape, lambda i: index)

    return pl.pallas_call(
        functools.partial(_mixer_kernel, True, batch, layer, ()),
        grid=(1,),
        in_specs=[full((rows, D_MODEL)), of_layer((batch, 6, D_MODEL))] + _mixer_weight_specs(layer) + [
            of_layer((batch, state_rows, GLA_DV)),
            of_layer((batch, WINDOW, SWA_KV)),
            of_layer((batch, WINDOW, SWA_KV)),
        ],
        out_specs=[
            full((rows, D_MODEL)),
            full((batch, state_rows, GLA_DV)),
            full((rows, SWA_KV)),
            full((rows, SWA_KV)),
        ],
        out_shape=[
            jax.ShapeDtypeStruct((rows, D_MODEL), F32),
            jax.ShapeDtypeStruct((batch, state_rows, GLA_DV), F32),
            jax.ShapeDtypeStruct((rows, SWA_KV), F32),
            jax.ShapeDtypeStruct((rows, SWA_KV), F32),
        ],
        scratch_shapes=_mixer_scratch(rows, batch, batch * BAND),
        compiler_params=pltpu.CompilerParams(
            dimension_semantics=("arbitrary",), vmem_limit_bytes=VMEM_LIMIT),
        name="mixer_sample",
    )(x.reshape(rows, D_MODEL), mod, *weights, s0, k_cache, v_cache)


def _ffn_kernel(sample, final, n_seg, seg_len, preps, *refs):
    if sample:
        (x_ref, mod_ref, g_ffn_ref, w_up_ref, conv_w_ref, conv_b_ref, w_down_ref, g_final_ref,
         past_ref, xo_ref, co_ref, h_scr, act_scr) = refs
    else:
        n_in, n_out, n_prep = 8, 2, len(preps)
        prep_src = refs[n_in:n_in + n_prep]
        prep_dst = refs[n_in + n_prep + n_out:n_in + n_prep + n_out + n_prep]
        refs = refs[:n_in] + refs[n_in + n_prep:n_in + n_prep + n_out] + refs[n_in + 2 * n_prep + n_out:]
        (x_ref, mod_ref, g_ffn_ref, w_up_ref, conv_w_ref, conv_b_ref, w_down_ref, g_final_ref,
         xo_ref, co_ref, h_scr, act_scr, past_scr) = refs
        for body, src, dst in zip(preps, prep_src, prep_dst):
            body(src, dst)
        t = pl.program_id(1)
        n_t = pl.num_programs(1)

        @pl.when(t == 0)
        def _():
            past_scr[...] = jnp.zeros_like(past_scr)

    g_ffn = g_ffn_ref[...]

    def mod_row(c, idx):
        if sample:
            return mod_ref[c, idx:idx + 1, :]
        return mod_ref[idx:idx + 1, :]

    for c in range(n_seg):
        rs = slice(c * seg_len, (c + 1) * seg_len)
        x = x_ref[rs, :]
        gain = g_ffn * (1.0 + mod_row(c, 4))
        h_scr[rs, :] = (x * _rms_scale(x) * gain + mod_row(c, 3)).astype(BF16)

    def up(j):
        lo, hi = FF_PARTS[j]
        h = h_scr[...]
        return _dot(h, w_up_ref[:, lo:hi]), _dot(h, w_up_ref[:, D_FF + lo:D_FF + hi])

    def activate(j, u, val):
        lo, hi = FF_PARTS[j]
        width = hi - lo
        w0 = conv_w_ref[0:1, lo:hi]
        w1 = conv_w_ref[1:2, lo:hi]
        w2 = conv_w_ref[2:3, lo:hi]
        cb = conv_b_ref[:, lo:hi]
        head = 2 * SUBLANE
        row_id = lax.broadcasted_iota(jnp.int32, (head, width), 0)
        for c in range(n_seg):
            rs = slice(c * seg_len, (c + 1) * seg_len)
            past = past_ref[c, :, lo:hi] if sample else past_scr[:, lo:hi]
            u_seg = u[rs, :]
            u1 = pltpu.roll(u_seg, 1, axis=0)
            u2 = pltpu.roll(u_seg, 2, axis=0)
            p0, p1 = past[0:1, :], past[1:2, :]
            u1_top = jnp.where(row_id == 0, p1, u1[0:head, :])
            u2_top = jnp.where(row_id == 0, p0, jnp.where(row_id == 1, p1, u2[0:head, :]))
            uc_top = w0 * u2_top + w1 * u1_top + w2 * u_seg[0:head, :] + cb
            uc = w0 * u2[head:, :] + w1 * u1[head:, :] + w2 * u_seg[head:, :] + cb
            top = slice(rs.start, rs.start + head)
            rest = slice(rs.start + head, rs.stop)
            act_scr[top, lo:hi] = (_silu(uc_top) * val[top, :]).astype(BF16)
            act_scr[rest, lo:hi] = (_silu(uc) * val[rest, :]).astype(BF16)
            tail = u_seg[seg_len - 2:seg_len, :]
            if sample:
                co_ref[c, :, lo:hi] = tail
            else:
                past_scr[:, lo:hi] = tail

    def down(j):
        lo, hi = FF_PARTS[j]
        return _dot(act_scr[:, lo:hi], w_down_ref[lo:hi, :])

    n_parts = len(FF_PARTS)
    pending = {0: up(0)}
    acc = None
    for j in range(n_parts):
        if j + 1 < n_parts:
            pending[j + 1] = up(j + 1)
        activate(j, *pending.pop(j))
        if j > 0:
            d = down(j - 1)
            acc = d if acc is None else acc + d
    acc = acc + down(n_parts - 1)

    for c in range(n_seg):
        rs = slice(c * seg_len, (c + 1) * seg_len)
        y = x_ref[rs, :] + mod_row(c, 5) * acc[rs, :]
        if final:
            y = y * _rms_scale(y) * g_final_ref[...]
        xo_ref[rs, :] = y

    if not sample:
        @pl.when(t == n_t - 1)
        def _():
            co_ref[...] = past_scr[...]


def _ffn_weight_specs(layer):
    return [
        _layer_spec(layer, (1, D_MODEL)),
        _layer_spec(0, (D_MODEL, 2 * D_FF)),
        _layer_spec(layer, (CONV_W, D_FF)),
        _layer_spec(layer, (1, D_FF)),
        _layer_spec(0, (D_FF, D_MODEL)),
        _const_spec((1, D_MODEL)),
    ]


def _ffn_prompt_call(layer, x, mod, mod_row0, weights, final, preps):
    batch, seq, _ = x.shape
    tile = min(PROMPT_TILE, seq)
    n_t = seq // tile
    prep_specs = [p.specs(n_t) for p in preps]
    assert all(p.n_blocks <= batch * n_t for p in preps)
    return pl.pallas_call(
        functools.partial(_ffn_kernel, False, final, 1, tile, tuple(p.body for p in preps)),
        grid=(batch, n_t),
        in_specs=[
            pl.BlockSpec((None, tile, D_MODEL), lambda b, t: (b, t, 0)),
            pl.BlockSpec((None, None, 6, D_MODEL), lambda b, t: (layer, mod_row0 + b, 0, 0)),
        ] + _ffn_weight_specs(layer) + [s[0] for s in prep_specs],
        out_specs=[
            pl.BlockSpec((None, tile, D_MODEL), lambda b, t: (b, t, 0)),
            pl.BlockSpec((None, CONV_W - 1, D_FF), lambda b, t: (b, 0, 0)),
        ] + [s[1] for s in prep_specs],
        out_shape=[
            jax.ShapeDtypeStruct((batch, seq, D_MODEL), F32),
            jax.ShapeDtypeStruct((batch, CONV_W - 1, D_FF), F32),
        ] + [s[2] for s in prep_specs],
        scratch_shapes=[
            pltpu.VMEM((tile, D_MODEL), BF16),
            pltpu.VMEM((tile, D_FF), BF16),
            pltpu.VMEM((CONV_W - 1, D_FF), F32),
        ],
        compiler_params=pltpu.CompilerParams(
            dimension_semantics=("arbitrary", "arbitrary"), vmem_limit_bytes=VMEM_LIMIT),
        name="ffn_prompt",
    )(x, mod, *weights, *[p.weight for p in preps])


def _ffn_sample_call(layer, x2d, mod, weights, past, final, batch, seq):
    rows = batch * seq

    def full(shape):
        zeros = (0,) * len(shape)
        return pl.BlockSpec(shape, lambda i: zeros)

    def of_layer(shape):
        index = (layer,) + (0,) * len(shape)
        return pl.BlockSpec((None,) + shape, lambda i: index)

    return pl.pallas_call(
        functools.partial(_ffn_kernel, True, final, batch, seq, ()),
        grid=(1,),
        in_specs=[full((rows, D_MODEL)), of_layer((batch, 6, D_MODEL))] + _ffn_weight_specs(layer) + [
            of_layer((batch, CONV_W - 1, D_FF)),
        ],
        out_specs=[full((rows, D_MODEL)), full((batch, CONV_W - 1, D_FF))],
        out_shape=[
            jax.ShapeDtypeStruct((rows, D_MODEL), F32),
            jax.ShapeDtypeStruct((batch, CONV_W - 1, D_FF), F32),
        ],
        scratch_shapes=[
            pltpu.VMEM((rows, D_MODEL), BF16),
            pltpu.VMEM((rows, D_FF), BF16),
        ],
        compiler_params=pltpu.CompilerParams(
            dimension_semantics=("arbitrary",), vmem_limit_bytes=VMEM_LIMIT),
        name="ffn_sample",
    )(x2d, mod, *weights, past)


def _pad_w_in_first(w):
    n_blocks = D_MODEL // W_PREP_ROWS
    return pl.pallas_call(
        _pad_cast_w_in,
        grid=(n_blocks,),
        in_specs=[pl.BlockSpec((None, W_PREP_ROWS, D_IN), lambda r: (0, r, 0))],
        out_specs=pl.BlockSpec((None, W_PREP_ROWS, D_IN_P), lambda r: (0, r, 0)),
        out_shape=jax.ShapeDtypeStruct((1, D_MODEL, D_IN_P), BF16),
        compiler_params=pltpu.CompilerParams(
            dimension_semantics=("arbitrary",), vmem_limit_bytes=VMEM_LIMIT),
        name="pad_w_in",
    )(w)


def kernel(x_prompt, x_sample, state_gla, cache_swa_k, cache_swa_v, state_conv, c_prompt, c_sample,
           w_ada, b_ada, g_attn, g_ffn, w_in, w_gk2, b_gk, g_gla, sinks, w_out, w_up, conv_w, conv_b,
           w_down, g_final):
    depth = w_ada.shape[0]
    batch, seq, _ = x_prompt.shape
    dec_batch, dec_seq, _ = x_sample.shape
    state_rows = GLA_HEADS * GLA_DK

    c_all = jnp.concatenate(
        [c_sample, c_prompt, jnp.zeros((ADA_ROWS - batch - dec_batch, D_MODEL), F32)], axis=0)
    mod_all = _ada_call(c_all, w_ada, b_ada).reshape(depth, ADA_ROWS, 6, D_MODEL)

    w_gk2_p = jnp.concatenate(
        [w_gk2, jnp.zeros((depth, LR_PAD - GLA_RANK, GLA_QK), F32)], axis=1).astype(BF16)
    w_in16 = w_in.astype(BF16)
    w_in_b, w_out_b = _pad_w_in_first(w_in16), w_out[:1].astype(BF16)
    mixer_steps = batch * (seq // min(MIXER_TILE, seq))
    ffn_steps = batch * (seq // min(PROMPT_TILE, seq))
    prep_ahead = (D_MODEL // MIXER_PREP_ROWS_UP <= mixer_steps
                  and D_FF // MIXER_PREP_ROWS_DOWN <= mixer_steps
                  and D_MODEL // FFN_PREP_ROWS <= ffn_steps)
    s0_all = state_gla.reshape(depth, dec_batch, state_rows, GLA_DV)
    kc_all = cache_swa_k.reshape(depth, dec_batch, WINDOW, SWA_KV)
    vc_all = cache_swa_v.reshape(depth, dec_batch, WINDOW, SWA_KV)

    yp = x_prompt
    ys = x_sample.reshape(dec_batch * dec_seq, D_MODEL)
    outs = [[] for _ in range(8)]
    for i in range(depth):
        final = i == depth - 1
        mixer_w = (g_attn[:, None], w_in_b, w_gk2_p, b_gk[:, None], g_gla[:, None], sinks, w_out_b)
        mixer_preps, ffn_preps = [], []
        if prep_ahead:
            mixer_preps = [_WeightPrep(w_up, i, MIXER_PREP_ROWS_UP, 2 * D_FF, _cast_weight),
                           _WeightPrep(w_down, i, MIXER_PREP_ROWS_DOWN, D_MODEL, _cast_weight)]
            if not final:
                ffn_preps = [_WeightPrep(w_in16, i + 1, FFN_PREP_ROWS, D_IN_P, _pad_cast_w_in),
                             _WeightPrep(w_out, i + 1, FFN_PREP_ROWS, D_MODEL, _cast_weight)]

        yp, s_p, k_p, v_p, *prepared = _mixer_prompt_call(i, yp, mod_all, dec_batch, mixer_w, mixer_preps)
        w_up_b, w_down_b = prepared or (w_up[i:i + 1].astype(BF16), w_down[i:i + 1].astype(BF16))
        ffn_w = (g_ffn[:, None], w_up_b, conv_w, conv_b[:, None], w_down_b, g_final[None])
        yp, conv_p, *prepared = _ffn_prompt_call(i, yp, mod_all, dec_batch, ffn_w, final, ffn_preps)

        ys, s_s, k_s, v_s = _mixer_sample_call(
            i, ys.reshape(dec_batch, dec_seq, D_MODEL), mod_all, mixer_w, s0_all, kc_all, vc_all)
        ys, conv_s = _ffn_sample_call(i, ys, mod_all, ffn_w, state_conv, final, dec_batch, dec_seq)
        if not final:
            w_in_b, w_out_b = prepared or (
                _pad_w_in_first(w_in16[i + 1:i + 2]), w_out[i + 1:i + 2].astype(BF16))

        keep = min(WINDOW, seq)
        outs[0].append(s_p.reshape(batch, GLA_HEADS, GLA_DK, GLA_DV))
        outs[1].append(k_p.reshape(batch, keep, SWA_KV_HEADS, SWA_HD))
        outs[2].append(v_p.reshape(batch, keep, SWA_KV_HEADS, SWA_HD))
        outs[3].append(conv_p)
        outs[4].append(s_s.reshape(dec_batch, GLA_HEADS, GLA_DK, GLA_DV))
        outs[5].append(k_s.reshape(dec_batch, dec_seq, SWA_KV_HEADS, SWA_HD))
        outs[6].append(v_s.reshape(dec_batch, dec_seq, SWA_KV_HEADS, SWA_HD))
        outs[7].append(conv_s)

    return (yp, ys.reshape(dec_batch, dec_seq, D_MODEL)) + tuple(jnp.stack(o) for o in outs)
```

```python
import functools

import jax
import jax.numpy as jnp
from jax import lax
from jax.experimental import pallas as pl
from jax.experimental.pallas import tpu as pltpu

F32 = jnp.float32
BF16 = jnp.bfloat16

D_MODEL = 1024
CHUNK = 64
GLA_HEADS = 4
GLA_DK = 64
GLA_DV = 128
GLA_RANK = 16
GLA_NORMALIZER = 16.0
SWA_Q_HEADS = 8
SWA_KV_HEADS = 2
SWA_GROUP = SWA_Q_HEADS // SWA_KV_HEADS
SWA_HD = 64
WINDOW = 128
D_FF = 2816
CONV_W = 3
RMS_EPS = 1e-6

GLA_QK = GLA_HEADS * GLA_DK
GLA_V = GLA_HEADS * GLA_DV
SWA_Q = SWA_Q_HEADS * SWA_HD
SWA_KV = SWA_KV_HEADS * SWA_HD
D_MIX = GLA_V + SWA_Q
BAND = WINDOW + CHUNK

LANE = 128
SUBLANE = 8

C_QG = 0
C_KG = C_QG + GLA_QK
C_VG = C_KG + GLA_QK
C_OG = C_VG + GLA_V
C_LR = C_OG + GLA_V
LR_PAD = LANE
C_QS = C_LR + LR_PAD
C_KS = C_QS + SWA_Q
C_VS = C_KS + SWA_KV
D_IN_P = C_VS + SWA_KV
D_IN = 2 * GLA_QK + 2 * GLA_V + GLA_RANK + SWA_Q + 2 * SWA_KV

assert GLA_DK * 2 == LANE and SWA_HD * 2 == LANE and GLA_DV == LANE and SWA_KV == LANE
GLA_PAIRS = GLA_HEADS // 2
PAIR_V = 2 * GLA_DV
SWA_GRP_Q = SWA_GROUP * SWA_HD

ADA_TILE = 1536
ADA_ROWS = 16
PROMPT_TILE = 512
MIXER_TILE = 1024
SUB_CHUNKS = 4
DENSE_PIECE = 256
DENSE_LEAD = 2
W_PREP_ROWS = 256
MIXER_PREP_ROWS_UP = 32
MIXER_PREP_ROWS_DOWN = 128
FFN_PREP_ROWS = 16
FF_PART = 768
FF_PARTS = tuple((lo, min(lo + FF_PART, D_FF)) for lo in range(0, D_FF, FF_PART))
STREAM_BUFFERS = 3
VMEM_LIMIT = 60 * 1024 * 1024

LOG2E = 1.4426950408889634
NT_DIMS = (((1,), (1,)), ((), ()))
TN_DIMS = (((0,), (0,)), ((), ()))


def _dot(a, b):
    return jnp.dot(a, b, preferred_element_type=F32)


def _dot_nt(a, b):
    return lax.dot_general(a, b, NT_DIMS, preferred_element_type=F32)


def _dot_tn(a, b):
    return lax.dot_general(a, b, TN_DIMS, preferred_element_type=F32)


def _split_bf16(a):
    hi = a.astype(BF16)
    lo = (a - hi.astype(F32)).astype(BF16)
    return hi, lo


def _silu(a):
    half = 0.5 * a
    return half + half * jnp.tanh(half)


def _log_sigmoid(a):
    return jnp.minimum(a, 0.0) - jnp.log(1.0 + jnp.exp2(jnp.abs(a) * (-LOG2E)))


def _rms_scale(a):
    return lax.rsqrt(jnp.mean(a * a, axis=-1, keepdims=True) + RMS_EPS)


def _ada_kernel(c_ref, w_ref, b_ref, o_ref):
    c = c_ref[...]
    a = _silu(c).astype(BF16)
    o_ref[...] = _dot(a, w_ref[...].astype(BF16)) + b_ref[...]


def _ada_call(c_all, w_ada, b_ada):
    depth = w_ada.shape[0]
    n_tiles = (6 * D_MODEL) // ADA_TILE
    return pl.pallas_call(
        _ada_kernel,
        grid=(depth, n_tiles),
        in_specs=[
            pl.BlockSpec((ADA_ROWS, D_MODEL), lambda l, j: (0, 0)),
            pl.BlockSpec((None, D_MODEL, ADA_TILE), lambda l, j: (l, 0, j)),
            pl.BlockSpec((None, 1, ADA_TILE), lambda l, j: (l, 0, j)),
        ],
        out_specs=pl.BlockSpec((None, ADA_ROWS, ADA_TILE), lambda l, j: (l, 0, j)),
        out_shape=jax.ShapeDtypeStruct((depth, ADA_ROWS, 6 * D_MODEL), F32),
        compiler_params=pltpu.CompilerParams(
            dimension_semantics=("arbitrary", "arbitrary"), vmem_limit_bytes=VMEM_LIMIT),
        name="adaln_mod",
    )(c_all, w_ada, b_ada.reshape(depth, 1, 6 * D_MODEL))


def _dup_halves(a, low_half):
    swapped = pltpu.roll(a, LANE // 2, axis=1)
    return jnp.where(low_half, a, swapped), jnp.where(low_half, swapped, a)


def _mixer_kernel(sample, n_chunks, layer, preps, *refs):
    if sample:
        (x_ref, mod_ref, g_attn_ref, w_in_ref, w_gk2_ref, b_gk_ref, g_gla_ref, sinks_ref, w_out_ref,
         s0_ref, kc_ref, vc_ref,
         xo_ref, so_ref, ko_ref, vo_ref,
         h_scr, proj_scr, omix_scr, qm_scr, km_scr, qi_scr, ko_scr, vb_scr, gk_scr, dec_scr, u_scr, sb_scr,
         att_scr, pn_scr,
         kd_scr, vd_scr) = refs
        t = None
    else:
        n_in, n_out, n_prep = 9, 4, len(preps)
        prep_src = refs[n_in:n_in + n_prep]
        prep_dst = refs[n_in + n_prep + n_out:n_in + n_prep + n_out + n_prep]
        refs = refs[:n_in] + refs[n_in + n_prep:n_in + n_prep + n_out] + refs[n_in + 2 * n_prep + n_out:]
        (x_ref, mod_ref, g_attn_ref, w_in_ref, w_gk2_ref, b_gk_ref, g_gla_ref, sinks_ref, w_out_ref,
         xo_ref, so_ref, ko_ref, vo_ref,
         h_scr, proj_scr, omix_scr, qm_scr, km_scr, qi_scr, ko_scr, vb_scr, gk_scr, dec_scr, u_scr, sb_scr,
         att_scr, pn_scr,
         kd_scr, vd_scr, s_scr, x_buf, x_sem) = refs
        for body, src, dst in zip(preps, prep_src, prep_dst):
            body(src, dst)
        t = pl.program_id(1)
        n_t = pl.num_programs(1)

        x_hbm = x_ref
        tile_rows = n_chunks * CHUNK
        step = pl.program_id(0) * n_t + t
        n_steps = pl.num_programs(0) * n_t

        def x_copy(i):
            slot = lax.rem(i, STREAM_BUFFERS)
            return pltpu.make_async_copy(
                x_hbm.at[i // n_t, pl.ds(lax.rem(i, n_t) * tile_rows, tile_rows), :],
                x_buf.at[slot], x_sem.at[slot])

        @pl.when(step == 0)
        def _():
            for i in range(STREAM_BUFFERS - 1):
                @pl.when(i < n_steps)
                def _():
                    x_copy(i).start()

        x_copy(step).wait()

        @pl.when(step + STREAM_BUFFERS - 1 < n_steps)
        def _():
            x_copy(step + STREAM_BUFFERS - 1).start()

        x_ref = x_buf.at[lax.rem(step, STREAM_BUFFERS)]

        @pl.when(t == 0)
        def _():
            s_scr[...] = jnp.zeros_like(s_scr)
            kd_scr[0:WINDOW, :] = jnp.zeros((WINDOW, 2 * LANE), BF16)
            vd_scr[0:WINDOW, :] = jnp.zeros((WINDOW, 2 * LANE), BF16)

    rows = n_chunks * CHUNK
    g_attn = g_attn_ref[...]

    def mod_row(c, idx):
        if sample:
            return mod_ref[c, idx:idx + 1, :]
        return mod_ref[idx:idx + 1, :]

    def chunk_rows(c):
        return slice(c * CHUNK, (c + 1) * CHUNK)

    sub_chunks = min(SUB_CHUNKS, n_chunks)
    n_sub = n_chunks // sub_chunks
    sub_rows = sub_chunks * CHUNK

    def sub_slice(s):
        return slice(s * sub_rows, (s + 1) * sub_rows)

    def norm_groups(s):
        if sample:
            return [(c, chunk_rows(c)) for c in range(s * sub_chunks, (s + 1) * sub_chunks)]
        return [(0, sub_slice(s))]

    r64 = lax.broadcasted_iota(jnp.int32, (CHUNK, CHUNK), 0)
    c64 = lax.broadcasted_iota(jnp.int32, (CHUNK, CHUNK), 1)
    tri = jnp.where(r64 >= c64, 1.0, 0.0).astype(BF16)
    tri2 = jnp.concatenate([tri, tri], axis=1)
    l_pair = lax.broadcasted_iota(jnp.int32, (CHUNK, LANE), 1)
    low_half = l_pair < LANE // 2
    causal_heads = (lax.broadcasted_iota(jnp.int32, (CHUNK, GLA_QK), 0)
                    >= (lax.broadcasted_iota(jnp.int32, (CHUNK, GLA_QK), 1) & (CHUNK - 1)))
    diag_heads = ((lax.broadcasted_iota(jnp.int32, (GLA_QK, GLA_QK), 0) // GLA_DK)
                  == (lax.broadcasted_iota(jnp.int32, (GLA_QK, GLA_QK), 1) // GLA_DK))
    diag_v = ((lax.broadcasted_iota(jnp.int32, (LANE, PAIR_V), 0) >= LANE // 2)
              == (lax.broadcasted_iota(jnp.int32, (LANE, PAIR_V), 1) >= GLA_DV))
    g_gla = g_gla_ref[...]
    low_sub = lax.broadcasted_iota(jnp.int32, (sub_rows, LANE), 1) < LANE // 2

    def project_steps(s):
        ss = sub_slice(s)

        def norm():
            for mi, rs in norm_groups(s):
                x = x_ref[rs, :]
                gain = g_attn * (1.0 + mod_row(mi, 1))
                h_scr[rs, :] = (x * _rms_scale(x) * gain + mod_row(mi, 0)).astype(BF16)

        def piece(lo, hi):
            proj_scr[ss, lo:hi] = _dot(h_scr[ss, :], w_in_ref[:, lo:hi])

        def gates():
            lr = proj_scr[ss, C_LR:C_LR + LR_PAD].astype(BF16)
            gk_scr[ss, :] = (_log_sigmoid(_dot(lr, w_gk2_ref[...]) + b_gk_ref[...])
                             * (LOG2E / GLA_NORMALIZER))
            vb_scr[ss, :] = proj_scr[ss, C_VG:C_VG + GLA_V].astype(BF16)

        def bands():
            k_dup = _dup_halves(proj_scr[ss, C_KS:C_KS + SWA_KV], low_sub)
            v_dup = _dup_halves(proj_scr[ss, C_VS:C_VS + SWA_KV], low_sub)
            for kv in range(SWA_KV_HEADS):
                ls = slice(kv * LANE, (kv + 1) * LANE)
                if sample:
                    for c in range(s * sub_chunks, (s + 1) * sub_chunks):
                        local = slice((c - s * sub_chunks) * CHUNK, (c - s * sub_chunks + 1) * CHUNK)
                        kd_scr[c * BAND + WINDOW:(c + 1) * BAND, ls] = k_dup[kv][local, :].astype(BF16)
                        vd_scr[c * BAND + WINDOW:(c + 1) * BAND, ls] = v_dup[kv][local, :].astype(BF16)
                else:
                    band_rows = slice(WINDOW + s * sub_rows, WINDOW + (s + 1) * sub_rows)
                    kd_scr[band_rows, ls] = k_dup[kv].astype(BF16)
                    vd_scr[band_rows, ls] = v_dup[kv].astype(BF16)

        steps = [norm]
        for lo in range(0, D_IN_P, DENSE_PIECE):
            steps.append(functools.partial(piece, lo, min(lo + DENSE_PIECE, D_IN_P)))
        return steps + [gates, bands]

    def gla_prepare(c):
        rs = chunk_rows(c)
        gk_hi, gk_lo = _split_bf16(gk_scr[rs, :])
        cum = _dot(tri2, jnp.concatenate([gk_hi, gk_lo], axis=0))
        mid = cum[CHUNK // 2:CHUNK // 2 + 1, :]
        last = cum[CHUNK - 1:CHUNK, :]
        q = proj_scr[rs, C_QG:C_QG + GLA_QK] * (GLA_DK ** -0.5)
        k = proj_scr[rs, C_KG:C_KG + GLA_QK]
        qm_scr[rs, :] = (q * jnp.exp2(cum - mid)).astype(BF16)
        km_scr[rs, :] = (k * jnp.exp2(mid - cum)).astype(BF16)
        qi_scr[rs, :] = (q * jnp.exp2(cum)).astype(BF16)
        ko_scr[rs, :] = (k * jnp.exp2(last - cum)).astype(BF16)
        dec_scr[c] = jnp.broadcast_to(jnp.exp2(last), (LANE, GLA_QK)).T

    def gla_increment(c):
        rs = chunk_rows(c)
        for p in range(GLA_PAIRS):
            upd = _dot_tn(ko_scr[rs, p * LANE:(p + 1) * LANE], vb_scr[rs, p * PAIR_V:(p + 1) * PAIR_V])
            u_scr[c, p * LANE:p * LANE + GLA_DK, :] = upd[0:GLA_DK, 0:GLA_DV]
            u_scr[c, p * LANE + GLA_DK:(p + 1) * LANE, :] = upd[GLA_DK:LANE, GLA_DV:PAIR_V]

    def gla_recurrence(s):
        state = None if sample else s_scr[...]
        for c in range(s * sub_chunks, (s + 1) * sub_chunks):
            if sample:
                state = s0_ref[c]
            sb_scr[c] = state.astype(BF16)
            state = dec_scr[c] * state + u_scr[c]
            if sample:
                so_ref[c] = state
        if not sample:
            s_scr[...] = state

    def gla_scores(c):
        rs = chunk_rows(c)
        km = km_scr[rs, :]
        k_bd = jnp.where(diag_heads, jnp.concatenate([km] * GLA_HEADS, axis=0), 0.0)
        att_scr[rs, :] = jnp.where(causal_heads, _dot_nt(qm_scr[rs, :], k_bd), 0.0).astype(BF16)

    def gla_output(c, p):
        rs = chunk_rows(c)
        ls = slice(p * LANE, (p + 1) * LANE)
        v_pair = vb_scr[rs, p * PAIR_V:(p + 1) * PAIR_V]
        s_pair = sb_scr[c, ls, :]
        w_top = jnp.where(diag_v, jnp.concatenate([v_pair, v_pair], axis=0), 0.0)
        w_bot = jnp.where(diag_v, jnp.concatenate([s_pair, s_pair], axis=1), 0.0)
        o_pair = _dot(jnp.concatenate([att_scr[rs, ls], qi_scr[rs, ls]], axis=1),
                      jnp.concatenate([w_top, w_bot], axis=0))
        for hh in range(2):
            h = 2 * p + hh
            o = o_pair[:, hh * GLA_DV:(hh + 1) * GLA_DV]
            og = proj_scr[rs, C_OG + h * GLA_DV:C_OG + (h + 1) * GLA_DV]
            y = o * _rms_scale(o) * g_gla * _silu(og)
            omix_scr[rs, h * GLA_DV:(h + 1) * GLA_DV] = y.astype(BF16)

    if sample:
        low_win = lax.broadcasted_iota(jnp.int32, (WINDOW, LANE), 1) < LANE // 2
        for c in range(n_chunks):
            kc_dup = _dup_halves(kc_ref[c], low_win)
            vc_dup = _dup_halves(vc_ref[c], low_win)
            for kv in range(SWA_KV_HEADS):
                ls = slice(kv * LANE, (kv + 1) * LANE)
                kd_scr[c * BAND:c * BAND + WINDOW, ls] = kc_dup[kv].astype(BF16)
                vd_scr[c * BAND:c * BAND + WINDOW, ls] = vc_dup[kv].astype(BF16)

    lane_q = lax.broadcasted_iota(jnp.int32, (1, SWA_GRP_Q), 1)
    key_ids = lax.broadcasted_iota(jnp.int32, (BAND, SWA_GRP_Q), 0)
    sink_vecs = []
    for kv in range(SWA_KV_HEADS):
        vec = jnp.full((1, SWA_GRP_Q), sinks_ref[layer, kv * SWA_GROUP + SWA_GROUP - 1], F32)
        for g in range(SWA_GROUP - 2, -1, -1):
            vec = jnp.where(lane_q < (g + 1) * SWA_HD, sinks_ref[layer, kv * SWA_GROUP + g], vec)
        sink_vecs.append(vec * LOG2E)

    def band_rows(c):
        return slice(c * BAND, (c + 1) * BAND) if sample else slice(c * CHUNK, c * CHUNK + BAND)

    def swa_scores(c, kv):
        rs = chunk_rows(c)
        band = band_rows(c)
        ls = slice(kv * LANE, (kv + 1) * LANE)
        qg = (proj_scr[rs, C_QS + kv * SWA_GRP_Q:C_QS + (kv + 1) * SWA_GRP_Q]
              * (SWA_HD ** -0.5 * LOG2E)).astype(BF16)
        q_stack = jnp.concatenate(
            [jnp.where(low_half if hh == 0 else ~low_half, qg[:, pp * LANE:(pp + 1) * LANE], 0.0)
             for pp in range(SWA_GROUP // 2) for hh in range(2)], axis=0)
        s_t = _dot_nt(kd_scr[band, ls], q_stack)
        if not sample and c * CHUNK < WINDOW:
            first_valid = WINDOW - (t * rows + c * CHUNK)
            s_t = jnp.where(key_ids >= first_valid, s_t, -jnp.inf)
        sink = sink_vecs[kv]
        m = jnp.maximum(jnp.max(s_t, axis=0, keepdims=True), sink)
        p_t = jnp.exp2(s_t - m)
        den = jnp.sum(p_t, axis=0, keepdims=True) + jnp.exp2(sink - m)
        pn_scr[c * SWA_KV_HEADS + kv] = (p_t * (1.0 / den)).astype(BF16)

    def swa_output(c, kv):
        rs = chunk_rows(c)
        ls = slice(kv * LANE, (kv + 1) * LANE)
        o_t = _dot_tn(vd_scr[band_rows(c), ls], pn_scr[c * SWA_KV_HEADS + kv]).T
        for pp in range(SWA_GROUP // 2):
            o_pair = jnp.where(low_half, o_t[(2 * pp) * CHUNK:(2 * pp + 1) * CHUNK, :],
                               o_t[(2 * pp + 1) * CHUNK:(2 * pp + 2) * CHUNK, :])
            col = GLA_V + kv * SWA_GRP_Q + pp * LANE
            omix_scr[rs, col:col + LANE] = o_pair.astype(BF16)

    def out_steps(s):
        ss = sub_slice(s)
        base = s * sub_rows

        def piece(lo, hi):
            mix = _dot(omix_scr[ss, :], w_out_ref[:, lo:hi])
            for mi, rs in norm_groups(s):
                local = slice(rs.start - base, rs.stop - base)
                xo_ref[rs, lo:hi] = x_ref[rs, lo:hi] + mod_row(mi, 2)[:, lo:hi] * mix[local, :]

        return [functools.partial(piece, lo, lo + DENSE_PIECE) for lo in range(0, D_MODEL, DENSE_PIECE)]

    def block_steps(s):
        blocks = range(s * sub_chunks, (s + 1) * sub_chunks)
        pairs = [(c, p) for c in blocks for p in range(GLA_PAIRS)]
        groups = [(c, kv) for c in blocks for kv in range(SWA_KV_HEADS)]
        steps = [functools.partial(gla_prepare, c) for c in blocks]
        steps += [functools.partial(swa_scores, c, kv) for c, kv in groups]
        steps += [functools.partial(gla_increment, c) for c in blocks]
        steps += [functools.partial(gla_scores, c) for c in blocks]
        steps.append(functools.partial(gla_recurrence, s))
        for (c, p), (c2, kv) in zip(pairs, groups):
            steps.append(functools.partial(gla_output, c, p))
            steps.append(functools.partial(swa_output, c2, kv))
        return steps

    for step in project_steps(0):
        step()
    for s in range(n_sub):
        dense = (project_steps(s + 1) if s + 1 < n_sub else []) + (out_steps(s - 1) if s > 0 else [])
        work = block_steps(s)
        issued = 0
        for i, step in enumerate(work):
            while issued < len(dense) and (issued - DENSE_LEAD) * len(work) < (i + 1) * len(dense):
                dense[issued]()
                issued += 1
            step()
        for step in dense[issued:]:
            step()
    for step in out_steps(n_sub - 1):
        step()

    if sample:
        ko_ref[...] = proj_scr[:, C_KS:C_KS + SWA_KV]
        vo_ref[...] = proj_scr[:, C_VS:C_VS + SWA_KV]
    else:
        kd_scr[0:WINDOW, :] = kd_scr[rows:rows + WINDOW, :]
        vd_scr[0:WINDOW, :] = vd_scr[rows:rows + WINDOW, :]

        @pl.when(t == n_t - 1)
        def _():
            so_ref[...] = s_scr[...]
            ko_ref[...] = proj_scr[rows - WINDOW:rows, C_KS:C_KS + SWA_KV]
            vo_ref[...] = proj_scr[rows - WINDOW:rows, C_VS:C_VS + SWA_KV]


def _pad_cast_w_in(w_ref, o_ref):
    head = C_LR + GLA_RANK
    w = w_ref[...].astype(F32)
    o_ref[:, 0:C_LR] = w[:, 0:C_LR].astype(BF16)
    gate = jnp.concatenate(
        [w[:, C_LR:head], jnp.zeros((w.shape[0], LR_PAD - GLA_RANK), F32)], axis=1)
    o_ref[:, C_LR:C_QS] = gate.astype(BF16)
    o_ref[:, C_QS:D_IN_P] = w[:, head:D_IN].astype(BF16)


def _cast_weight(w_ref, o_ref):
    o_ref[...] = w_ref[...].astype(BF16)


class _WeightPrep:
    def __init__(self, weight, layer, rows_per_step, out_cols, body):
        self.weight, self.layer, self.rows_per_step = weight, layer, rows_per_step
        self.rows, self.in_cols = weight.shape[1], weight.shape[2]
        self.out_cols, self.body = out_cols, body
        assert self.rows % rows_per_step == 0
        self.n_blocks = self.rows // rows_per_step

    def specs(self, n_t):
        def block(b, t):
            return jnp.minimum(b * n_t + t, self.n_blocks - 1)
        layer = self.layer
        return (pl.BlockSpec((None, self.rows_per_step, self.in_cols), lambda b, t: (layer, block(b, t), 0)),
                pl.BlockSpec((None, self.rows_per_step, self.out_cols), lambda b, t: (0, block(b, t), 0)),
                jax.ShapeDtypeStruct((1, self.rows, self.out_cols), BF16))


def _const_spec(shape):
    zeros = (0,) * len(shape)
    return pl.BlockSpec(shape, lambda *_: zeros, pipeline_mode=pl.Buffered(1))


def _layer_spec(layer, shape):
    index = (layer,) + (0,) * len(shape)
    return pl.BlockSpec((None,) + shape, lambda *_: index, pipeline_mode=pl.Buffered(1))


def _mixer_weight_specs(layer):
    return [
        _layer_spec(layer, (1, D_MODEL)),
        _layer_spec(0, (D_MODEL, D_IN_P)),
        _layer_spec(layer, (LR_PAD, GLA_QK)),
        _layer_spec(layer, (1, GLA_QK)),
        _layer_spec(layer, (1, GLA_DV)),
        pl.BlockSpec(memory_space=pltpu.SMEM),
        _layer_spec(0, (D_MIX, D_MODEL)),
    ]


def _mixer_scratch(rows, n_chunks, band_rows):
    state_rows = GLA_HEADS * GLA_DK
    return [
        pltpu.VMEM((rows, D_MODEL), BF16),
        pltpu.VMEM((rows, D_IN_P), F32),
        pltpu.VMEM((rows, D_MIX), BF16),
        pltpu.VMEM((rows, GLA_QK), BF16),
        pltpu.VMEM((rows, GLA_QK), BF16),
        pltpu.VMEM((rows, GLA_QK), BF16),
        pltpu.VMEM((rows, GLA_QK), BF16),
        pltpu.VMEM((rows, GLA_V), BF16),
        pltpu.VMEM((rows, GLA_QK), F32),
        pltpu.VMEM((n_chunks, state_rows, GLA_DV), F32),
        pltpu.VMEM((n_chunks, state_rows, GLA_DV), F32),
        pltpu.VMEM((n_chunks, state_rows, GLA_DV), BF16),
        pltpu.VMEM((rows, GLA_QK), BF16),
        pltpu.VMEM((n_chunks * SWA_KV_HEADS, BAND, SWA_GRP_Q), BF16),
        pltpu.VMEM((band_rows, 2 * LANE), BF16),
        pltpu.VMEM((band_rows, 2 * LANE), BF16),
    ]


def _mixer_prompt_call(layer, x, mod, mod_row0, weights, preps):
    batch, seq, _ = x.shape
    tile = min(MIXER_TILE, seq)
    assert seq % tile == 0 and tile % CHUNK == 0 and tile >= WINDOW
    n_chunks = tile // CHUNK
    n_t = seq // tile
    state_rows = GLA_HEADS * GLA_DK
    prep_specs = [p.specs(n_t) for p in preps]
    assert all(p.n_blocks <= batch * n_t for p in preps)
    return pl.pallas_call(
        functools.partial(_mixer_kernel, False, n_chunks, layer, tuple(p.body for p in preps)),
        grid=(batch, n_t),
        in_specs=[
            pl.BlockSpec(memory_space=pl.ANY),
            pl.BlockSpec((None, None, 6, D_MODEL), lambda b, t: (layer, mod_row0 + b, 0, 0)),
        ] + _mixer_weight_specs(layer) + [s[0] for s in prep_specs],
        out_specs=[
            pl.BlockSpec((None, tile, D_MODEL), lambda b, t: (b, t, 0)),
            pl.BlockSpec((None, state_rows, GLA_DV), lambda b, t: (b, 0, 0)),
            pl.BlockSpec((None, WINDOW, SWA_KV), lambda b, t: (b, 0, 0)),
            pl.BlockSpec((None, WINDOW, SWA_KV), lambda b, t: (b, 0, 0)),
        ] + [s[1] for s in prep_specs],
        out_shape=[
            jax.ShapeDtypeStruct((batch, seq, D_MODEL), F32),
            jax.ShapeDtypeStruct((batch, state_rows, GLA_DV), F32),
            jax.ShapeDtypeStruct((batch, WINDOW, SWA_KV), F32),
            jax.ShapeDtypeStruct((batch, WINDOW, SWA_KV), F32),
        ] + [s[2] for s in prep_specs],
        scratch_shapes=_mixer_scratch(tile, n_chunks, WINDOW + tile) + [
            pltpu.VMEM((state_rows, GLA_DV), F32),
            pltpu.VMEM((STREAM_BUFFERS, tile, D_MODEL), F32),
            pltpu.SemaphoreType.DMA((STREAM_BUFFERS,)),
        ],
        compiler_params=pltpu.CompilerParams(
            dimension_semantics=("arbitrary", "arbitrary"), vmem_limit_bytes=VMEM_LIMIT),
        name="mixer_prompt",
    )(x, mod, *weights, *[p.weight for p in preps])


def _mixer_sample_call(layer, x, mod, weights, s0, k_cache, v_cache):
    batch, seq, _ = x.shape
    assert seq == CHUNK
    rows = batch * seq
    state_rows = GLA_HEADS * GLA_DK

    def full(shape):
        zeros = (0,) * len(shape)
        return pl.BlockSpec(shape, lambda i: zeros)

    def of_layer(shape):
        index = (layer,) + (0,) * len(shape)
        return pl.BlockSpec((None,) + shape, lambda i: index)

    return pl.pallas_call(
        functools.partial(_mixer_kernel, True, batch, layer, ()),
        grid=(1,),
        in_specs=[full((rows, D_MODEL)), of_layer((batch, 6, D_MODEL))] + _mixer_weight_specs(layer) + [
            of_layer((batch, state_rows, GLA_DV)),
            of_layer((batch, WINDOW, SWA_KV)),
            of_layer((batch, WINDOW, SWA_KV)),
        ],
        out_specs=[
            full((rows, D_MODEL)),
            full((batch, state_rows, GLA_DV)),
            full((rows, SWA_KV)),
            full((rows, SWA_KV)),
        ],
        out_shape=[
            jax.ShapeDtypeStruct((rows, D_MODEL), F32),
            jax.ShapeDtypeStruct((batch, state_rows, GLA_DV), F32),
            jax.ShapeDtypeStruct((rows, SWA_KV), F32),
            jax.ShapeDtypeStruct((rows, SWA_KV), F32),
        ],
        scratch_shapes=_mixer_scratch(rows, batch, batch * BAND),
        compiler_params=pltpu.CompilerParams(
            dimension_semantics=("arbitrary",), vmem_limit_bytes=VMEM_LIMIT),
        name="mixer_sample",
    )(x.reshape(rows, D_MODEL), mod, *weights, s0, k_cache, v_cache)


def _ffn_kernel(sample, final, n_seg, seg_len, preps, *refs):
    if sample:
        (x_ref, mod_ref, g_ffn_ref, w_up_ref, conv_w_ref, conv_b_ref, w_down_ref, g_final_ref,
         past_ref, xo_ref, co_ref, h_scr, act_scr) = refs
    else:
        n_in, n_out, n_prep = 8, 2, len(preps)
        prep_src = refs[n_in:n_in + n_prep]
        prep_dst = refs[n_in + n_prep + n_out:n_in + n_prep + n_out + n_prep]
        refs = refs[:n_in] + refs[n_in + n_prep:n_in + n_prep + n_out] + refs[n_in + 2 * n_prep + n_out:]
        (x_ref, mod_ref, g_ffn_ref, w_up_ref, conv_w_ref, conv_b_ref, w_down_ref, g_final_ref,
         xo_ref, co_ref, h_scr, act_scr, past_scr) = refs
        for body, src, dst in zip(preps, prep_src, prep_dst):
            body(src, dst)
        t = pl.program_id(1)
        n_t = pl.num_programs(1)

        @pl.when(t == 0)
        def _():
            past_scr[...] = jnp.zeros_like(past_scr)

    g_ffn = g_ffn_ref[...]

    def mod_row(c, idx):
        if sample:
            return mod_ref[c, idx:idx + 1, :]
        return mod_ref[idx:idx + 1, :]

    for c in range(n_seg):
        rs = slice(c * seg_len, (c + 1) * seg_len)
        x = x_ref[rs, :]
        gain = g_ffn * (1.0 + mod_row(c, 4))
        h_scr[rs, :] = (x * _rms_scale(x) * gain + mod_row(c, 3)).astype(BF16)

    def up(j):
        lo, hi = FF_PARTS[j]
        h = h_scr[...]
        return _dot(h, w_up_ref[:, lo:hi]), _dot(h, w_up_ref[:, D_FF + lo:D_FF + hi])

    def activate(j, u, val):
        lo, hi = FF_PARTS[j]
        width = hi - lo
        w0 = conv_w_ref[0:1, lo:hi]
        w1 = conv_w_ref[1:2, lo:hi]
        w2 = conv_w_ref[2:3, lo:hi]
        cb = conv_b_ref[:, lo:hi]
        head = 2 * SUBLANE
        row_id = lax.broadcasted_iota(jnp.int32, (head, width), 0)
        for c in range(n_seg):
            rs = slice(c * seg_len, (c + 1) * seg_len)
            past = past_ref[c, :, lo:hi] if sample else past_scr[:, lo:hi]
            u_seg = u[rs, :]
            u1 = pltpu.roll(u_seg, 1, axis=0)
            u2 = pltpu.roll(u_seg, 2, axis=0)
            p0, p1 = past[0:1, :], past[1:2, :]
            u1_top = jnp.where(row_id == 0, p1, u1[0:head, :])
            u2_top = jnp.where(row_id == 0, p0, jnp.where(row_id == 1, p1, u2[0:head, :]))
            uc_top = w0 * u2_top + w1 * u1_top + w2 * u_seg[0:head, :] + cb
            uc = w0 * u2[head:, :] + w1 * u1[head:, :] + w2 * u_seg[head:, :] + cb
            top = slice(rs.start, rs.start + head)
            rest = slice(rs.start + head, rs.stop)
            act_scr[top, lo:hi] = (_silu(uc_top) * val[top, :]).astype(BF16)
            act_scr[rest, lo:hi] = (_silu(uc) * val[rest, :]).astype(BF16)
            tail = u_seg[seg_len - 2:seg_len, :]
            if sample:
                co_ref[c, :, lo:hi] = tail
            else:
                past_scr[:, lo:hi] = tail

    def down(j):
        lo, hi = FF_PARTS[j]
        return _dot(act_scr[:, lo:hi], w_down_ref[lo:hi, :])

    n_parts = len(FF_PARTS)
    pending = {0: up(0)}
    acc = None
    for j in range(n_parts):
        if j + 1 < n_parts:
            pending[j + 1] = up(j + 1)
        activate(j, *pending.pop(j))
        if j > 0:
            d = down(j - 1)
            acc = d if acc is None else acc + d
    acc = acc + down(n_parts - 1)

    for c in range(n_seg):
        rs = slice(c * seg_len, (c + 1) * seg_len)
        y = x_ref[rs, :] + mod_row(c, 5) * acc[rs, :]
        if final:
            y = y * _rms_scale(y) * g_final_ref[...]
        xo_ref[rs, :] = y

    if not sample:
        @pl.when(t == n_t - 1)
        def _():
            co_ref[...] = past_scr[...]


def _ffn_weight_specs(layer):
    return [
        _layer_spec(layer, (1, D_MODEL)),
        _layer_spec(0, (D_MODEL, 2 * D_FF)),
        _layer_spec(layer, (CONV_W, D_FF)),
        _layer_spec(layer, (1, D_FF)),
        _layer_spec(0, (D_FF, D_MODEL)),
        _const_spec((1, D_MODEL)),
    ]


def _ffn_prompt_call(layer, x, mod, mod_row0, weights, final, preps):
    batch, seq, _ = x.shape
    tile = min(PROMPT_TILE, seq)
    n_t = seq // tile
    prep_specs = [p.specs(n_t) for p in preps]
    assert all(p.n_blocks <= batch * n_t for p in preps)
    return pl.pallas_call(
        functools.partial(_ffn_kernel, False, final, 1, tile, tuple(p.body for p in preps)),
        grid=(batch, n_t),
        in_specs=[
            pl.BlockSpec((None, tile, D_MODEL), lambda b, t: (b, t, 0)),
            pl.BlockSpec((None, None, 6, D_MODEL), lambda b, t: (layer, mod_row0 + b, 0, 0)),
        ] + _ffn_weight_specs(layer) + [s[0] for s in prep_specs],
        out_specs=[
            pl.BlockSpec((None, tile, D_MODEL), lambda b, t: (b, t, 0)),
            pl.BlockSpec((None, CONV_W - 1, D_FF), lambda b, t: (b, 0, 0)),
        ] + [s[1] for s in prep_specs],
        out_shape=[
            jax.ShapeDtypeStruct((batch, seq, D_MODEL), F32),
            jax.ShapeDtypeStruct((batch, CONV_W - 1, D_FF), F32),
        ] + [s[2] for s in prep_specs],
        scratch_shapes=[
            pltpu.VMEM((tile, D_MODEL), BF16),
            pltpu.VMEM((tile, D_FF), BF16),
            pltpu.VMEM((CONV_W - 1, D_FF), F32),
        ],
        compiler_params=pltpu.CompilerParams(
            dimension_semantics=("arbitrary", "arbitrary"), vmem_limit_bytes=VMEM_LIMIT),
        name="ffn_prompt",
    )(x, mod, *weights, *[p.weight for p in preps])


def _ffn_sample_call(layer, x2d, mod, weights, past, final, batch, seq):
    rows = batch * seq

    def full(shape):
        zeros = (0,) * len(shape)
        return pl.BlockSpec(shape, lambda i: zeros)

    def of_layer(shape):
        index = (layer,) + (0,) * len(shape)
        return pl.BlockSpec((None,) + shape, lambda i: index)

    return pl.pallas_call(
        functools.partial(_ffn_kernel, True, final, batch, seq, ()),
        grid=(1,),
        in_specs=[full((rows, D_MODEL)), of_layer((batch, 6, D_MODEL))] + _ffn_weight_specs(layer) + [
            of_layer((batch, CONV_W - 1, D_FF)),
        ],
        out_specs=[full((rows, D_MODEL)), full((batch, CONV_W - 1, D_FF))],
        out_shape=[
            jax.ShapeDtypeStruct((rows, D_MODEL), F32),
            jax.ShapeDtypeStruct((batch, CONV_W - 1, D_FF), F32),
        ],
        scratch_shapes=[
            pltpu.VMEM((rows, D_MODEL), BF16),
            pltpu.VMEM((rows, D_FF), BF16),
        ],
        compiler_params=pltpu.CompilerParams(
            dimension_semantics=("arbitrary",), vmem_limit_bytes=VMEM_LIMIT),
        name="ffn_sample",
    )(x2d, mod, *weights, past)


def _pad_w_in_first(w):
    n_blocks = D_MODEL // W_PREP_ROWS
    return pl.pallas_call(
        _pad_cast_w_in,
        grid=(n_blocks,),
        in_specs=[pl.BlockSpec((None, W_PREP_ROWS, D_IN), lambda r: (0, r, 0))],
        out_specs=pl.BlockSpec((None, W_PREP_ROWS, D_IN_P), lambda r: (0, r, 0)),
        out_shape=jax.ShapeDtypeStruct((1, D_MODEL, D_IN_P), BF16),
        compiler_params=pltpu.CompilerParams(
            dimension_semantics=("arbitrary",), vmem_limit_bytes=VMEM_LIMIT),
        name="pad_w_in",
    )(w)


def kernel(x_prompt, x_sample, state_gla, cache_swa_k, cache_swa_v, state_conv, c_prompt, c_sample,
           w_ada, b_ada, g_attn, g_ffn, w_in, w_gk2, b_gk, g_gla, sinks, w_out, w_up, conv_w, conv_b,
           w_down, g_final):
    depth = w_ada.shape[0]
    batch, seq, _ = x_prompt.shape
    dec_batch, dec_seq, _ = x_sample.shape
    state_rows = GLA_HEADS * GLA_DK

    c_all = jnp.concatenate(
        [c_sample, c_prompt, jnp.zeros((ADA_ROWS - batch - dec_batch, D_MODEL), F32)], axis=0)
    mod_all = _ada_call(c_all, w_ada, b_ada).reshape(depth, ADA_ROWS, 6, D_MODEL)

    w_gk2_p = jnp.concatenate(
        [w_gk2, jnp.zeros((depth, LR_PAD - GLA_RANK, GLA_QK), F32)], axis=1).astype(BF16)
    w_in16 = w_in.astype(BF16)
    w_in_b, w_out_b = _pad_w_in_first(w_in16), w_out[:1].astype(BF16)
    mixer_steps = batch * (seq // min(MIXER_TILE, seq))
    ffn_steps = batch * (seq // min(PROMPT_TILE, seq))
    prep_ahead = (D_MODEL // MIXER_PREP_ROWS_UP <= mixer_steps
                  and D_FF // MIXER_PREP_ROWS_DOWN <= mixer_steps
                  and D_MODEL // FFN_PREP_ROWS <= ffn_steps)
    s0_all = state_gla.reshape(depth, dec_batch, state_rows, GLA_DV)
    kc_all = cache_swa_k.reshape(depth, dec_batch, WINDOW, SWA_KV)
    vc_all = cache_swa_v.reshape(depth, dec_batch, WINDOW, SWA_KV)

    yp = x_prompt
    ys = x_sample.reshape(dec_batch * dec_seq, D_MODEL)
    outs = [[] for _ in range(8)]
    for i in range(depth):
        final = i == depth - 1
        mixer_w = (g_attn[:, None], w_in_b, w_gk2_p, b_gk[:, None], g_gla[:, None], sinks, w_out_b)
        mixer_preps, ffn_preps = [], []
        if prep_ahead:
            mixer_preps = [_WeightPrep(w_up, i, MIXER_PREP_ROWS_UP, 2 * D_FF, _cast_weight),
                           _WeightPrep(w_down, i, MIXER_PREP_ROWS_DOWN, D_MODEL, _cast_weight)]
            if not final:
                ffn_preps = [_WeightPrep(w_in16, i + 1, FFN_PREP_ROWS, D_IN_P, _pad_cast_w_in),
                             _WeightPrep(w_out, i + 1, FFN_PREP_ROWS, D_MODEL, _cast_weight)]

        yp, s_p, k_p, v_p, *prepared = _mixer_prompt_call(i, yp, mod_all, dec_batch, mixer_w, mixer_preps)
        w_up_b, w_down_b = prepared or (w_up[i:i + 1].astype(BF16), w_down[i:i + 1].astype(BF16))
        ffn_w = (g_ffn[:, None], w_up_b, conv_w, conv_b[:, None], w_down_b, g_final[None])
        yp, conv_p, *prepared = _ffn_prompt_call(i, yp, mod_all, dec_batch, ffn_w, final, ffn_preps)

        ys, s_s, k_s, v_s = _mixer_sample_call(
            i, ys.reshape(dec_batch, dec_seq, D_MODEL), mod_all, mixer_w, s0_all, kc_all, vc_all)
        ys, conv_s = _ffn_sample_call(i, ys, mod_all, ffn_w, state_conv, final, dec_batch, dec_seq)
        if not final:
            w_in_b, w_out_b = prepared or (
                _pad_w_in_first(w_in16[i + 1:i + 2]), w_out[i + 1:i + 2].astype(BF16))

        keep = min(WINDOW, seq)
        outs[0].append(s_p.reshape(batch, GLA_HEADS, GLA_DK, GLA_DV))
        outs[1].append(k_p.reshape(batch, keep, SWA_KV_HEADS, SWA_HD))
        outs[2].append(v_p.reshape(batch, keep, SWA_KV_HEADS, SWA_HD))
        outs[3].append(conv_p)
        outs[4].append(s_s.reshape(dec_batch, GLA_HEADS, GLA_DK, GLA_DV))
        outs[5].append(k_s.reshape(dec_batch, dec_seq, SWA_KV_HEADS, SWA_HD))
        outs[6].append(v_s.reshape(dec_batch, dec_seq, SWA_KV_HEADS, SWA_HD))
        outs[7].append(conv_s)

    return (yp, ys.reshape(dec_batch, dec_seq, D_MODEL)) + tuple(jnp.stack(o) for o in outs)
```
